```python
import jax, jax.numpy as jnp
from jax import lax
import numpy as np

D_MODEL = 2048
BATCH = 16
SEQ = 256
DEPTH = 1
DEC_BATCH = 4
DEC_SEQ = 4096
PAST_LEN = 512

GRID_W = 64
MLSTM_HEADS = 4
MLSTM_DK = 128
MLSTM_DV = 256
MLSTM_WIDTH = MLSTM_HEADS * MLSTM_DV
MLSTM_QK = MLSTM_HEADS * MLSTM_DK
MLSTM_CHUNK = 64
RGLRU_WIDTH = 1024
RGLRU_BLOCKS = 8
RGLRU_BLOCK_DIM = RGLRU_WIDTH // RGLRU_BLOCKS
RGLRU_C = 8.0
CONV_WIDTH = 4
CONV_LEFT = 2
MIX_WIDTH = MLSTM_WIDTH + RGLRU_WIDTH
IN_COLS = 2 * MLSTM_QK + 2 * MLSTM_WIDTH + 4 * MLSTM_HEADS + 2 * RGLRU_WIDTH
N_GROUPS = 4
EXPERTS_PER_GROUP = 8
N_EXPERTS = N_GROUPS * EXPERTS_PER_GROUP
TOP_K = 2
EXPERT_FF = 1024
MOE_BLOCK = 128
EPS = 1e-6

kernel_name = "hybrid_mlstm_rglru_hmoe_diffusion_step"


def _rms_norm(x, w):
    xf = x.astype(jnp.float32)
    y = xf * lax.rsqrt(jnp.mean(xf * xf, axis=-1, keepdims=True) + EPS)
    return (y * w.astype(jnp.float32)).astype(x.dtype)


def _modulate(x, w, shift, scale):
    return _rms_norm(x, w) * (1.0 + scale[:, None, :]) + shift[:, None, :]


def _mlstm_chunkwise(q, k, v, i_pre, f_pre, c0, n0, m0):
    f32 = jnp.float32
    B, H, T, _ = q.shape
    L = MLSTM_CHUNK
    nc = T // L
    log_i = i_pre.astype(f32)
    log_f = jax.nn.log_sigmoid(f_pre.astype(f32))

    def chunks(a):
        return jnp.moveaxis(a.astype(f32).reshape(B, H, nc, L, *a.shape[3:]), 2, 0)

    xs = (chunks(q), chunks(k), chunks(v), chunks(log_i), chunks(log_f))
    lower = jnp.tril(jnp.ones((L, L), dtype=bool))

    def step(carry, xc):
        c, n, m = carry
        qc, kc, vc, li, lf = xc
        b = jnp.cumsum(lf, axis=-1)
        d = jnp.where(lower, b[..., :, None] - b[..., None, :] + li[..., None, :], -jnp.inf)
        inter = b + m[..., None]
        m_t = jnp.maximum(inter, jnp.max(d, axis=-1))
        s = jnp.einsum('bhtd,bhsd->bhts', qc, kc) * jnp.exp(d - m_t[..., None])
        w_inter = jnp.exp(inter - m_t)
        num = jnp.einsum('bhts,bhsv->bhtv', s, vc) + w_inter[..., None] * jnp.einsum('bhtd,bhdv->bhtv', qc, c)
        den = jnp.sum(s, axis=-1) + w_inter * jnp.einsum('bhtd,bhd->bht', qc, n)
        h = num / jnp.maximum(jnp.abs(den), jnp.exp(-m_t))[..., None]
        b_end = b[..., -1]
        g = b_end[..., None] - b + li
        m_new = jnp.maximum(b_end + m, jnp.max(g, axis=-1))
        wk = jnp.exp(g - m_new[..., None])
        decay = jnp.exp(b_end + m - m_new)
        c_new = decay[..., None, None] * c + jnp.einsum('bhsd,bhsv->bhdv', kc * wk[..., None], vc)
        n_new = decay[..., None] * n + jnp.einsum('bhs,bhsd->bhd', wk, kc)
        return (c_new, n_new, m_new), h

    carry0 = (c0.astype(f32), n0.astype(f32), m0.astype(f32))
    (c, n, m), hs = lax.scan(step, carry0, xs)
    h = jnp.moveaxis(hs, 0, 2).reshape(B, H, T, v.shape[-1])
    return h, c, n, m


def _linear_combine(e1, e2):
    a1, b1 = e1
    a2, b2 = e2
    return a1 * a2, a2 * b1 + b2


def _rglru(x, wa, ba, wx, bx, lam, h0):
    B, T, W = x.shape
    xf = x.astype(jnp.float32)
    xb = xf.reshape(B, T, RGLRU_BLOCKS, RGLRU_BLOCK_DIM)
    r = jax.nn.sigmoid(jnp.einsum('btgi,gij->btgj', xb, wa).reshape(B, T, W) + ba)
    ig = jax.nn.sigmoid(jnp.einsum('btgi,gij->btgj', xb, wx).reshape(B, T, W) + bx)
    log_a = -RGLRU_C * r * jax.nn.softplus(-lam.astype(jnp.float32))
    a = jnp.exp(log_a)
    u = jnp.sqrt(-jnp.expm1(2.0 * log_a)) * (ig * xf)
    a_cum, h = lax.associative_scan(_linear_combine, (a, u), axis=1)
    h = h + a_cum * h0.astype(jnp.float32)[:, None, :]
    return h, h[:, -1]


def _dwconv(x, w, b):
    T = x.shape[1]
    xp = jnp.pad(x, ((0, 0), (CONV_LEFT, CONV_WIDTH - 1 - CONV_LEFT), (0, 0)))
    out = b + xp[:, 0:T] * w[0]
    for j in range(1, CONV_WIDTH):
        out = out + xp[:, j:j + T] * w[j]
    return out


def _mixers(h, st, latent, w_in, b_gates, conv_w, conv_b, rg_wa, rg_ba, rg_wx, rg_bx, rg_lambda, mlstm_norm_w, w_out):
    st_c, st_n, st_m, st_h = st
    B, T, _ = h.shape
    z = jnp.einsum('btd,de->bte', h, w_in)
    sizes = (MLSTM_QK, MLSTM_QK, MLSTM_WIDTH, MLSTM_WIDTH, 4 * MLSTM_HEADS, RGLRU_WIDTH, RGLRU_WIDTH)
    q, k, v, o, gates, xr, xg = jnp.split(z, np.cumsum(sizes)[:-1].tolist(), axis=-1)

    def heads(a, dh):
        return a.reshape(B, T, MLSTM_HEADS, dh).transpose(0, 2, 1, 3)

    q = heads(q, MLSTM_DK) * (MLSTM_DK ** -0.5)
    k = heads(k, MLSTM_DK)
    v = heads(v, MLSTM_DV)
    gates = (gates + b_gates).reshape(B, T, 4, MLSTM_HEADS).transpose(2, 0, 3, 1)

    def flip_t(a):
        return jnp.flip(a, axis=2)

    hf, cf, nf, mf = _mlstm_chunkwise(q, k, v, gates[0], gates[1], st_c[:, 0], st_n[:, 0], st_m[:, 0])
    hb, cb, nb, mb = _mlstm_chunkwise(flip_t(q), flip_t(k), flip_t(v), flip_t(gates[2]), flip_t(gates[3]),
                                      st_c[:, 1], st_n[:, 1], st_m[:, 1])
    h_a = (hf + flip_t(hb)).transpose(0, 2, 1, 3)
    h_a = _rms_norm(h_a, mlstm_norm_w.reshape(MLSTM_HEADS, MLSTM_DV)).reshape(B, T, MLSTM_WIDTH)
    y_a = jax.nn.sigmoid(o) * h_a

    if latent:
        rows = T // GRID_W
        xc = _dwconv(xr.reshape(B * rows, GRID_W, RGLRU_WIDTH), conv_w, conv_b).reshape(B, T, RGLRU_WIDTH)
    else:
        xc = _dwconv(xr, conv_w, conv_b)
    hr_f, last_f = _rglru(xc, rg_wa[0], rg_ba[0], rg_wx[0], rg_bx[0], rg_lambda[0], st_h[:, 0])
    hr_b, last_b = _rglru(jnp.flip(xc, axis=1), rg_wa[1], rg_ba[1], rg_wx[1], rg_bx[1], rg_lambda[1], st_h[:, 1])
    y_b = (hr_f + jnp.flip(hr_b, axis=1)) * jax.nn.gelu(xg)

    y = jnp.einsum('btm,md->btd', jnp.concatenate([y_a, y_b.astype(y_a.dtype)], axis=-1), w_out)
    new_st = (jnp.stack([cf, cb], axis=1), jnp.stack([nf, nb], axis=1),
              jnp.stack([mf, mb], axis=1), jnp.stack([last_f, last_b], axis=1))
    return y.astype(h.dtype), new_st


def _hier_moe(h, rg_w, rg_b, re_w, re_b, wg, wu, wd):
    B, T, D = h.shape
    n_tok = B * T
    xt = h.reshape(n_tok, D)
    glog = jnp.einsum('nd,dg->ng', xt, rg_w).astype(jnp.float32) + rg_b
    grp = jnp.argmax(glog, axis=-1)
    p_grp = jnp.take_along_axis(jax.nn.softmax(glog, axis=-1), grp[:, None], axis=-1)
    elog = jnp.einsum('nd,gde->nge', xt, re_w).astype(jnp.float32) + re_b
    elog = jnp.take_along_axis(elog, grp[:, None, None], axis=1)[:, 0]
    top_v, top_i = lax.top_k(elog, TOP_K)
    w_assign = jax.nn.softmax(top_v, axis=-1) * p_grp
    e_assign = grp[:, None].astype(jnp.int32) * EXPERTS_PER_GROUP + top_i.astype(jnp.int32)

    n_assign = n_tok * TOP_K
    flat_e = e_assign.reshape(-1)
    order = jnp.argsort(flat_e)
    sorted_e = flat_e[order]
    counts = jnp.bincount(flat_e, length=N_EXPERTS)
    padded = (counts + MOE_BLOCK - 1) // MOE_BLOCK * MOE_BLOCK
    pad_end = jnp.cumsum(padded)
    pad_start = pad_end - padded
    start = jnp.cumsum(counts) - counts
    dest = pad_start[sorted_e] + jnp.arange(n_assign, dtype=jnp.int32) - start[sorted_e]
    n_blocks = -(-n_assign // MOE_BLOCK) + N_EXPERTS
    n_rows = n_blocks * MOE_BLOCK
    src_tok = (order // TOP_K).astype(jnp.int32)
    row_tok = jnp.full((n_rows,), n_tok, dtype=jnp.int32).at[dest].set(src_tok)
    x_pad = jnp.concatenate([xt, jnp.zeros((1, D), xt.dtype)], axis=0)
    rows = x_pad[row_tok].reshape(n_blocks, MOE_BLOCK, D)
    blk_e = jnp.minimum(jnp.searchsorted(pad_end, jnp.arange(n_blocks, dtype=jnp.int32) * MOE_BLOCK, side='right'),
                        N_EXPERTS - 1)

    def run_block(args):
        xb, e = args
        return (jax.nn.silu(xb @ wg[e]) * (xb @ wu[e])) @ wd[e]

    y_rows = lax.map(run_block, (rows, blk_e)).reshape(n_rows, D)
    contrib = y_rows[dest] * w_assign.reshape(-1)[order][:, None]
    out = jax.ops.segment_sum(contrib, src_tok, num_segments=n_tok)
    return out.reshape(B, T, D).astype(h.dtype)


def _layer(x, mod, st, latent, w_in, b_gates, conv_w, conv_b, rg_wa, rg_ba, rg_wx, rg_bx, rg_lambda,
           mlstm_norm_w, w_out, norm1_w, norm2_w, rgw, rgb, rew, reb, ewg, ewu, ewd):
    shift1, scale1, gate1, shift2, scale2, gate2 = jnp.split(mod.astype(x.dtype), 6, axis=-1)
    h = _modulate(x, norm1_w, shift1, scale1)
    y, new_st = _mixers(h, st, latent, w_in, b_gates, conv_w, conv_b, rg_wa, rg_ba, rg_wx, rg_bx, rg_lambda,
                        mlstm_norm_w, w_out)
    x = x + gate1[:, None, :] * y
    h = _modulate(x, norm2_w, shift2, scale2)
    x = x + gate2[:, None, :] * _hier_moe(h, rgw, rgb, rew, reb, ewg, ewu, ewd)
    return x, new_st


def setup_inputs(seed: int = 0) -> dict:
    key = jax.random.key(seed)
    keys = list(jax.random.split(key, 40))

    def nrm(shape, s):
        return jax.random.normal(keys.pop(), shape, jnp.float32) * s

    D = D_MODEL
    x_prompt = nrm((BATCH, SEQ, D), 1.0)
    x_sample = nrm((DEC_BATCH, DEC_SEQ, D), 1.0)
    state_mlstm_c = nrm((DEC_BATCH, DEPTH, 2, MLSTM_HEADS, MLSTM_DK, MLSTM_DV), 0.5)
    state_mlstm_n = nrm((DEC_BATCH, DEPTH, 2, MLSTM_HEADS, MLSTM_DK), 0.5)
    state_mlstm_m = jax.random.uniform(keys.pop(), (DEC_BATCH, DEPTH, 2, MLSTM_HEADS), jnp.float32, 0.0, 4.0)
    state_rglru_h = nrm((DEC_BATCH, DEPTH, 2, RGLRU_WIDTH), 0.5)
    c = nrm((DEC_BATCH, D), 1.0)
    c_ctx = nrm((D,), 1.0)
    w_ada = nrm((DEPTH, D, 6 * D), 0.5 * D ** -0.5)
    b_ada = nrm((DEPTH, 6 * D), 0.01)
    norm1_w = 1.0 + nrm((DEPTH, D), 0.02)
    w_in = nrm((DEPTH, D, IN_COLS), D ** -0.5)
    gate_i = nrm((DEPTH, 2, 1, MLSTM_HEADS), 0.1)
    gate_f = jnp.linspace(3.0, 6.0, MLSTM_HEADS, dtype=jnp.float32) + nrm((DEPTH, 2, 1, MLSTM_HEADS), 0.1)
    b_gates = jnp.concatenate([gate_i, gate_f], axis=2).reshape(DEPTH, 4 * MLSTM_HEADS)
    conv_w = nrm((DEPTH, CONV_WIDTH, RGLRU_WIDTH), CONV_WIDTH ** -0.5)
    conv_b = nrm((DEPTH, RGLRU_WIDTH), 0.01)
    rg_wa = nrm((DEPTH, 2, RGLRU_BLOCKS, RGLRU_BLOCK_DIM, RGLRU_BLOCK_DIM), RGLRU_BLOCK_DIM ** -0.5)
    rg_ba = nrm((DEPTH, 2, RGLRU_WIDTH), 0.01)
    rg_wx = nrm((DEPTH, 2, RGLRU_BLOCKS, RGLRU_BLOCK_DIM, RGLRU_BLOCK_DIM), RGLRU_BLOCK_DIM ** -0.5)
    rg_bx = nrm((DEPTH, 2, RGLRU_WIDTH), 0.01)
    a0 = jax.random.uniform(keys.pop(), (DEPTH, 2, RGLRU_WIDTH), jnp.float32, 0.9, 0.999)
    s0 = a0 ** (1.0 / RGLRU_C)
    rg_lambda = jnp.log(s0) - jnp.log1p(-s0)
    mlstm_norm_w = 1.0 + nrm((DEPTH, MLSTM_WIDTH), 0.02)
    w_out = nrm((DEPTH, MIX_WIDTH, D), MIX_WIDTH ** -0.5)
    norm2_w = 1.0 + nrm((DEPTH, D), 0.02)
    router_group_w = nrm((DEPTH, D, N_GROUPS), D ** -0.5)
    router_group_b = nrm((DEPTH, N_GROUPS), 0.01)
    router_expert_w = nrm((DEPTH, N_GROUPS, D, EXPERTS_PER_GROUP), D ** -0.5)
    router_expert_b = nrm((DEPTH, N_GROUPS, EXPERTS_PER_GROUP), 0.01)
    expert_w_gate = nrm((DEPTH, N_EXPERTS, D, EXPERT_FF), D ** -0.5)
    expert_w_up = nrm((DEPTH, N_EXPERTS, D, EXPERT_FF), D ** -0.5)
    expert_w_down = nrm((DEPTH, N_EXPERTS, EXPERT_FF, D), EXPERT_FF ** -0.5)
    final_norm_w = 1.0 + nrm((D,), 0.02)
    return {"x_prompt": x_prompt, "x_sample": x_sample, "state_mlstm_c": state_mlstm_c,
            "state_mlstm_n": state_mlstm_n, "state_mlstm_m": state_mlstm_m, "state_rglru_h": state_rglru_h,
            "c": c, "c_ctx": c_ctx, "w_ada": w_ada, "b_ada": b_ada, "norm1_w": norm1_w, "w_in": w_in,
            "b_gates": b_gates, "conv_w": conv_w, "conv_b": conv_b, "rg_wa": rg_wa, "rg_ba": rg_ba,
            "rg_wx": rg_wx, "rg_bx": rg_bx, "rg_lambda": rg_lambda, "mlstm_norm_w": mlstm_norm_w,
            "w_out": w_out, "norm2_w": norm2_w, "router_group_w": router_group_w,
            "router_group_b": router_group_b, "router_expert_w": router_expert_w,
            "router_expert_b": router_expert_b, "expert_w_gate": expert_w_gate, "expert_w_up": expert_w_up,
            "expert_w_down": expert_w_down, "final_norm_w": final_norm_w}


def reference(x_prompt, x_sample, state_mlstm_c, state_mlstm_n, state_mlstm_m, state_rglru_h, c, c_ctx,
              w_ada, b_ada, norm1_w, w_in, b_gates, conv_w, conv_b, rg_wa, rg_ba, rg_wx, rg_bx, rg_lambda,
              mlstm_norm_w, w_out, norm2_w, router_group_w, router_group_b, router_expert_w, router_expert_b,
              expert_w_gate, expert_w_up, expert_w_down, final_norm_w):
    n_req = x_prompt.shape[0]
    zero_st = (jnp.zeros((n_req, 2, MLSTM_HEADS, MLSTM_DK, MLSTM_DV), jnp.float32),
               jnp.zeros((n_req, 2, MLSTM_HEADS, MLSTM_DK), jnp.float32),
               jnp.zeros((n_req, 2, MLSTM_HEADS), jnp.float32),
               jnp.zeros((n_req, 2, RGLRU_WIDTH), jnp.float32))
    yp, ys = x_prompt, x_sample
    new_c, new_n, new_m, new_h = [], [], [], []
    for l in range(DEPTH):
        lp = (w_in[l], b_gates[l], conv_w[l], conv_b[l], rg_wa[l], rg_ba[l], rg_wx[l], rg_bx[l], rg_lambda[l],
              mlstm_norm_w[l], w_out[l], norm1_w[l], norm2_w[l], router_group_w[l], router_group_b[l],
              router_expert_w[l], router_expert_b[l], expert_w_gate[l], expert_w_up[l], expert_w_down[l])
        mod_ctx = jax.nn.silu(c_ctx) @ w_ada[l] + b_ada[l]
        mod_ctx = jnp.broadcast_to(mod_ctx[None, :], (n_req, 6 * D_MODEL))
        mod_lat = jax.nn.silu(c) @ w_ada[l] + b_ada[l]
        yp, (sc, sn, sm, sh) = _layer(yp, mod_ctx, zero_st, False, *lp)
        lat_st = (state_mlstm_c[:, l], state_mlstm_n[:, l], state_mlstm_m[:, l], state_rglru_h[:, l])
        ys, _ = _layer(ys, mod_lat, lat_st, True, *lp)
        new_c.append(sc)
        new_n.append(sn)
        new_m.append(sm)
        new_h.append(sh)
    y_prompt = _rms_norm(yp, final_norm_w)
    y_sample = _rms_norm(ys, final_norm_w)
    return (y_prompt, y_sample, jnp.stack(new_c, axis=1), jnp.stack(new_n, axis=1),
            jnp.stack(new_m, axis=1), jnp.stack(new_h, axis=1))
```

```python
import functools

import jax
import jax.numpy as jnp
import numpy as np
from jax import lax
from jax.experimental import pallas as pl
from jax.experimental.pallas import tpu as pltpu

F32 = jnp.float32
BF16 = jnp.bfloat16
I32 = jnp.int32
U32 = jnp.uint32

EPS = 1e-6
GRID_W = 64
CONV_LEFT = 2
RGLRU_C = 8.0
TOP_K = 2
LANES = 128
SUBLANES = 8
MLSTM_L = 128
NEG = -1e30
VMEM_LIMIT = 56 * 1024 * 1024

_HIGHEST = lax.Precision.HIGHEST


def _cparams(sem, vmem=VMEM_LIMIT):
    return pltpu.CompilerParams(dimension_semantics=sem, vmem_limit_bytes=vmem)


def _sigmoid(x):
    return 1.0 / (1.0 + jnp.exp(-x))


def _row_to_col(r, n):
    return jnp.broadcast_to(r, (LANES, n)).T


def _lane_tile(x, reps):
    return x if reps == 1 else jnp.concatenate([x] * reps, axis=1)


def _ada_kernel(c_ref, w_ref, b_ref, o_ref):
    c = c_ref[...]
    s = (c * _sigmoid(c)).astype(BF16)
    o_ref[...] = jnp.dot(s, w_ref[...].astype(BF16), preferred_element_type=F32) + b_ref[...]


def _ada(cvec, w_ada, b_ada):
    d, n = w_ada.shape
    tn = 1024 if n % 1024 == 0 else 512
    assert n % tn == 0
    return pl.pallas_call(
        _ada_kernel,
        grid=(n // tn,),
        in_specs=[pl.BlockSpec((SUBLANES, d), lambda j: (0, 0)),
                  pl.BlockSpec((d, tn), lambda j: (0, j)),
                  pl.BlockSpec((1, tn), lambda j: (0, j))],
        out_specs=pl.BlockSpec((SUBLANES, tn), lambda j: (0, j)),
        out_shape=jax.ShapeDtypeStruct((SUBLANES, n), F32),
        compiler_params=_cparams(("arbitrary",)),
        name="ada",
    )(cvec, w_ada, b_ada.reshape(1, n))


def _modulated_norm(x, w, shift, scale):
    ms = jnp.mean(x * x, axis=-1, keepdims=True)
    return (x * lax.rsqrt(ms + EPS) * w) * (1.0 + scale) + shift


def _mod_row(i, nctx_tiles, tiles_per_lat):
    return jnp.where(i < nctx_tiles, 0, 1 + (i - nctx_tiles) // tiles_per_lat)


def _inproj_kernel(xp_ref, xs_ref, mod_ref, n1_ref, w_ref, bg_ref,
                   q_ref, k_ref, v_ref, o_ref, gt_ref, xr_ref, xg_ref,
                   *, nctx_tiles, tiles_per_lat, d, qk, mw, rw, gh, qscale):
    i = pl.program_id(0)
    x = jnp.where(i < nctx_tiles, xp_ref[...], xs_ref[...])
    row = _mod_row(i, nctx_tiles, tiles_per_lat)
    shift = mod_ref[pl.ds(row, 1), 0:d]
    scale = mod_ref[pl.ds(row, 1), d:2 * d]
    hb = _modulated_norm(x, n1_ref[...], shift, scale).astype(BF16)

    def proj(c0, width):
        return jnp.dot(hb, w_ref[:, c0:c0 + width], preferred_element_type=F32)

    c0 = 0
    q_ref[...] = (proj(c0, qk) * qscale).astype(BF16); c0 += qk
    k_ref[...] = proj(c0, qk).astype(BF16); c0 += qk
    v_ref[...] = proj(c0, mw).astype(BF16); c0 += mw
    o_ref[...] = proj(c0, mw); c0 += mw
    zg = proj(c0, LANES) + bg_ref[...]; c0 += LANES
    lane = lax.broadcasted_iota(I32, zg.shape, 1)
    log_sig = jnp.minimum(zg, 0.0) - jnp.log1p(jnp.exp(-jnp.abs(zg)))
    zg = jnp.where(lane % 2 == 1, log_sig, zg)
    gt_ref[...] = zg.T[0:gh, :]
    xr_ref[...] = proj(c0, rw); c0 += rw
    xg_ref[...] = proj(c0, rw)


def _in_proj(xp, xs, mod, norm1_w, w_cat, bg, *, lat_seq, heads, dk, dv, rw):
    nc, d = xp.shape
    nl = xs.shape[0]
    nt = nc + nl
    tm = 256
    qk, mw, gh = heads * dk, heads * dv, heads * SUBLANES
    nctx_tiles = nc // tm
    kern = functools.partial(_inproj_kernel, nctx_tiles=nctx_tiles, tiles_per_lat=lat_seq // tm, d=d, qk=qk, mw=mw,
                             rw=rw, gh=gh, qscale=dk ** -0.5)
    row = lambda i: (i, 0)
    const = lambda i: (0, 0)
    return pl.pallas_call(
        kern,
        grid=(nt // tm,),
        in_specs=[pl.BlockSpec((tm, d), lambda i: (jnp.minimum(i, nctx_tiles - 1), 0)),
                  pl.BlockSpec((tm, d), lambda i: (jnp.maximum(i - nctx_tiles, 0), 0)),
                  pl.BlockSpec(mod.shape, const),
                  pl.BlockSpec((1, d), const),
                  pl.BlockSpec(w_cat.shape, const, pipeline_mode=pl.Buffered(1)),
                  pl.BlockSpec((1, LANES), const)],
        out_specs=[pl.BlockSpec((tm, qk), row), pl.BlockSpec((tm, qk), row), pl.BlockSpec((tm, mw), row),
                   pl.BlockSpec((tm, mw), row), pl.BlockSpec((gh, tm), lambda i: (0, i)),
                   pl.BlockSpec((tm, rw), row), pl.BlockSpec((tm, rw), row)],
        out_shape=[jax.ShapeDtypeStruct((nt, qk), BF16), jax.ShapeDtypeStruct((nt, qk), BF16),
                   jax.ShapeDtypeStruct((nt, mw), BF16), jax.ShapeDtypeStruct((nt, mw), F32),
                   jax.ShapeDtypeStruct((gh, nt), F32),
                   jax.ShapeDtypeStruct((nt, rw), F32), jax.ShapeDtypeStruct((nt, rw), F32)],
        compiler_params=_cparams(("arbitrary",)),
        name="in_proj",
    )(xp, xs, mod, norm1_w.reshape(1, d), w_cat, bg)


def _mlstm_kernel(*refs, t_len, dk, dv, has_state, emit_state):
    it = iter(refs)
    q_ref, k_ref, v_ref, o_ref, gt_ref, nw_ref, tri_ref = (next(it) for _ in range(7))
    if has_state:
        c0_ref, n0_ref, m0_ref = (next(it) for _ in range(3))
    ya_ref = next(it)
    if emit_state:
        cn_ref, nn_ref, mn_ref = (next(it) for _ in range(3))
    hf_scr, hb_scr, c_scr, n_scr, m_scr = (next(it) for _ in range(5))
    ln = MLSTM_L
    nchunks = t_len // ln
    h_scr = (hf_scr, hb_scr)

    if has_state:
        c_scr[...] = c0_ref[...]
        n_scr[...] = n0_ref[...]
        m_scr[...] = m0_ref[...]
    else:
        c_scr[...] = jnp.zeros_like(c_scr)
        n_scr[...] = jnp.zeros_like(n_scr)
        m_scr[...] = jnp.zeros_like(m_scr)

    def chunk(d, t0):
        q = q_ref[pl.ds(t0, ln), :]
        k = k_ref[pl.ds(t0, ln), :]
        v = v_ref[pl.ds(t0, ln), :]
        g8 = gt_ref[:, pl.ds(t0, ln)]
        cum8 = jnp.dot(g8, tri_ref[d], precision=_HIGHEST, preferred_element_type=F32)
        valid = tri_ref[1 - d] > 0.5
        li = g8[2 * d:2 * d + 1]
        lf = g8[2 * d + 1:2 * d + 2]
        cum_row = cum8[2 * d + 1:2 * d + 2]
        total = jnp.sum(lf, axis=1, keepdims=True)
        a_row = li - cum_row
        cum_col = _row_to_col(cum_row, ln)
        a_col = _row_to_col(a_row, ln)
        m_prev = m_scr[d]
        c_st = c_scr[d]
        n_st = n_scr[d]

        dmat = jnp.where(valid, _lane_tile(cum_col, ln // LANES) + a_row, NEG)
        inter = cum_col + m_prev
        m_t = jnp.maximum(inter, jnp.max(dmat, axis=1, keepdims=True))
        s = lax.dot_general(q, k, (((1,), (1,)), ((), ())), preferred_element_type=F32)
        p = s * jnp.exp(dmat - _lane_tile(m_t, ln // LANES))
        w_inter = jnp.exp(inter - m_t)
        num = (jnp.dot(p.astype(BF16), v, preferred_element_type=F32)
               + _lane_tile(w_inter, dv // LANES) * jnp.dot(q, c_st.astype(BF16), preferred_element_type=F32))
        qn = jnp.sum(q.astype(F32) * n_st, axis=1, keepdims=True)
        den = jnp.sum(p, axis=1, keepdims=True) + w_inter * qn
        inv = 1.0 / jnp.maximum(jnp.abs(den), jnp.exp(-m_t))
        h_scr[d][pl.ds(t0, ln), :] = num * _lane_tile(inv, dv // LANES)

        g_row = total + a_row
        m_new = jnp.maximum(total + m_prev, jnp.max(g_row, axis=1, keepdims=True))
        wk = jnp.exp(total + a_col - m_new)
        decay = jnp.exp(total + m_prev - m_new)
        kw = k.astype(F32) * _lane_tile(wk, dk // LANES)
        c_scr[d] = decay * c_st + lax.dot_general(kw.astype(BF16), v, (((0,), (0,)), ((), ())),
                                                  preferred_element_type=F32)
        n_scr[d] = decay * n_st + jnp.sum(kw, axis=0, keepdims=True)
        m_scr[d] = m_new

    def step(j, carry):
        chunk(0, pl.multiple_of(j * ln, ln))
        chunk(1, pl.multiple_of((nchunks - 1 - j) * ln, ln))
        return carry

    lax.fori_loop(0, nchunks, step, 0)

    def finish(j, carry):
        t0 = pl.multiple_of(j * ln, ln)
        hs = hf_scr[pl.ds(t0, ln), :] + hb_scr[pl.ds(t0, ln), :]
        ms = jnp.mean(hs * hs, axis=1, keepdims=True)
        y = hs * lax.rsqrt(ms + EPS) * nw_ref[...]
        ya_ref[pl.ds(t0, ln), :] = (_sigmoid(o_ref[pl.ds(t0, ln), :]) * y).astype(BF16)
        return carry

    lax.fori_loop(0, nchunks, finish, 0)

    if emit_state:
        cn_ref[...] = c_scr[...]
        nn_ref[...] = n_scr[...]
        mn_ref[...] = m_scr[...]


def _mlstm_tri():
    r = np.arange(MLSTM_L)
    fwd = (r[:, None] <= r[None, :]).astype(np.float32)
    return jnp.asarray(np.stack([fwd, fwd.T]))


def _mlstm(q, k, v, o, gt, norm_w, *, row0, n_seq, t_len, heads, dk, dv, state=None, emit_state=False):
    blk0 = row0 // t_len
    tok = lambda s, h: (blk0 + s, h)
    in_specs = [pl.BlockSpec((t_len, dk), tok), pl.BlockSpec((t_len, dk), tok), pl.BlockSpec((t_len, dv), tok),
                pl.BlockSpec((t_len, dv), tok),
                pl.BlockSpec((SUBLANES, t_len), lambda s, h: (h, blk0 + s)),
                pl.BlockSpec((None, 1, dv), lambda s, h: (h, 0, 0)),
                pl.BlockSpec((2, MLSTM_L, MLSTM_L), lambda s, h: (0, 0, 0))]
    args = [q, k, v, o, gt, norm_w.reshape(heads, 1, dv), _mlstm_tri()]
    if state is not None:
        c0, n0, m0 = state
        in_specs += [pl.BlockSpec((None, 2, None, dk, dv), lambda s, h: (s, 0, h, 0, 0)),
                     pl.BlockSpec((None, 2, None, 1, dk), lambda s, h: (s, 0, h, 0, 0)),
                     pl.BlockSpec((None, 2, None, 1, 1), lambda s, h: (s, 0, h, 0, 0))]
        args += [c0, n0.reshape(n_seq, 2, heads, 1, dk), m0.reshape(n_seq, 2, heads, 1, 1)]
    out_specs = [pl.BlockSpec((t_len, dv), lambda s, h: (s, h))]
    out_shape = [jax.ShapeDtypeStruct((n_seq * t_len, heads * dv), BF16)]
    if emit_state:
        out_specs += [pl.BlockSpec((None, 2, None, dk, dv), lambda s, h: (s, 0, h, 0, 0)),
                      pl.BlockSpec((None, 2, None, 1, dk), lambda s, h: (s, 0, h, 0, 0)),
                      pl.BlockSpec((None, 2, None, 1, 1), lambda s, h: (s, 0, h, 0, 0))]
        out_shape += [jax.ShapeDtypeStruct((n_seq, 2, heads, dk, dv), F32),
                      jax.ShapeDtypeStruct((n_seq, 2, heads, 1, dk), F32),
                      jax.ShapeDtypeStruct((n_seq, 2, heads, 1, 1), F32)]
    kern = functools.partial(_mlstm_kernel, t_len=t_len, dk=dk, dv=dv, has_state=state is not None,
                             emit_state=emit_state)
    return pl.pallas_call(
        kern,
        grid=(n_seq, heads),
        in_specs=in_specs,
        out_specs=out_specs,
        out_shape=out_shape,
        scratch_shapes=[pltpu.VMEM((t_len, dv), F32), pltpu.VMEM((t_len, dv), F32),
                        pltpu.VMEM((2, dk, dv), F32), pltpu.VMEM((2, 1, dk), F32), pltpu.VMEM((2, 1, 1), F32)],
        compiler_params=_cparams(("arbitrary", "arbitrary")),
        name="mlstm_state" if emit_state else "mlstm",
    )(*args)


def _gelu_tanh(x):
    return x * (0.5 * (1.0 + jnp.tanh(0.7978845608028654 * (x + 0.044715 * (x * x * x)))))


def _softplus(x):
    return jnp.maximum(x, 0.0) + jnp.log1p(jnp.exp(-jnp.abs(x)))


def _rglru_kernel(*refs, t_len, seg, sub, pitch, tc, has_state, emit_state):
    it = iter(refs)
    xr_ref, xg_ref, cw_ref, cb_ref, wg_ref, bg_ref, lam_ref = (next(it) for _ in range(7))
    if has_state:
        h0_ref = next(it)
    yb_ref = next(it)
    if emit_state:
        hn_ref = next(it)
    a_scr, u_scr, cin_scr = (next(it) for _ in range(3))
    nchunks = t_len // tc
    piece = min(tc, sub)
    npieces = tc // piece

    def scan_rows(t0, p):
        t = t0 + p * piece
        i = t // sub
        return pl.ds(pl.multiple_of(i * pitch + (t - i * sub), SUBLANES), piece), i

    sp = _softplus(-lam_ref[...])

    def gates(c, carry):
        t0 = pl.multiple_of(c * tc, tc)
        x = xr_ref[pl.ds(t0, tc), :]
        pos = lax.broadcasted_iota(I32, x.shape, 0) % seg
        xc = cb_ref[...] + cw_ref[CONV_LEFT:CONV_LEFT + 1, :] * x
        for j in range(cw_ref.shape[0]):
            off = j - CONV_LEFT
            if off == 0:
                continue
            shifted = pltpu.roll(x, (-off) % tc, 0)
            ok = (pos >= -off) if off < 0 else (pos < seg - off)
            xc = xc + cw_ref[j:j + 1, :] * jnp.where(ok, shifted, 0.0)
        r_all = jnp.dot(xc.astype(BF16), wg_ref[...], preferred_element_type=F32) + bg_ref[...]
        for d in range(2):
            r = _sigmoid(r_all[:, (2 * d) * LANES:(2 * d + 1) * LANES])
            ig = _sigmoid(r_all[:, (2 * d + 1) * LANES:(2 * d + 2) * LANES])
            log_a = (-RGLRU_C) * r * sp[d:d + 1]
            a = jnp.exp(log_a)
            th = jnp.tanh(log_a)
            one_minus_a2 = (-2.0) * th / (1.0 - th)
            u = jnp.sqrt(one_minus_a2) * (ig * xc)
            for p in range(npieces):
                rows, _ = scan_rows(t0, p)
                a_scr[d, rows, :] = a[p * piece:(p + 1) * piece]
                u_scr[d, rows, :] = u[p * piece:(p + 1) * piece]
        return carry

    lax.fori_loop(0, nchunks, gates, 0)

    def scan(j, carry):
        hf, pf, hb, pb = carry
        jf = pl.ds(j, SUBLANES, stride=pitch)
        jb = pl.ds(sub - 1 - j, SUBLANES, stride=pitch)
        af = a_scr[0, jf, :]
        hf = af * hf + u_scr[0, jf, :]
        pf = af * pf
        a_scr[0, jf, :] = pf
        u_scr[0, jf, :] = hf
        ab = a_scr[1, jb, :]
        hb = ab * hb + u_scr[1, jb, :]
        pb = ab * pb
        a_scr[1, jb, :] = pb
        u_scr[1, jb, :] = hb
        return hf, pf, hb, pb

    zero = jnp.zeros((SUBLANES, LANES), F32)
    one = jnp.ones((SUBLANES, LANES), F32)
    hf, pf, hb, pb = lax.fori_loop(0, sub, scan, (zero, one, zero, one))

    cf = h0_ref[0:1, :] if has_state else jnp.zeros((1, LANES), F32)
    for i in range(SUBLANES):
        cin_scr[0, i:i + 1, :] = cf
        cf = hf[i:i + 1] + pf[i:i + 1] * cf
    cb = h0_ref[1:2, :] if has_state else jnp.zeros((1, LANES), F32)
    for i in reversed(range(SUBLANES)):
        cin_scr[1, i:i + 1, :] = cb
        cb = hb[i:i + 1] + pb[i:i + 1] * cb
    if emit_state:
        hn_ref[0:1, :] = cf
        hn_ref[1:2, :] = cb

    def finish(c, carry):
        t0 = pl.multiple_of(c * tc, tc)
        for p in range(npieces):
            rows, i = scan_rows(t0, p)
            h = (u_scr[0, rows, :] + a_scr[0, rows, :] * cin_scr[0, pl.ds(i, 1), :]
                 + u_scr[1, rows, :] + a_scr[1, rows, :] * cin_scr[1, pl.ds(i, 1), :])
            nat = pl.ds(pl.multiple_of(t0 + p * piece, SUBLANES), piece)
            yb_ref[nat, :] = (h * _gelu_tanh(xg_ref[nat, :])).astype(BF16)
        return carry

    lax.fori_loop(0, nchunks, finish, 0)


def _rglru(xr, xg, conv_w, conv_b, wg, bg, lam, *, row0, n_seq, t_len, seg, state=None, emit_state=False):
    rw = xr.shape[1]
    nblk = rw // LANES
    blk0 = row0 // t_len
    sub = t_len // SUBLANES
    pitch = sub + SUBLANES
    tc = 256
    tok = lambda s, g: (blk0 + s, g)
    in_specs = [pl.BlockSpec((t_len, LANES), tok), pl.BlockSpec((t_len, LANES), tok),
                pl.BlockSpec((conv_w.shape[0], LANES), lambda s, g: (0, g)),
                pl.BlockSpec((1, LANES), lambda s, g: (0, g)),
                pl.BlockSpec((None, LANES, 4 * LANES), lambda s, g: (g, 0, 0)),
                pl.BlockSpec((None, 1, 4 * LANES), lambda s, g: (g, 0, 0)),
                pl.BlockSpec((2, LANES), lambda s, g: (0, g))]
    args = [xr, xg, conv_w, conv_b.reshape(1, rw), wg, bg, lam]
    if state is not None:
        in_specs.append(pl.BlockSpec((None, 2, LANES), lambda s, g: (s, 0, g)))
        args.append(state)
    out_specs = [pl.BlockSpec((t_len, LANES), lambda s, g: (s, g))]
    out_shape = [jax.ShapeDtypeStruct((n_seq * t_len, rw), BF16)]
    if emit_state:
        out_specs.append(pl.BlockSpec((None, 2, LANES), lambda s, g: (s, 0, g)))
        out_shape.append(jax.ShapeDtypeStruct((n_seq, 2, rw), F32))
    kern = functools.partial(_rglru_kernel, t_len=t_len, seg=seg, sub=sub, pitch=pitch, tc=tc,
                             has_state=state is not None, emit_state=emit_state)
    return pl.pallas_call(
        kern,
        grid=(n_seq, nblk),
        in_specs=in_specs,
        out_specs=out_specs,
        out_shape=out_shape,
        scratch_shapes=[pltpu.VMEM((2, SUBLANES * pitch, LANES), F32), pltpu.VMEM((2, SUBLANES * pitch, LANES), F32),
                        pltpu.VMEM((2, SUBLANES, LANES), F32)],
        compiler_params=_cparams(("arbitrary", "arbitrary")),
        name="rglru_state" if emit_state else "rglru",
    )(*args)


def _pack_bf16_pair(lo, hi):
    def rne(x):
        b = pltpu.bitcast(x, U32)
        return (b + jnp.uint32(0x7FFF) + ((b >> 16) & jnp.uint32(1))) >> 16
    return rne(lo) | (rne(hi) << 16)


def _unpack_bf16_pair(w):
    lo = pltpu.bitcast(w << 16, F32).astype(BF16)
    hi = pltpu.bitcast(w & jnp.uint32(0xFFFF0000), F32).astype(BF16)
    return lo, hi


def _outproj_kernel(yac_ref, yal_ref, ybc_ref, ybl_ref, xp_ref, xs_ref, mod_ref, n2_ref, wo_ref, wr_ref, br_ref,
                    x1_ref, hp_ref, ridx_ref, rwt_ref,
                    *, nctx_tiles, ntok_tiles, tiles_per_lat, d, mw, n_groups, epg):
    i = pl.program_id(0)

    @pl.when(i == ntok_tiles)
    def _():
        x1_ref[...] = jnp.zeros_like(x1_ref)
        hp_ref[...] = jnp.zeros_like(hp_ref)
        ridx_ref[...] = jnp.zeros_like(ridx_ref)
        rwt_ref[...] = jnp.zeros_like(rwt_ref)

    @pl.when(i < ntok_tiles)
    def _():
        is_ctx = i < nctx_tiles
        x = jnp.where(is_ctx, xp_ref[...], xs_ref[...])
        ya = jnp.where(is_ctx, yac_ref[...], yal_ref[...])
        yb = jnp.where(is_ctx, ybc_ref[...], ybl_ref[...])
        row = _mod_row(i, nctx_tiles, tiles_per_lat)
        gate1 = mod_ref[pl.ds(row, 1), 2 * d:3 * d]
        shift2 = mod_ref[pl.ds(row, 1), 3 * d:4 * d]
        scale2 = mod_ref[pl.ds(row, 1), 4 * d:5 * d]
        y = (jnp.dot(ya, wo_ref[0:mw, :], preferred_element_type=F32)
             + jnp.dot(yb, wo_ref[mw:, :], preferred_element_type=F32))
        x1 = x + gate1 * y
        x1_ref[...] = x1
        h2 = _modulated_norm(x1, n2_ref[...], shift2, scale2)
        half = d // 2
        hp_ref[...] = _pack_bf16_pair(h2[:, :half], h2[:, half:])

        lt = lax.dot_general(wr_ref[...], h2.astype(BF16), (((1,), (1,)), ((), ())),
                             preferred_element_type=F32) + br_ref[:, 0:1]
        gidx = lax.broadcasted_iota(I32, (SUBLANES, lt.shape[1]), 0)
        gl = jnp.where(gidx < n_groups, lt[0:SUBLANES], -jnp.inf)
        gmax = jnp.max(gl, axis=0, keepdims=True)
        grp = jnp.min(jnp.where(gl == gmax, gidx, n_groups), axis=0, keepdims=True)
        p_grp = 1.0 / jnp.sum(jnp.exp(gl - gmax), axis=0, keepdims=True)
        el = lt[SUBLANES:SUBLANES + epg]
        for g in range(1, n_groups):
            el = jnp.where(grp == g, lt[SUBLANES + g * epg:SUBLANES + (g + 1) * epg], el)
        eidx = lax.broadcasted_iota(I32, el.shape, 0)
        v1 = jnp.max(el, axis=0, keepdims=True)
        i1 = jnp.min(jnp.where(el == v1, eidx, epg), axis=0, keepdims=True)
        el2 = jnp.where(eidx == i1, -jnp.inf, el)
        v2 = jnp.max(el2, axis=0, keepdims=True)
        i2 = jnp.min(jnp.where(el2 == v2, eidx, epg), axis=0, keepdims=True)
        e2 = jnp.exp(v2 - v1)
        w1 = p_grp / (1.0 + e2)
        w2 = p_grp * e2 / (1.0 + e2)
        rid = lax.broadcasted_iota(I32, ridx_ref.shape, 0)
        ridx_ref[...] = jnp.where(rid == 0, grp * epg + i1, jnp.where(rid == 1, grp * epg + i2, 0))
        rwt_ref[...] = jnp.where(rid == 0, w1, jnp.where(rid == 1, w2, 0.0))


def _out_proj(ya_c, ya_l, yb_c, yb_l, xp, xs, mod, norm2_w, w_out, wr, br, *, lat_seq, n_groups, epg):
    nc, d = xp.shape
    nl = xs.shape[0]
    nt = nc + nl
    mw = ya_c.shape[1]
    tm = 256
    nctx_tiles, ntok_tiles = nc // tm, nt // tm
    kern = functools.partial(_outproj_kernel, nctx_tiles=nctx_tiles, ntok_tiles=ntok_tiles,
                             tiles_per_lat=lat_seq // tm, d=d, mw=mw, n_groups=n_groups, epg=epg)
    ctx = lambda i: (jnp.minimum(i, nctx_tiles - 1), 0)
    lat = lambda i: (jnp.clip(i - nctx_tiles, 0, nl // tm - 1), 0)
    row = lambda i: (i, 0)
    const = lambda i: (0, 0)
    return pl.pallas_call(
        kern,
        grid=(ntok_tiles + 1,),
        in_specs=[pl.BlockSpec((tm, mw), ctx), pl.BlockSpec((tm, mw), lat),
                  pl.BlockSpec((tm, yb_c.shape[1]), ctx), pl.BlockSpec((tm, yb_c.shape[1]), lat),
                  pl.BlockSpec((tm, d), ctx), pl.BlockSpec((tm, d), lat),
                  pl.BlockSpec(mod.shape, const),
                  pl.BlockSpec((1, d), const),
                  pl.BlockSpec(w_out.shape, const, pipeline_mode=pl.Buffered(1)),
                  pl.BlockSpec(wr.shape, const),
                  pl.BlockSpec(br.shape, const)],
        out_specs=[pl.BlockSpec((tm, d), row), pl.BlockSpec((tm, d // 2), row),
                   pl.BlockSpec((SUBLANES, tm), lambda i: (0, i)), pl.BlockSpec((SUBLANES, tm), lambda i: (0, i))],
        out_shape=[jax.ShapeDtypeStruct((nt + tm, d), F32), jax.ShapeDtypeStruct((nt + tm, d // 2), U32),
                   jax.ShapeDtypeStruct((SUBLANES, nt + tm), I32), jax.ShapeDtypeStruct((SUBLANES, nt + tm), F32)],
        compiler_params=_cparams(("arbitrary",)),
        name="out_proj",
    )(ya_c, ya_l, yb_c, yb_l, xp, xs, mod, norm2_w.reshape(1, d), w_out, wr, br)


MOE_ROWS = 256


def _gather_kernel(nused_ref, tok_ref, src_ref, dst_ref, sem, *, zero_row):
    b = pl.program_id(0)
    nb = pl.num_programs(0)
    rows = tok_ref.shape[-1]

    def block_copy(blk, slot):
        return pltpu.make_async_copy(src_ref.at[pl.ds(zero_row, rows)], dst_ref.at[pl.ds(blk * rows, rows)],
                                     sem.at[slot])

    @pl.when(b < nused_ref[0])
    def _():
        def body(r, c):
            pltpu.make_async_copy(src_ref.at[tok_ref[0, r]], dst_ref.at[b * rows + r], sem.at[b % 2]).start()
            return c
        lax.fori_loop(0, rows, body, 0)

    @pl.when(b >= nused_ref[0])
    def _():
        block_copy(b, b % 2).start()

    @pl.when(b >= 1)
    def _():
        block_copy(b - 1, (b - 1) % 2).wait()

    @pl.when(b == nb - 1)
    def _():
        block_copy(b, b % 2).wait()


def _gather_rows(nused, row_tok, src, n_blocks, zero_row):
    width = src.shape[1]
    return pl.pallas_call(
        functools.partial(_gather_kernel, zero_row=zero_row),
        grid_spec=pltpu.PrefetchScalarGridSpec(
            num_scalar_prefetch=1,
            grid=(n_blocks,),
            in_specs=[pl.BlockSpec((None, 1, MOE_ROWS), lambda b, n: (b, 0, 0), memory_space=pltpu.SMEM),
                      pl.BlockSpec(memory_space=pl.ANY)],
            out_specs=pl.BlockSpec(memory_space=pl.ANY),
            scratch_shapes=[pltpu.SemaphoreType.DMA((2,))]),
        out_shape=jax.ShapeDtypeStruct((n_blocks * MOE_ROWS, width), src.dtype),
        compiler_params=pltpu.CompilerParams(dimension_semantics=("arbitrary",)),
        name="moe_gather",
    )(nused, row_tok.reshape(n_blocks, 1, MOE_ROWS), src)


def _expert_kernel(blk_e_ref, nused_ref, x_ref, wg_ref, wu_ref, wd_ref, y_ref, *, half):
    @pl.when(pl.program_id(0) >= nused_ref[0])
    def _():
        y_ref[...] = jnp.zeros_like(y_ref)

    @pl.when(pl.program_id(0) < nused_ref[0])
    def _():
        lo, hi = _unpack_bf16_pair(x_ref[...])

        def up(w_ref):
            return (jnp.dot(lo, w_ref[0:half, :], preferred_element_type=F32)
                    + jnp.dot(hi, w_ref[half:, :], preferred_element_type=F32))

        g = up(wg_ref)
        h = ((g * _sigmoid(g)) * up(wu_ref)).astype(BF16)
        y_ref[...] = jnp.dot(h, wd_ref[...], preferred_element_type=F32)


def _experts(blk_e, nused, xs, wg, wu, wd, n_blocks):
    n_exp, d, ff = wg.shape
    blk = lambda b, be, nu: (jnp.minimum(b, nu[0] - 1), 0)
    wmap = lambda b, be, nu: (be[jnp.minimum(b, nu[0] - 1)], 0, 0)
    return pl.pallas_call(
        functools.partial(_expert_kernel, half=d // 2),
        grid_spec=pltpu.PrefetchScalarGridSpec(
            num_scalar_prefetch=2,
            grid=(n_blocks,),
            in_specs=[pl.BlockSpec((MOE_ROWS, d // 2), blk),
                      pl.BlockSpec((None, d, ff), wmap), pl.BlockSpec((None, d, ff), wmap),
                      pl.BlockSpec((None, ff, d), wmap)],
            out_specs=pl.BlockSpec((MOE_ROWS, d), lambda b, be, nu: (b, 0))),
        out_shape=jax.ShapeDtypeStruct((n_blocks * MOE_ROWS, d), F32),
        compiler_params=_cparams(("arbitrary",)),
        name="experts",
    )(blk_e, nused, xs, wg, wu, wd)


def _combine_kernel(dest_ref, x1_ref, wt_ref, mod_ref, fw_ref, y_ref, o_ref, ybuf, sem, *, tile0, tiles_per_seq, d, lat):
    i = pl.program_id(0)
    tm = x1_ref.shape[0]

    def copy(kk, r):
        return pltpu.make_async_copy(y_ref.at[dest_ref[kk, r]], ybuf.at[kk, r], sem.at[0])

    def start(r, c):
        copy(0, r).start()
        copy(1, r).start()
        return c

    def wait(r, c):
        copy(0, r).wait()
        copy(1, r).wait()
        return c

    lax.fori_loop(0, tm, start, 0)
    row = (1 + i // tiles_per_seq) if lat else 0
    gate2 = mod_ref[pl.ds(row, 1), 5 * d:6 * d]
    wt = wt_ref[...]
    w0 = _lane_tile(_row_to_col(wt[0:1], tm), d // LANES)
    w1 = _lane_tile(_row_to_col(wt[1:2], tm), d // LANES)
    lax.fori_loop(0, tm, wait, 0)
    x = x1_ref[...] + gate2 * (w0 * ybuf[0] + w1 * ybuf[1])
    ms = jnp.mean(x * x, axis=-1, keepdims=True)
    o_ref[...] = x * lax.rsqrt(ms + EPS) * fw_ref[...]


def _combine(dest_tiles, x1, rwt, mod, final_w, y_rows, *, row0, n_tok, seq_len, lat):
    d = x1.shape[1]
    tm = 256
    tile0 = row0 // tm
    kern = functools.partial(_combine_kernel, tile0=tile0, tiles_per_seq=seq_len // tm, d=d, lat=lat)
    return pl.pallas_call(
        kern,
        grid=(n_tok // tm,),
        in_specs=[pl.BlockSpec((None, 2, tm), lambda i: (tile0 + i, 0, 0), memory_space=pltpu.SMEM),
                  pl.BlockSpec((tm, d), lambda i: (tile0 + i, 0)),
                  pl.BlockSpec((SUBLANES, tm), lambda i: (0, tile0 + i)),
                  pl.BlockSpec(mod.shape, lambda i: (0, 0)),
                  pl.BlockSpec((1, d), lambda i: (0, 0)),
                  pl.BlockSpec(memory_space=pl.ANY)],
        out_specs=pl.BlockSpec((tm, d), lambda i: (i, 0)),
        out_shape=jax.ShapeDtypeStruct((n_tok, d), F32),
        scratch_shapes=[pltpu.VMEM((2, tm, d), F32), pltpu.SemaphoreType.DMA((1,))],
        compiler_params=_cparams(("arbitrary",)),
        name="combine_lat" if lat else "combine_ctx",
    )(dest_tiles, x1, rwt, mod, final_w.reshape(1, d), y_rows)


def _routing(ridx, n_tok, n_exp, n_blocks, zero_row):
    flat_e = ridx[0:TOP_K, :n_tok].reshape(-1)
    onehot = (flat_e[:, None] == jnp.arange(n_exp, dtype=I32)[None, :]).astype(I32)
    csum = jnp.cumsum(onehot, axis=0)
    rank = jnp.take_along_axis(csum, flat_e[:, None], axis=1)[:, 0] - 1
    counts = csum[-1]
    padded = (counts + MOE_ROWS - 1) // MOE_ROWS * MOE_ROWS
    pad_end = jnp.cumsum(padded)
    pad_start = pad_end - padded
    dest = (pad_start[flat_e] + rank).astype(I32)
    tok = jnp.tile(jnp.arange(n_tok, dtype=I32), TOP_K)
    row_tok = jnp.full((n_blocks * MOE_ROWS,), zero_row, I32).at[dest].set(tok)
    blk_e = jnp.minimum(jnp.searchsorted(pad_end, jnp.arange(n_blocks, dtype=I32) * MOE_ROWS, side="right"),
                        n_exp - 1).astype(I32)
    nused = (pad_end[-1:] // MOE_ROWS).astype(I32)
    return dest.reshape(TOP_K, n_tok), row_tok, blk_e, nused


def _gate_layout(w_gates, b_gates, heads):
    d = w_gates.shape[0]
    w = w_gates.reshape(d, 4, heads).transpose(0, 2, 1)
    w = jnp.pad(w, ((0, 0), (0, 0), (0, SUBLANES - 4))).reshape(d, heads * SUBLANES)
    b = b_gates.reshape(4, heads).T
    b = jnp.pad(b, ((0, 0), (0, SUBLANES - 4))).reshape(1, heads * SUBLANES)
    padl = LANES - heads * SUBLANES
    return jnp.pad(w, ((0, 0), (0, padl))), jnp.pad(b, ((0, 0), (0, padl)))


def kernel(x_prompt, x_sample, state_mlstm_c, state_mlstm_n, state_mlstm_m, state_rglru_h, c, c_ctx, w_ada, b_ada,
           norm1_w, w_in, b_gates, conv_w, conv_b, rg_wa, rg_ba, rg_wx, rg_bx, rg_lambda, mlstm_norm_w, w_out,
           norm2_w, router_group_w, router_group_b, router_expert_w, router_expert_b, expert_w_gate, expert_w_up,
           expert_w_down, final_norm_w):
    n_req, seq, d = x_prompt.shape
    n_lat, lat_seq, _ = x_sample.shape
    depth = w_in.shape[0]
    assert depth == 1, "the token-axis plumbing below is written for the single-layer trunk"
    heads, dk, dv = state_mlstm_c.shape[3:]
    rw = state_rglru_h.shape[-1]
    nblk = rg_wa.shape[2]
    assert rw // nblk == LANES
    n_groups, epg = router_expert_w.shape[1], router_expert_w.shape[3]
    n_exp = n_groups * epg
    qk, mw = heads * dk, heads * dv
    nc, nl = n_req * seq, n_lat * lat_seq
    nt = nc + nl
    assert nc % lat_seq == 0 and n_lat + 1 <= SUBLANES
    l = 0

    xp = x_prompt.reshape(nc, d)
    xs = x_sample.reshape(nl, d)
    cvec = jnp.zeros((SUBLANES, d), F32).at[0].set(c_ctx).at[1:1 + n_lat].set(c)
    mod = _ada(cvec, w_ada[l], b_ada[l])

    w = w_in[l]
    g0 = 2 * qk + 2 * mw
    wgate, bgate = _gate_layout(w[:, g0:g0 + 4 * heads], b_gates[l], heads)
    w_cat = jnp.concatenate([w[:, :g0], wgate, w[:, g0 + 4 * heads:]], axis=1).astype(BF16)
    q, k, v, o, gt, xr, xg = _in_proj(xp, xs, mod, norm1_w[l], w_cat, bgate, lat_seq=lat_seq, heads=heads, dk=dk,
                                      dv=dv, rw=rw)

    mkw = dict(heads=heads, dk=dk, dv=dv)
    ya_c, new_c, new_n, new_m = _mlstm(q, k, v, o, gt, mlstm_norm_w[l], row0=0, n_seq=n_req, t_len=seq,
                                       emit_state=True, **mkw)
    (ya_l,) = _mlstm(q, k, v, o, gt, mlstm_norm_w[l], row0=nc, n_seq=n_lat, t_len=lat_seq,
                     state=(state_mlstm_c[:, l], state_mlstm_n[:, l], state_mlstm_m[:, l]), **mkw)

    wg = jnp.concatenate([rg_wa[l, 0], rg_wx[l, 0], rg_wa[l, 1], rg_wx[l, 1]], axis=-1).astype(BF16)
    bg = jnp.concatenate([rg_ba[l, 0].reshape(nblk, 1, LANES), rg_bx[l, 0].reshape(nblk, 1, LANES),
                          rg_ba[l, 1].reshape(nblk, 1, LANES), rg_bx[l, 1].reshape(nblk, 1, LANES)], axis=-1)
    rargs = (xr, xg, conv_w[l], conv_b[l], wg, bg, rg_lambda[l])
    yb_c, new_h = _rglru(*rargs, row0=0, n_seq=n_req, t_len=seq, seg=seq, emit_state=True)
    (yb_l,) = _rglru(*rargs, row0=nc, n_seq=n_lat, t_len=lat_seq, seg=GRID_W, state=state_rglru_h[:, l])

    r_rows = -(-(SUBLANES + n_exp) // 16) * 16
    wr = jnp.zeros((r_rows, d), F32)
    wr = wr.at[0:n_groups].set(router_group_w[l].T)
    wr = wr.at[SUBLANES:SUBLANES + n_exp].set(router_expert_w[l].transpose(0, 2, 1).reshape(n_exp, d)).astype(BF16)
    br = jnp.zeros((r_rows, LANES), F32)
    br = br.at[0:n_groups, 0].set(router_group_b[l])
    br = br.at[SUBLANES:SUBLANES + n_exp, 0].set(router_expert_b[l].reshape(n_exp))
    x1, hp, ridx, rwt = _out_proj(ya_c, ya_l, yb_c, yb_l, xp, xs, mod, norm2_w[l], w_out[l].astype(BF16), wr, br,
                                  lat_seq=lat_seq, n_groups=n_groups, epg=epg)

    n_blocks = (nt * TOP_K) // MOE_ROWS + n_exp
    dest, row_tok, blk_e, nused = _routing(ridx, nt, n_exp, n_blocks, zero_row=nt)
    xsorted = _gather_rows(nused, row_tok, hp, n_blocks, zero_row=nt)
    y_rows = _experts(blk_e, nused, xsorted, expert_w_gate[l].astype(BF16), expert_w_up[l].astype(BF16),
                      expert_w_down[l].astype(BF16), n_blocks)

    tm = 256
    dest_tiles = dest.reshape(TOP_K, nt // tm, tm).transpose(1, 0, 2)
    y_prompt = _combine(dest_tiles, x1, rwt, mod, final_norm_w, y_rows, row0=0, n_tok=nc, seq_len=seq, lat=False)
    y_sample = _combine(dest_tiles, x1, rwt, mod, final_norm_w, y_rows, row0=nc, n_tok=nl, seq_len=lat_seq, lat=True)

    return (y_prompt.reshape(n_req, seq, d), y_sample.reshape(n_lat, lat_seq, d),
            new_c[:, None], new_n.reshape(n_req, 1, 2, heads, dk), new_m.reshape(n_req, 1, 2, heads),
            new_h[:, None])
```

```python
import functools

import jax
import jax.numpy as jnp
import numpy as np
from jax import lax
from jax.experimental import pallas as pl
from jax.experimental.pallas import tpu as pltpu

F32 = jnp.float32
BF16 = jnp.bfloat16
I32 = jnp.int32
U32 = jnp.uint32

EPS = 1e-6
GRID_W = 64
CONV_LEFT = 2
RGLRU_C = 8.0
TOP_K = 2
LANES = 128
SUBLANES = 8
MLSTM_L = 128
NEG = -1e30
VMEM_LIMIT = 56 * 1024 * 1024

_HIGHEST = lax.Precision.HIGHEST


def _cparams(sem, vmem=VMEM_LIMIT):
    return pltpu.CompilerParams(dimension_semantics=sem, vmem_limit_bytes=vmem)


def _sigmoid(x):
    return 1.0 / (1.0 + jnp.exp(-x))


def _row_to_col(r, n):
    return jnp.broadcast_to(r, (LANES, n)).T


def _lane_tile(x, reps):
    return x if reps == 1 else jnp.concatenate([x] * reps, axis=1)


def _ada_kernel(c_ref, w_ref, b_ref, o_ref):
    c = c_ref[...]
    s = (c * _sigmoid(c)).astype(BF16)
    o_ref[...] = jnp.dot(s, w_ref[...].astype(BF16), preferred_element_type=F32) + b_ref[...]


def _ada(cvec, w_ada, b_ada):
    d, n = w_ada.shape
    tn = 1024 if n % 1024 == 0 else 512
    assert n % tn == 0
    return pl.pallas_call(
        _ada_kernel,
        grid=(n // tn,),
        in_specs=[pl.BlockSpec((SUBLANES, d), lambda j: (0, 0)),
                  pl.BlockSpec((d, tn), lambda j: (0, j)),
                  pl.BlockSpec((1, tn), lambda j: (0, j))],
        out_specs=pl.BlockSpec((SUBLANES, tn), lambda j: (0, j)),
        out_shape=jax.ShapeDtypeStruct((SUBLANES, n), F32),
        compiler_params=_cparams(("arbitrary",)),
        name="ada",
    )(cvec, w_ada, b_ada.reshape(1, n))


def _modulated_norm(x, w, shift, scale):
    ms = jnp.mean(x * x, axis=-1, keepdims=True)
    return (x * lax.rsqrt(ms + EPS) * w) * (1.0 + scale) + shift


def _mod_row(i, nctx_tiles, tiles_per_lat):
    return jnp.where(i < nctx_tiles, 0, 1 + (i - nctx_tiles) // tiles_per_lat)


def _inproj_kernel(xp_ref, xs_ref, mod_ref, n1_ref, w_ref, bg_ref,
                   q_ref, k_ref, v_ref, o_ref, gt_ref, xr_ref, xg_ref,
                   *, nctx_tiles, tiles_per_lat, d, qk, mw, rw, gh, qscale):
    i = pl.program_id(0)
    x = jnp.where(i < nctx_tiles, xp_ref[...], xs_ref[...])
    row = _mod_row(i, nctx_tiles, tiles_per_lat)
    shift = mod_ref[pl.ds(row, 1), 0:d]
    scale = mod_ref[pl.ds(row, 1), d:2 * d]
    hb = _modulated_norm(x, n1_ref[...], shift, scale).astype(BF16)

    def proj(c0, width):
        return jnp.dot(hb, w_ref[:, c0:c0 + width], preferred_element_type=F32)

    c0 = 0
    q_ref[...] = (proj(c0, qk) * qscale).astype(BF16); c0 += qk
    k_ref[...] = proj(c0, qk).astype(BF16); c0 += qk
    v_ref[...] = proj(c0, mw).astype(BF16); c0 += mw
    o_ref[...] = proj(c0, mw); c0 += mw
    zg = proj(c0, LANES) + bg_ref[...]; c0 += LANES
    lane = lax.broadcasted_iota(I32, zg.shape, 1)
    log_sig = jnp.minimum(zg, 0.0) - jnp.log1p(jnp.exp(-jnp.abs(zg)))
    zg = jnp.where(lane % 2 == 1, log_sig, zg)
    gt_ref[...] = zg.T[0:gh, :]
    xr_ref[...] = proj(c0, rw); c0 += rw
    xg_ref[...] = proj(c0, rw)


def _in_proj(xp, xs, mod, norm1_w, w_cat, bg, *, lat_seq, heads, dk, dv, rw):
    nc, d = xp.shape
    nl = xs.shape[0]
    nt = nc + nl
    tm = 256
    qk, mw, gh = heads * dk, heads * dv, heads * SUBLANES
    nctx_tiles = nc // tm
    kern = functools.partial(_inproj_kernel, nctx_tiles=nctx_tiles, tiles_per_lat=lat_seq // tm, d=d, qk=qk, mw=mw,
                             rw=rw, gh=gh, qscale=dk ** -0.5)
    row = lambda i: (i, 0)
    const = lambda i: (0, 0)
    return pl.pallas_call(
        kern,
        grid=(nt // tm,),
        in_specs=[pl.BlockSpec((tm, d), lambda i: (jnp.minimum(i, nctx_tiles - 1), 0)),
                  pl.BlockSpec((tm, d), lambda i: (jnp.maximum(i - nctx_tiles, 0), 0)),
                  pl.BlockSpec(mod.shape, const),
                  pl.BlockSpec((1, d), const),
                  pl.BlockSpec(w_cat.shape, const, pipeline_mode=pl.Buffered(1)),
                  pl.BlockSpec((1, LANES), const)],
        out_specs=[pl.BlockSpec((tm, qk), row), pl.BlockSpec((tm, qk), row), pl.BlockSpec((tm, mw), row),
                   pl.BlockSpec((tm, mw), row), pl.BlockSpec((gh, tm), lambda i: (0, i)),
                   pl.BlockSpec((tm, rw), row), pl.BlockSpec((tm, rw), row)],
        out_shape=[jax.ShapeDtypeStruct((nt, qk), BF16), jax.ShapeDtypeStruct((nt, qk), BF16),
                   jax.ShapeDtypeStruct((nt, mw), BF16), jax.ShapeDtypeStruct((nt, mw), F32),
                   jax.ShapeDtypeStruct((gh, nt), F32),
                   jax.ShapeDtypeStruct((nt, rw), F32), jax.ShapeDtypeStruct((nt, rw), F32)],
        compiler_params=_cparams(("arbitrary",)),
        name="in_proj",
    )(xp, xs, mod, norm1_w.reshape(1, d), w_cat, bg)


def _mlstm_kernel(*refs, t_len, dk, dv, has_state, emit_state):
    it = iter(refs)
    q_ref, k_ref, v_ref, o_ref, gt_ref, nw_ref, tri_ref = (next(it) for _ in range(7))
    if has_state:
        c0_ref, n0_ref, m0_ref = (next(it) for _ in range(3))
    ya_ref = next(it)
    if emit_state:
        cn_ref, nn_ref, mn_ref = (next(it) for _ in range(3))
    hf_scr, hb_scr, c_scr, n_scr, m_scr = (next(it) for _ in range(5))
    ln = MLSTM_L
    nchunks = t_len // ln
    h_scr = (hf_scr, hb_scr)

    if has_state:
        c_scr[...] = c0_ref[...]
        n_scr[...] = n0_ref[...]
        m_scr[...] = m0_ref[...]
    else:
        c_scr[...] = jnp.zeros_like(c_scr)
        n_scr[...] = jnp.zeros_like(n_scr)
        m_scr[...] = jnp.zeros_like(m_scr)

    def chunk(d, t0):
        q = q_ref[pl.ds(t0, ln), :]
        k = k_ref[pl.ds(t0, ln), :]
        v = v_ref[pl.ds(t0, ln), :]
        g8 = gt_ref[:, pl.ds(t0, ln)]
        cum8 = jnp.dot(g8, tri_ref[d], precision=_HIGHEST, preferred_element_type=F32)
        valid = tri_ref[1 - d] > 0.5
        li = g8[2 * d:2 * d + 1]
        lf = g8[2 * d + 1:2 * d + 2]
        cum_row = cum8[2 * d + 1:2 * d + 2]
        total = jnp.sum(lf, axis=1, keepdims=True)
        a_row = li - cum_row
        cum_col = _row_to_col(cum_row, ln)
        a_col = _row_to_col(a_row, ln)
        m_prev = m_scr[d]
        c_st = c_scr[d]
        n_st = n_scr[d]

        dmat = jnp.where(valid, _lane_tile(cum_col, ln // LANES) + a_row, NEG)
        inter = cum_col + m_prev
        m_t = jnp.maximum(inter, jnp.max(dmat, axis=1, keepdims=True))
        s = lax.dot_general(q, k, (((1,), (1,)), ((), ())), preferred_element_type=F32)
        p = s * jnp.exp(dmat - _lane_tile(m_t, ln // LANES))
        w_inter = jnp.exp(inter - m_t)
        num = (jnp.dot(p.astype(BF16), v, preferred_element_type=F32)
               + _lane_tile(w_inter, dv // LANES) * jnp.dot(q, c_st.astype(BF16), preferred_element_type=F32))
        qn = jnp.sum(q.astype(F32) * n_st, axis=1, keepdims=True)
        den = jnp.sum(p, axis=1, keepdims=True) + w_inter * qn
        inv = 1.0 / jnp.maximum(jnp.abs(den), jnp.exp(-m_t))
        h_scr[d][pl.ds(t0, ln), :] = num * _lane_tile(inv, dv // LANES)

        g_row = total + a_row
        m_new = jnp.maximum(total + m_prev, jnp.max(g_row, axis=1, keepdims=True))
        wk = jnp.exp(total + a_col - m_new)
        decay = jnp.exp(total + m_prev - m_new)
        kw = k.astype(F32) * _lane_tile(wk, dk // LANES)
        c_scr[d] = decay * c_st + lax.dot_general(kw.astype(BF16), v, (((0,), (0,)), ((), ())),
                                                  preferred_element_type=F32)
        n_scr[d] = decay * n_st + jnp.sum(kw, axis=0, keepdims=True)
        m_scr[d] = m_new

    def step(j, carry):
        chunk(0, pl.multiple_of(j * ln, ln))
        chunk(1, pl.multiple_of((nchunks - 1 - j) * ln, ln))
        return carry

    lax.fori_loop(0, nchunks, step, 0)

    def finish(j, carry):
        t0 = pl.multiple_of(j * ln, ln)
        hs = hf_scr[pl.ds(t0, ln), :] + hb_scr[pl.ds(t0, ln), :]
        ms = jnp.mean(hs * hs, axis=1, keepdims=True)
        y = hs * lax.rsqrt(ms + EPS) * nw_ref[...]
        ya_ref[pl.ds(t0, ln), :] = (_sigmoid(o_ref[pl.ds(t0, ln), :]) * y).astype(BF16)
        return carry

    lax.fori_loop(0, nchunks, finish, 0)

    if emit_state:
        cn_ref[...] = c_scr[...]
        nn_ref[...] = n_scr[...]
        mn_ref[...] = m_scr[...]


def _mlstm_tri():
    r = np.arange(MLSTM_L)
    fwd = (r[:, None] <= r[None, :]).astype(np.float32)
    return jnp.asarray(np.stack([fwd, fwd.T]))


def _mlstm(q, k, v, o, gt, norm_w, *, row0, n_seq, t_len, heads, dk, dv, state=None, emit_state=False):
    blk0 = row0 // t_len
    tok = lambda s, h: (blk0 + s, h)
    in_specs = [pl.BlockSpec((t_len, dk), tok), pl.BlockSpec((t_len, dk), tok), pl.BlockSpec((t_len, dv), tok),
                pl.BlockSpec((t_len, dv), tok),
                pl.BlockSpec((SUBLANES, t_len), lambda s, h: (h, blk0 + s)),
                pl.BlockSpec((None, 1, dv), lambda s, h: (h, 0, 0)),
                pl.BlockSpec((2, MLSTM_L, MLSTM_L), lambda s, h: (0, 0, 0))]
    args = [q, k, v, o, gt, norm_w.reshape(heads, 1, dv), _mlstm_tri()]
    if state is not None:
        c0, n0, m0 = state
        in_specs += [pl.BlockSpec((None, 2, None, dk, dv), lambda s, h: (s, 0, h, 0, 0)),
                     pl.BlockSpec((None, 2, None, 1, dk), lambda s, h: (s, 0, h, 0, 0)),
                     pl.BlockSpec((None, 2, None, 1, 1), lambda s, h: (s, 0, h, 0, 0))]
        args += [c0, n0.reshape(n_seq, 2, heads, 1, dk), m0.reshape(n_seq, 2, heads, 1, 1)]
    out_specs = [pl.BlockSpec((t_len, dv), lambda s, h: (s, h))]
    out_shape = [jax.ShapeDtypeStruct((n_seq * t_len, heads * dv), BF16)]
    if emit_state:
        out_specs += [pl.BlockSpec((None, 2, None, dk, dv), lambda s, h: (s, 0, h, 0, 0)),
                      pl.BlockSpec((None, 2, None, 1, dk), lambda s, h: (s, 0, h, 0, 0)),
                      pl.BlockSpec((None, 2, None, 1, 1), lambda s, h: (s, 0, h, 0, 0))]
        out_shape += [jax.ShapeDtypeStruct((n_seq, 2, heads, dk, dv), F32),
                      jax.ShapeDtypeStruct((n_seq, 2, heads, 1, dk), F32),
                      jax.ShapeDtypeStruct((n_seq, 2, heads, 1, 1), F32)]
    kern = functools.partial(_mlstm_kernel, t_len=t_len, dk=dk, dv=dv, has_state=state is not None,
                             emit_state=emit_state)
    return pl.pallas_call(
        kern,
        grid=(n_seq, heads),
        in_specs=in_specs,
        out_specs=out_specs,
        out_shape=out_shape,
        scratch_shapes=[pltpu.VMEM((t_len, dv), F32), pltpu.VMEM((t_len, dv), F32),
                        pltpu.VMEM((2, dk, dv), F32), pltpu.VMEM((2, 1, dk), F32), pltpu.VMEM((2, 1, 1), F32)],
        compiler_params=_cparams(("arbitrary", "arbitrary")),
        name="mlstm_state" if emit_state else "mlstm",
    )(*args)


def _gelu_tanh(x):
    return x * (0.5 * (1.0 + jnp.tanh(0.7978845608028654 * (x + 0.044715 * (x * x * x)))))


def _softplus(x):
    return jnp.maximum(x, 0.0) + jnp.log1p(jnp.exp(-jnp.abs(x)))


def _rglru_kernel(*refs, t_len, seg, sub, pitch, tc, has_state, emit_state):
    it = iter(refs)
    xr_ref, xg_ref, cw_ref, cb_ref, wg_ref, bg_ref, lam_ref = (next(it) for _ in range(7))
    if has_state:
        h0_ref = next(it)
    yb_ref = next(it)
    if emit_state:
        hn_ref = next(it)
    a_scr, u_scr, cin_scr = (next(it) for _ in range(3))
    nchunks = t_len // tc
    piece = min(tc, sub)
    npieces = tc // piece

    def scan_rows(t0, p):
        t = t0 + p * piece
        i = t // sub
        return pl.ds(pl.multiple_of(i * pitch + (t - i * sub), SUBLANES), piece), i

    sp = _softplus(-lam_ref[...])

    def gates(c, carry):
        t0 = pl.multiple_of(c * tc, tc)
        x = xr_ref[pl.ds(t0, tc), :]
        pos = lax.broadcasted_iota(I32, x.shape, 0) % seg
        xc = cb_ref[...] + cw_ref[CONV_LEFT:CONV_LEFT + 1, :] * x
        for j in range(cw_ref.shape[0]):
            off = j - CONV_LEFT
            if off == 0:
                continue
            shifted = pltpu.roll(x, (-off) % tc, 0)
            ok = (pos >= -off) if off < 0 else (pos < seg - off)
            xc = xc + cw_ref[j:j + 1, :] * jnp.where(ok, shifted, 0.0)
        r_all = jnp.dot(xc.astype(BF16), wg_ref[...], preferred_element_type=F32) + bg_ref[...]
        for d in range(2):
            r = _sigmoid(r_all[:, (2 * d) * LANES:(2 * d + 1) * LANES])
            ig = _sigmoid(r_all[:, (2 * d + 1) * LANES:(2 * d + 2) * LANES])
            log_a = (-RGLRU_C) * r * sp[d:d + 1]
            a = jnp.exp(log_a)
            th = jnp.tanh(log_a)
            one_minus_a2 = (-2.0) * th / (1.0 - th)
            u = jnp.sqrt(one_minus_a2) * (ig * xc)
            for p in range(npieces):
                rows, _ = scan_rows(t0, p)
                a_scr[d, rows, :] = a[p * piece:(p + 1) * piece]
                u_scr[d, rows, :] = u[p * piece:(p + 1) * piece]
        return carry

    lax.fori_loop(0, nchunks, gates, 0)

    def scan(j, carry):
        hf, pf, hb, pb = carry
        jf = pl.ds(j, SUBLANES, stride=pitch)
        jb = pl.ds(sub - 1 - j, SUBLANES, stride=pitch)
        af = a_scr[0, jf, :]
        hf = af * hf + u_scr[0, jf, :]
        pf = af * pf
        a_scr[0, jf, :] = pf
        u_scr[0, jf, :] = hf
        ab = a_scr[1, jb, :]
        hb = ab * hb + u_scr[1, jb, :]
        pb = ab * pb
        a_scr[1, jb, :] = pb
        u_scr[1, jb, :] = hb
        return hf, pf, hb, pb

    zero = jnp.zeros((SUBLANES, LANES), F32)
    one = jnp.ones((SUBLANES, LANES), F32)
    hf, pf, hb, pb = lax.fori_loop(0, sub, scan, (zero, one, zero, one))

    cf = h0_ref[0:1, :] if has_state else jnp.zeros((1, LANES), F32)
    for i in range(SUBLANES):
        cin_scr[0, i:i + 1, :] = cf
        cf = hf[i:i + 1] + pf[i:i + 1] * cf
    cb = h0_ref[1:2, :] if has_state else jnp.zeros((1, LANES), F32)
    for i in reversed(range(SUBLANES)):
        cin_scr[1, i:i + 1, :] = cb
        cb = hb[i:i + 1] + pb[i:i + 1] * cb
    if emit_state:
        hn_ref[0:1, :] = cf
        hn_ref[1:2, :] = cb

    def finish(c, carry):
        t0 = pl.multiple_of(c * tc, tc)
        for p in range(npieces):
            rows, i = scan_rows(t0, p)
            h = (u_scr[0, rows, :] + a_scr[0, rows, :] * cin_scr[0, pl.ds(i, 1), :]
                 + u_scr[1, rows, :] + a_scr[1, rows, :] * cin_scr[1, pl.ds(i, 1), :])
            nat = pl.ds(pl.multiple_of(t0 + p * piece, SUBLANES), piece)
            yb_ref[nat, :] = (h * _gelu_tanh(xg_ref[nat, :])).astype(BF16)
        return carry

    lax.fori_loop(0, nchunks, finish, 0)


def _rglru(xr, xg, conv_w, conv_b, wg, bg, lam, *, row0, n_seq, t_len, seg, state=None, emit_state=False):
    rw = xr.shape[1]
    nblk = rw // LANES
    blk0 = row0 // t_len
    sub = t_len // SUBLANES
    pitch = sub + SUBLANES
    tc = 256
    tok = lambda s, g: (blk0 + s, g)
    in_specs = [pl.BlockSpec((t_len, LANES), tok), pl.BlockSpec((t_len, LANES), tok),
                pl.BlockSpec((conv_w.shape[0], LANES), lambda s, g: (0, g)),
                pl.BlockSpec((1, LANES), lambda s, g: (0, g)),
                pl.BlockSpec((None, LANES, 4 * LANES), lambda s, g: (g, 0, 0)),
                pl.BlockSpec((None, 1, 4 * LANES), lambda s, g: (g, 0, 0)),
                pl.BlockSpec((2, LANES), lambda s, g: (0, g))]
    args = [xr, xg, conv_w, conv_b.reshape(1, rw), wg, bg, lam]
    if state is not None:
        in_specs.append(pl.BlockSpec((None, 2, LANES), lambda s, g: (s, 0, g)))
        args.append(state)
    out_specs = [pl.BlockSpec((t_len, LANES), lambda s, g: (s, g))]
    out_shape = [jax.ShapeDtypeStruct((n_seq * t_len, rw), BF16)]
    if emit_state:
        out_specs.append(pl.BlockSpec((None, 2, LANES), lambda s, g: (s, 0, g)))
        out_shape.append(jax.ShapeDtypeStruct((n_seq, 2, rw), F32))
    kern = functools.partial(_rglru_kernel, t_len=t_len, seg=seg, sub=sub, pitch=pitch, tc=tc,
                             has_state=state is not None, emit_state=emit_state)
    return pl.pallas_call(
        kern,
        grid=(n_seq, nblk),
        in_specs=in_specs,
        out_specs=out_specs,
        out_shape=out_shape,
        scratch_shapes=[pltpu.VMEM((2, SUBLANES * pitch, LANES), F32), pltpu.VMEM((2, SUBLANES * pitch, LANES), F32),
                        pltpu.VMEM((2, SUBLANES, LANES), F32)],
        compiler_params=_cparams(("arbitrary", "arbitrary")),
        name="rglru_state" if emit_state else "rglru",
    )(*args)


def _pack_bf16_pair(lo, hi):
    def rne(x):
        b = pltpu.bitcast(x, U32)
        return (b + jnp.uint32(0x7FFF) + ((b >> 16) & jnp.uint32(1))) >> 16
    return rne(lo) | (rne(hi) << 16)


def _unpack_bf16_pair(w):
    lo = pltpu.bitcast(w << 16, F32).astype(BF16)
    hi = pltpu.bitcast(w & jnp.uint32(0xFFFF0000), F32).astype(BF16)
    return lo, hi


def _outproj_kernel(yac_ref, yal_ref, ybc_ref, ybl_ref, xp_ref, xs_ref, mod_ref, n2_ref, wo_ref, wr_ref, br_ref,
                    x1_ref, hp_ref, ridx_ref, rwt_ref,
                    *, nctx_tiles, ntok_tiles, tiles_per_lat, d, mw, n_groups, epg):
    i = pl.program_id(0)

    @pl.when(i == ntok_tiles)
    def _():
        x1_ref[...] = jnp.zeros_like(x1_ref)
        hp_ref[...] = jnp.zeros_like(hp_ref)
        ridx_ref[...] = jnp.zeros_like(ridx_ref)
        rwt_ref[...] = jnp.zeros_like(rwt_ref)

    @pl.when(i < ntok_tiles)
    def _():
        is_ctx = i < nctx_tiles
        x = jnp.where(is_ctx, xp_ref[...], xs_ref[...])
        ya = jnp.where(is_ctx, yac_ref[...], yal_ref[...])
        yb = jnp.where(is_ctx, ybc_ref[...], ybl_ref[...])
        row = _mod_row(i, nctx_tiles, tiles_per_lat)
        gate1 = mod_ref[pl.ds(row, 1), 2 * d:3 * d]
        shift2 = mod_ref[pl.ds(row, 1), 3 * d:4 * d]
        scale2 = mod_ref[pl.ds(row, 1), 4 * d:5 * d]
        y = (jnp.dot(ya, wo_ref[0:mw, :], preferred_element_type=F32)
             + jnp.dot(yb, wo_ref[mw:, :], preferred_element_type=F32))
        x1 = x + gate1 * y
        x1_ref[...] = x1
        h2 = _modulated_norm(x1, n2_ref[...], shift2, scale2)
        half = d // 2
        hp_ref[...] = _pack_bf16_pair(h2[:, :half], h2[:, half:])

        lt = lax.dot_general(wr_ref[...], h2.astype(BF16), (((1,), (1,)), ((), ())),
                             preferred_element_type=F32) + br_ref[:, 0:1]
        gidx = lax.broadcasted_iota(I32, (SUBLANES, lt.shape[1]), 0)
        gl = jnp.where(gidx < n_groups, lt[0:SUBLANES], -jnp.inf)
        gmax = jnp.max(gl, axis=0, keepdims=True)
        grp = jnp.min(jnp.where(gl == gmax, gidx, n_groups), axis=0, keepdims=True)
        p_grp = 1.0 / jnp.sum(jnp.exp(gl - gmax), axis=0, keepdims=True)
        el = lt[SUBLANES:SUBLANES + epg]
        for g in range(1, n_groups):
            el = jnp.where(grp == g, lt[SUBLANES + g * epg:SUBLANES + (g + 1) * epg], el)
        eidx = lax.broadcasted_iota(I32, el.shape, 0)
        v1 = jnp.max(el, axis=0, keepdims=True)
        i1 = jnp.min(jnp.where(el == v1, eidx, epg), axis=0, keepdims=True)
        el2 = jnp.where(eidx == i1, -jnp.inf, el)
        v2 = jnp.max(el2, axis=0, keepdims=True)
        i2 = jnp.min(jnp.where(el2 == v2, eidx, epg), axis=0, keepdims=True)
        e2 = jnp.exp(v2 - v1)
        w1 = p_grp / (1.0 + e2)
        w2 = p_grp * e2 / (1.0 + e2)
        rid = lax.broadcasted_iota(I32, ridx_ref.shape, 0)
        ridx_ref[...] = jnp.where(rid == 0, grp * epg + i1, jnp.where(rid == 1, grp * epg + i2, 0))
        rwt_ref[...] = jnp.where(rid == 0, w1, jnp.where(rid == 1, w2, 0.0))


def _out_proj(ya_c, ya_l, yb_c, yb_l, xp, xs, mod, norm2_w, w_out, wr, br, *, lat_seq, n_groups, epg):
    nc, d = xp.shape
    nl = xs.shape[0]
    nt = nc + nl
    mw = ya_c.shape[1]
    tm = 256
    nctx_tiles, ntok_tiles = nc // tm, nt // tm
    kern = functools.partial(_outproj_kernel, nctx_tiles=nctx_tiles, ntok_tiles=ntok_tiles,
                             tiles_per_lat=lat_seq // tm, d=d, mw=mw, n_groups=n_groups, epg=epg)
    ctx = lambda i: (jnp.minimum(i, nctx_tiles - 1), 0)
    lat = lambda i: (jnp.clip(i - nctx_tiles, 0, nl // tm - 1), 0)
    row = lambda i: (i, 0)
    const = lambda i: (0, 0)
    return pl.pallas_call(
        kern,
        grid=(ntok_tiles + 1,),
        in_specs=[pl.BlockSpec((tm, mw), ctx), pl.BlockSpec((tm, mw), lat),
                  pl.BlockSpec((tm, yb_c.shape[1]), ctx), pl.BlockSpec((tm, yb_c.shape[1]), lat),
                  pl.BlockSpec((tm, d), ctx), pl.BlockSpec((tm, d), lat),
                  pl.BlockSpec(mod.shape, const),
                  pl.BlockSpec((1, d), const),
                  pl.BlockSpec(w_out.shape, const, pipeline_mode=pl.Buffered(1)),
                  pl.BlockSpec(wr.shape, const),
                  pl.BlockSpec(br.shape, const)],
        out_specs=[pl.BlockSpec((tm, d), row), pl.BlockSpec((tm, d // 2), row),
                   pl.BlockSpec((SUBLANES, tm), lambda i: (0, i)), pl.BlockSpec((SUBLANES, tm), lambda i: (0, i))],
        out_shape=[jax.ShapeDtypeStruct((nt + tm, d), F32), jax.ShapeDtypeStruct((nt + tm, d // 2), U32),
                   jax.ShapeDtypeStruct((SUBLANES, nt + tm), I32), jax.ShapeDtypeStruct((SUBLANES, nt + tm), F32)],
        compiler_params=_cparams(("arbitrary",)),
        name="out_proj",
    )(ya_c, ya_l, yb_c, yb_l, xp, xs, mod, norm2_w.reshape(1, d), w_out, wr, br)


MOE_ROWS = 256


ROUTE_TILE = 512


def _rank_kernel(ridx_ref, tri_ref, rank_ref, cnt_ref, carry_scr, *, n_exp):
    @pl.when(pl.program_id(0) == 0)
    def _():
        carry_scr[...] = jnp.zeros_like(carry_scr)

    e = ridx_ref[...]
    tr = e.shape[1]
    eid = lax.broadcasted_iota(I32, (n_exp, tr), 0)
    carry = carry_scr[:, 0:1]
    ranks = []
    for kk in range(TOP_K):
        hit = eid == e[kk:kk + 1]
        cum = jnp.dot(jnp.where(hit, 1.0, 0.0).astype(BF16), tri_ref[...], preferred_element_type=F32)
        ranks.append(jnp.sum(jnp.where(hit, cum + carry, 0.0), axis=0, keepdims=True) - 1.0)
        carry = carry + cum[:, tr - 1:tr]
    carry_scr[...] = jnp.broadcast_to(carry, carry_scr.shape)
    cnt_ref[...] = jnp.broadcast_to(carry, cnt_ref.shape)
    rid = lax.broadcasted_iota(I32, rank_ref.shape, 0)
    rank_ref[...] = jnp.where(rid == 0, ranks[0], jnp.where(rid == 1, ranks[1], 0.0)).astype(I32)


def _dest_kernel(ridx_ref, rank_ref, pstart_ref, dest_ref, *, n_exp):
    e = ridx_ref[...]
    tr = e.shape[1]
    eid = lax.broadcasted_iota(I32, (n_exp, tr), 0)
    ps = pstart_ref[:, 0:1]
    rows = [jnp.sum(jnp.where(eid == e[kk:kk + 1], ps, 0.0), axis=0, keepdims=True) for kk in range(TOP_K)]
    rid = lax.broadcasted_iota(I32, dest_ref.shape, 0)
    dest_ref[...] = rank_ref[...] + jnp.where(rid == 0, rows[0], jnp.where(rid == 1, rows[1], 0.0)).astype(I32)


def _invert_kernel(dest_ref, rowtok_ref, *, zero_row):
    i = pl.program_id(0)
    tr = dest_ref.shape[1]

    @pl.when(i == 0)
    def _():
        def fill(r, c):
            rowtok_ref[r] = zero_row
            return c
        lax.fori_loop(0, rowtok_ref.shape[0], fill, 0, unroll=8)

    def body(r, c):
        for kk in range(TOP_K):
            rowtok_ref[dest_ref[kk, r]] = i * tr + r
        return c
    lax.fori_loop(0, tr, body, 0, unroll=8)


def _routing(ridx, n_tok, n_exp, n_blocks, zero_row):
    tr = ROUTE_TILE
    steps = n_tok // tr
    tri = jnp.asarray(np.triu(np.ones((tr, tr), np.float32)), BF16)
    tile = pl.BlockSpec((SUBLANES, tr), lambda i: (0, i))
    cnt_spec = pl.BlockSpec((n_exp, LANES), lambda i: (0, 0))
    rank, cnt = pl.pallas_call(
        functools.partial(_rank_kernel, n_exp=n_exp),
        grid=(steps,),
        in_specs=[tile, pl.BlockSpec((tr, tr), lambda i: (0, 0))],
        out_specs=[tile, cnt_spec],
        out_shape=[jax.ShapeDtypeStruct((SUBLANES, n_tok), I32), jax.ShapeDtypeStruct((n_exp, LANES), F32)],
        scratch_shapes=[pltpu.VMEM((n_exp, LANES), F32)],
        compiler_params=_cparams(("arbitrary",)),
        name="route_rank",
    )(ridx, tri)
    counts = cnt[:, 0].astype(I32)
    padded = (counts + MOE_ROWS - 1) // MOE_ROWS * MOE_ROWS
    pad_end = jnp.cumsum(padded)
    pad_start = pad_end - padded
    dest = pl.pallas_call(
        functools.partial(_dest_kernel, n_exp=n_exp),
        grid=(steps,),
        in_specs=[tile, tile, cnt_spec],
        out_specs=tile,
        out_shape=jax.ShapeDtypeStruct((SUBLANES, n_tok), I32),
        compiler_params=_cparams(("arbitrary",)),
        name="route_dest",
    )(ridx, rank, jnp.broadcast_to(pad_start.astype(F32)[:, None], (n_exp, LANES)))
    row_tok = pl.pallas_call(
        functools.partial(_invert_kernel, zero_row=zero_row),
        grid=(steps,),
        in_specs=[pl.BlockSpec((SUBLANES, tr), lambda i: (0, i), memory_space=pltpu.SMEM)],
        out_specs=pl.BlockSpec(memory_space=pltpu.SMEM),
        out_shape=jax.ShapeDtypeStruct((n_blocks * MOE_ROWS,), I32),
        compiler_params=_cparams(("arbitrary",)),
        name="route_invert",
    )(dest)
    blk_e = jnp.minimum(jnp.searchsorted(pad_end, jnp.arange(n_blocks, dtype=I32) * MOE_ROWS, side="right"),
                        n_exp - 1).astype(I32)
    nused = (pad_end[-1:] // MOE_ROWS).astype(I32)
    return dest, row_tok, blk_e, nused


def _expert_kernel(blk_e_ref, nused_ref, tok_ref, nxt_ref, src_ref, wg_ref, wu_ref, wd_ref, y_ref, xbuf, sem, *, half):
    b = pl.program_id(0)
    nused = nused_ref[0]
    rows = xbuf.shape[1]

    def start_gather(t_ref, slot):
        def body(r, c):
            pltpu.make_async_copy(src_ref.at[t_ref[0, r]], xbuf.at[slot, r], sem.at[slot]).start()
            return c
        lax.fori_loop(0, rows, body, 0, unroll=8)

    @pl.when(b == 0)
    def _():
        start_gather(tok_ref, 0)

    @pl.when(b + 1 < nused)
    def _():
        start_gather(nxt_ref, (b + 1) % 2)

    @pl.when(b >= nused)
    def _():
        y_ref[...] = jnp.zeros_like(y_ref)

    @pl.when(b < nused)
    def _():
        slot = b % 2
        pltpu.make_async_copy(src_ref.at[pl.ds(0, rows)], xbuf.at[slot], sem.at[slot]).wait()
        lo, hi = _unpack_bf16_pair(xbuf[slot])

        def up(w_ref):
            return (jnp.dot(lo, w_ref[0:half, :], preferred_element_type=F32)
                    + jnp.dot(hi, w_ref[half:, :], preferred_element_type=F32))

        g = up(wg_ref)
        h = ((g * _sigmoid(g)) * up(wu_ref)).astype(BF16)
        y_ref[...] = jnp.dot(h, wd_ref[...], preferred_element_type=F32)


def _experts(blk_e, nused, row_tok, src, wg, wu, wd, n_blocks):
    n_exp, d, ff = wg.shape
    wmap = lambda b, be, nu: (be[jnp.minimum(b, nu[0] - 1)], 0, 0)
    return pl.pallas_call(
        functools.partial(_expert_kernel, half=d // 2),
        grid_spec=pltpu.PrefetchScalarGridSpec(
            num_scalar_prefetch=2,
            grid=(n_blocks,),
            in_specs=[pl.BlockSpec((None, 1, MOE_ROWS), lambda b, be, nu: (b, 0, 0), memory_space=pltpu.SMEM),
                      pl.BlockSpec((None, 1, MOE_ROWS), lambda b, be, nu: (jnp.minimum(b + 1, n_blocks - 1), 0, 0),
                                   memory_space=pltpu.SMEM),
                      pl.BlockSpec(memory_space=pl.ANY),
                      pl.BlockSpec((None, d, ff), wmap), pl.BlockSpec((None, d, ff), wmap),
                      pl.BlockSpec((None, ff, d), wmap)],
            out_specs=pl.BlockSpec((MOE_ROWS, d), lambda b, be, nu: (b, 0)),
            scratch_shapes=[pltpu.VMEM((2, MOE_ROWS, d // 2), U32), pltpu.SemaphoreType.DMA((2,))]),
        out_shape=jax.ShapeDtypeStruct((n_blocks * MOE_ROWS, d), F32),
        compiler_params=_cparams(("arbitrary",)),
        name="experts",
    )(blk_e, nused, row_tok.reshape(n_blocks, 1, MOE_ROWS), row_tok.reshape(n_blocks, 1, MOE_ROWS), src, wg, wu, wd)


def _combine_kernel(dest_ref, nxt_ref, x1_ref, wt_ref, mod_ref, fw_ref, y_ref, o_ref, ybuf, sem,
                    *, tiles_per_seq, d, lat):
    i = pl.program_id(0)
    tm = x1_ref.shape[0]

    def start_gather(d_ref, slot):
        def body(r, c):
            for kk in range(TOP_K):
                pltpu.make_async_copy(y_ref.at[d_ref[kk, r]], ybuf.at[slot, kk, r], sem.at[slot]).start()
            return c
        lax.fori_loop(0, tm, body, 0, unroll=8)

    @pl.when(i == 0)
    def _():
        start_gather(dest_ref, 0)

    @pl.when(i + 1 < pl.num_programs(0))
    def _():
        start_gather(nxt_ref, (i + 1) % 2)

    slot = i % 2
    row = (1 + i // tiles_per_seq) if lat else 0
    gate2 = mod_ref[pl.ds(row, 1), 5 * d:6 * d]
    wt = wt_ref[...]
    w0 = _lane_tile(_row_to_col(wt[0:1], tm), d // LANES)
    w1 = _lane_tile(_row_to_col(wt[1:2], tm), d // LANES)
    for kk in range(TOP_K):
        pltpu.make_async_copy(y_ref.at[pl.ds(0, tm)], ybuf.at[slot, kk], sem.at[slot]).wait()
    x = x1_ref[...] + gate2 * (w0 * ybuf[slot, 0] + w1 * ybuf[slot, 1])
    ms = jnp.mean(x * x, axis=-1, keepdims=True)
    o_ref[...] = x * lax.rsqrt(ms + EPS) * fw_ref[...]


def _combine(dest, x1, rwt, mod, final_w, y_rows, *, row0, n_tok, seq_len, lat):
    d = x1.shape[1]
    tm = 256
    tile0 = row0 // tm
    last = tile0 + n_tok // tm - 1
    kern = functools.partial(_combine_kernel, tiles_per_seq=seq_len // tm, d=d, lat=lat)
    return pl.pallas_call(
        kern,
        grid=(n_tok // tm,),
        in_specs=[pl.BlockSpec((SUBLANES, tm), lambda i: (0, tile0 + i), memory_space=pltpu.SMEM),
                  pl.BlockSpec((SUBLANES, tm), lambda i: (0, jnp.minimum(tile0 + i + 1, last)),
                               memory_space=pltpu.SMEM),
                  pl.BlockSpec((tm, d), lambda i: (tile0 + i, 0)),
                  pl.BlockSpec((SUBLANES, tm), lambda i: (0, tile0 + i)),
                  pl.BlockSpec(mod.shape, lambda i: (0, 0)),
                  pl.BlockSpec((1, d), lambda i: (0, 0)),
                  pl.BlockSpec(memory_space=pl.ANY)],
        out_specs=pl.BlockSpec((tm, d), lambda i: (i, 0)),
        out_shape=jax.ShapeDtypeStruct((n_tok, d), F32),
        scratch_shapes=[pltpu.VMEM((2, TOP_K, tm, d), F32), pltpu.SemaphoreType.DMA((2,))],
        compiler_params=_cparams(("arbitrary",)),
        name="combine_lat" if lat else "combine_ctx",
    )(dest, dest, x1, rwt, mod, final_w.reshape(1, d), y_rows)


def _gate_layout(w_gates, b_gates, heads):
    d = w_gates.shape[0]
    w = w_gates.reshape(d, 4, heads).transpose(0, 2, 1)
    w = jnp.pad(w, ((0, 0), (0, 0), (0, SUBLANES - 4))).reshape(d, heads * SUBLANES)
    b = b_gates.reshape(4, heads).T
    b = jnp.pad(b, ((0, 0), (0, SUBLANES - 4))).reshape(1, heads * SUBLANES)
    padl = LANES - heads * SUBLANES
    return jnp.pad(w, ((0, 0), (0, padl))), jnp.pad(b, ((0, 0), (0, padl)))


def kernel(x_prompt, x_sample, state_mlstm_c, state_mlstm_n, state_mlstm_m, state_rglru_h, c, c_ctx, w_ada, b_ada,
           norm1_w, w_in, b_gates, conv_w, conv_b, rg_wa, rg_ba, rg_wx, rg_bx, rg_lambda, mlstm_norm_w, w_out,
           norm2_w, router_group_w, router_group_b, router_expert_w, router_expert_b, expert_w_gate, expert_w_up,
           expert_w_down, final_norm_w):
    n_req, seq, d = x_prompt.shape
    n_lat, lat_seq, _ = x_sample.shape
    depth = w_in.shape[0]
    assert depth == 1, "the token-axis plumbing below is written for the single-layer trunk"
    heads, dk, dv = state_mlstm_c.shape[3:]
    rw = state_rglru_h.shape[-1]
    nblk = rg_wa.shape[2]
    assert rw // nblk == LANES
    n_groups, epg = router_expert_w.shape[1], router_expert_w.shape[3]
    n_exp = n_groups * epg
    qk, mw = heads * dk, heads * dv
    nc, nl = n_req * seq, n_lat * lat_seq
    nt = nc + nl
    assert nc % lat_seq == 0 and n_lat + 1 <= SUBLANES
    l = 0

    xp = x_prompt.reshape(nc, d)
    xs = x_sample.reshape(nl, d)
    cvec = jnp.zeros((SUBLANES, d), F32).at[0].set(c_ctx).at[1:1 + n_lat].set(c)
    mod = _ada(cvec, w_ada[l], b_ada[l])

    w = w_in[l]
    g0 = 2 * qk + 2 * mw
    wgate, bgate = _gate_layout(w[:, g0:g0 + 4 * heads], b_gates[l], heads)
    w_cat = jnp.concatenate([w[:, :g0], wgate, w[:, g0 + 4 * heads:]], axis=1).astype(BF16)
    q, k, v, o, gt, xr, xg = _in_proj(xp, xs, mod, norm1_w[l], w_cat, bgate, lat_seq=lat_seq, heads=heads, dk=dk,
                                      dv=dv, rw=rw)

    mkw = dict(heads=heads, dk=dk, dv=dv)
    ya_c, new_c, new_n, new_m = _mlstm(q, k, v, o, gt, mlstm_norm_w[l], row0=0, n_seq=n_req, t_len=seq,
                                       emit_state=True, **mkw)
    (ya_l,) = _mlstm(q, k, v, o, gt, mlstm_norm_w[l], row0=nc, n_seq=n_lat, t_len=lat_seq,
                     state=(state_mlstm_c[:, l], state_mlstm_n[:, l], state_mlstm_m[:, l]), **mkw)

    wg = jnp.concatenate([rg_wa[l, 0], rg_wx[l, 0], rg_wa[l, 1], rg_wx[l, 1]], axis=-1).astype(BF16)
    bg = jnp.concatenate([rg_ba[l, 0].reshape(nblk, 1, LANES), rg_bx[l, 0].reshape(nblk, 1, LANES),
                          rg_ba[l, 1].reshape(nblk, 1, LANES), rg_bx[l, 1].reshape(nblk, 1, LANES)], axis=-1)
    rargs = (xr, xg, conv_w[l], conv_b[l], wg, bg, rg_lambda[l])
    yb_c, new_h = _rglru(*rargs, row0=0, n_seq=n_req, t_len=seq, seg=seq, emit_state=True)
    (yb_l,) = _rglru(*rargs, row0=nc, n_seq=n_lat, t_len=lat_seq, seg=GRID_W, state=state_rglru_h[:, l])

    r_rows = -(-(SUBLANES + n_exp) // 16) * 16
    wr = jnp.zeros((r_rows, d), F32)
    wr = wr.at[0:n_groups].set(router_group_w[l].T)
    wr = wr.at[SUBLANES:SUBLANES + n_exp].set(router_expert_w[l].transpose(0, 2, 1).reshape(n_exp, d)).astype(BF16)
    br = jnp.zeros((r_rows, LANES), F32)
    br = br.at[0:n_groups, 0].set(router_group_b[l])
    br = br.at[SUBLANES:SUBLANES + n_exp, 0].set(router_expert_b[l].reshape(n_exp))
    x1, hp, ridx, rwt = _out_proj(ya_c, ya_l, yb_c, yb_l, xp, xs, mod, norm2_w[l], w_out[l].astype(BF16), wr, br,
                                  lat_seq=lat_seq, n_groups=n_groups, epg=epg)

    n_blocks = (nt * TOP_K) // MOE_ROWS + n_exp
    dest, row_tok, blk_e, nused = _routing(ridx, nt, n_exp, n_blocks, zero_row=nt)
    y_rows = _experts(blk_e, nused, row_tok, hp, expert_w_gate[l].astype(BF16), expert_w_up[l].astype(BF16),
                      expert_w_down[l].astype(BF16), n_blocks)

    y_prompt = _combine(dest, x1, rwt, mod, final_norm_w, y_rows, row0=0, n_tok=nc, seq_len=seq, lat=False)
    y_sample = _combine(dest, x1, rwt, mod, final_norm_w, y_rows, row0=nc, n_tok=nl, seq_len=lat_seq, lat=True)

    return (y_prompt.reshape(n_req, seq, d), y_sample.reshape(n_lat, lat_seq, d),
            new_c[:, None], new_n.reshape(n_req, 1, 2, heads, dk), new_m.reshape(n_req, 1, 2, heads),
            new_h[:, None])
```

```python
import functools

import jax
import jax.numpy as jnp
import numpy as np
from jax import lax
from jax.experimental import pallas as pl
from jax.experimental.pallas import tpu as pltpu

F32 = jnp.float32
BF16 = jnp.bfloat16
I32 = jnp.int32
U32 = jnp.uint32

EPS = 1e-6
GRID_W = 64
CONV_LEFT = 2
RGLRU_C = 8.0
TOP_K = 2
LANES = 128
SUBLANES = 8
MLSTM_L = 128
NEG = -1e30
VMEM_LIMIT = 56 * 1024 * 1024

_HIGHEST = lax.Precision.HIGHEST


def _cparams(sem, vmem=VMEM_LIMIT):
    return pltpu.CompilerParams(dimension_semantics=sem, vmem_limit_bytes=vmem)


def _sigmoid(x):
    return 0.5 * jnp.tanh(0.5 * x) + 0.5


def _row_to_col(r, n):
    return jnp.broadcast_to(r, (LANES, n)).T


def _lane_tile(x, reps):
    return x if reps == 1 else jnp.concatenate([x] * reps, axis=1)


def _ada_kernel(c_ref, w_ref, b_ref, o_ref):
    c = c_ref[...]
    s = (c * _sigmoid(c)).astype(BF16)
    o_ref[...] = jnp.dot(s, w_ref[...].astype(BF16), preferred_element_type=F32) + b_ref[...]


def _ada(cvec, w_ada, b_ada):
    d, n = w_ada.shape
    tn = 1024 if n % 1024 == 0 else 512
    assert n % tn == 0
    return pl.pallas_call(
        _ada_kernel,
        grid=(n // tn,),
        in_specs=[pl.BlockSpec((SUBLANES, d), lambda j: (0, 0)),
                  pl.BlockSpec((d, tn), lambda j: (0, j)),
                  pl.BlockSpec((1, tn), lambda j: (0, j))],
        out_specs=pl.BlockSpec((SUBLANES, tn), lambda j: (0, j)),
        out_shape=jax.ShapeDtypeStruct((SUBLANES, n), F32),
        compiler_params=_cparams(("arbitrary",)),
        name="ada",
    )(cvec, w_ada, b_ada.reshape(1, n))


def _modulated_norm(x, w, shift, scale):
    ms = jnp.mean(x * x, axis=-1, keepdims=True)
    return (x * lax.rsqrt(ms + EPS) * w) * (1.0 + scale) + shift


def _mod_row(i, nctx_tiles, tiles_per_lat):
    return jnp.where(i < nctx_tiles, 0, 1 + (i - nctx_tiles) // tiles_per_lat)


def _inproj_kernel(xp_ref, xs_ref, mod_ref, n1_ref, w_ref, wkt_ref, bg_ref,
                   q_ref, kt_ref, v_ref, o_ref, gt_ref, xr_ref, xg_ref,
                   *, nctx_tiles, tiles_per_lat, d, qk, mw, rw, gh, qscale):
    i = pl.program_id(0)
    x = jnp.where(i < nctx_tiles, xp_ref[...], xs_ref[...])
    row = _mod_row(i, nctx_tiles, tiles_per_lat)
    shift = mod_ref[pl.ds(row, 1), 0:d]
    scale = mod_ref[pl.ds(row, 1), d:2 * d]
    hb = _modulated_norm(x, n1_ref[...], shift, scale).astype(BF16)

    def proj(c0, width):
        return jnp.dot(hb, w_ref[:, c0:c0 + width], preferred_element_type=F32)

    c0 = 0
    q_ref[...] = (proj(c0, qk) * qscale).astype(BF16); c0 += qk
    kt_ref[...] = lax.dot_general(wkt_ref[...], hb, (((1,), (1,)), ((), ())),
                                  preferred_element_type=F32).astype(BF16)
    v_ref[...] = proj(c0, mw).astype(BF16); c0 += mw
    o_ref[...] = proj(c0, mw); c0 += mw
    zg = proj(c0, LANES) + bg_ref[...]; c0 += LANES
    lane = lax.broadcasted_iota(I32, zg.shape, 1)
    log_sig = jnp.minimum(zg, 0.0) - jnp.log1p(jnp.exp(-jnp.abs(zg)))
    zg = jnp.where(lane % 2 == 1, log_sig, zg)
    gt_ref[...] = zg.T[0:gh, :]
    xr_ref[...] = proj(c0, rw); c0 += rw
    xg_ref[...] = proj(c0, rw)


def _in_proj(xp, xs, mod, norm1_w, w_cat, w_kt, bg, *, lat_seq, heads, dk, dv, rw):
    nc, d = xp.shape
    nl = xs.shape[0]
    nt = nc + nl
    tm = 256
    qk, mw, gh = heads * dk, heads * dv, heads * SUBLANES
    nctx_tiles = nc // tm
    kern = functools.partial(_inproj_kernel, nctx_tiles=nctx_tiles, tiles_per_lat=lat_seq // tm, d=d, qk=qk, mw=mw,
                             rw=rw, gh=gh, qscale=dk ** -0.5)
    row = lambda i: (i, 0)
    const = lambda i: (0, 0)
    return pl.pallas_call(
        kern,
        grid=(nt // tm,),
        in_specs=[pl.BlockSpec((tm, d), lambda i: (jnp.minimum(i, nctx_tiles - 1), 0)),
                  pl.BlockSpec((tm, d), lambda i: (jnp.maximum(i - nctx_tiles, 0), 0)),
                  pl.BlockSpec(mod.shape, const),
                  pl.BlockSpec((1, d), const),
                  pl.BlockSpec(w_cat.shape, const, pipeline_mode=pl.Buffered(1)),
                  pl.BlockSpec(w_kt.shape, const, pipeline_mode=pl.Buffered(1)),
                  pl.BlockSpec((1, LANES), const)],
        out_specs=[pl.BlockSpec((tm, qk), row), pl.BlockSpec((qk, tm), lambda i: (0, i)), pl.BlockSpec((tm, mw), row),
                   pl.BlockSpec((tm, mw), row), pl.BlockSpec((gh, tm), lambda i: (0, i)),
                   pl.BlockSpec((tm, rw), row), pl.BlockSpec((tm, rw), row)],
        out_shape=[jax.ShapeDtypeStruct((nt, qk), BF16), jax.ShapeDtypeStruct((qk, nt), BF16),
                   jax.ShapeDtypeStruct((nt, mw), BF16), jax.ShapeDtypeStruct((nt, mw), F32),
                   jax.ShapeDtypeStruct((gh, nt), F32),
                   jax.ShapeDtypeStruct((nt, rw), F32), jax.ShapeDtypeStruct((nt, rw), F32)],
        compiler_params=_cparams(("arbitrary",)),
        name="in_proj",
    )(xp, xs, mod, norm1_w.reshape(1, d), w_cat, w_kt, bg)


def _mlstm_kernel(*refs, t_len, dk, dv, has_state, emit_state):
    it = iter(refs)
    q_ref, kt_ref, v_ref, o_ref, gt_ref, nw_ref, tri_ref = (next(it) for _ in range(7))
    if has_state:
        c0_ref, n0_ref, m0_ref = (next(it) for _ in range(3))
    ya_ref = next(it)
    if emit_state:
        cn_ref, nn_ref, mn_ref = (next(it) for _ in range(3))
    hf_scr, hb_scr, c_scr, ma_scr, p_scr, w_scr, em_scr, kw_scr, dm_scr = (next(it) for _ in range(9))
    ln = MLSTM_L
    nchunks = t_len // ln
    assert nchunks % 2 == 0 and ln == LANES and dk == LANES
    h_scr = (hf_scr, hb_scr)
    ext = dv + LANES

    for d in range(2):
        if has_state:
            c_scr[d, :, 0:dv] = c0_ref[d]
            c_scr[d, :, dv:ext] = _row_to_col(n0_ref[d], dk)
            ma_scr[d] = m0_ref[d]
        else:
            c_scr[d] = jnp.zeros((dk, ext), F32)
            ma_scr[d] = jnp.zeros((1, 1), F32)

    def chunk_start(d, j):
        return pl.multiple_of((j if d == 0 else nchunks - 1 - j) * ln, ln)

    def stage_a(d, j, slot):
        t0 = chunk_start(d, j)
        q = q_ref[pl.ds(t0, ln), :]
        kt = kt_ref[:, pl.ds(t0, ln)]
        g8 = gt_ref[:, pl.ds(t0, ln)]
        cum8 = jnp.dot(g8, tri_ref[d], precision=_HIGHEST, preferred_element_type=F32)
        valid = tri_ref[1 - d] > 0.5
        li = g8[2 * d:2 * d + 1]
        lf = g8[2 * d + 1:2 * d + 2]
        cum_row = cum8[2 * d + 1:2 * d + 2]
        total = jnp.sum(lf, axis=1, keepdims=True)
        a_row = li - cum_row
        cum_col = _row_to_col(cum_row, ln)
        m_prev = ma_scr[d]
        dmat = jnp.where(valid, cum_col + a_row, NEG)
        inter = cum_col + m_prev
        m_t = jnp.maximum(inter, jnp.max(dmat, axis=1, keepdims=True))
        s = jnp.dot(q, kt, preferred_element_type=F32)
        bank, par = slot
        p_scr[d, bank, par] = (s * jnp.exp(dmat - m_t)).astype(BF16)
        w_scr[d, bank, par] = jnp.exp(inter - m_t)
        em_scr[d, bank, par] = jnp.exp(-m_t)
        g_row = total + a_row
        m_new = jnp.maximum(total + m_prev, jnp.max(g_row, axis=1, keepdims=True))
        kw_scr[d, bank, par] = (kt.astype(F32) * jnp.exp(g_row - m_new)).astype(BF16)
        decay = jnp.exp(total + m_prev - m_new)
        rid = lax.broadcasted_iota(I32, (SUBLANES, LANES), 0)
        dm_scr[d, bank, par] = jnp.where(rid == 0, decay, m_new)
        ma_scr[d] = m_new

    def stage_b(d, j, slot):
        t0 = chunk_start(d, j)
        q = q_ref[pl.ds(t0, ln), :]
        v_ext = jnp.concatenate([v_ref[pl.ds(t0, ln), :], jnp.ones((ln, LANES), BF16)], axis=1)
        c_st = c_scr[d]
        bank, par = slot
        full = (jnp.dot(p_scr[d, bank, par], v_ext, preferred_element_type=F32)
                + _lane_tile(w_scr[d, bank, par], ext // LANES) * jnp.dot(q, c_st.astype(BF16),
                                                                         preferred_element_type=F32))
        inv = 1.0 / jnp.maximum(jnp.abs(full[:, dv:ext]), em_scr[d, bank, par])
        h_scr[d][pl.ds(t0, ln), :] = full[:, 0:dv] * _lane_tile(inv, dv // LANES)
        decay = dm_scr[d, bank, par, 0:1, 0:1]
        c_scr[d] = decay * c_st + jnp.dot(kw_scr[d, bank, par], v_ext, preferred_element_type=F32)

    npairs = nchunks // 2

    def pair_a(jj, bank):
        for par in range(2):
            for d in range(2):
                stage_a(d, 2 * jj + par, (bank, par))

    def pair_b(jj, bank):
        for par in range(2):
            for d in range(2):
                stage_b(d, 2 * jj + par, (bank, par))

    pair_a(0, 0)

    def body(jj, carry):
        bank = jj % 2
        pair_b(jj, bank)
        pair_a(jj + 1, 1 - bank)
        return carry

    lax.fori_loop(0, npairs - 1, body, 0)
    pair_b(npairs - 1, (npairs - 1) % 2)

    def finish(j, carry):
        t0 = pl.multiple_of(j * ln, ln)
        hs = hf_scr[pl.ds(t0, ln), :] + hb_scr[pl.ds(t0, ln), :]
        ms = jnp.mean(hs * hs, axis=1, keepdims=True)
        y = hs * lax.rsqrt(ms + EPS) * nw_ref[...]
        ya_ref[pl.ds(t0, ln), :] = (_sigmoid(o_ref[pl.ds(t0, ln), :]) * y).astype(BF16)
        return carry

    lax.fori_loop(0, nchunks, finish, 0)

    if emit_state:
        for d in range(2):
            cn_ref[d] = c_scr[d, :, 0:dv]
            nn_ref[d] = c_scr[d, :, dv:ext].T[0:1, :]
            mn_ref[d] = dm_scr[d, (npairs - 1) % 2, 1, 1:2, 0:1]


def _mlstm_tri():
    r = np.arange(MLSTM_L)
    fwd = (r[:, None] <= r[None, :]).astype(np.float32)
    return jnp.asarray(np.stack([fwd, fwd.T]))


def _mlstm(q, kt, v, o, gt, norm_w, *, row0, n_seq, t_len, heads, dk, dv, state=None, emit_state=False):
    blk0 = row0 // t_len
    tok = lambda s, h: (blk0 + s, h)
    in_specs = [pl.BlockSpec((t_len, dk), tok), pl.BlockSpec((dk, t_len), lambda s, h: (h, blk0 + s)),
                pl.BlockSpec((t_len, dv), tok), pl.BlockSpec((t_len, dv), tok),
                pl.BlockSpec((SUBLANES, t_len), lambda s, h: (h, blk0 + s)),
                pl.BlockSpec((None, 1, dv), lambda s, h: (h, 0, 0)),
                pl.BlockSpec((2, MLSTM_L, MLSTM_L), lambda s, h: (0, 0, 0))]
    args = [q, kt, v, o, gt, norm_w.reshape(heads, 1, dv), _mlstm_tri()]
    if state is not None:
        c0, n0, m0 = state
        in_specs += [pl.BlockSpec((None, 2, None, dk, dv), lambda s, h: (s, 0, h, 0, 0)),
                     pl.BlockSpec((None, 2, None, 1, dk), lambda s, h: (s, 0, h, 0, 0)),
                     pl.BlockSpec((None, 2, None, 1, 1), lambda s, h: (s, 0, h, 0, 0))]
        args += [c0, n0.reshape(n_seq, 2, heads, 1, dk), m0.reshape(n_seq, 2, heads, 1, 1)]
    out_specs = [pl.BlockSpec((t_len, dv), lambda s, h: (s, h))]
    out_shape = [jax.ShapeDtypeStruct((n_seq * t_len, heads * dv), BF16)]
    if emit_state:
        out_specs += [pl.BlockSpec((None, 2, None, dk, dv), lambda s, h: (s, 0, h, 0, 0)),
                      pl.BlockSpec((None, 2, None, 1, dk), lambda s, h: (s, 0, h, 0, 0)),
                      pl.BlockSpec((None, 2, None, 1, 1), lambda s, h: (s, 0, h, 0, 0))]
        out_shape += [jax.ShapeDtypeStruct((n_seq, 2, heads, dk, dv), F32),
                      jax.ShapeDtypeStruct((n_seq, 2, heads, 1, dk), F32),
                      jax.ShapeDtypeStruct((n_seq, 2, heads, 1, 1), F32)]
    kern = functools.partial(_mlstm_kernel, t_len=t_len, dk=dk, dv=dv, has_state=state is not None,
                             emit_state=emit_state)
    return pl.pallas_call(
        kern,
        grid=(n_seq, heads),
        in_specs=in_specs,
        out_specs=out_specs,
        out_shape=out_shape,
        scratch_shapes=[pltpu.VMEM((t_len, dv), F32), pltpu.VMEM((t_len, dv), F32),
                        pltpu.VMEM((2, dk, dv + LANES), F32), pltpu.VMEM((2, 1, 1), F32),
                        pltpu.VMEM((2, 2, 2, MLSTM_L, MLSTM_L), BF16), pltpu.VMEM((2, 2, 2, MLSTM_L, LANES), F32),
                        pltpu.VMEM((2, 2, 2, MLSTM_L, LANES), F32), pltpu.VMEM((2, 2, 2, dk, MLSTM_L), BF16),
                        pltpu.VMEM((2, 2, 2, SUBLANES, LANES), F32)],
        compiler_params=_cparams(("arbitrary", "arbitrary")),
        name="mlstm_state" if emit_state else "mlstm",
    )(*args)


def _gelu_tanh(x):
    return x * (0.5 * (1.0 + jnp.tanh(0.7978845608028654 * (x + 0.044715 * (x * x * x)))))


def _softplus(x):
    return jnp.maximum(x, 0.0) + jnp.log1p(jnp.exp(-jnp.abs(x)))


def _rglru_kernel(*refs, t_len, seg, sub, pitch, tc, has_state, emit_state):
    it = iter(refs)
    xr_ref, xg_ref, cw_ref, cb_ref, wg_ref, bg_ref, lam_ref = (next(it) for _ in range(7))
    if has_state:
        h0_ref = next(it)
    yb_ref = next(it)
    if emit_state:
        hn_ref = next(it)
    a_scr, u_scr, cin_scr = (next(it) for _ in range(3))
    nchunks = t_len // tc
    piece = min(tc, sub)
    npieces = tc // piece
    ntile = xr_ref.shape[1] // LANES
    chains = [(d, lt) for d in range(2) for lt in range(ntile)]

    def scan_rows(t0, p):
        t = t0 + p * piece
        i = t // sub
        return pl.ds(pl.multiple_of(i * pitch + (t - i * sub), SUBLANES), piece), i

    sp = _softplus(-lam_ref[...])

    def gates(c, carry):
        t0 = pl.multiple_of(c * tc, tc)
        pos = lax.broadcasted_iota(I32, (tc, LANES), 0) % seg
        for lt in range(ntile):
            cols = slice(lt * LANES, (lt + 1) * LANES)
            x = xr_ref[pl.ds(t0, tc), cols]
            xc = cb_ref[:, cols] + cw_ref[CONV_LEFT:CONV_LEFT + 1, cols] * x
            for j in range(cw_ref.shape[0]):
                off = j - CONV_LEFT
                if off == 0:
                    continue
                shifted = pltpu.roll(x, (-off) % tc, 0)
                ok = (pos >= -off) if off < 0 else (pos < seg - off)
                xc = xc + cw_ref[j:j + 1, cols] * jnp.where(ok, shifted, 0.0)
            r_all = jnp.dot(xc.astype(BF16), wg_ref[lt], preferred_element_type=F32) + bg_ref[lt]
            for d in range(2):
                r = _sigmoid(r_all[:, (2 * d) * LANES:(2 * d + 1) * LANES])
                ig = _sigmoid(r_all[:, (2 * d + 1) * LANES:(2 * d + 2) * LANES])
                a = jnp.exp((-RGLRU_C) * r * sp[d:d + 1, cols])
                u = jnp.sqrt(1.0 - a * a) * (ig * xc)
                for p in range(npieces):
                    rows, _ = scan_rows(t0, p)
                    a_scr[d, lt, rows, :] = a[p * piece:(p + 1) * piece]
                    u_scr[d, lt, rows, :] = u[p * piece:(p + 1) * piece]
        return carry

    lax.fori_loop(0, nchunks, gates, 0)

    def scan(j, carry):
        out = []
        for (d, lt), (h, p) in zip(chains, carry):
            rows = pl.ds(j if d == 0 else sub - 1 - j, SUBLANES, stride=pitch)
            a = a_scr[d, lt, rows, :]
            h = a * h + u_scr[d, lt, rows, :]
            p = a * p
            a_scr[d, lt, rows, :] = p
            u_scr[d, lt, rows, :] = h
            out.append((h, p))
        return tuple(out)

    zero = jnp.zeros((SUBLANES, LANES), F32)
    one = jnp.ones((SUBLANES, LANES), F32)
    ends = lax.fori_loop(0, sub, scan, tuple((zero, one) for _ in chains))

    for (d, lt), (h, p) in zip(chains, ends):
        cols = slice(lt * LANES, (lt + 1) * LANES)
        cin = h0_ref[d:d + 1, cols] if has_state else jnp.zeros((1, LANES), F32)
        for i in (range(SUBLANES) if d == 0 else reversed(range(SUBLANES))):
            cin_scr[d, lt, i:i + 1, :] = cin
            cin = h[i:i + 1] + p[i:i + 1] * cin
        if emit_state:
            hn_ref[d:d + 1, cols] = cin

    def finish(c, carry):
        t0 = pl.multiple_of(c * tc, tc)
        for lt in range(ntile):
            cols = slice(lt * LANES, (lt + 1) * LANES)
            for p in range(npieces):
                rows, i = scan_rows(t0, p)
                h = (u_scr[0, lt, rows, :] + a_scr[0, lt, rows, :] * cin_scr[0, lt, pl.ds(i, 1), :]
                     + u_scr[1, lt, rows, :] + a_scr[1, lt, rows, :] * cin_scr[1, lt, pl.ds(i, 1), :])
                nat = pl.ds(pl.multiple_of(t0 + p * piece, SUBLANES), piece)
                yb_ref[nat, cols] = (h * _gelu_tanh(xg_ref[nat, cols])).astype(BF16)
        return carry

    lax.fori_loop(0, nchunks, finish, 0)


def _rglru(xr, xg, conv_w, conv_b, wg, bg, lam, *, row0, n_seq, t_len, seg, state=None, emit_state=False):
    rw = xr.shape[1]
    ntile = 2
    cb = ntile * LANES
    assert rw % cb == 0
    blk0 = row0 // t_len
    sub = t_len // SUBLANES
    pitch = sub + SUBLANES
    tc = 256
    tok = lambda s, g: (blk0 + s, g)
    in_specs = [pl.BlockSpec((t_len, cb), tok), pl.BlockSpec((t_len, cb), tok),
                pl.BlockSpec((conv_w.shape[0], cb), lambda s, g: (0, g)),
                pl.BlockSpec((1, cb), lambda s, g: (0, g)),
                pl.BlockSpec((ntile, LANES, 4 * LANES), lambda s, g: (g, 0, 0)),
                pl.BlockSpec((ntile, 1, 4 * LANES), lambda s, g: (g, 0, 0)),
                pl.BlockSpec((2, cb), lambda s, g: (0, g))]
    args = [xr, xg, conv_w, conv_b.reshape(1, rw), wg, bg, lam]
    if state is not None:
        in_specs.append(pl.BlockSpec((None, 2, cb), lambda s, g: (s, 0, g)))
        args.append(state)
    out_specs = [pl.BlockSpec((t_len, cb), lambda s, g: (s, g))]
    out_shape = [jax.ShapeDtypeStruct((n_seq * t_len, rw), BF16)]
    if emit_state:
        out_specs.append(pl.BlockSpec((None, 2, cb), lambda s, g: (s, 0, g)))
        out_shape.append(jax.ShapeDtypeStruct((n_seq, 2, rw), F32))
    kern = functools.partial(_rglru_kernel, t_len=t_len, seg=seg, sub=sub, pitch=pitch, tc=tc,
                             has_state=state is not None, emit_state=emit_state)
    return pl.pallas_call(
        kern,
        grid=(n_seq, rw // cb),
        in_specs=in_specs,
        out_specs=out_specs,
        out_shape=out_shape,
        scratch_shapes=[pltpu.VMEM((2, ntile, SUBLANES * pitch, LANES), F32),
                        pltpu.VMEM((2, ntile, SUBLANES * pitch, LANES), F32),
                        pltpu.VMEM((2, ntile, SUBLANES, LANES), F32)],
        compiler_params=_cparams(("arbitrary", "arbitrary")),
        name="rglru_state" if emit_state else "rglru",
    )(*args)


def _pack_bf16_pair(lo, hi):
    def rne(x):
        b = pltpu.bitcast(x, U32)
        return (b + jnp.uint32(0x7FFF) + ((b >> 16) & jnp.uint32(1))) >> 16
    return rne(lo) | (rne(hi) << 16)


def _unpack_bf16_pair(w):
    lo = pltpu.bitcast(w << 16, F32).astype(BF16)
    hi = pltpu.bitcast(w & jnp.uint32(0xFFFF0000), F32).astype(BF16)
    return lo, hi


def _outproj_kernel(yac_ref, yal_ref, ybc_ref, ybl_ref, xp_ref, xs_ref, mod_ref, n2_ref, wo_ref, wr_ref, br_ref,
                    x1_ref, hp_ref, ridx_ref, rwt_ref,
                    *, nctx_tiles, ntok_tiles, tiles_per_lat, d, mw, n_groups, epg):
    i = pl.program_id(0)

    @pl.when(i == ntok_tiles)
    def _():
        x1_ref[...] = jnp.zeros_like(x1_ref)
        hp_ref[...] = jnp.zeros_like(hp_ref)
        ridx_ref[...] = jnp.zeros_like(ridx_ref)
        rwt_ref[...] = jnp.zeros_like(rwt_ref)

    @pl.when(i < ntok_tiles)
    def _():
        is_ctx = i < nctx_tiles
        x = jnp.where(is_ctx, xp_ref[...], xs_ref[...])
        ya = jnp.where(is_ctx, yac_ref[...], yal_ref[...])
        yb = jnp.where(is_ctx, ybc_ref[...], ybl_ref[...])
        row = _mod_row(i, nctx_tiles, tiles_per_lat)
        gate1 = mod_ref[pl.ds(row, 1), 2 * d:3 * d]
        shift2 = mod_ref[pl.ds(row, 1), 3 * d:4 * d]
        scale2 = mod_ref[pl.ds(row, 1), 4 * d:5 * d]
        y = (jnp.dot(ya, wo_ref[0:mw, :], preferred_element_type=F32)
             + jnp.dot(yb, wo_ref[mw:, :], preferred_element_type=F32))
        x1 = x + gate1 * y
        x1_ref[...] = x1
        h2 = _modulated_norm(x1, n2_ref[...], shift2, scale2)
        half = d // 2
        hp_ref[...] = _pack_bf16_pair(h2[:, :half], h2[:, half:])

        lt = lax.dot_general(wr_ref[...], h2.astype(BF16), (((1,), (1,)), ((), ())),
                             preferred_element_type=F32) + br_ref[:, 0:1]
        gidx = lax.broadcasted_iota(I32, (SUBLANES, lt.shape[1]), 0)
        gl = jnp.where(gidx < n_groups, lt[0:SUBLANES], -jnp.inf)
        gmax = jnp.max(gl, axis=0, keepdims=True)
        grp = jnp.min(jnp.where(gl == gmax, gidx, n_groups), axis=0, keepdims=True)
        p_grp = 1.0 / jnp.sum(jnp.exp(gl - gmax), axis=0, keepdims=True)
        el = lt[SUBLANES:SUBLANES + epg]
        for g in range(1, n_groups):
            el = jnp.where(grp == g, lt[SUBLANES + g * epg:SUBLANES + (g + 1) * epg], el)
        eidx = lax.broadcasted_iota(I32, el.shape, 0)
        v1 = jnp.max(el, axis=0, keepdims=True)
        i1 = jnp.min(jnp.where(el == v1, eidx, epg), axis=0, keepdims=True)
        el2 = jnp.where(eidx == i1, -jnp.inf, el)
        v2 = jnp.max(el2, axis=0, keepdims=True)
        i2 = jnp.min(jnp.where(el2 == v2, eidx, epg), axis=0, keepdims=True)
        e2 = jnp.exp(v2 - v1)
        w1 = p_grp / (1.0 + e2)
        w2 = p_grp * e2 / (1.0 + e2)
        rid = lax.broadcasted_iota(I32, ridx_ref.shape, 0)
        ridx_ref[...] = jnp.where(rid == 0, grp * epg + i1, jnp.where(rid == 1, grp * epg + i2, 0))
        rwt_ref[...] = jnp.where(rid == 0, w1, jnp.where(rid == 1, w2, 0.0))


def _out_proj(ya_c, ya_l, yb_c, yb_l, xp, xs, mod, norm2_w, w_out, wr, br, *, lat_seq, n_groups, epg):
    nc, d = xp.shape
    nl = xs.shape[0]
    nt = nc + nl
    mw = ya_c.shape[1]
    tm = 256
    nctx_tiles, ntok_tiles = nc // tm, nt // tm
    kern = functools.partial(_outproj_kernel, nctx_tiles=nctx_tiles, ntok_tiles=ntok_tiles,
                             tiles_per_lat=lat_seq // tm, d=d, mw=mw, n_groups=n_groups, epg=epg)
    ctx = lambda i: (jnp.minimum(i, nctx_tiles - 1), 0)
    lat = lambda i: (jnp.clip(i - nctx_tiles, 0, nl // tm - 1), 0)
    row = lambda i: (i, 0)
    const = lambda i: (0, 0)
    return pl.pallas_call(
        kern,
        grid=(ntok_tiles + 1,),
        in_specs=[pl.BlockSpec((tm, mw), ctx), pl.BlockSpec((tm, mw), lat),
                  pl.BlockSpec((tm, yb_c.shape[1]), ctx), pl.BlockSpec((tm, yb_c.shape[1]), lat),
                  pl.BlockSpec((tm, d), ctx), pl.BlockSpec((tm, d), lat),
                  pl.BlockSpec(mod.shape, const),
                  pl.BlockSpec((1, d), const),
                  pl.BlockSpec(w_out.shape, const, pipeline_mode=pl.Buffered(1)),
                  pl.BlockSpec(wr.shape, const),
                  pl.BlockSpec(br.shape, const)],
        out_specs=[pl.BlockSpec((tm, d), row), pl.BlockSpec((tm, d // 2), row),
                   pl.BlockSpec((SUBLANES, tm), lambda i: (0, i)), pl.BlockSpec((SUBLANES, tm), lambda i: (0, i))],
        out_shape=[jax.ShapeDtypeStruct((nt + tm, d), F32), jax.ShapeDtypeStruct((nt + tm, d // 2), U32),
                   jax.ShapeDtypeStruct((SUBLANES, nt + tm), I32), jax.ShapeDtypeStruct((SUBLANES, nt + tm), F32)],
        compiler_params=_cparams(("arbitrary",)),
        name="out_proj",
    )(ya_c, ya_l, yb_c, yb_l, xp, xs, mod, norm2_w.reshape(1, d), w_out, wr, br)


MOE_ROWS = 256


ROUTE_TILE = 512


def _rank_kernel(ridx_ref, tri_ref, rank_ref, cnt_ref, carry_scr, *, n_exp):
    @pl.when(pl.program_id(0) == 0)
    def _():
        carry_scr[...] = jnp.zeros_like(carry_scr)

    e = ridx_ref[...]
    tr = e.shape[1]
    eid = lax.broadcasted_iota(I32, (n_exp, tr), 0)
    carry = carry_scr[:, 0:1]
    ranks = []
    for kk in range(TOP_K):
        hit = eid == e[kk:kk + 1]
        cum = jnp.dot(jnp.where(hit, 1.0, 0.0).astype(BF16), tri_ref[...], preferred_element_type=F32)
        ranks.append(jnp.sum(jnp.where(hit, cum + carry, 0.0), axis=0, keepdims=True) - 1.0)
        carry = carry + cum[:, tr - 1:tr]
    carry_scr[...] = jnp.broadcast_to(carry, carry_scr.shape)
    cnt_ref[...] = jnp.broadcast_to(carry, cnt_ref.shape)
    rid = lax.broadcasted_iota(I32, rank_ref.shape, 0)
    rank_ref[...] = jnp.where(rid == 0, ranks[0], jnp.where(rid == 1, ranks[1], 0.0)).astype(I32)


def _dest_kernel(ridx_ref, rank_ref, pstart_ref, dest_ref, *, n_exp):
    e = ridx_ref[...]
    tr = e.shape[1]
    eid = lax.broadcasted_iota(I32, (n_exp, tr), 0)
    ps = pstart_ref[:, 0:1]
    rows = [jnp.sum(jnp.where(eid == e[kk:kk + 1], ps, 0.0), axis=0, keepdims=True) for kk in range(TOP_K)]
    rid = lax.broadcasted_iota(I32, dest_ref.shape, 0)
    dest_ref[...] = rank_ref[...] + jnp.where(rid == 0, rows[0], jnp.where(rid == 1, rows[1], 0.0)).astype(I32)


def _invert_kernel(dest_ref, fill_ref, rowtok_ref, sem):
    i = pl.program_id(0)
    tr = dest_ref.shape[1]

    @pl.when(i == 0)
    def _():
        fill = pltpu.make_async_copy(fill_ref, rowtok_ref, sem.at[0])
        fill.start()
        fill.wait()

    def body(r, c):
        for kk in range(TOP_K):
            rowtok_ref[dest_ref[kk, r]] = i * tr + r
        return c
    lax.fori_loop(0, tr, body, 0, unroll=16)


def _routing(ridx, n_tok, n_exp, n_blocks, zero_row):
    tr = ROUTE_TILE
    steps = n_tok // tr
    tri = jnp.asarray(np.triu(np.ones((tr, tr), np.float32)), BF16)
    tile = pl.BlockSpec((SUBLANES, tr), lambda i: (0, i))
    cnt_spec = pl.BlockSpec((n_exp, LANES), lambda i: (0, 0))
    rank, cnt = pl.pallas_call(
        functools.partial(_rank_kernel, n_exp=n_exp),
        grid=(steps,),
        in_specs=[tile, pl.BlockSpec((tr, tr), lambda i: (0, 0))],
        out_specs=[tile, cnt_spec],
        out_shape=[jax.ShapeDtypeStruct((SUBLANES, n_tok), I32), jax.ShapeDtypeStruct((n_exp, LANES), F32)],
        scratch_shapes=[pltpu.VMEM((n_exp, LANES), F32)],
        compiler_params=_cparams(("arbitrary",)),
        name="route_rank",
    )(ridx, tri)
    counts = cnt[:, 0].astype(I32)
    padded = (counts + MOE_ROWS - 1) // MOE_ROWS * MOE_ROWS
    pad_end = jnp.cumsum(padded)
    pad_start = pad_end - padded
    dest = pl.pallas_call(
        functools.partial(_dest_kernel, n_exp=n_exp),
        grid=(steps,),
        in_specs=[tile, tile, cnt_spec],
        out_specs=tile,
        out_shape=jax.ShapeDtypeStruct((SUBLANES, n_tok), I32),
        compiler_params=_cparams(("arbitrary",)),
        name="route_dest",
    )(ridx, rank, jnp.broadcast_to(pad_start.astype(F32)[:, None], (n_exp, LANES)))
    row_tok = pl.pallas_call(
        _invert_kernel,
        grid=(steps,),
        in_specs=[pl.BlockSpec((SUBLANES, tr), lambda i: (0, i), memory_space=pltpu.SMEM),
                  pl.BlockSpec(memory_space=pl.ANY)],
        out_specs=pl.BlockSpec(memory_space=pltpu.SMEM),
        out_shape=jax.ShapeDtypeStruct((n_blocks * MOE_ROWS,), I32),
        scratch_shapes=[pltpu.SemaphoreType.DMA((1,))],
        compiler_params=_cparams(("arbitrary",)),
        name="route_invert",
    )(dest, jnp.full((n_blocks * MOE_ROWS,), zero_row, I32))
    blk_row0 = jnp.arange(n_blocks, dtype=I32) * MOE_ROWS
    blk_e = jnp.minimum(jnp.sum((pad_end[None, :] <= blk_row0[:, None]).astype(I32), axis=1), n_exp - 1)
    nused = (pad_end[-1:] // MOE_ROWS).astype(I32)
    return dest, row_tok, blk_e, nused


CAST_ROWS = 256


def _expert_kernel(blk_e_ref, nused_ref, tok_ref, nxt_ref, src_ref, wg_hbm, wu_hbm, wd_hbm, y_ref,
                   xbuf, xsem, stage_g, stage_u, stage_d, bf_g, bf_u, bf_d, wsem, *, half):
    b = pl.program_id(0)
    nused = nused_ref[0]
    rows = xbuf.shape[1]
    w_hbm = (wg_hbm, wu_hbm, wd_hbm)
    stage = (stage_g, stage_u, stage_d)
    wbf = (bf_g, bf_u, bf_d)

    def weight_copy(e, j):
        return pltpu.make_async_copy(w_hbm[j].at[e], stage[j], wsem.at[j])

    def start_gather(t_ref, slot):
        for r in range(rows):
            pltpu.make_async_copy(src_ref.at[t_ref[0, r]], xbuf.at[slot, r], xsem.at[slot]).start()

    @pl.when(b >= nused)
    def _():
        y_ref[...] = jnp.zeros_like(y_ref)

    @pl.when(b < nused)
    def _():
        e = blk_e_ref[b]

        @pl.when(b == 0)
        def _():
            for j in range(3):
                weight_copy(e, j).start()
            def body(r, c):
                pltpu.make_async_copy(src_ref.at[tok_ref[0, r]], xbuf.at[0, r], xsem.at[0]).start()
                return c
            lax.fori_loop(0, rows, body, 0, unroll=8)

        @pl.when((b == 0) | (blk_e_ref[jnp.maximum(b - 1, 0)] != e))
        def _():
            for j in range(3):
                weight_copy(e, j).wait()
                n_steps = stage[j].shape[0] // CAST_ROWS

                def cast(c, carry, j=j):
                    sl = pl.ds(pl.multiple_of(c * CAST_ROWS, CAST_ROWS), CAST_ROWS)
                    wbf[j][sl, :] = stage[j][sl, :].astype(BF16)
                    return carry
                lax.fori_loop(0, n_steps, cast, 0)
            nb = lax.while_loop(lambda k: (k < nused) & (blk_e_ref[jnp.minimum(k, nused - 1)] == e),
                                lambda k: k + 1, b + 1)

            @pl.when(nb < nused)
            def _():
                for j in range(3):
                    weight_copy(blk_e_ref[jnp.minimum(nb, nused - 1)], j).start()

        slot = b % 2
        pltpu.make_async_copy(src_ref.at[pl.ds(0, rows)], xbuf.at[slot], xsem.at[slot]).wait()

        def compute(prefetch):
            lo, hi = _unpack_bf16_pair(xbuf[slot])
            if prefetch:
                start_gather(nxt_ref, 1 - slot)

            def up(w_ref):
                return (jnp.dot(lo, w_ref[0:half, :], preferred_element_type=F32)
                        + jnp.dot(hi, w_ref[half:, :], preferred_element_type=F32))

            g = up(wbf[0])
            h = ((g * _sigmoid(g)) * up(wbf[1])).astype(BF16)
            y_ref[...] = jnp.dot(h, wbf[2][...], preferred_element_type=F32)

        @pl.when(b + 1 < nused)
        def _():
            compute(True)

        @pl.when(b + 1 >= nused)
        def _():
            compute(False)


def _experts(blk_e, nused, row_tok, src, wg, wu, wd, n_blocks):
    n_exp, d, ff = wg.shape
    any_spec = pl.BlockSpec(memory_space=pl.ANY)
    tok = row_tok.reshape(n_blocks, 1, MOE_ROWS)
    return pl.pallas_call(
        functools.partial(_expert_kernel, half=d // 2),
        grid_spec=pltpu.PrefetchScalarGridSpec(
            num_scalar_prefetch=2,
            grid=(n_blocks,),
            in_specs=[pl.BlockSpec((None, 1, MOE_ROWS), lambda b, be, nu: (b, 0, 0), memory_space=pltpu.SMEM),
                      pl.BlockSpec((None, 1, MOE_ROWS), lambda b, be, nu: (jnp.minimum(b + 1, n_blocks - 1), 0, 0),
                                   memory_space=pltpu.SMEM),
                      any_spec, any_spec, any_spec, any_spec],
            out_specs=pl.BlockSpec((MOE_ROWS, d), lambda b, be, nu: (b, 0)),
            scratch_shapes=[pltpu.VMEM((2, MOE_ROWS, d // 2), U32), pltpu.SemaphoreType.DMA((2,)),
                            pltpu.VMEM((d, ff), F32), pltpu.VMEM((d, ff), F32), pltpu.VMEM((ff, d), F32),
                            pltpu.VMEM((d, ff), BF16), pltpu.VMEM((d, ff), BF16), pltpu.VMEM((ff, d), BF16),
                            pltpu.SemaphoreType.DMA((3,))]),
        out_shape=jax.ShapeDtypeStruct((n_blocks * MOE_ROWS, d), F32),
        compiler_params=_cparams(("arbitrary",)),
        name="experts",
    )(blk_e, nused, tok, tok, src, wg, wu, wd)


def _combine_kernel(dest_ref, nxt_ref, x1_ref, wt_ref, mod_ref, fw_ref, y_ref, o_ref, ybuf, sem,
                    *, tiles_per_seq, d, lat):
    i = pl.program_id(0)
    tm = x1_ref.shape[0]

    def start_gather(d_ref, slot):
        def body(r, c):
            for kk in range(TOP_K):
                pltpu.make_async_copy(y_ref.at[d_ref[kk, r]], ybuf.at[slot, kk, r], sem.at[slot]).start()
            return c
        lax.fori_loop(0, tm, body, 0, unroll=8)

    @pl.when(i == 0)
    def _():
        start_gather(dest_ref, 0)

    @pl.when(i + 1 < pl.num_programs(0))
    def _():
        start_gather(nxt_ref, (i + 1) % 2)

    slot = i % 2
    row = (1 + i // tiles_per_seq) if lat else 0
    gate2 = mod_ref[pl.ds(row, 1), 5 * d:6 * d]
    wt = wt_ref[...]
    w0 = _lane_tile(_row_to_col(wt[0:1], tm), d // LANES)
    w1 = _lane_tile(_row_to_col(wt[1:2], tm), d // LANES)
    for kk in range(TOP_K):
        pltpu.make_async_copy(y_ref.at[pl.ds(0, tm)], ybuf.at[slot, kk], sem.at[slot]).wait()
    x = x1_ref[...] + gate2 * (w0 * ybuf[slot, 0] + w1 * ybuf[slot, 1])
    ms = jnp.mean(x * x, axis=-1, keepdims=True)
    o_ref[...] = x * lax.rsqrt(ms + EPS) * fw_ref[...]


def _combine(dest, x1, rwt, mod, final_w, y_rows, *, row0, n_tok, seq_len, lat):
    d = x1.shape[1]
    tm = 256
    tile0 = row0 // tm
    last = tile0 + n_tok // tm - 1
    kern = functools.partial(_combine_kernel, tiles_per_seq=seq_len // tm, d=d, lat=lat)
    return pl.pallas_call(
        kern,
        grid=(n_tok // tm,),
        in_specs=[pl.BlockSpec((SUBLANES, tm), lambda i: (0, tile0 + i), memory_space=pltpu.SMEM),
                  pl.BlockSpec((SUBLANES, tm), lambda i: (0, jnp.minimum(tile0 + i + 1, last)),
                               memory_space=pltpu.SMEM),
                  pl.BlockSpec((tm, d), lambda i: (tile0 + i, 0)),
                  pl.BlockSpec((SUBLANES, tm), lambda i: (0, tile0 + i)),
                  pl.BlockSpec(mod.shape, lambda i: (0, 0)),
                  pl.BlockSpec((1, d), lambda i: (0, 0)),
                  pl.BlockSpec(memory_space=pl.ANY)],
        out_specs=pl.BlockSpec((tm, d), lambda i: (i, 0)),
        out_shape=jax.ShapeDtypeStruct((n_tok, d), F32),
        scratch_shapes=[pltpu.VMEM((2, TOP_K, tm, d), F32), pltpu.SemaphoreType.DMA((2,))],
        compiler_params=_cparams(("arbitrary",)),
        name="combine_lat" if lat else "combine_ctx",
    )(dest, dest, x1, rwt, mod, final_w.reshape(1, d), y_rows)


def _gate_layout(w_gates, b_gates, heads):
    d = w_gates.shape[0]
    w = w_gates.reshape(d, 4, heads).transpose(0, 2, 1)
    w = jnp.pad(w, ((0, 0), (0, 0), (0, SUBLANES - 4))).reshape(d, heads * SUBLANES)
    b = b_gates.reshape(4, heads).T
    b = jnp.pad(b, ((0, 0), (0, SUBLANES - 4))).reshape(1, heads * SUBLANES)
    padl = LANES - heads * SUBLANES
    return jnp.pad(w, ((0, 0), (0, padl))), jnp.pad(b, ((0, 0), (0, padl)))


def kernel(x_prompt, x_sample, state_mlstm_c, state_mlstm_n, state_mlstm_m, state_rglru_h, c, c_ctx, w_ada, b_ada,
           norm1_w, w_in, b_gates, conv_w, conv_b, rg_wa, rg_ba, rg_wx, rg_bx, rg_lambda, mlstm_norm_w, w_out,
           norm2_w, router_group_w, router_group_b, router_expert_w, router_expert_b, expert_w_gate, expert_w_up,
           expert_w_down, final_norm_w):
    n_req, seq, d = x_prompt.shape
    n_lat, lat_seq, _ = x_sample.shape
    depth = w_in.shape[0]
    assert depth == 1, "the token-axis plumbing below is written for the single-layer trunk"
    heads, dk, dv = state_mlstm_c.shape[3:]
    rw = state_rglru_h.shape[-1]
    nblk = rg_wa.shape[2]
    assert rw // nblk == LANES
    n_groups, epg = router_expert_w.shape[1], router_expert_w.shape[3]
    n_exp = n_groups * epg
    qk, mw = heads * dk, heads * dv
    nc, nl = n_req * seq, n_lat * lat_seq
    nt = nc + nl
    assert nc % lat_seq == 0 and n_lat + 1 <= SUBLANES
    l = 0

    xp = x_prompt.reshape(nc, d)
    xs = x_sample.reshape(nl, d)
    cvec = jnp.zeros((SUBLANES, d), F32).at[0].set(c_ctx).at[1:1 + n_lat].set(c)
    mod = _ada(cvec, w_ada[l], b_ada[l])

    w = w_in[l]
    g0 = 2 * qk + 2 * mw
    wgate, bgate = _gate_layout(w[:, g0:g0 + 4 * heads], b_gates[l], heads)
    w_cat = jnp.concatenate([w[:, :qk], w[:, 2 * qk:g0], wgate, w[:, g0 + 4 * heads:]], axis=1).astype(BF16)
    w_kt = w[:, qk:2 * qk].T.astype(BF16)
    q, kt, v, o, gt, xr, xg = _in_proj(xp, xs, mod, norm1_w[l], w_cat, w_kt, bgate, lat_seq=lat_seq, heads=heads,
                                       dk=dk, dv=dv, rw=rw)

    mkw = dict(heads=heads, dk=dk, dv=dv)
    ya_c, new_c, new_n, new_m = _mlstm(q, kt, v, o, gt, mlstm_norm_w[l], row0=0, n_seq=n_req, t_len=seq,
                                       emit_state=True, **mkw)
    (ya_l,) = _mlstm(q, kt, v, o, gt, mlstm_norm_w[l], row0=nc, n_seq=n_lat, t_len=lat_seq,
                     state=(state_mlstm_c[:, l], state_mlstm_n[:, l], state_mlstm_m[:, l]), **mkw)

    wg = jnp.concatenate([rg_wa[l, 0], rg_wx[l, 0], rg_wa[l, 1], rg_wx[l, 1]], axis=-1).astype(BF16)
    bg = jnp.concatenate([rg_ba[l, 0].reshape(nblk, 1, LANES), rg_bx[l, 0].reshape(nblk, 1, LANES),
                          rg_ba[l, 1].reshape(nblk, 1, LANES), rg_bx[l, 1].reshape(nblk, 1, LANES)], axis=-1)
    rargs = (xr, xg, conv_w[l], conv_b[l], wg, bg, rg_lambda[l])
    yb_c, new_h = _rglru(*rargs, row0=0, n_seq=n_req, t_len=seq, seg=seq, emit_state=True)
    (yb_l,) = _rglru(*rargs, row0=nc, n_seq=n_lat, t_len=lat_seq, seg=GRID_W, state=state_rglru_h[:, l])

    r_rows = -(-(SUBLANES + n_exp) // 16) * 16
    wr = jnp.zeros((r_rows, d), F32)
    wr = wr.at[0:n_groups].set(router_group_w[l].T)
    wr = wr.at[SUBLANES:SUBLANES + n_exp].set(router_expert_w[l].transpose(0, 2, 1).reshape(n_exp, d)).astype(BF16)
    br = jnp.zeros((r_rows, LANES), F32)
    br = br.at[0:n_groups, 0].set(router_group_b[l])
    br = br.at[SUBLANES:SUBLANES + n_exp, 0].set(router_expert_b[l].reshape(n_exp))
    x1, hp, ridx, rwt = _out_proj(ya_c, ya_l, yb_c, yb_l, xp, xs, mod, norm2_w[l], w_out[l].astype(BF16), wr, br,
                                  lat_seq=lat_seq, n_groups=n_groups, epg=epg)

    n_blocks = (nt * TOP_K) // MOE_ROWS + n_exp
    dest, row_tok, blk_e, nused = _routing(ridx, nt, n_exp, n_blocks, zero_row=nt)
    only_layer = lambda a: a.reshape(a.shape[1:])
    y_rows = _experts(blk_e, nused, row_tok, hp, only_layer(expert_w_gate), only_layer(expert_w_up),
                      only_layer(expert_w_down), n_blocks)

    y_prompt = _combine(dest, x1, rwt, mod, final_norm_w, y_rows, row0=0, n_tok=nc, seq_len=seq, lat=False)
    y_sample = _combine(dest, x1, rwt, mod, final_norm_w, y_rows, row0=nc, n_tok=nl, seq_len=lat_seq, lat=True)

    return (y_prompt.reshape(n_req, seq, d), y_sample.reshape(n_lat, lat_seq, d),
            new_c[:, None], new_n.reshape(n_req, 1, 2, heads, dk), new_m.reshape(n_req, 1, 2, heads),
            new_h[:, None])
```

```python
import functools

import jax
import jax.numpy as jnp
import numpy as np
from jax import lax
from jax.experimental import pallas as pl
from jax.experimental.pallas import tpu as pltpu

F32 = jnp.float32
BF16 = jnp.bfloat16
I32 = jnp.int32
U32 = jnp.uint32

EPS = 1e-6
GRID_W = 64
CONV_LEFT = 2
RGLRU_C = 8.0
TOP_K = 2
LANES = 128
SUBLANES = 8
MLSTM_L = 128
NEG = -1e30
VMEM_LIMIT = 56 * 1024 * 1024

_HIGHEST = lax.Precision.HIGHEST


def _cparams(sem, vmem=VMEM_LIMIT):
    return pltpu.CompilerParams(dimension_semantics=sem, vmem_limit_bytes=vmem)


def _sigmoid(x):
    return 0.5 * jnp.tanh(0.5 * x) + 0.5


def _row_to_col(r, n):
    return jnp.broadcast_to(r, (LANES, n)).T


def _lane_tile(x, reps):
    return x if reps == 1 else jnp.concatenate([x] * reps, axis=1)


def _ada_kernel(c_ref, w_ref, b_ref, o_ref):
    c = c_ref[...]
    s = (c * _sigmoid(c)).astype(BF16)
    o_ref[...] = jnp.dot(s, w_ref[...].astype(BF16), preferred_element_type=F32) + b_ref[...]


def _ada(cvec, w_ada, b_ada):
    d, n = w_ada.shape
    tn = 1024 if n % 1024 == 0 else 512
    assert n % tn == 0
    return pl.pallas_call(
        _ada_kernel,
        grid=(n // tn,),
        in_specs=[pl.BlockSpec((SUBLANES, d), lambda j: (0, 0)),
                  pl.BlockSpec((d, tn), lambda j: (0, j)),
                  pl.BlockSpec((1, tn), lambda j: (0, j))],
        out_specs=pl.BlockSpec((SUBLANES, tn), lambda j: (0, j)),
        out_shape=jax.ShapeDtypeStruct((SUBLANES, n), F32),
        compiler_params=_cparams(("arbitrary",)),
        name="ada",
    )(cvec, w_ada, b_ada.reshape(1, n))


def _modulated_norm(x, w, shift, scale):
    ms = jnp.mean(x * x, axis=-1, keepdims=True)
    return (x * lax.rsqrt(ms + EPS) * w) * (1.0 + scale) + shift


def _mod_row(i, nctx_tiles, tiles_per_lat):
    return jnp.where(i < nctx_tiles, 0, 1 + (i - nctx_tiles) // tiles_per_lat)


def _inproj_kernel(xp_ref, xs_ref, mod_ref, n1_ref, w_ref, wkt_ref, bg_ref,
                   q_ref, kt_ref, v_ref, o_ref, gt_ref, xr_ref, xg_ref,
                   *, nctx_tiles, tiles_per_lat, d, qk, mw, rw, gh, qscale):
    i = pl.program_id(0)
    x = jnp.where(i < nctx_tiles, xp_ref[...], xs_ref[...])
    row = _mod_row(i, nctx_tiles, tiles_per_lat)
    shift = mod_ref[pl.ds(row, 1), 0:d]
    scale = mod_ref[pl.ds(row, 1), d:2 * d]
    hb = _modulated_norm(x, n1_ref[...], shift, scale).astype(BF16)

    def proj(c0, width):
        return jnp.dot(hb, w_ref[:, c0:c0 + width], preferred_element_type=F32)

    c0 = 0
    q_ref[...] = (proj(c0, qk) * qscale).astype(BF16); c0 += qk
    kt_ref[...] = lax.dot_general(wkt_ref[...], hb, (((1,), (1,)), ((), ())),
                                  preferred_element_type=F32).astype(BF16)
    v_ref[...] = proj(c0, mw).astype(BF16); c0 += mw
    o_ref[...] = proj(c0, mw); c0 += mw
    zg = proj(c0, LANES) + bg_ref[...]; c0 += LANES
    lane = lax.broadcasted_iota(I32, zg.shape, 1)
    log_sig = jnp.minimum(zg, 0.0) - jnp.log1p(jnp.exp(-jnp.abs(zg)))
    zg = jnp.where(lane % 2 == 1, log_sig, zg)
    gt_ref[...] = zg.T[0:gh, :]
    xr_ref[...] = proj(c0, rw); c0 += rw
    xg_ref[...] = proj(c0, rw)


def _in_proj(xp, xs, mod, norm1_w, w_cat, w_kt, bg, *, lat_seq, heads, dk, dv, rw):
    nc, d = xp.shape
    nl = xs.shape[0]
    nt = nc + nl
    tm = 256
    qk, mw, gh = heads * dk, heads * dv, heads * SUBLANES
    nctx_tiles = nc // tm
    kern = functools.partial(_inproj_kernel, nctx_tiles=nctx_tiles, tiles_per_lat=lat_seq // tm, d=d, qk=qk, mw=mw,
                             rw=rw, gh=gh, qscale=dk ** -0.5)
    row = lambda i: (i, 0)
    const = lambda i: (0, 0)
    return pl.pallas_call(
        kern,
        grid=(nt // tm,),
        in_specs=[pl.BlockSpec((tm, d), lambda i: (jnp.minimum(i, nctx_tiles - 1), 0)),
                  pl.BlockSpec((tm, d), lambda i: (jnp.maximum(i - nctx_tiles, 0), 0)),
                  pl.BlockSpec(mod.shape, const),
                  pl.BlockSpec((1, d), const),
                  pl.BlockSpec(w_cat.shape, const, pipeline_mode=pl.Buffered(1)),
                  pl.BlockSpec(w_kt.shape, const, pipeline_mode=pl.Buffered(1)),
                  pl.BlockSpec((1, LANES), const)],
        out_specs=[pl.BlockSpec((tm, qk), row), pl.BlockSpec((qk, tm), lambda i: (0, i)), pl.BlockSpec((tm, mw), row),
                   pl.BlockSpec((tm, mw), row), pl.BlockSpec((gh, tm), lambda i: (0, i)),
                   pl.BlockSpec((tm, rw), row), pl.BlockSpec((tm, rw), row)],
        out_shape=[jax.ShapeDtypeStruct((nt, qk), BF16), jax.ShapeDtypeStruct((qk, nt), BF16),
                   jax.ShapeDtypeStruct((nt, mw), BF16), jax.ShapeDtypeStruct((nt, mw), F32),
                   jax.ShapeDtypeStruct((gh, nt), F32),
                   jax.ShapeDtypeStruct((nt, rw), F32), jax.ShapeDtypeStruct((nt, rw), F32)],
        compiler_params=_cparams(("arbitrary",)),
        name="in_proj",
    )(xp, xs, mod, norm1_w.reshape(1, d), w_cat, w_kt, bg)


def _mlstm_kernel(*refs, t_len, dk, dv, has_state, emit_state):
    it = iter(refs)
    q_ref, kt_ref, v_ref, o_ref, gt_ref, nw_ref, tri_ref = (next(it) for _ in range(7))
    if has_state:
        c0_ref, n0_ref, m0_ref = (next(it) for _ in range(3))
    ya_ref = next(it)
    if emit_state:
        cn_ref, nn_ref, mn_ref = (next(it) for _ in range(3))
    hf_scr, hb_scr, c_scr, ma_scr, p_scr, w_scr, em_scr, kw_scr, dm_scr = (next(it) for _ in range(9))
    ln = MLSTM_L
    nchunks = t_len // ln
    assert nchunks % 2 == 0 and ln == LANES and dk == LANES
    h_scr = (hf_scr, hb_scr)
    ext = dv + LANES

    for d in range(2):
        if has_state:
            c_scr[d, :, 0:dv] = c0_ref[d]
            c_scr[d, :, dv:ext] = _row_to_col(n0_ref[d], dk)
            ma_scr[d] = m0_ref[d]
        else:
            c_scr[d] = jnp.zeros((dk, ext), F32)
            ma_scr[d] = jnp.zeros((1, 1), F32)

    def chunk_start(d, j):
        return pl.multiple_of((j if d == 0 else nchunks - 1 - j) * ln, ln)

    def stage_a(d, j, slot):
        t0 = chunk_start(d, j)
        q = q_ref[pl.ds(t0, ln), :]
        kt = kt_ref[:, pl.ds(t0, ln)]
        g8 = gt_ref[:, pl.ds(t0, ln)]
        cum8 = jnp.dot(g8, tri_ref[d], precision=_HIGHEST, preferred_element_type=F32)
        valid = tri_ref[1 - d] > 0.5
        li = g8[2 * d:2 * d + 1]
        lf = g8[2 * d + 1:2 * d + 2]
        cum_row = cum8[2 * d + 1:2 * d + 2]
        total = jnp.sum(lf, axis=1, keepdims=True)
        a_row = li - cum_row
        cum_col = _row_to_col(cum_row, ln)
        m_prev = ma_scr[d]
        dmat = jnp.where(valid, cum_col + a_row, NEG)
        inter = cum_col + m_prev
        m_t = jnp.maximum(inter, jnp.max(dmat, axis=1, keepdims=True))
        s = jnp.dot(q, kt, preferred_element_type=F32)
        bank, par = slot
        p_scr[d, bank, par] = (s * jnp.exp(dmat - m_t)).astype(BF16)
        w_scr[d, bank, par] = jnp.exp(inter - m_t)
        em_scr[d, bank, par] = jnp.exp(-m_t)
        g_row = total + a_row
        m_new = jnp.maximum(total + m_prev, jnp.max(g_row, axis=1, keepdims=True))
        kw_scr[d, bank, par] = (kt.astype(F32) * jnp.exp(g_row - m_new)).astype(BF16)
        decay = jnp.exp(total + m_prev - m_new)
        rid = lax.broadcasted_iota(I32, (SUBLANES, LANES), 0)
        dm_scr[d, bank, par] = jnp.where(rid == 0, decay, m_new)
        ma_scr[d] = m_new

    def stage_b(d, j, slot):
        t0 = chunk_start(d, j)
        q = q_ref[pl.ds(t0, ln), :]
        v_ext = jnp.concatenate([v_ref[pl.ds(t0, ln), :], jnp.ones((ln, LANES), BF16)], axis=1)
        c_st = c_scr[d]
        bank, par = slot
        full = (jnp.dot(p_scr[d, bank, par], v_ext, preferred_element_type=F32)
                + _lane_tile(w_scr[d, bank, par], ext // LANES) * jnp.dot(q, c_st.astype(BF16),
                                                                         preferred_element_type=F32))
        inv = 1.0 / jnp.maximum(jnp.abs(full[:, dv:ext]), em_scr[d, bank, par])
        h_scr[d][pl.ds(t0, ln), :] = full[:, 0:dv] * _lane_tile(inv, dv // LANES)
        decay = dm_scr[d, bank, par, 0:1, 0:1]
        c_scr[d] = decay * c_st + jnp.dot(kw_scr[d, bank, par], v_ext, preferred_element_type=F32)

    npairs = nchunks // 2

    def pair_a(jj, bank):
        for par in range(2):
            for d in range(2):
                stage_a(d, 2 * jj + par, (bank, par))

    def pair_b(jj, bank):
        for par in range(2):
            for d in range(2):
                stage_b(d, 2 * jj + par, (bank, par))

    pair_a(0, 0)

    def body(jj, carry):
        bank = jj % 2
        pair_b(jj, bank)
        pair_a(jj + 1, 1 - bank)
        return carry

    lax.fori_loop(0, npairs - 1, body, 0)
    pair_b(npairs - 1, (npairs - 1) % 2)

    def finish(j, carry):
        t0 = pl.multiple_of(j * ln, ln)
        hs = hf_scr[pl.ds(t0, ln), :] + hb_scr[pl.ds(t0, ln), :]
        ms = jnp.mean(hs * hs, axis=1, keepdims=True)
        y = hs * lax.rsqrt(ms + EPS) * nw_ref[...]
        ya_ref[pl.ds(t0, ln), :] = (_sigmoid(o_ref[pl.ds(t0, ln), :]) * y).astype(BF16)
        return carry

    lax.fori_loop(0, nchunks, finish, 0)

    if emit_state:
        for d in range(2):
            cn_ref[d] = c_scr[d, :, 0:dv]
            nn_ref[d] = c_scr[d, :, dv:ext].T[0:1, :]
            mn_ref[d] = dm_scr[d, (npairs - 1) % 2, 1, 1:2, 0:1]


def _mlstm_tri():
    r = np.arange(MLSTM_L)
    fwd = (r[:, None] <= r[None, :]).astype(np.float32)
    return jnp.asarray(np.stack([fwd, fwd.T]))


def _mlstm(q, kt, v, o, gt, norm_w, *, row0, n_seq, t_len, heads, dk, dv, state=None, emit_state=False):
    blk0 = row0 // t_len
    tok = lambda s, h: (blk0 + s, h)
    in_specs = [pl.BlockSpec((t_len, dk), tok), pl.BlockSpec((dk, t_len), lambda s, h: (h, blk0 + s)),
                pl.BlockSpec((t_len, dv), tok), pl.BlockSpec((t_len, dv), tok),
                pl.BlockSpec((SUBLANES, t_len), lambda s, h: (h, blk0 + s)),
                pl.BlockSpec((None, 1, dv), lambda s, h: (h, 0, 0)),
                pl.BlockSpec((2, MLSTM_L, MLSTM_L), lambda s, h: (0, 0, 0))]
    args = [q, kt, v, o, gt, norm_w.reshape(heads, 1, dv), _mlstm_tri()]
    if state is not None:
        c0, n0, m0 = state
        in_specs += [pl.BlockSpec((None, 2, None, dk, dv), lambda s, h: (s, 0, h, 0, 0)),
                     pl.BlockSpec((None, 2, None, 1, dk), lambda s, h: (s, 0, h, 0, 0)),
                     pl.BlockSpec((None, 2, None, 1, 1), lambda s, h: (s, 0, h, 0, 0))]
        args += [c0, n0.reshape(n_seq, 2, heads, 1, dk), m0.reshape(n_seq, 2, heads, 1, 1)]
    out_specs = [pl.BlockSpec((t_len, dv), lambda s, h: (s, h))]
    out_shape = [jax.ShapeDtypeStruct((n_seq * t_len, heads * dv), BF16)]
    if emit_state:
        out_specs += [pl.BlockSpec((None, 2, None, dk, dv), lambda s, h: (s, 0, h, 0, 0)),
                      pl.BlockSpec((None, 2, None, 1, dk), lambda s, h: (s, 0, h, 0, 0)),
                      pl.BlockSpec((None, 2, None, 1, 1), lambda s, h: (s, 0, h, 0, 0))]
        out_shape += [jax.ShapeDtypeStruct((n_seq, 2, heads, dk, dv), F32),
                      jax.ShapeDtypeStruct((n_seq, 2, heads, 1, dk), F32),
                      jax.ShapeDtypeStruct((n_seq, 2, heads, 1, 1), F32)]
    kern = functools.partial(_mlstm_kernel, t_len=t_len, dk=dk, dv=dv, has_state=state is not None,
                             emit_state=emit_state)
    return pl.pallas_call(
        kern,
        grid=(n_seq, heads),
        in_specs=in_specs,
        out_specs=out_specs,
        out_shape=out_shape,
        scratch_shapes=[pltpu.VMEM((t_len, dv), F32), pltpu.VMEM((t_len, dv), F32),
                        pltpu.VMEM((2, dk, dv + LANES), F32), pltpu.VMEM((2, 1, 1), F32),
                        pltpu.VMEM((2, 2, 2, MLSTM_L, MLSTM_L), BF16), pltpu.VMEM((2, 2, 2, MLSTM_L, LANES), F32),
                        pltpu.VMEM((2, 2, 2, MLSTM_L, LANES), F32), pltpu.VMEM((2, 2, 2, dk, MLSTM_L), BF16),
                        pltpu.VMEM((2, 2, 2, SUBLANES, LANES), F32)],
        compiler_params=_cparams(("arbitrary", "arbitrary")),
        name="mlstm_state" if emit_state else "mlstm",
    )(*args)


def _gelu_tanh(x):
    return x * (0.5 * (1.0 + jnp.tanh(0.7978845608028654 * (x + 0.044715 * (x * x * x)))))


def _softplus(x):
    return jnp.maximum(x, 0.0) + jnp.log1p(jnp.exp(-jnp.abs(x)))


def _rglru_kernel(*refs, t_len, seg, sub, pitch, tc, has_state, emit_state):
    it = iter(refs)
    xr_ref, xg_ref, cw_ref, cb_ref, wg_ref, bg_ref, lam_ref = (next(it) for _ in range(7))
    if has_state:
        h0_ref = next(it)
    yb_ref = next(it)
    if emit_state:
        hn_ref = next(it)
    a_scr, u_scr, cin_scr = (next(it) for _ in range(3))
    nchunks = t_len // tc
    piece = min(tc, sub)
    npieces = tc // piece
    ntile = xr_ref.shape[1] // LANES
    chains = [(d, lt) for d in range(2) for lt in range(ntile)]

    def scan_rows(t0, p):
        t = t0 + p * piece
        i = t // sub
        return pl.ds(pl.multiple_of(i * pitch + (t - i * sub), SUBLANES), piece), i

    sp = _softplus(-lam_ref[...])

    def gates(c, carry):
        t0 = pl.multiple_of(c * tc, tc)
        pos = lax.broadcasted_iota(I32, (tc, LANES), 0) % seg
        for lt in range(ntile):
            cols = slice(lt * LANES, (lt + 1) * LANES)
            x = xr_ref[pl.ds(t0, tc), cols]
            xc = cb_ref[:, cols] + cw_ref[CONV_LEFT:CONV_LEFT + 1, cols] * x
            for j in range(cw_ref.shape[0]):
                off = j - CONV_LEFT
                if off == 0:
                    continue
                shifted = pltpu.roll(x, (-off) % tc, 0)
                ok = (pos >= -off) if off < 0 else (pos < seg - off)
                xc = xc + cw_ref[j:j + 1, cols] * jnp.where(ok, shifted, 0.0)
            r_all = jnp.dot(xc.astype(BF16), wg_ref[lt], preferred_element_type=F32) + bg_ref[lt]
            for d in range(2):
                r = _sigmoid(r_all[:, (2 * d) * LANES:(2 * d + 1) * LANES])
                ig = _sigmoid(r_all[:, (2 * d + 1) * LANES:(2 * d + 2) * LANES])
                a = jnp.exp((-RGLRU_C) * r * sp[d:d + 1, cols])
                u = jnp.sqrt(1.0 - a * a) * (ig * xc)
                for p in range(npieces):
                    rows, _ = scan_rows(t0, p)
                    a_scr[d, lt, rows, :] = a[p * piece:(p + 1) * piece]
                    u_scr[d, lt, rows, :] = u[p * piece:(p + 1) * piece]
        return carry

    lax.fori_loop(0, nchunks, gates, 0)

    def scan(j, carry):
        out = []
        for (d, lt), (h, p) in zip(chains, carry):
            rows = pl.ds(j if d == 0 else sub - 1 - j, SUBLANES, stride=pitch)
            a = a_scr[d, lt, rows, :]
            h = a * h + u_scr[d, lt, rows, :]
            p = a * p
            a_scr[d, lt, rows, :] = p
            u_scr[d, lt, rows, :] = h
            out.append((h, p))
        return tuple(out)

    zero = jnp.zeros((SUBLANES, LANES), F32)
    one = jnp.ones((SUBLANES, LANES), F32)
    ends = lax.fori_loop(0, sub, scan, tuple((zero, one) for _ in chains))

    for (d, lt), (h, p) in zip(chains, ends):
        cols = slice(lt * LANES, (lt + 1) * LANES)
        cin = h0_ref[d:d + 1, cols] if has_state else jnp.zeros((1, LANES), F32)
        for i in (range(SUBLANES) if d == 0 else reversed(range(SUBLANES))):
            cin_scr[d, lt, i:i + 1, :] = cin
            cin = h[i:i + 1] + p[i:i + 1] * cin
        if emit_state:
            hn_ref[d:d + 1, cols] = cin

    def finish(c, carry):
        t0 = pl.multiple_of(c * tc, tc)
        for lt in range(ntile):
            cols = slice(lt * LANES, (lt + 1) * LANES)
            for p in range(npieces):
                rows, i = scan_rows(t0, p)
                h = (u_scr[0, lt, rows, :] + a_scr[0, lt, rows, :] * cin_scr[0, lt, pl.ds(i, 1), :]
                     + u_scr[1, lt, rows, :] + a_scr[1, lt, rows, :] * cin_scr[1, lt, pl.ds(i, 1), :])
                nat = pl.ds(pl.multiple_of(t0 + p * piece, SUBLANES), piece)
                yb_ref[nat, cols] = (h * _gelu_tanh(xg_ref[nat, cols])).astype(BF16)
        return carry

    lax.fori_loop(0, nchunks, finish, 0)


def _rglru(xr, xg, conv_w, conv_b, wg, bg, lam, *, row0, n_seq, t_len, seg, state=None, emit_state=False):
    rw = xr.shape[1]
    ntile = 2
    cb = ntile * LANES
    assert rw % cb == 0
    blk0 = row0 // t_len
    sub = t_len // SUBLANES
    pitch = sub + SUBLANES
    tc = 256
    tok = lambda s, g: (blk0 + s, g)
    in_specs = [pl.BlockSpec((t_len, cb), tok), pl.BlockSpec((t_len, cb), tok),
                pl.BlockSpec((conv_w.shape[0], cb), lambda s, g: (0, g)),
                pl.BlockSpec((1, cb), lambda s, g: (0, g)),
                pl.BlockSpec((ntile, LANES, 4 * LANES), lambda s, g: (g, 0, 0)),
                pl.BlockSpec((ntile, 1, 4 * LANES), lambda s, g: (g, 0, 0)),
                pl.BlockSpec((2, cb), lambda s, g: (0, g))]
    args = [xr, xg, conv_w, conv_b.reshape(1, rw), wg, bg, lam]
    if state is not None:
        in_specs.append(pl.BlockSpec((None, 2, cb), lambda s, g: (s, 0, g)))
        args.append(state)
    out_specs = [pl.BlockSpec((t_len, cb), lambda s, g: (s, g))]
    out_shape = [jax.ShapeDtypeStruct((n_seq * t_len, rw), BF16)]
    if emit_state:
        out_specs.append(pl.BlockSpec((None, 2, cb), lambda s, g: (s, 0, g)))
        out_shape.append(jax.ShapeDtypeStruct((n_seq, 2, rw), F32))
    kern = functools.partial(_rglru_kernel, t_len=t_len, seg=seg, sub=sub, pitch=pitch, tc=tc,
                             has_state=state is not None, emit_state=emit_state)
    return pl.pallas_call(
        kern,
        grid=(n_seq, rw // cb),
        in_specs=in_specs,
        out_specs=out_specs,
        out_shape=out_shape,
        scratch_shapes=[pltpu.VMEM((2, ntile, SUBLANES * pitch, LANES), F32),
                        pltpu.VMEM((2, ntile, SUBLANES * pitch, LANES), F32),
                        pltpu.VMEM((2, ntile, SUBLANES, LANES), F32)],
        compiler_params=_cparams(("arbitrary", "arbitrary")),
        name="rglru_state" if emit_state else "rglru",
    )(*args)


def _pack_bf16_pair(lo, hi):
    def rne(x):
        b = pltpu.bitcast(x, U32)
        return (b + jnp.uint32(0x7FFF) + ((b >> 16) & jnp.uint32(1))) >> 16
    return rne(lo) | (rne(hi) << 16)


def _unpack_bf16_pair(w):
    lo = pltpu.bitcast(w << 16, F32).astype(BF16)
    hi = pltpu.bitcast(w & jnp.uint32(0xFFFF0000), F32).astype(BF16)
    return lo, hi


def _outproj_kernel(yac_ref, yal_ref, ybc_ref, ybl_ref, xp_ref, xs_ref, mod_ref, n2_ref, wo_ref, wr_ref, br_ref,
                    x1_ref, hp_ref, ridx_ref, rwt_ref,
                    *, nctx_tiles, ntok_tiles, tiles_per_lat, d, mw, n_groups, epg):
    i = pl.program_id(0)

    @pl.when(i == ntok_tiles)
    def _():
        x1_ref[...] = jnp.zeros_like(x1_ref)
        hp_ref[...] = jnp.zeros_like(hp_ref)
        ridx_ref[...] = jnp.zeros_like(ridx_ref)
        rwt_ref[...] = jnp.zeros_like(rwt_ref)

    @pl.when(i < ntok_tiles)
    def _():
        is_ctx = i < nctx_tiles
        x = jnp.where(is_ctx, xp_ref[...], xs_ref[...])
        ya = jnp.where(is_ctx, yac_ref[...], yal_ref[...])
        yb = jnp.where(is_ctx, ybc_ref[...], ybl_ref[...])
        row = _mod_row(i, nctx_tiles, tiles_per_lat)
        gate1 = mod_ref[pl.ds(row, 1), 2 * d:3 * d]
        shift2 = mod_ref[pl.ds(row, 1), 3 * d:4 * d]
        scale2 = mod_ref[pl.ds(row, 1), 4 * d:5 * d]
        y = (jnp.dot(ya, wo_ref[0:mw, :], preferred_element_type=F32)
             + jnp.dot(yb, wo_ref[mw:, :], preferred_element_type=F32))
        x1 = x + gate1 * y
        x1_ref[...] = x1
        h2 = _modulated_norm(x1, n2_ref[...], shift2, scale2)
        half = d // 2
        hp_ref[...] = _pack_bf16_pair(h2[:, :half], h2[:, half:])

        lt = lax.dot_general(wr_ref[...], h2.astype(BF16), (((1,), (1,)), ((), ())),
                             preferred_element_type=F32) + br_ref[:, 0:1]
        gidx = lax.broadcasted_iota(I32, (SUBLANES, lt.shape[1]), 0)
        gl = jnp.where(gidx < n_groups, lt[0:SUBLANES], -jnp.inf)
        gmax = jnp.max(gl, axis=0, keepdims=True)
        grp = jnp.min(jnp.where(gl == gmax, gidx, n_groups), axis=0, keepdims=True)
        p_grp = 1.0 / jnp.sum(jnp.exp(gl - gmax), axis=0, keepdims=True)
        el = lt[SUBLANES:SUBLANES + epg]
        for g in range(1, n_groups):
            el = jnp.where(grp == g, lt[SUBLANES + g * epg:SUBLANES + (g + 1) * epg], el)
        eidx = lax.broadcasted_iota(I32, el.shape, 0)
        v1 = jnp.max(el, axis=0, keepdims=True)
        i1 = jnp.min(jnp.where(el == v1, eidx, epg), axis=0, keepdims=True)
        el2 = jnp.where(eidx == i1, -jnp.inf, el)
        v2 = jnp.max(el2, axis=0, keepdims=True)
        i2 = jnp.min(jnp.where(el2 == v2, eidx, epg), axis=0, keepdims=True)
        e2 = jnp.exp(v2 - v1)
        w1 = p_grp / (1.0 + e2)
        w2 = p_grp * e2 / (1.0 + e2)
        rid = lax.broadcasted_iota(I32, ridx_ref.shape, 0)
        ridx_ref[...] = jnp.where(rid == 0, grp * epg + i1, jnp.where(rid == 1, grp * epg + i2, 0))
        rwt_ref[...] = jnp.where(rid == 0, w1, jnp.where(rid == 1, w2, 0.0))


def _out_proj(ya_c, ya_l, yb_c, yb_l, xp, xs, mod, norm2_w, w_out, wr, br, *, lat_seq, n_groups, epg):
    nc, d = xp.shape
    nl = xs.shape[0]
    nt = nc + nl
    mw = ya_c.shape[1]
    tm = 256
    nctx_tiles, ntok_tiles = nc // tm, nt // tm
    kern = functools.partial(_outproj_kernel, nctx_tiles=nctx_tiles, ntok_tiles=ntok_tiles,
                             tiles_per_lat=lat_seq // tm, d=d, mw=mw, n_groups=n_groups, epg=epg)
    ctx = lambda i: (jnp.minimum(i, nctx_tiles - 1), 0)
    lat = lambda i: (jnp.clip(i - nctx_tiles, 0, nl // tm - 1), 0)
    row = lambda i: (i, 0)
    const = lambda i: (0, 0)
    return pl.pallas_call(
        kern,
        grid=(ntok_tiles + 1,),
        in_specs=[pl.BlockSpec((tm, mw), ctx), pl.BlockSpec((tm, mw), lat),
                  pl.BlockSpec((tm, yb_c.shape[1]), ctx), pl.BlockSpec((tm, yb_c.shape[1]), lat),
                  pl.BlockSpec((tm, d), ctx), pl.BlockSpec((tm, d), lat),
                  pl.BlockSpec(mod.shape, const),
                  pl.BlockSpec((1, d), const),
                  pl.BlockSpec(w_out.shape, const, pipeline_mode=pl.Buffered(1)),
                  pl.BlockSpec(wr.shape, const),
                  pl.BlockSpec(br.shape, const)],
        out_specs=[pl.BlockSpec((tm, d), row), pl.BlockSpec((tm, d // 2), row),
                   pl.BlockSpec((SUBLANES, tm), lambda i: (0, i)), pl.BlockSpec((SUBLANES, tm), lambda i: (0, i))],
        out_shape=[jax.ShapeDtypeStruct((nt + tm, d), F32), jax.ShapeDtypeStruct((nt + tm, d // 2), U32),
                   jax.ShapeDtypeStruct((SUBLANES, nt + tm), I32), jax.ShapeDtypeStruct((SUBLANES, nt + tm), F32)],
        compiler_params=_cparams(("arbitrary",)),
        name="out_proj",
    )(ya_c, ya_l, yb_c, yb_l, xp, xs, mod, norm2_w.reshape(1, d), w_out, wr, br)


MOE_ROWS = 256


ROUTE_TILE = 512


def _rank_kernel(ridx_ref, tri_ref, rank_ref, cnt_ref, carry_scr, *, n_exp):
    @pl.when(pl.program_id(0) == 0)
    def _():
        carry_scr[...] = jnp.zeros_like(carry_scr)

    e = ridx_ref[...]
    tr = e.shape[1]
    eid = lax.broadcasted_iota(I32, (n_exp, tr), 0)
    carry = carry_scr[:, 0:1]
    ranks = []
    for kk in range(TOP_K):
        hit = eid == e[kk:kk + 1]
        cum = jnp.dot(jnp.where(hit, 1.0, 0.0).astype(BF16), tri_ref[...], preferred_element_type=F32)
        ranks.append(jnp.sum(jnp.where(hit, cum + carry, 0.0), axis=0, keepdims=True) - 1.0)
        carry = carry + cum[:, tr - 1:tr]
    carry_scr[...] = jnp.broadcast_to(carry, carry_scr.shape)
    cnt_ref[...] = jnp.broadcast_to(carry, cnt_ref.shape)
    rid = lax.broadcasted_iota(I32, rank_ref.shape, 0)
    rank_ref[...] = jnp.where(rid == 0, ranks[0], jnp.where(rid == 1, ranks[1], 0.0)).astype(I32)


def _dest_kernel(ridx_ref, rank_ref, pstart_ref, dest_ref, *, n_exp):
    e = ridx_ref[...]
    tr = e.shape[1]
    eid = lax.broadcasted_iota(I32, (n_exp, tr), 0)
    ps = pstart_ref[:, 0:1]
    rows = [jnp.sum(jnp.where(eid == e[kk:kk + 1], ps, 0.0), axis=0, keepdims=True) for kk in range(TOP_K)]
    rid = lax.broadcasted_iota(I32, dest_ref.shape, 0)
    dest_ref[...] = rank_ref[...] + jnp.where(rid == 0, rows[0], jnp.where(rid == 1, rows[1], 0.0)).astype(I32)


def _invert_kernel(dest_ref, fill_ref, rowtok_ref, sem):
    i = pl.program_id(0)
    tr = dest_ref.shape[1]

    @pl.when(i == 0)
    def _():
        fill = pltpu.make_async_copy(fill_ref, rowtok_ref, sem.at[0])
        fill.start()
        fill.wait()

    def body(r, c):
        for kk in range(TOP_K):
            rowtok_ref[dest_ref[kk, r]] = i * tr + r
        return c
    lax.fori_loop(0, tr, body, 0, unroll=16)


def _routing(ridx, n_tok, n_exp, n_blocks, zero_row):
    tr = ROUTE_TILE
    steps = n_tok // tr
    tri = jnp.asarray(np.triu(np.ones((tr, tr), np.float32)), BF16)
    tile = pl.BlockSpec((SUBLANES, tr), lambda i: (0, i))
    cnt_spec = pl.BlockSpec((n_exp, LANES), lambda i: (0, 0))
    rank, cnt = pl.pallas_call(
        functools.partial(_rank_kernel, n_exp=n_exp),
        grid=(steps,),
        in_specs=[tile, pl.BlockSpec((tr, tr), lambda i: (0, 0))],
        out_specs=[tile, cnt_spec],
        out_shape=[jax.ShapeDtypeStruct((SUBLANES, n_tok), I32), jax.ShapeDtypeStruct((n_exp, LANES), F32)],
        scratch_shapes=[pltpu.VMEM((n_exp, LANES), F32)],
        compiler_params=_cparams(("arbitrary",)),
        name="route_rank",
    )(ridx, tri)
    counts = cnt[:, 0].astype(I32)
    padded = (counts + MOE_ROWS - 1) // MOE_ROWS * MOE_ROWS
    pad_end = jnp.cumsum(padded)
    pad_start = pad_end - padded
    dest = pl.pallas_call(
        functools.partial(_dest_kernel, n_exp=n_exp),
        grid=(steps,),
        in_specs=[tile, tile, cnt_spec],
        out_specs=tile,
        out_shape=jax.ShapeDtypeStruct((SUBLANES, n_tok), I32),
        compiler_params=_cparams(("arbitrary",)),
        name="route_dest",
    )(ridx, rank, jnp.broadcast_to(pad_start.astype(F32)[:, None], (n_exp, LANES)))
    row_tok = pl.pallas_call(
        _invert_kernel,
        grid=(steps,),
        in_specs=[pl.BlockSpec((SUBLANES, tr), lambda i: (0, i), memory_space=pltpu.SMEM),
                  pl.BlockSpec(memory_space=pl.ANY)],
        out_specs=pl.BlockSpec(memory_space=pltpu.SMEM),
        out_shape=jax.ShapeDtypeStruct((n_blocks * MOE_ROWS,), I32),
        scratch_shapes=[pltpu.SemaphoreType.DMA((1,))],
        compiler_params=_cparams(("arbitrary",)),
        name="route_invert",
    )(dest, jnp.full((n_blocks * MOE_ROWS,), zero_row, I32))
    blk_row0 = jnp.arange(n_blocks, dtype=I32) * MOE_ROWS
    blk_e = jnp.minimum(jnp.sum((pad_end[None, :] <= blk_row0[:, None]).astype(I32), axis=1), n_exp - 1)
    nused = (pad_end[-1:] // MOE_ROWS).astype(I32)
    return dest, row_tok, blk_e, nused


CAST_ROWS = 256
WEIGHT_DMA_SPLIT = 8


def _expert_kernel(blk_e_ref, nused_ref, tok_ref, nxt_ref, src_ref, wg_hbm, wu_hbm, wd_hbm, y_ref,
                   xbuf, xsem, stage_g, stage_u, stage_d, bf_g, bf_u, bf_d, wsem, *, half, rows, trows):
    b = pl.program_id(0)
    nused = nused_ref[0]
    w_hbm = (wg_hbm, wu_hbm, wd_hbm)
    stage = (stage_g, stage_u, stage_d)
    wbf = (bf_g, bf_u, bf_d)

    def weight_copy(e, j):
        return pltpu.make_async_copy(w_hbm[j].at[e], stage[j], wsem.at[j])

    def start_weights(e):
        for j in range(3):
            step = stage[j].shape[0] // WEIGHT_DMA_SPLIT
            for c in range(WEIGHT_DMA_SPLIT):
                sl = pl.ds(c * step, step)
                pltpu.make_async_copy(w_hbm[j].at[e, sl, :], stage[j].at[sl, :], wsem.at[j]).start()

    def row_copy(tok, slot, r):
        src = src_ref.at[pl.ds(pl.multiple_of(tok * trows, trows), trows), :]
        return pltpu.make_async_copy(src, xbuf.at[slot, pl.ds(r * trows, trows), :], xsem.at[slot])

    def start_gather(t_ref, slot):
        for r in range(rows):
            row_copy(t_ref[0, r], slot, r).start()

    @pl.when(b >= nused)
    def _():
        y_ref[...] = jnp.zeros_like(y_ref)

    @pl.when(b < nused)
    def _():
        e = blk_e_ref[b]

        @pl.when(b == 0)
        def _():
            start_weights(e)
            def body(r, c):
                row_copy(tok_ref[0, r], 0, r).start()
                return c
            lax.fori_loop(0, rows, body, 0, unroll=8)

        @pl.when((b == 0) | (blk_e_ref[jnp.maximum(b - 1, 0)] != e))
        def _():
            for j in range(3):
                weight_copy(e, j).wait()
                n_steps = stage[j].shape[0] // CAST_ROWS

                def cast(c, carry, j=j):
                    sl = pl.ds(pl.multiple_of(c * CAST_ROWS, CAST_ROWS), CAST_ROWS)
                    wbf[j][sl, :] = stage[j][sl, :].astype(BF16)
                    return carry
                lax.fori_loop(0, n_steps, cast, 0)
            nb = lax.while_loop(lambda k: (k < nused) & (blk_e_ref[jnp.minimum(k, nused - 1)] == e),
                                lambda k: k + 1, b + 1)

            @pl.when(nb < nused)
            def _():
                start_weights(blk_e_ref[jnp.minimum(nb, nused - 1)])

        slot = b % 2
        pltpu.make_async_copy(src_ref.at[pl.ds(0, rows * trows), :], xbuf.at[slot], xsem.at[slot]).wait()

        def compute(prefetch):
            parts = [_unpack_bf16_pair(xbuf[slot, pl.ds(s, rows, stride=trows), :]) for s in range(trows)]
            lo = jnp.concatenate([p[0] for p in parts], axis=1)
            hi = jnp.concatenate([p[1] for p in parts], axis=1)
            if prefetch:
                start_gather(nxt_ref, 1 - slot)

            def up(w_ref):
                return (jnp.dot(lo, w_ref[0:half, :], preferred_element_type=F32)
                        + jnp.dot(hi, w_ref[half:, :], preferred_element_type=F32))

            g = up(wbf[0])
            h = ((g * _sigmoid(g)) * up(wbf[1])).astype(BF16)
            y_ref[...] = jnp.dot(h, wbf[2][...], preferred_element_type=F32)

        @pl.when(b + 1 < nused)
        def _():
            compute(True)

        @pl.when(b + 1 >= nused)
        def _():
            compute(False)


def _experts(blk_e, nused, row_tok, src, wg, wu, wd, n_blocks):
    n_exp, d, ff = wg.shape
    any_spec = pl.BlockSpec(memory_space=pl.ANY)
    tok = row_tok.reshape(n_blocks, 1, MOE_ROWS)
    trows = src.shape[1] // LANES
    src = src.reshape(src.shape[0] * trows, LANES)
    return pl.pallas_call(
        functools.partial(_expert_kernel, half=d // 2, rows=MOE_ROWS, trows=trows),
        grid_spec=pltpu.PrefetchScalarGridSpec(
            num_scalar_prefetch=2,
            grid=(n_blocks,),
            in_specs=[pl.BlockSpec((None, 1, MOE_ROWS), lambda b, be, nu: (b, 0, 0), memory_space=pltpu.SMEM),
                      pl.BlockSpec((None, 1, MOE_ROWS), lambda b, be, nu: (jnp.minimum(b + 1, n_blocks - 1), 0, 0),
                                   memory_space=pltpu.SMEM),
                      any_spec, any_spec, any_spec, any_spec],
            out_specs=pl.BlockSpec((MOE_ROWS, d), lambda b, be, nu: (b, 0)),
            scratch_shapes=[pltpu.VMEM((2, MOE_ROWS * trows, LANES), U32), pltpu.SemaphoreType.DMA((2,)),
                            pltpu.VMEM((d, ff), F32), pltpu.VMEM((d, ff), F32), pltpu.VMEM((ff, d), F32),
                            pltpu.VMEM((d, ff), BF16), pltpu.VMEM((d, ff), BF16), pltpu.VMEM((ff, d), BF16),
                            pltpu.SemaphoreType.DMA((3,))]),
        out_shape=jax.ShapeDtypeStruct((n_blocks * MOE_ROWS, d), F32),
        compiler_params=_cparams(("arbitrary",)),
        name="experts",
    )(blk_e, nused, tok, tok, src, wg, wu, wd)


def _combine_kernel(dest_ref, nxt_ref, x1_ref, wt_ref, mod_ref, fw_ref, y_ref, o_ref, ybuf_even, ybuf_odd, sem,
                    *, tiles_per_seq, d, lat):
    i = pl.program_id(0)
    last = pl.num_programs(0) - 1
    tm = x1_ref.shape[0]
    bufs = (ybuf_even, ybuf_odd)

    def row_copy(d_ref, par, kk, r):
        return pltpu.make_async_copy(y_ref.at[d_ref[kk, r]], bufs[par].at[kk, r], sem.at[par])

    @pl.when(i == 0)
    def _():
        def body(r, c):
            for kk in range(TOP_K):
                row_copy(dest_ref, 0, kk, r).start()
            return c
        lax.fori_loop(0, tm, body, 0, unroll=8)

    def step(par, prefetch):
        for kk in range(TOP_K):
            pltpu.make_async_copy(y_ref.at[pl.ds(0, tm)], bufs[par].at[kk], sem.at[par]).wait()
        if prefetch:
            for r in range(tm):
                for kk in range(TOP_K):
                    row_copy(nxt_ref, 1 - par, kk, r).start()
        row = (1 + i // tiles_per_seq) if lat else 0
        gate2 = mod_ref[pl.ds(row, 1), 5 * d:6 * d]
        wt = wt_ref[...]
        w0 = _lane_tile(_row_to_col(wt[0:1], tm), d // LANES)
        w1 = _lane_tile(_row_to_col(wt[1:2], tm), d // LANES)
        x = x1_ref[...] + gate2 * (w0 * bufs[par][0] + w1 * bufs[par][1])
        ms = jnp.mean(x * x, axis=-1, keepdims=True)
        o_ref[...] = x * lax.rsqrt(ms + EPS) * fw_ref[...]

    for par in range(2):
        @pl.when((i % 2 == par) & (i < last))
        def _(par=par):
            step(par, True)

        @pl.when((i % 2 == par) & (i == last))
        def _(par=par):
            step(par, False)


def _combine(dest, x1, rwt, mod, final_w, y_rows, *, row0, n_tok, seq_len, lat):
    d = x1.shape[1]
    tm = 256
    tile0 = row0 // tm
    last = tile0 + n_tok // tm - 1
    kern = functools.partial(_combine_kernel, tiles_per_seq=seq_len // tm, d=d, lat=lat)
    return pl.pallas_call(
        kern,
        grid=(n_tok // tm,),
        in_specs=[pl.BlockSpec((SUBLANES, tm), lambda i: (0, tile0 + i), memory_space=pltpu.SMEM),
                  pl.BlockSpec((SUBLANES, tm), lambda i: (0, jnp.minimum(tile0 + i + 1, last)),
                               memory_space=pltpu.SMEM),
                  pl.BlockSpec((tm, d), lambda i: (tile0 + i, 0)),
                  pl.BlockSpec((SUBLANES, tm), lambda i: (0, tile0 + i)),
                  pl.BlockSpec(mod.shape, lambda i: (0, 0)),
                  pl.BlockSpec((1, d), lambda i: (0, 0)),
                  pl.BlockSpec(memory_space=pl.ANY)],
        out_specs=pl.BlockSpec((tm, d), lambda i: (i, 0)),
        out_shape=jax.ShapeDtypeStruct((n_tok, d), F32),
        scratch_shapes=[pltpu.VMEM((TOP_K, tm, d), F32), pltpu.VMEM((TOP_K, tm, d), F32),
                        pltpu.SemaphoreType.DMA((2,))],
        compiler_params=_cparams(("arbitrary",)),
        name="combine_lat" if lat else "combine_ctx",
    )(dest, dest, x1, rwt, mod, final_w.reshape(1, d), y_rows)


def _gate_layout(w_gates, b_gates, heads):
    d = w_gates.shape[0]
    w = w_gates.reshape(d, 4, heads).transpose(0, 2, 1)
    w = jnp.pad(w, ((0, 0), (0, 0), (0, SUBLANES - 4))).reshape(d, heads * SUBLANES)
    b = b_gates.reshape(4, heads).T
    b = jnp.pad(b, ((0, 0), (0, SUBLANES - 4))).reshape(1, heads * SUBLANES)
    padl = LANES - heads * SUBLANES
    return jnp.pad(w, ((0, 0), (0, padl))), jnp.pad(b, ((0, 0), (0, padl)))


def kernel(x_prompt, x_sample, state_mlstm_c, state_mlstm_n, state_mlstm_m, state_rglru_h, c, c_ctx, w_ada, b_ada,
           norm1_w, w_in, b_gates, conv_w, conv_b, rg_wa, rg_ba, rg_wx, rg_bx, rg_lambda, mlstm_norm_w, w_out,
           norm2_w, router_group_w, router_group_b, router_expert_w, router_expert_b, expert_w_gate, expert_w_up,
           expert_w_down, final_norm_w):
    n_req, seq, d = x_prompt.shape
    n_lat, lat_seq, _ = x_sample.shape
    depth = w_in.shape[0]
    assert depth == 1, "the token-axis plumbing below is written for the single-layer trunk"
    heads, dk, dv = state_mlstm_c.shape[3:]
    rw = state_rglru_h.shape[-1]
    nblk = rg_wa.shape[2]
    assert rw // nblk == LANES
    n_groups, epg = router_expert_w.shape[1], router_expert_w.shape[3]
    n_exp = n_groups * epg
    qk, mw = heads * dk, heads * dv
    nc, nl = n_req * seq, n_lat * lat_seq
    nt = nc + nl
    assert nc % lat_seq == 0 and n_lat + 1 <= SUBLANES
    l = 0

    xp = x_prompt.reshape(nc, d)
    xs = x_sample.reshape(nl, d)
    cvec = jnp.zeros((SUBLANES, d), F32).at[0].set(c_ctx).at[1:1 + n_lat].set(c)
    mod = _ada(cvec, w_ada[l], b_ada[l])

    w = w_in[l]
    g0 = 2 * qk + 2 * mw
    wgate, bgate = _gate_layout(w[:, g0:g0 + 4 * heads], b_gates[l], heads)
    w_cat = jnp.concatenate([w[:, :qk], w[:, 2 * qk:g0], wgate, w[:, g0 + 4 * heads:]], axis=1).astype(BF16)
    w_kt = w[:, qk:2 * qk].T.astype(BF16)
    q, kt, v, o, gt, xr, xg = _in_proj(xp, xs, mod, norm1_w[l], w_cat, w_kt, bgate, lat_seq=lat_seq, heads=heads,
                                       dk=dk, dv=dv, rw=rw)

    mkw = dict(heads=heads, dk=dk, dv=dv)
    ya_c, new_c, new_n, new_m = _mlstm(q, kt, v, o, gt, mlstm_norm_w[l], row0=0, n_seq=n_req, t_len=seq,
                                       emit_state=True, **mkw)
    (ya_l,) = _mlstm(q, kt, v, o, gt, mlstm_norm_w[l], row0=nc, n_seq=n_lat, t_len=lat_seq,
                     state=(state_mlstm_c[:, l], state_mlstm_n[:, l], state_mlstm_m[:, l]), **mkw)

    wg = jnp.concatenate([rg_wa[l, 0], rg_wx[l, 0], rg_wa[l, 1], rg_wx[l, 1]], axis=-1).astype(BF16)
    bg = jnp.concatenate([rg_ba[l, 0].reshape(nblk, 1, LANES), rg_bx[l, 0].reshape(nblk, 1, LANES),
                          rg_ba[l, 1].reshape(nblk, 1, LANES), rg_bx[l, 1].reshape(nblk, 1, LANES)], axis=-1)
    rargs = (xr, xg, conv_w[l], conv_b[l], wg, bg, rg_lambda[l])
    yb_c, new_h = _rglru(*rargs, row0=0, n_seq=n_req, t_len=seq, seg=seq, emit_state=True)
    (yb_l,) = _rglru(*rargs, row0=nc, n_seq=n_lat, t_len=lat_seq, seg=GRID_W, state=state_rglru_h[:, l])

    r_rows = -(-(SUBLANES + n_exp) // 16) * 16
    wr = jnp.zeros((r_rows, d), F32)
    wr = wr.at[0:n_groups].set(router_group_w[l].T)
    wr = wr.at[SUBLANES:SUBLANES + n_exp].set(router_expert_w[l].transpose(0, 2, 1).reshape(n_exp, d)).astype(BF16)
    br = jnp.zeros((r_rows, LANES), F32)
    br = br.at[0:n_groups, 0].set(router_group_b[l])
    br = br.at[SUBLANES:SUBLANES + n_exp, 0].set(router_expert_b[l].reshape(n_exp))
    x1, hp, ridx, rwt = _out_proj(ya_c, ya_l, yb_c, yb_l, xp, xs, mod, norm2_w[l], w_out[l].astype(BF16), wr, br,
                                  lat_seq=lat_seq, n_groups=n_groups, epg=epg)

    n_blocks = (nt * TOP_K) // MOE_ROWS + n_exp
    dest, row_tok, blk_e, nused = _routing(ridx, nt, n_exp, n_blocks, zero_row=nt)
    only_layer = lambda a: a.reshape(a.shape[1:])
    y_rows = _experts(blk_e, nused, row_tok, hp, only_layer(expert_w_gate), only_layer(expert_w_up),
                      only_layer(expert_w_down), n_blocks)

    y_prompt = _combine(dest, x1, rwt, mod, final_norm_w, y_rows, row0=0, n_tok=nc, seq_len=seq, lat=False)
    y_sample = _combine(dest, x1, rwt, mod, final_norm_w, y_rows, row0=nc, n_tok=nl, seq_len=lat_seq, lat=True)

    return (y_prompt.reshape(n_req, seq, d), y_sample.reshape(n_lat, lat_seq, d),
            new_c[:, None], new_n.reshape(n_req, 1, 2, heads, dk), new_m.reshape(n_req, 1, 2, heads),
            new_h[:, None])
```

```python
import functools

import jax
import jax.numpy as jnp
import numpy as np
from jax import lax
from jax.experimental import pallas as pl
from jax.experimental.pallas import tpu as pltpu

F32 = jnp.float32
BF16 = jnp.bfloat16
I32 = jnp.int32
U32 = jnp.uint32

EPS = 1e-6
GRID_W = 64
CONV_LEFT = 2
RGLRU_C = 8.0
TOP_K = 2
LANES = 128
SUBLANES = 8
MLSTM_L = 256
NEG = -1e30
VMEM_LIMIT = 56 * 1024 * 1024

_HIGHEST = lax.Precision.HIGHEST


def _cparams(sem, vmem=VMEM_LIMIT):
    return pltpu.CompilerParams(dimension_semantics=sem, vmem_limit_bytes=vmem)


def _sigmoid(x):
    return 0.5 * jnp.tanh(0.5 * x) + 0.5


def _row_to_col(r, n):
    return jnp.broadcast_to(r, (LANES, n)).T


def _lane_tile(x, reps):
    return x if reps == 1 else jnp.concatenate([x] * reps, axis=1)


def _ada_kernel(c_ref, w_ref, b_ref, o_ref):
    c = c_ref[...]
    s = (c * _sigmoid(c)).astype(BF16)
    o_ref[...] = jnp.dot(s, w_ref[...].astype(BF16), preferred_element_type=F32) + b_ref[...]


def _ada(cvec, w_ada, b_ada):
    d, n = w_ada.shape
    tn = 1024 if n % 1024 == 0 else 512
    assert n % tn == 0
    return pl.pallas_call(
        _ada_kernel,
        grid=(n // tn,),
        in_specs=[pl.BlockSpec((SUBLANES, d), lambda j: (0, 0)),
                  pl.BlockSpec((d, tn), lambda j: (0, j)),
                  pl.BlockSpec((1, tn), lambda j: (0, j))],
        out_specs=pl.BlockSpec((SUBLANES, tn), lambda j: (0, j)),
        out_shape=jax.ShapeDtypeStruct((SUBLANES, n), F32),
        compiler_params=_cparams(("arbitrary",)),
        name="ada",
    )(cvec, w_ada, b_ada.reshape(1, n))


def _modulated_norm(x, w, shift, scale):
    ms = jnp.mean(x * x, axis=-1, keepdims=True)
    return (x * lax.rsqrt(ms + EPS) * w) * (1.0 + scale) + shift


def _mod_row(i, nctx_tiles, tiles_per_lat):
    return jnp.where(i < nctx_tiles, 0, 1 + (i - nctx_tiles) // tiles_per_lat)


def _inproj_kernel(xp_ref, xs_ref, mod_ref, n1_ref, w_ref, wkt_ref, bg_ref,
                   q_ref, kt_ref, v_ref, o_ref, gt_ref, xr_ref, xg_ref,
                   *, nctx_tiles, tiles_per_lat, d, qk, mw, rw, gh, qscale):
    i = pl.program_id(0)
    x = jnp.where(i < nctx_tiles, xp_ref[...], xs_ref[...])
    row = _mod_row(i, nctx_tiles, tiles_per_lat)
    shift = mod_ref[pl.ds(row, 1), 0:d]
    scale = mod_ref[pl.ds(row, 1), d:2 * d]
    hb = _modulated_norm(x, n1_ref[...], shift, scale).astype(BF16)

    def proj(c0, width):
        return jnp.dot(hb, w_ref[:, c0:c0 + width], preferred_element_type=F32)

    c0 = 0
    q_ref[...] = (proj(c0, qk) * qscale).astype(BF16); c0 += qk
    kt_ref[...] = lax.dot_general(wkt_ref[...], hb, (((1,), (1,)), ((), ())),
                                  preferred_element_type=F32).astype(BF16)
    v_ref[...] = proj(c0, mw).astype(BF16); c0 += mw
    o_ref[...] = proj(c0, mw); c0 += mw
    zg = proj(c0, LANES) + bg_ref[...]; c0 += LANES
    lane = lax.broadcasted_iota(I32, zg.shape, 1)
    log_sig = jnp.minimum(zg, 0.0) - jnp.log1p(jnp.exp(-jnp.abs(zg)))
    zg = jnp.where(lane % 2 == 1, log_sig, zg)
    gt_ref[...] = zg.T[0:gh, :]
    xr_ref[...] = proj(c0, rw); c0 += rw
    xg_ref[...] = proj(c0, rw)


def _in_proj(xp, xs, mod, norm1_w, w_cat, w_kt, bg, *, lat_seq, heads, dk, dv, rw):
    nc, d = xp.shape
    nl = xs.shape[0]
    nt = nc + nl
    tm = 256
    qk, mw, gh = heads * dk, heads * dv, heads * SUBLANES
    nctx_tiles = nc // tm
    kern = functools.partial(_inproj_kernel, nctx_tiles=nctx_tiles, tiles_per_lat=lat_seq // tm, d=d, qk=qk, mw=mw,
                             rw=rw, gh=gh, qscale=dk ** -0.5)
    row = lambda i: (i, 0)
    const = lambda i: (0, 0)
    return pl.pallas_call(
        kern,
        grid=(nt // tm,),
        in_specs=[pl.BlockSpec((tm, d), lambda i: (jnp.minimum(i, nctx_tiles - 1), 0)),
                  pl.BlockSpec((tm, d), lambda i: (jnp.maximum(i - nctx_tiles, 0), 0)),
                  pl.BlockSpec(mod.shape, const),
                  pl.BlockSpec((1, d), const),
                  pl.BlockSpec(w_cat.shape, const, pipeline_mode=pl.Buffered(1)),
                  pl.BlockSpec(w_kt.shape, const, pipeline_mode=pl.Buffered(1)),
                  pl.BlockSpec((1, LANES), const)],
        out_specs=[pl.BlockSpec((tm, qk), row), pl.BlockSpec((qk, tm), lambda i: (0, i)), pl.BlockSpec((tm, mw), row),
                   pl.BlockSpec((tm, mw), row), pl.BlockSpec((gh, tm), lambda i: (0, i)),
                   pl.BlockSpec((tm, rw), row), pl.BlockSpec((tm, rw), row)],
        out_shape=[jax.ShapeDtypeStruct((nt, qk), BF16), jax.ShapeDtypeStruct((qk, nt), BF16),
                   jax.ShapeDtypeStruct((nt, mw), BF16), jax.ShapeDtypeStruct((nt, mw), F32),
                   jax.ShapeDtypeStruct((gh, nt), F32),
                   jax.ShapeDtypeStruct((nt, rw), F32), jax.ShapeDtypeStruct((nt, rw), F32)],
        compiler_params=_cparams(("arbitrary",)),
        name="in_proj",
    )(xp, xs, mod, norm1_w.reshape(1, d), w_cat, w_kt, bg)


def _mlstm_kernel(*refs, t_len, dk, dv, has_state, emit_state):
    it = iter(refs)
    q_ref, kt_ref, v_ref, o_ref, gt_ref, nw_ref, tri_ref = (next(it) for _ in range(7))
    if has_state:
        c0_ref, n0_ref, m0_ref = (next(it) for _ in range(3))
    ya_ref = next(it)
    if emit_state:
        cn_ref, nn_ref, mn_ref = (next(it) for _ in range(3))
    hf_scr, hb_scr, c_scr, ma_scr, p_scr, w_scr, em_scr, kw_scr, dm_scr = (next(it) for _ in range(9))
    ln = MLSTM_L
    nchunks = t_len // ln
    assert ln % LANES == 0 and dk == LANES
    gsz = 2 if nchunks % 2 == 0 else 1
    ngroups = nchunks // gsz
    h_scr = (hf_scr, hb_scr)
    ext = dv + LANES
    lrep = ln // LANES

    for d in range(2):
        if has_state:
            c_scr[d, :, 0:dv] = c0_ref[d]
            c_scr[d, :, dv:ext] = _row_to_col(n0_ref[d], dk)
            ma_scr[d] = m0_ref[d]
        else:
            c_scr[d] = jnp.zeros((dk, ext), F32)
            ma_scr[d] = jnp.zeros((1, 1), F32)

    def chunk_start(d, j):
        return pl.multiple_of((j if d == 0 else nchunks - 1 - j) * ln, ln)

    def stage_a(d, j, slot):
        t0 = chunk_start(d, j)
        q = q_ref[pl.ds(t0, ln), :]
        kt = kt_ref[:, pl.ds(t0, ln)]
        g8 = gt_ref[:, pl.ds(t0, ln)]
        cum8 = jnp.dot(g8, tri_ref[d], precision=_HIGHEST, preferred_element_type=F32)
        valid = tri_ref[1 - d] > 0.5
        li = g8[2 * d:2 * d + 1]
        lf = g8[2 * d + 1:2 * d + 2]
        cum_row = cum8[2 * d + 1:2 * d + 2]
        total = jnp.sum(lf, axis=1, keepdims=True)
        a_row = li - cum_row
        cum_col = _row_to_col(cum_row, ln)
        m_prev = ma_scr[d]
        dmat = jnp.where(valid, _lane_tile(cum_col, lrep) + a_row, NEG)
        inter = cum_col + m_prev
        m_t = jnp.maximum(inter, jnp.max(dmat, axis=1, keepdims=True))
        s = jnp.dot(q, kt, preferred_element_type=F32)
        bank, par = slot
        p_scr[d, bank, par] = (s * jnp.exp(dmat - _lane_tile(m_t, lrep))).astype(BF16)
        w_scr[d, bank, par] = jnp.exp(inter - m_t)
        em_scr[d, bank, par] = jnp.exp(-m_t)
        g_row = total + a_row
        m_new = jnp.maximum(total + m_prev, jnp.max(g_row, axis=1, keepdims=True))
        kw_scr[d, bank, par] = (kt.astype(F32) * jnp.exp(g_row - m_new)).astype(BF16)
        decay = jnp.exp(total + m_prev - m_new)
        rid = lax.broadcasted_iota(I32, (SUBLANES, LANES), 0)
        dm_scr[d, bank, par] = jnp.where(rid == 0, decay, m_new)
        ma_scr[d] = m_new

    def stage_b(d, j, slot):
        t0 = chunk_start(d, j)
        q = q_ref[pl.ds(t0, ln), :]
        v_ext = jnp.concatenate([v_ref[pl.ds(t0, ln), :], jnp.ones((ln, LANES), BF16)], axis=1)
        c_st = c_scr[d]
        bank, par = slot
        full = (jnp.dot(p_scr[d, bank, par], v_ext, preferred_element_type=F32)
                + _lane_tile(w_scr[d, bank, par], ext // LANES) * jnp.dot(q, c_st.astype(BF16),
                                                                         preferred_element_type=F32))
        inv = 1.0 / jnp.maximum(jnp.abs(full[:, dv:ext]), em_scr[d, bank, par])
        h_scr[d][pl.ds(t0, ln), :] = full[:, 0:dv] * _lane_tile(inv, dv // LANES)
        decay = dm_scr[d, bank, par, 0:1, 0:1]
        c_scr[d] = decay * c_st + jnp.dot(kw_scr[d, bank, par], v_ext, preferred_element_type=F32)

    def group_a(jj, bank):
        for par in range(gsz):
            for d in range(2):
                stage_a(d, gsz * jj + par, (bank, par))

    def group_b(jj, bank):
        for par in range(gsz):
            for d in range(2):
                stage_b(d, gsz * jj + par, (bank, par))

    group_a(0, 0)

    def body(jj, carry):
        bank = jj % 2
        group_b(jj, bank)
        group_a(jj + 1, 1 - bank)
        return carry

    lax.fori_loop(0, ngroups - 1, body, 0)
    group_b(ngroups - 1, (ngroups - 1) % 2)

    def finish(j, carry):
        t0 = pl.multiple_of(j * ln, ln)
        hs = hf_scr[pl.ds(t0, ln), :] + hb_scr[pl.ds(t0, ln), :]
        ms = jnp.mean(hs * hs, axis=1, keepdims=True)
        y = hs * lax.rsqrt(ms + EPS) * nw_ref[...]
        ya_ref[pl.ds(t0, ln), :] = (_sigmoid(o_ref[pl.ds(t0, ln), :]) * y).astype(BF16)
        return carry

    lax.fori_loop(0, nchunks, finish, 0)

    if emit_state:
        for d in range(2):
            cn_ref[d] = c_scr[d, :, 0:dv]
            nn_ref[d] = c_scr[d, :, dv:ext].T[0:1, :]
            mn_ref[d] = dm_scr[d, (ngroups - 1) % 2, gsz - 1, 1:2, 0:1]


def _mlstm_tri():
    r = np.arange(MLSTM_L)
    fwd = (r[:, None] <= r[None, :]).astype(np.float32)
    return jnp.asarray(np.stack([fwd, fwd.T]))


def _mlstm(q, kt, v, o, gt, norm_w, *, row0, n_seq, t_len, heads, dk, dv, state=None, emit_state=False):
    blk0 = row0 // t_len
    tok = lambda s, h: (blk0 + s, h)
    in_specs = [pl.BlockSpec((t_len, dk), tok), pl.BlockSpec((dk, t_len), lambda s, h: (h, blk0 + s)),
                pl.BlockSpec((t_len, dv), tok), pl.BlockSpec((t_len, dv), tok),
                pl.BlockSpec((SUBLANES, t_len), lambda s, h: (h, blk0 + s)),
                pl.BlockSpec((None, 1, dv), lambda s, h: (h, 0, 0)),
                pl.BlockSpec((2, MLSTM_L, MLSTM_L), lambda s, h: (0, 0, 0))]
    args = [q, kt, v, o, gt, norm_w.reshape(heads, 1, dv), _mlstm_tri()]
    if state is not None:
        c0, n0, m0 = state
        in_specs += [pl.BlockSpec((None, 2, None, dk, dv), lambda s, h: (s, 0, h, 0, 0)),
                     pl.BlockSpec((None, 2, None, 1, dk), lambda s, h: (s, 0, h, 0, 0)),
                     pl.BlockSpec((None, 2, None, 1, 1), lambda s, h: (s, 0, h, 0, 0))]
        args += [c0, n0.reshape(n_seq, 2, heads, 1, dk), m0.reshape(n_seq, 2, heads, 1, 1)]
    out_specs = [pl.BlockSpec((t_len, dv), lambda s, h: (s, h))]
    out_shape = [jax.ShapeDtypeStruct((n_seq * t_len, heads * dv), BF16)]
    if emit_state:
        out_specs += [pl.BlockSpec((None, 2, None, dk, dv), lambda s, h: (s, 0, h, 0, 0)),
                      pl.BlockSpec((None, 2, None, 1, dk), lambda s, h: (s, 0, h, 0, 0)),
                      pl.BlockSpec((None, 2, None, 1, 1), lambda s, h: (s, 0, h, 0, 0))]
        out_shape += [jax.ShapeDtypeStruct((n_seq, 2, heads, dk, dv), F32),
                      jax.ShapeDtypeStruct((n_seq, 2, heads, 1, dk), F32),
                      jax.ShapeDtypeStruct((n_seq, 2, heads, 1, 1), F32)]
    kern = functools.partial(_mlstm_kernel, t_len=t_len, dk=dk, dv=dv, has_state=state is not None,
                             emit_state=emit_state)
    return pl.pallas_call(
        kern,
        grid=(n_seq, heads),
        in_specs=in_specs,
        out_specs=out_specs,
        out_shape=out_shape,
        scratch_shapes=[pltpu.VMEM((t_len, dv), F32), pltpu.VMEM((t_len, dv), F32),
                        pltpu.VMEM((2, dk, dv + LANES), F32), pltpu.VMEM((2, 1, 1), F32),
                        pltpu.VMEM((2, 2, 2, MLSTM_L, MLSTM_L), BF16), pltpu.VMEM((2, 2, 2, MLSTM_L, LANES), F32),
                        pltpu.VMEM((2, 2, 2, MLSTM_L, LANES), F32), pltpu.VMEM((2, 2, 2, dk, MLSTM_L), BF16),
                        pltpu.VMEM((2, 2, 2, SUBLANES, LANES), F32)],
        compiler_params=_cparams(("arbitrary", "arbitrary")),
        name="mlstm_state" if emit_state else "mlstm",
    )(*args)


def _gelu_tanh(x):
    return x * (0.5 * (1.0 + jnp.tanh(0.7978845608028654 * (x + 0.044715 * (x * x * x)))))


def _softplus(x):
    return jnp.maximum(x, 0.0) + jnp.log1p(jnp.exp(-jnp.abs(x)))


def _rglru_kernel(*refs, t_len, seg, sub, pitch, tc, has_state, emit_state):
    it = iter(refs)
    xr_ref, xg_ref, cw_ref, cb_ref, wg_ref, bg_ref, lam_ref = (next(it) for _ in range(7))
    if has_state:
        h0_ref = next(it)
    yb_ref = next(it)
    if emit_state:
        hn_ref = next(it)
    a_scr, u_scr, cin_scr = (next(it) for _ in range(3))
    nchunks = t_len // tc
    piece = min(tc, sub)
    npieces = tc // piece
    ntile = xr_ref.shape[1] // LANES
    chains = [(d, lt) for d in range(2) for lt in range(ntile)]

    def scan_rows(t0, p):
        t = t0 + p * piece
        i = t // sub
        return pl.ds(pl.multiple_of(i * pitch + (t - i * sub), SUBLANES), piece), i

    ka = (-0.5 * RGLRU_C * 1.4426950408889634) * _softplus(-lam_ref[...])

    def gates(c, carry):
        t0 = pl.multiple_of(c * tc, tc)
        pos = lax.broadcasted_iota(I32, (tc, LANES), 0) % seg
        for lt in range(ntile):
            cols = slice(lt * LANES, (lt + 1) * LANES)
            x = xr_ref[pl.ds(t0, tc), cols]
            xc = cb_ref[:, cols] + cw_ref[CONV_LEFT:CONV_LEFT + 1, cols] * x
            for j in range(cw_ref.shape[0]):
                off = j - CONV_LEFT
                if off == 0:
                    continue
                shifted = pltpu.roll(x, (-off) % tc, 0)
                ok = (pos >= -off) if off < 0 else (pos < seg - off)
                xc = xc + cw_ref[j:j + 1, cols] * jnp.where(ok, shifted, 0.0)
            zh = jnp.dot(xc.astype(BF16), wg_ref[lt], preferred_element_type=F32) + bg_ref[lt]
            hx = 0.5 * xc
            for d in range(2):
                kd = ka[d:d + 1, cols]
                a = jnp.exp2(jnp.tanh(zh[:, (2 * d) * LANES:(2 * d + 1) * LANES]) * kd + kd)
                igx = hx * jnp.tanh(zh[:, (2 * d + 1) * LANES:(2 * d + 2) * LANES]) + hx
                u = jnp.sqrt(1.0 - a * a) * igx
                for p in range(npieces):
                    rows, _ = scan_rows(t0, p)
                    a_scr[d, lt, rows, :] = a[p * piece:(p + 1) * piece]
                    u_scr[d, lt, rows, :] = u[p * piece:(p + 1) * piece]
        return carry

    lax.fori_loop(0, nchunks, gates, 0)

    def scan(j, carry):
        out = []
        for (d, lt), (h, p) in zip(chains, carry):
            rows = pl.ds(j if d == 0 else sub - 1 - j, SUBLANES, stride=pitch)
            a = a_scr[d, lt, rows, :]
            h = a * h + u_scr[d, lt, rows, :]
            p = a * p
            a_scr[d, lt, rows, :] = p
            u_scr[d, lt, rows, :] = h
            out.append((h, p))
        return tuple(out)

    zero = jnp.zeros((SUBLANES, LANES), F32)
    one = jnp.ones((SUBLANES, LANES), F32)
    ends = lax.fori_loop(0, sub, scan, tuple((zero, one) for _ in chains), unroll=2)

    for (d, lt), (h, p) in zip(chains, ends):
        cols = slice(lt * LANES, (lt + 1) * LANES)
        cin = h0_ref[d:d + 1, cols] if has_state else jnp.zeros((1, LANES), F32)
        for i in (range(SUBLANES) if d == 0 else reversed(range(SUBLANES))):
            cin_scr[d, lt, i:i + 1, :] = cin
            cin = h[i:i + 1] + p[i:i + 1] * cin
        if emit_state:
            hn_ref[d:d + 1, cols] = cin

    def finish(c, carry):
        t0 = pl.multiple_of(c * tc, tc)
        for lt in range(ntile):
            cols = slice(lt * LANES, (lt + 1) * LANES)
            for p in range(npieces):
                rows, i = scan_rows(t0, p)
                h = (u_scr[0, lt, rows, :] + a_scr[0, lt, rows, :] * cin_scr[0, lt, pl.ds(i, 1), :]
                     + u_scr[1, lt, rows, :] + a_scr[1, lt, rows, :] * cin_scr[1, lt, pl.ds(i, 1), :])
                nat = pl.ds(pl.multiple_of(t0 + p * piece, SUBLANES), piece)
                yb_ref[nat, cols] = (h * _gelu_tanh(xg_ref[nat, cols])).astype(BF16)
        return carry

    lax.fori_loop(0, nchunks, finish, 0)


def _rglru(xr, xg, conv_w, conv_b, wg, bg, lam, *, row0, n_seq, t_len, seg, state=None, emit_state=False):
    rw = xr.shape[1]
    ntile = 2
    cb = ntile * LANES
    assert rw % cb == 0
    blk0 = row0 // t_len
    sub = t_len // SUBLANES
    pitch = sub + SUBLANES
    tc = 256
    tok = lambda s, g: (blk0 + s, g)
    in_specs = [pl.BlockSpec((t_len, cb), tok), pl.BlockSpec((t_len, cb), tok),
                pl.BlockSpec((conv_w.shape[0], cb), lambda s, g: (0, g)),
                pl.BlockSpec((1, cb), lambda s, g: (0, g)),
                pl.BlockSpec((ntile, LANES, 4 * LANES), lambda s, g: (g, 0, 0)),
                pl.BlockSpec((ntile, 1, 4 * LANES), lambda s, g: (g, 0, 0)),
                pl.BlockSpec((2, cb), lambda s, g: (0, g))]
    args = [xr, xg, conv_w, conv_b.reshape(1, rw), wg, bg, lam]
    if state is not None:
        in_specs.append(pl.BlockSpec((None, 2, cb), lambda s, g: (s, 0, g)))
        args.append(state)
    out_specs = [pl.BlockSpec((t_len, cb), lambda s, g: (s, g))]
    out_shape = [jax.ShapeDtypeStruct((n_seq * t_len, rw), BF16)]
    if emit_state:
        out_specs.append(pl.BlockSpec((None, 2, cb), lambda s, g: (s, 0, g)))
        out_shape.append(jax.ShapeDtypeStruct((n_seq, 2, rw), F32))
    kern = functools.partial(_rglru_kernel, t_len=t_len, seg=seg, sub=sub, pitch=pitch, tc=tc,
                             has_state=state is not None, emit_state=emit_state)
    return pl.pallas_call(
        kern,
        grid=(n_seq, rw // cb),
        in_specs=in_specs,
        out_specs=out_specs,
        out_shape=out_shape,
        scratch_shapes=[pltpu.VMEM((2, ntile, SUBLANES * pitch, LANES), F32),
                        pltpu.VMEM((2, ntile, SUBLANES * pitch, LANES), F32),
                        pltpu.VMEM((2, ntile, SUBLANES, LANES), F32)],
        compiler_params=_cparams(("arbitrary", "arbitrary")),
        name="rglru_state" if emit_state else "rglru",
    )(*args)


def _pack_bf16_pair(lo, hi):
    def rne(x):
        b = pltpu.bitcast(x, U32)
        return (b + jnp.uint32(0x7FFF) + ((b >> 16) & jnp.uint32(1))) >> 16
    return rne(lo) | (rne(hi) << 16)


def _unpack_bf16_pair(w):
    lo = pltpu.bitcast(w << 16, F32).astype(BF16)
    hi = pltpu.bitcast(w & jnp.uint32(0xFFFF0000), F32).astype(BF16)
    return lo, hi


def _outproj_kernel(yac_ref, yal_ref, ybc_ref, ybl_ref, xp_ref, xs_ref, mod_ref, n2_ref, wo_ref, wr_ref, br_ref,
                    x1_ref, hp_ref, ridx_ref, rwt_ref,
                    *, nctx_tiles, ntok_tiles, tiles_per_lat, d, mw, n_groups, epg):
    i = pl.program_id(0)

    @pl.when(i == ntok_tiles)
    def _():
        x1_ref[...] = jnp.zeros_like(x1_ref)
        hp_ref[...] = jnp.zeros_like(hp_ref)
        ridx_ref[...] = jnp.zeros_like(ridx_ref)
        rwt_ref[...] = jnp.zeros_like(rwt_ref)

    @pl.when(i < ntok_tiles)
    def _():
        is_ctx = i < nctx_tiles
        x = jnp.where(is_ctx, xp_ref[...], xs_ref[...])
        ya = jnp.where(is_ctx, yac_ref[...], yal_ref[...])
        yb = jnp.where(is_ctx, ybc_ref[...], ybl_ref[...])
        row = _mod_row(i, nctx_tiles, tiles_per_lat)
        gate1 = mod_ref[pl.ds(row, 1), 2 * d:3 * d]
        shift2 = mod_ref[pl.ds(row, 1), 3 * d:4 * d]
        scale2 = mod_ref[pl.ds(row, 1), 4 * d:5 * d]
        y = (jnp.dot(ya, wo_ref[0:mw, :], preferred_element_type=F32)
             + jnp.dot(yb, wo_ref[mw:, :], preferred_element_type=F32))
        x1 = x + gate1 * y
        x1_ref[...] = x1
        h2 = _modulated_norm(x1, n2_ref[...], shift2, scale2)
        half = d // 2
        packed = _pack_bf16_pair(h2[:, :half], h2[:, half:])
        trows = half // LANES
        for s in range(trows):
            hp_ref[pl.ds(s, x.shape[0], stride=trows), :] = packed[:, s * LANES:(s + 1) * LANES]

        lt = lax.dot_general(wr_ref[...], h2.astype(BF16), (((1,), (1,)), ((), ())),
                             preferred_element_type=F32) + br_ref[:, 0:1]
        gidx = lax.broadcasted_iota(I32, (SUBLANES, lt.shape[1]), 0)
        gl = jnp.where(gidx < n_groups, lt[0:SUBLANES], -jnp.inf)
        gmax = jnp.max(gl, axis=0, keepdims=True)
        grp = jnp.min(jnp.where(gl == gmax, gidx, n_groups), axis=0, keepdims=True)
        p_grp = 1.0 / jnp.sum(jnp.exp(gl - gmax), axis=0, keepdims=True)
        el = lt[SUBLANES:SUBLANES + epg]
        for g in range(1, n_groups):
            el = jnp.where(grp == g, lt[SUBLANES + g * epg:SUBLANES + (g + 1) * epg], el)
        eidx = lax.broadcasted_iota(I32, el.shape, 0)
        v1 = jnp.max(el, axis=0, keepdims=True)
        i1 = jnp.min(jnp.where(el == v1, eidx, epg), axis=0, keepdims=True)
        el2 = jnp.where(eidx == i1, -jnp.inf, el)
        v2 = jnp.max(el2, axis=0, keepdims=True)
        i2 = jnp.min(jnp.where(el2 == v2, eidx, epg), axis=0, keepdims=True)
        e2 = jnp.exp(v2 - v1)
        w1 = p_grp / (1.0 + e2)
        w2 = p_grp * e2 / (1.0 + e2)
        rid = lax.broadcasted_iota(I32, ridx_ref.shape, 0)
        ridx_ref[...] = jnp.where(rid == 0, grp * epg + i1, jnp.where(rid == 1, grp * epg + i2, 0))
        rwt_ref[...] = jnp.where(rid == 0, w1, jnp.where(rid == 1, w2, 0.0))


def _out_proj(ya_c, ya_l, yb_c, yb_l, xp, xs, mod, norm2_w, w_out, wr, br, *, lat_seq, n_groups, epg):
    nc, d = xp.shape
    nl = xs.shape[0]
    nt = nc + nl
    mw = ya_c.shape[1]
    tm = 256
    trows = (d // 2) // LANES
    nctx_tiles, ntok_tiles = nc // tm, nt // tm
    kern = functools.partial(_outproj_kernel, nctx_tiles=nctx_tiles, ntok_tiles=ntok_tiles,
                             tiles_per_lat=lat_seq // tm, d=d, mw=mw, n_groups=n_groups, epg=epg)
    ctx = lambda i: (jnp.minimum(i, nctx_tiles - 1), 0)
    lat = lambda i: (jnp.clip(i - nctx_tiles, 0, nl // tm - 1), 0)
    row = lambda i: (i, 0)
    const = lambda i: (0, 0)
    return pl.pallas_call(
        kern,
        grid=(ntok_tiles + 1,),
        in_specs=[pl.BlockSpec((tm, mw), ctx), pl.BlockSpec((tm, mw), lat),
                  pl.BlockSpec((tm, yb_c.shape[1]), ctx), pl.BlockSpec((tm, yb_c.shape[1]), lat),
                  pl.BlockSpec((tm, d), ctx), pl.BlockSpec((tm, d), lat),
                  pl.BlockSpec(mod.shape, const),
                  pl.BlockSpec((1, d), const),
                  pl.BlockSpec(w_out.shape, const, pipeline_mode=pl.Buffered(1)),
                  pl.BlockSpec(wr.shape, const),
                  pl.BlockSpec(br.shape, const)],
        out_specs=[pl.BlockSpec((tm, d), row), pl.BlockSpec((tm * trows, LANES), row),
                   pl.BlockSpec((SUBLANES, tm), lambda i: (0, i)), pl.BlockSpec((SUBLANES, tm), lambda i: (0, i))],
        out_shape=[jax.ShapeDtypeStruct((nt + tm, d), F32), jax.ShapeDtypeStruct(((nt + tm) * trows, LANES), U32),
                   jax.ShapeDtypeStruct((SUBLANES, nt + tm), I32), jax.ShapeDtypeStruct((SUBLANES, nt + tm), F32)],
        compiler_params=_cparams(("arbitrary",)),
        name="out_proj",
    )(ya_c, ya_l, yb_c, yb_l, xp, xs, mod, norm2_w.reshape(1, d), w_out, wr, br)


MOE_ROWS = 256


ROUTE_TILE = 512


def _rank_kernel(ridx_ref, tri_ref, rank_ref, cnt_ref, carry_scr, *, n_exp):
    @pl.when(pl.program_id(0) == 0)
    def _():
        carry_scr[...] = jnp.zeros_like(carry_scr)

    e = ridx_ref[...]
    tr = e.shape[1]
    eid = lax.broadcasted_iota(I32, (n_exp, tr), 0)
    carry = carry_scr[:, 0:1]
    ranks = []
    for kk in range(TOP_K):
        hit = eid == e[kk:kk + 1]
        cum = jnp.dot(jnp.where(hit, 1.0, 0.0).astype(BF16), tri_ref[...], preferred_element_type=F32)
        ranks.append(jnp.sum(jnp.where(hit, cum + carry, 0.0), axis=0, keepdims=True) - 1.0)
        carry = carry + cum[:, tr - 1:tr]
    carry_scr[...] = jnp.broadcast_to(carry, carry_scr.shape)
    cnt_ref[...] = jnp.broadcast_to(carry, cnt_ref.shape)
    rid = lax.broadcasted_iota(I32, rank_ref.shape, 0)
    rank_ref[...] = jnp.where(rid == 0, ranks[0], jnp.where(rid == 1, ranks[1], 0.0)).astype(I32)


def _dest_kernel(ridx_ref, rank_ref, pstart_ref, dest_ref, *, n_exp):
    e = ridx_ref[...]
    tr = e.shape[1]
    eid = lax.broadcasted_iota(I32, (n_exp, tr), 0)
    ps = pstart_ref[:, 0:1]
    rows = [jnp.sum(jnp.where(eid == e[kk:kk + 1], ps, 0.0), axis=0, keepdims=True) for kk in range(TOP_K)]
    rid = lax.broadcasted_iota(I32, dest_ref.shape, 0)
    dest_ref[...] = rank_ref[...] + jnp.where(rid == 0, rows[0], jnp.where(rid == 1, rows[1], 0.0)).astype(I32)


def _invert_kernel(dest_ref, fill_ref, rowtok_ref, sem):
    i = pl.program_id(0)
    tr = dest_ref.shape[1]

    @pl.when(i == 0)
    def _():
        fill = pltpu.make_async_copy(fill_ref, rowtok_ref, sem.at[0])
        fill.start()
        fill.wait()

    def body(r, c):
        for kk in range(TOP_K):
            rowtok_ref[dest_ref[kk, r]] = i * tr + r
        return c
    lax.fori_loop(0, tr, body, 0, unroll=16)


def _routing(ridx, n_tok, n_exp, n_blocks, zero_row):
    tr = ROUTE_TILE
    steps = n_tok // tr
    tri = jnp.asarray(np.triu(np.ones((tr, tr), np.float32)), BF16)
    tile = pl.BlockSpec((SUBLANES, tr), lambda i: (0, i))
    cnt_spec = pl.BlockSpec((n_exp, LANES), lambda i: (0, 0))
    rank, cnt = pl.pallas_call(
        functools.partial(_rank_kernel, n_exp=n_exp),
        grid=(steps,),
        in_specs=[tile, pl.BlockSpec((tr, tr), lambda i: (0, 0))],
        out_specs=[tile, cnt_spec],
        out_shape=[jax.ShapeDtypeStruct((SUBLANES, n_tok), I32), jax.ShapeDtypeStruct((n_exp, LANES), F32)],
        scratch_shapes=[pltpu.VMEM((n_exp, LANES), F32)],
        compiler_params=_cparams(("arbitrary",)),
        name="route_rank",
    )(ridx, tri)
    counts = cnt[:, 0].astype(I32)
    padded = (counts + MOE_ROWS - 1) // MOE_ROWS * MOE_ROWS
    pad_end = jnp.cumsum(padded)
    pad_start = pad_end - padded
    dest = pl.pallas_call(
        functools.partial(_dest_kernel, n_exp=n_exp),
        grid=(steps,),
        in_specs=[tile, tile, cnt_spec],
        out_specs=tile,
        out_shape=jax.ShapeDtypeStruct((SUBLANES, n_tok), I32),
        compiler_params=_cparams(("arbitrary",)),
        name="route_dest",
    )(ridx, rank, jnp.broadcast_to(pad_start.astype(F32)[:, None], (n_exp, LANES)))
    row_tok = pl.pallas_call(
        _invert_kernel,
        grid=(steps,),
        in_specs=[pl.BlockSpec((SUBLANES, tr), lambda i: (0, i), memory_space=pltpu.SMEM),
                  pl.BlockSpec(memory_space=pl.ANY)],
        out_specs=pl.BlockSpec(memory_space=pltpu.SMEM),
        out_shape=jax.ShapeDtypeStruct((n_blocks * MOE_ROWS,), I32),
        scratch_shapes=[pltpu.SemaphoreType.DMA((1,))],
        compiler_params=_cparams(("arbitrary",)),
        name="route_invert",
    )(dest, jnp.full((n_blocks * MOE_ROWS,), zero_row, I32))
    blk_row0 = jnp.arange(n_blocks, dtype=I32) * MOE_ROWS
    blk_e = jnp.minimum(jnp.sum((pad_end[None, :] <= blk_row0[:, None]).astype(I32), axis=1), n_exp - 1)
    nused = (pad_end[-1:] // MOE_ROWS).astype(I32)
    return dest, row_tok, blk_e, nused


CAST_ROWS = 256
WEIGHT_DMA_SPLIT = 8


def _expert_kernel(blk_e_ref, nused_ref, tok_ref, nxt_ref, src_ref, wg_hbm, wu_hbm, wd_hbm, y_ref,
                   xbuf, xsem, stage_g, stage_u, stage_d, bf_g, bf_u, bf_d, wsem, *, half, rows, trows):
    b = pl.program_id(0)
    nused = nused_ref[0]
    w_hbm = (wg_hbm, wu_hbm, wd_hbm)
    stage = (stage_g, stage_u, stage_d)
    wbf = (bf_g, bf_u, bf_d)

    def weight_copy(e, j):
        return pltpu.make_async_copy(w_hbm[j].at[e], stage[j], wsem.at[j])

    def start_weights(e, j):
        step = stage[j].shape[0] // WEIGHT_DMA_SPLIT
        for c in range(WEIGHT_DMA_SPLIT):
            sl = pl.ds(c * step, step)
            pltpu.make_async_copy(w_hbm[j].at[e, sl, :], stage[j].at[sl, :], wsem.at[j]).start()

    def row_copy(tok, slot, r):
        src = src_ref.at[pl.ds(pl.multiple_of(tok * trows, trows), trows), :]
        return pltpu.make_async_copy(src, xbuf.at[slot, pl.ds(r * trows, trows), :], xsem.at[slot])

    def start_gather(t_ref, slot):
        for r in range(rows):
            row_copy(t_ref[0, r], slot, r).start()

    @pl.when(b >= nused)
    def _():
        y_ref[...] = jnp.zeros_like(y_ref)

    @pl.when(b < nused)
    def _():
        e = blk_e_ref[b]

        @pl.when(b == 0)
        def _():
            for j in range(3):
                start_weights(e, j)
            def body(r, c):
                row_copy(tok_ref[0, r], 0, r).start()
                return c
            lax.fori_loop(0, rows, body, 0, unroll=8)

        @pl.when((b == 0) | (blk_e_ref[jnp.maximum(b - 1, 0)] != e))
        def _():
            nb = lax.while_loop(lambda k: (k < nused) & (blk_e_ref[jnp.minimum(k, nused - 1)] == e),
                                lambda k: k + 1, b + 1)
            for j in range(3):
                weight_copy(e, j).wait()
                n_steps = stage[j].shape[0] // CAST_ROWS

                def cast(c, carry, j=j):
                    sl = pl.ds(pl.multiple_of(c * CAST_ROWS, CAST_ROWS), CAST_ROWS)
                    wbf[j][sl, :] = stage[j][sl, :].astype(BF16)
                    return carry
                lax.fori_loop(0, n_steps, cast, 0)

                @pl.when(nb < nused)
                def _(j=j):
                    start_weights(blk_e_ref[jnp.minimum(nb, nused - 1)], j)

        slot = b % 2
        pltpu.make_async_copy(src_ref.at[pl.ds(0, rows * trows), :], xbuf.at[slot], xsem.at[slot]).wait()

        def compute(prefetch):
            parts = [_unpack_bf16_pair(xbuf[slot, pl.ds(s, rows, stride=trows), :]) for s in range(trows)]
            lo = jnp.concatenate([p[0] for p in parts], axis=1)
            hi = jnp.concatenate([p[1] for p in parts], axis=1)
            if prefetch:
                start_gather(nxt_ref, 1 - slot)

            def up(w_ref):
                return (jnp.dot(lo, w_ref[0:half, :], preferred_element_type=F32)
                        + jnp.dot(hi, w_ref[half:, :], preferred_element_type=F32))

            g = up(wbf[0])
            h = ((g * _sigmoid(g)) * up(wbf[1])).astype(BF16)
            y_ref[...] = jnp.dot(h, wbf[2][...], preferred_element_type=F32)

        @pl.when(b + 1 < nused)
        def _():
            compute(True)

        @pl.when(b + 1 >= nused)
        def _():
            compute(False)


def _experts(blk_e, nused, row_tok, src, wg, wu, wd, n_blocks):
    n_exp, d, ff = wg.shape
    any_spec = pl.BlockSpec(memory_space=pl.ANY)
    tok = row_tok.reshape(n_blocks, 1, MOE_ROWS)
    trows = (d // 2) // LANES
    return pl.pallas_call(
        functools.partial(_expert_kernel, half=d // 2, rows=MOE_ROWS, trows=trows),
        grid_spec=pltpu.PrefetchScalarGridSpec(
            num_scalar_prefetch=2,
            grid=(n_blocks,),
            in_specs=[pl.BlockSpec((None, 1, MOE_ROWS), lambda b, be, nu: (b, 0, 0), memory_space=pltpu.SMEM),
                      pl.BlockSpec((None, 1, MOE_ROWS), lambda b, be, nu: (jnp.minimum(b + 1, n_blocks - 1), 0, 0),
                                   memory_space=pltpu.SMEM),
                      any_spec, any_spec, any_spec, any_spec],
            out_specs=pl.BlockSpec((MOE_ROWS, d), lambda b, be, nu: (b, 0)),
            scratch_shapes=[pltpu.VMEM((2, MOE_ROWS * trows, LANES), U32), pltpu.SemaphoreType.DMA((2,)),
                            pltpu.VMEM((d, ff), F32), pltpu.VMEM((d, ff), F32), pltpu.VMEM((ff, d), F32),
                            pltpu.VMEM((d, ff), BF16), pltpu.VMEM((d, ff), BF16), pltpu.VMEM((ff, d), BF16),
                            pltpu.SemaphoreType.DMA((3,))]),
        out_shape=jax.ShapeDtypeStruct((n_blocks * MOE_ROWS, d), F32),
        compiler_params=_cparams(("arbitrary",)),
        name="experts",
    )(blk_e, nused, tok, tok, src, wg, wu, wd)


def _combine_kernel(dest_ref, nxt_ref, x1_ref, wt_ref, mod_ref, fw_ref, y_ref, o_ref, ybuf_even, ybuf_odd, sem,
                    *, tiles_per_seq, d, lat):
    i = pl.program_id(0)
    last = pl.num_programs(0) - 1
    tm = x1_ref.shape[0]
    bufs = (ybuf_even, ybuf_odd)

    def row_copy(d_ref, par, kk, r):
        return pltpu.make_async_copy(y_ref.at[d_ref[kk, r]], bufs[par].at[kk, r], sem.at[par])

    @pl.when(i == 0)
    def _():
        def body(r, c):
            for kk in range(TOP_K):
                row_copy(dest_ref, 0, kk, r).start()
            return c
        lax.fori_loop(0, tm, body, 0, unroll=8)

    def step(par, prefetch):
        for kk in range(TOP_K):
            pltpu.make_async_copy(y_ref.at[pl.ds(0, tm)], bufs[par].at[kk], sem.at[par]).wait()
        if prefetch:
            for r in range(tm):
                for kk in range(TOP_K):
                    row_copy(nxt_ref, 1 - par, kk, r).start()
        row = (1 + i // tiles_per_seq) if lat else 0
        gate2 = mod_ref[pl.ds(row, 1), 5 * d:6 * d]
        wt = wt_ref[...]
        w0 = _lane_tile(_row_to_col(wt[0:1], tm), d // LANES)
        w1 = _lane_tile(_row_to_col(wt[1:2], tm), d // LANES)
        x = x1_ref[...] + gate2 * (w0 * bufs[par][0] + w1 * bufs[par][1])
        ms = jnp.mean(x * x, axis=-1, keepdims=True)
        o_ref[...] = x * lax.rsqrt(ms + EPS) * fw_ref[...]

    for par in range(2):
        @pl.when((i % 2 == par) & (i < last))
        def _(par=par):
            step(par, True)

        @pl.when((i % 2 == par) & (i == last))
        def _(par=par):
            step(par, False)


def _combine(dest, x1, rwt, mod, final_w, y_rows, *, row0, n_tok, seq_len, lat):
    d = x1.shape[1]
    tm = 256
    tile0 = row0 // tm
    last = tile0 + n_tok // tm - 1
    kern = functools.partial(_combine_kernel, tiles_per_seq=seq_len // tm, d=d, lat=lat)
    return pl.pallas_call(
        kern,
        grid=(n_tok // tm,),
        in_specs=[pl.BlockSpec((SUBLANES, tm), lambda i: (0, tile0 + i), memory_space=pltpu.SMEM),
                  pl.BlockSpec((SUBLANES, tm), lambda i: (0, jnp.minimum(tile0 + i + 1, last)),
                               memory_space=pltpu.SMEM),
                  pl.BlockSpec((tm, d), lambda i: (tile0 + i, 0)),
                  pl.BlockSpec((SUBLANES, tm), lambda i: (0, tile0 + i)),
                  pl.BlockSpec(mod.shape, lambda i: (0, 0)),
                  pl.BlockSpec((1, d), lambda i: (0, 0)),
                  pl.BlockSpec(memory_space=pl.ANY)],
        out_specs=pl.BlockSpec((tm, d), lambda i: (i, 0)),
        out_shape=jax.ShapeDtypeStruct((n_tok, d), F32),
        scratch_shapes=[pltpu.VMEM((TOP_K, tm, d), F32), pltpu.VMEM((TOP_K, tm, d), F32),
                        pltpu.SemaphoreType.DMA((2,))],
        compiler_params=_cparams(("arbitrary",)),
        name="combine_lat" if lat else "combine_ctx",
    )(dest, dest, x1, rwt, mod, final_w.reshape(1, d), y_rows)


def _gate_layout(w_gates, b_gates, heads):
    d = w_gates.shape[0]
    w = w_gates.reshape(d, 4, heads).transpose(0, 2, 1)
    w = jnp.pad(w, ((0, 0), (0, 0), (0, SUBLANES - 4))).reshape(d, heads * SUBLANES)
    b = b_gates.reshape(4, heads).T
    b = jnp.pad(b, ((0, 0), (0, SUBLANES - 4))).reshape(1, heads * SUBLANES)
    padl = LANES - heads * SUBLANES
    return jnp.pad(w, ((0, 0), (0, padl))), jnp.pad(b, ((0, 0), (0, padl)))


def kernel(x_prompt, x_sample, state_mlstm_c, state_mlstm_n, state_mlstm_m, state_rglru_h, c, c_ctx, w_ada, b_ada,
           norm1_w, w_in, b_gates, conv_w, conv_b, rg_wa, rg_ba, rg_wx, rg_bx, rg_lambda, mlstm_norm_w, w_out,
           norm2_w, router_group_w, router_group_b, router_expert_w, router_expert_b, expert_w_gate, expert_w_up,
           expert_w_down, final_norm_w):
    n_req, seq, d = x_prompt.shape
    n_lat, lat_seq, _ = x_sample.shape
    depth = w_in.shape[0]
    assert depth == 1, "the token-axis plumbing below is written for the single-layer trunk"
    heads, dk, dv = state_mlstm_c.shape[3:]
    rw = state_rglru_h.shape[-1]
    nblk = rg_wa.shape[2]
    assert rw // nblk == LANES
    n_groups, epg = router_expert_w.shape[1], router_expert_w.shape[3]
    n_exp = n_groups * epg
    qk, mw = heads * dk, heads * dv
    nc, nl = n_req * seq, n_lat * lat_seq
    nt = nc + nl
    assert nc % lat_seq == 0 and n_lat + 1 <= SUBLANES
    l = 0

    xp = x_prompt.reshape(nc, d)
    xs = x_sample.reshape(nl, d)
    cvec = jnp.zeros((SUBLANES, d), F32).at[0].set(c_ctx).at[1:1 + n_lat].set(c)
    mod = _ada(cvec, w_ada[l], b_ada[l])

    w = w_in[l]
    g0 = 2 * qk + 2 * mw
    wgate, bgate = _gate_layout(w[:, g0:g0 + 4 * heads], b_gates[l], heads)
    w_cat = jnp.concatenate([w[:, :qk], w[:, 2 * qk:g0], wgate, w[:, g0 + 4 * heads:]], axis=1).astype(BF16)
    w_kt = w[:, qk:2 * qk].T.astype(BF16)
    q, kt, v, o, gt, xr, xg = _in_proj(xp, xs, mod, norm1_w[l], w_cat, w_kt, bgate, lat_seq=lat_seq, heads=heads,
                                       dk=dk, dv=dv, rw=rw)

    mkw = dict(heads=heads, dk=dk, dv=dv)
    ya_c, new_c, new_n, new_m = _mlstm(q, kt, v, o, gt, mlstm_norm_w[l], row0=0, n_seq=n_req, t_len=seq,
                                       emit_state=True, **mkw)
    (ya_l,) = _mlstm(q, kt, v, o, gt, mlstm_norm_w[l], row0=nc, n_seq=n_lat, t_len=lat_seq,
                     state=(state_mlstm_c[:, l], state_mlstm_n[:, l], state_mlstm_m[:, l]), **mkw)

    wg = (0.5 * jnp.concatenate([rg_wa[l, 0], rg_wx[l, 0], rg_wa[l, 1], rg_wx[l, 1]], axis=-1)).astype(BF16)
    bg = 0.5 * jnp.concatenate([rg_ba[l, 0].reshape(nblk, 1, LANES), rg_bx[l, 0].reshape(nblk, 1, LANES),
                                rg_ba[l, 1].reshape(nblk, 1, LANES), rg_bx[l, 1].reshape(nblk, 1, LANES)], axis=-1)
    rargs = (xr, xg, conv_w[l], conv_b[l], wg, bg, rg_lambda[l])
    yb_c, new_h = _rglru(*rargs, row0=0, n_seq=n_req, t_len=seq, seg=seq, emit_state=True)
    (yb_l,) = _rglru(*rargs, row0=nc, n_seq=n_lat, t_len=lat_seq, seg=GRID_W, state=state_rglru_h[:, l])

    r_rows = -(-(SUBLANES + n_exp) // 16) * 16
    wr = jnp.zeros((r_rows, d), F32)
    wr = wr.at[0:n_groups].set(router_group_w[l].T)
    wr = wr.at[SUBLANES:SUBLANES + n_exp].set(router_expert_w[l].transpose(0, 2, 1).reshape(n_exp, d)).astype(BF16)
    br = jnp.zeros((r_rows, LANES), F32)
    br = br.at[0:n_groups, 0].set(router_group_b[l])
    br = br.at[SUBLANES:SUBLANES + n_exp, 0].set(router_expert_b[l].reshape(n_exp))
    x1, hp, ridx, rwt = _out_proj(ya_c, ya_l, yb_c, yb_l, xp, xs, mod, norm2_w[l], w_out[l].astype(BF16), wr, br,
                                  lat_seq=lat_seq, n_groups=n_groups, epg=epg)

    n_blocks = (nt * TOP_K) // MOE_ROWS + n_exp
    dest, row_tok, blk_e, nused = _routing(ridx, nt, n_exp, n_blocks, zero_row=nt)
    only_layer = lambda a: a.reshape(a.shape[1:])
    y_rows = _experts(blk_e, nused, row_tok, hp, only_layer(expert_w_gate), only_layer(expert_w_up),
                      only_layer(expert_w_down), n_blocks)

    y_prompt = _combine(dest, x1, rwt, mod, final_norm_w, y_rows, row0=0, n_tok=nc, seq_len=seq, lat=False)
    y_sample = _combine(dest, x1, rwt, mod, final_norm_w, y_rows, row0=nc, n_tok=nl, seq_len=lat_seq, lat=True)

    return (y_prompt.reshape(n_req, seq, d), y_sample.reshape(n_lat, lat_seq, d),
            new_c[:, None], new_n.reshape(n_req, 1, 2, heads, dk), new_m.reshape(n_req, 1, 2, heads),
            new_h[:, None])
```

```python
import functools

import jax
import jax.numpy as jnp
import numpy as np
from jax import lax
from jax.experimental import pallas as pl
from jax.experimental.pallas import tpu as pltpu

F32 = jnp.float32
BF16 = jnp.bfloat16
I32 = jnp.int32
U32 = jnp.uint32

EPS = 1e-6
GRID_W = 64
CONV_LEFT = 2
RGLRU_C = 8.0
TOP_K = 2
LANES = 128
SUBLANES = 8
MLSTM_L = 256
NEG = -1e30
VMEM_LIMIT = 56 * 1024 * 1024

_HIGHEST = lax.Precision.HIGHEST


def _cparams(sem, vmem=VMEM_LIMIT):
    return pltpu.CompilerParams(dimension_semantics=sem, vmem_limit_bytes=vmem)


def _sigmoid(x):
    return 0.5 * jnp.tanh(0.5 * x) + 0.5


def _row_to_col(r, n):
    return jnp.broadcast_to(r, (LANES, n)).T


def _lane_tile(x, reps):
    return x if reps == 1 else jnp.concatenate([x] * reps, axis=1)


def _ada_kernel(c_ref, w_ref, b_ref, o_ref):
    c = c_ref[...]
    s = (c * _sigmoid(c)).astype(BF16)
    o_ref[...] = jnp.dot(s, w_ref[...].astype(BF16), preferred_element_type=F32) + b_ref[...]


def _ada(cvec, w_ada, b_ada):
    d, n = w_ada.shape
    tn = 1024 if n % 1024 == 0 else 512
    assert n % tn == 0
    return pl.pallas_call(
        _ada_kernel,
        grid=(n // tn,),
        in_specs=[pl.BlockSpec((SUBLANES, d), lambda j: (0, 0)),
                  pl.BlockSpec((d, tn), lambda j: (0, j)),
                  pl.BlockSpec((1, tn), lambda j: (0, j))],
        out_specs=pl.BlockSpec((SUBLANES, tn), lambda j: (0, j)),
        out_shape=jax.ShapeDtypeStruct((SUBLANES, n), F32),
        compiler_params=_cparams(("arbitrary",)),
        name="ada",
    )(cvec, w_ada, b_ada.reshape(1, n))


def _modulated_norm(x, w, shift, scale):
    ms = jnp.mean(x * x, axis=-1, keepdims=True)
    return (x * lax.rsqrt(ms + EPS) * w) * (1.0 + scale) + shift


def _mod_row(i, nctx_tiles, tiles_per_lat):
    return jnp.where(i < nctx_tiles, 0, 1 + (i - nctx_tiles) // tiles_per_lat)


def _inproj_kernel(xp_ref, xs_ref, mod_ref, n1_ref, w_ref, wkt_ref, bg_ref,
                   q_ref, kt_ref, v_ref, o_ref, gt_ref, xr_ref, xg_ref,
                   *, nctx_tiles, tiles_per_lat, d, qk, mw, rw, gh, qscale):
    i = pl.program_id(0)
    x = jnp.where(i < nctx_tiles, xp_ref[...], xs_ref[...])
    row = _mod_row(i, nctx_tiles, tiles_per_lat)
    shift = mod_ref[pl.ds(row, 1), 0:d]
    scale = mod_ref[pl.ds(row, 1), d:2 * d]
    hb = _modulated_norm(x, n1_ref[...], shift, scale).astype(BF16)

    def proj(c0, width):
        return jnp.dot(hb, w_ref[:, c0:c0 + width], preferred_element_type=F32)

    c0 = 0
    q_ref[...] = (proj(c0, qk) * qscale).astype(BF16); c0 += qk
    kt_ref[...] = lax.dot_general(wkt_ref[...], hb, (((1,), (1,)), ((), ())),
                                  preferred_element_type=F32).astype(BF16)
    v_ref[...] = proj(c0, mw).astype(BF16); c0 += mw
    o_ref[...] = proj(c0, mw); c0 += mw
    zg = proj(c0, LANES) + bg_ref[...]; c0 += LANES
    lane = lax.broadcasted_iota(I32, zg.shape, 1)
    log_sig = jnp.minimum(zg, 0.0) - jnp.log1p(jnp.exp(-jnp.abs(zg)))
    zg = jnp.where(lane % 2 == 1, log_sig, zg)
    gt_ref[...] = zg.T[0:gh, :]
    xr_ref[...] = proj(c0, rw); c0 += rw
    xg_ref[...] = proj(c0, rw)


def _in_proj(xp, xs, mod, norm1_w, w_cat, w_kt, bg, *, lat_seq, heads, dk, dv, rw):
    nc, d = xp.shape
    nl = xs.shape[0]
    nt = nc + nl
    tm = 256
    qk, mw, gh = heads * dk, heads * dv, heads * SUBLANES
    nctx_tiles = nc // tm
    kern = functools.partial(_inproj_kernel, nctx_tiles=nctx_tiles, tiles_per_lat=lat_seq // tm, d=d, qk=qk, mw=mw,
                             rw=rw, gh=gh, qscale=dk ** -0.5)
    row = lambda i: (i, 0)
    const = lambda i: (0, 0)
    return pl.pallas_call(
        kern,
        grid=(nt // tm,),
        in_specs=[pl.BlockSpec((tm, d), lambda i: (jnp.minimum(i, nctx_tiles - 1), 0)),
                  pl.BlockSpec((tm, d), lambda i: (jnp.maximum(i - nctx_tiles, 0), 0)),
                  pl.BlockSpec(mod.shape, const),
                  pl.BlockSpec((1, d), const),
                  pl.BlockSpec(w_cat.shape, const, pipeline_mode=pl.Buffered(1)),
                  pl.BlockSpec(w_kt.shape, const, pipeline_mode=pl.Buffered(1)),
                  pl.BlockSpec((1, LANES), const)],
        out_specs=[pl.BlockSpec((tm, qk), row), pl.BlockSpec((qk, tm), lambda i: (0, i)), pl.BlockSpec((tm, mw), row),
                   pl.BlockSpec((tm, mw), row), pl.BlockSpec((gh, tm), lambda i: (0, i)),
                   pl.BlockSpec((tm, rw), row), pl.BlockSpec((tm, rw), row)],
        out_shape=[jax.ShapeDtypeStruct((nt, qk), BF16), jax.ShapeDtypeStruct((qk, nt), BF16),
                   jax.ShapeDtypeStruct((nt, mw), BF16), jax.ShapeDtypeStruct((nt, mw), F32),
                   jax.ShapeDtypeStruct((gh, nt), F32),
                   jax.ShapeDtypeStruct((nt, rw), F32), jax.ShapeDtypeStruct((nt, rw), F32)],
        compiler_params=_cparams(("arbitrary",)),
        name="in_proj",
    )(xp, xs, mod, norm1_w.reshape(1, d), w_cat, w_kt, bg)


def _mlstm_kernel(*refs, t_len, dk, dv, has_state, emit_state):
    it = iter(refs)
    q_ref, kt_ref, v_ref, o_ref, gt_ref, nw_ref, tri_ref = (next(it) for _ in range(7))
    if has_state:
        c0_ref, n0_ref, m0_ref = (next(it) for _ in range(3))
    ya_ref = next(it)
    if emit_state:
        cn_ref, nn_ref, mn_ref = (next(it) for _ in range(3))
    hf_scr, hb_scr, c_scr, ma_scr, p_scr, w_scr, em_scr, kw_scr, dm_scr = (next(it) for _ in range(9))
    ln = MLSTM_L
    nchunks = t_len // ln
    assert ln % LANES == 0 and dk == LANES
    gsz = 2 if nchunks % 2 == 0 else 1
    ngroups = nchunks // gsz
    h_scr = (hf_scr, hb_scr)
    ext = dv + LANES
    lrep = ln // LANES

    for d in range(2):
        if has_state:
            c_scr[d, :, 0:dv] = c0_ref[d]
            c_scr[d, :, dv:ext] = _row_to_col(n0_ref[d], dk)
            ma_scr[d] = m0_ref[d]
        else:
            c_scr[d] = jnp.zeros((dk, ext), F32)
            ma_scr[d] = jnp.zeros((1, 1), F32)

    def chunk_start(d, j):
        return pl.multiple_of((j if d == 0 else nchunks - 1 - j) * ln, ln)

    def stage_a(d, j, slot):
        t0 = chunk_start(d, j)
        q = q_ref[pl.ds(t0, ln), :]
        kt = kt_ref[:, pl.ds(t0, ln)]
        g8 = gt_ref[:, pl.ds(t0, ln)]
        cum8 = jnp.dot(g8, tri_ref[d], precision=_HIGHEST, preferred_element_type=F32)
        valid = tri_ref[1 - d] > 0.5
        li = g8[2 * d:2 * d + 1]
        lf = g8[2 * d + 1:2 * d + 2]
        cum_row = cum8[2 * d + 1:2 * d + 2]
        total = jnp.sum(lf, axis=1, keepdims=True)
        a_row = li - cum_row
        cum_col = _row_to_col(cum_row, ln)
        m_prev = ma_scr[d]
        dmat = jnp.where(valid, _lane_tile(cum_col, lrep) + a_row, NEG)
        inter = cum_col + m_prev
        m_t = jnp.maximum(inter, jnp.max(dmat, axis=1, keepdims=True))
        s = jnp.dot(q, kt, preferred_element_type=F32)
        bank, par = slot
        p_scr[d, bank, par] = (s * jnp.exp(dmat - _lane_tile(m_t, lrep))).astype(BF16)
        w_scr[d, bank, par] = jnp.exp(inter - m_t)
        em_scr[d, bank, par] = jnp.exp(-m_t)
        g_row = total + a_row
        m_new = jnp.maximum(total + m_prev, jnp.max(g_row, axis=1, keepdims=True))
        kw_scr[d, bank, par] = (kt.astype(F32) * jnp.exp(g_row - m_new)).astype(BF16)
        decay = jnp.exp(total + m_prev - m_new)
        rid = lax.broadcasted_iota(I32, (SUBLANES, LANES), 0)
        dm_scr[d, bank, par] = jnp.where(rid == 0, decay, m_new)
        ma_scr[d] = m_new

    def stage_b(d, j, slot):
        t0 = chunk_start(d, j)
        q = q_ref[pl.ds(t0, ln), :]
        v_ext = jnp.concatenate([v_ref[pl.ds(t0, ln), :], jnp.ones((ln, LANES), BF16)], axis=1)
        c_st = c_scr[d]
        bank, par = slot
        full = (jnp.dot(p_scr[d, bank, par], v_ext, preferred_element_type=F32)
                + _lane_tile(w_scr[d, bank, par], ext // LANES) * jnp.dot(q, c_st.astype(BF16),
                                                                         preferred_element_type=F32))
        inv = 1.0 / jnp.maximum(jnp.abs(full[:, dv:ext]), em_scr[d, bank, par])
        h_scr[d][pl.ds(t0, ln), :] = full[:, 0:dv] * _lane_tile(inv, dv // LANES)
        decay = dm_scr[d, bank, par, 0:1, 0:1]
        c_scr[d] = decay * c_st + jnp.dot(kw_scr[d, bank, par], v_ext, preferred_element_type=F32)

    def group_a(jj, bank):
        for par in range(gsz):
            for d in range(2):
                stage_a(d, gsz * jj + par, (bank, par))

    def group_b(jj, bank):
        for par in range(gsz):
            for d in range(2):
                stage_b(d, gsz * jj + par, (bank, par))

    group_a(0, 0)

    def body(jj, carry):
        bank = jj % 2
        group_b(jj, bank)
        group_a(jj + 1, 1 - bank)
        return carry

    lax.fori_loop(0, ngroups - 1, body, 0)
    group_b(ngroups - 1, (ngroups - 1) % 2)

    def finish(j, carry):
        t0 = pl.multiple_of(j * ln, ln)
        hs = hf_scr[pl.ds(t0, ln), :] + hb_scr[pl.ds(t0, ln), :]
        ms = jnp.mean(hs * hs, axis=1, keepdims=True)
        y = hs * lax.rsqrt(ms + EPS) * nw_ref[...]
        ya_ref[pl.ds(t0, ln), :] = (_sigmoid(o_ref[pl.ds(t0, ln), :]) * y).astype(BF16)
        return carry

    lax.fori_loop(0, nchunks, finish, 0)

    if emit_state:
        for d in range(2):
            cn_ref[d] = c_scr[d, :, 0:dv]
            nn_ref[d] = c_scr[d, :, dv:ext].T[0:1, :]
            mn_ref[d] = dm_scr[d, (ngroups - 1) % 2, gsz - 1, 1:2, 0:1]


def _mlstm_tri():
    r = np.arange(MLSTM_L)
    fwd = (r[:, None] <= r[None, :]).astype(np.float32)
    return jnp.asarray(np.stack([fwd, fwd.T]))


def _mlstm(q, kt, v, o, gt, norm_w, *, row0, n_seq, t_len, heads, dk, dv, state=None, emit_state=False):
    blk0 = row0 // t_len
    tok = lambda s, h: (blk0 + s, h)
    in_specs = [pl.BlockSpec((t_len, dk), tok), pl.BlockSpec((dk, t_len), lambda s, h: (h, blk0 + s)),
                pl.BlockSpec((t_len, dv), tok), pl.BlockSpec((t_len, dv), tok),
                pl.BlockSpec((SUBLANES, t_len), lambda s, h: (h, blk0 + s)),
                pl.BlockSpec((None, 1, dv), lambda s, h: (h, 0, 0)),
                pl.BlockSpec((2, MLSTM_L, MLSTM_L), lambda s, h: (0, 0, 0))]
    args = [q, kt, v, o, gt, norm_w.reshape(heads, 1, dv), _mlstm_tri()]
    if state is not None:
        c0, n0, m0 = state
        in_specs += [pl.BlockSpec((None, 2, None, dk, dv), lambda s, h: (s, 0, h, 0, 0)),
                     pl.BlockSpec((None, 2, None, 1, dk), lambda s, h: (s, 0, h, 0, 0)),
                     pl.BlockSpec((None, 2, None, 1, 1), lambda s, h: (s, 0, h, 0, 0))]
        args += [c0, n0.reshape(n_seq, 2, heads, 1, dk), m0.reshape(n_seq, 2, heads, 1, 1)]
    out_specs = [pl.BlockSpec((t_len, dv), lambda s, h: (s, h))]
    out_shape = [jax.ShapeDtypeStruct((n_seq * t_len, heads * dv), BF16)]
    if emit_state:
        out_specs += [pl.BlockSpec((None, 2, None, dk, dv), lambda s, h: (s, 0, h, 0, 0)),
                      pl.BlockSpec((None, 2, None, 1, dk), lambda s, h: (s, 0, h, 0, 0)),
                      pl.BlockSpec((None, 2, None, 1, 1), lambda s, h: (s, 0, h, 0, 0))]
        out_shape += [jax.ShapeDtypeStruct((n_seq, 2, heads, dk, dv), F32),
                      jax.ShapeDtypeStruct((n_seq, 2, heads, 1, dk), F32),
                      jax.ShapeDtypeStruct((n_seq, 2, heads, 1, 1), F32)]
    kern = functools.partial(_mlstm_kernel, t_len=t_len, dk=dk, dv=dv, has_state=state is not None,
                             emit_state=emit_state)
    return pl.pallas_call(
        kern,
        grid=(n_seq, heads),
        in_specs=in_specs,
        out_specs=out_specs,
        out_shape=out_shape,
        scratch_shapes=[pltpu.VMEM((t_len, dv), F32), pltpu.VMEM((t_len, dv), F32),
                        pltpu.VMEM((2, dk, dv + LANES), F32), pltpu.VMEM((2, 1, 1), F32),
                        pltpu.VMEM((2, 2, 2, MLSTM_L, MLSTM_L), BF16), pltpu.VMEM((2, 2, 2, MLSTM_L, LANES), F32),
                        pltpu.VMEM((2, 2, 2, MLSTM_L, LANES), F32), pltpu.VMEM((2, 2, 2, dk, MLSTM_L), BF16),
                        pltpu.VMEM((2, 2, 2, SUBLANES, LANES), F32)],
        compiler_params=_cparams(("arbitrary", "arbitrary")),
        name="mlstm_state" if emit_state else "mlstm",
    )(*args)


def _gelu_tanh(x):
    return x * (0.5 * (1.0 + jnp.tanh(0.7978845608028654 * (x + 0.044715 * (x * x * x)))))


def _softplus(x):
    return jnp.maximum(x, 0.0) + jnp.log1p(jnp.exp(-jnp.abs(x)))


def _rglru_kernel(*refs, t_len, seg, sub, pitch, tc, has_state, emit_state):
    it = iter(refs)
    xr_ref, xg_ref, cw_ref, cb_ref, wg_ref, bg_ref, lam_ref = (next(it) for _ in range(7))
    if has_state:
        h0_ref = next(it)
    yb_ref = next(it)
    if emit_state:
        hn_ref = next(it)
    a_scr, u_scr, cin_scr = (next(it) for _ in range(3))
    nchunks = t_len // tc
    piece = min(tc, sub)
    npieces = tc // piece
    ntile = xr_ref.shape[1] // LANES
    chains = [(d, lt) for d in range(2) for lt in range(ntile)]

    def scan_rows(t0, p):
        t = t0 + p * piece
        i = t // sub
        return pl.ds(pl.multiple_of(i * pitch + (t - i * sub), SUBLANES), piece), i

    ka = (-0.5 * RGLRU_C * 1.4426950408889634) * _softplus(-lam_ref[...])

    def gates(c, carry):
        t0 = pl.multiple_of(c * tc, tc)
        pos = lax.broadcasted_iota(I32, (tc, LANES), 0) % seg
        for lt in range(ntile):
            cols = slice(lt * LANES, (lt + 1) * LANES)
            x = xr_ref[pl.ds(t0, tc), cols]
            xc = cb_ref[:, cols] + cw_ref[CONV_LEFT:CONV_LEFT + 1, cols] * x
            for j in range(cw_ref.shape[0]):
                off = j - CONV_LEFT
                if off == 0:
                    continue
                shifted = pltpu.roll(x, (-off) % tc, 0)
                ok = (pos >= -off) if off < 0 else (pos < seg - off)
                xc = xc + cw_ref[j:j + 1, cols] * jnp.where(ok, shifted, 0.0)
            zh = jnp.dot(xc.astype(BF16), wg_ref[lt], preferred_element_type=F32) + bg_ref[lt]
            hx = 0.5 * xc
            for d in range(2):
                kd = ka[d:d + 1, cols]
                a = jnp.exp2(jnp.tanh(zh[:, (2 * d) * LANES:(2 * d + 1) * LANES]) * kd + kd)
                igx = hx * jnp.tanh(zh[:, (2 * d + 1) * LANES:(2 * d + 2) * LANES]) + hx
                u = jnp.sqrt(1.0 - a * a) * igx
                for p in range(npieces):
                    rows, _ = scan_rows(t0, p)
                    a_scr[d, lt, rows, :] = a[p * piece:(p + 1) * piece]
                    u_scr[d, lt, rows, :] = u[p * piece:(p + 1) * piece]
        return carry

    lax.fori_loop(0, nchunks, gates, 0)

    def scan(j, carry):
        out = []
        for (d, lt), (h, p) in zip(chains, carry):
            rows = pl.ds(j if d == 0 else sub - 1 - j, SUBLANES, stride=pitch)
            a = a_scr[d, lt, rows, :]
            h = a * h + u_scr[d, lt, rows, :]
            p = a * p
            a_scr[d, lt, rows, :] = p
            u_scr[d, lt, rows, :] = h
            out.append((h, p))
        return tuple(out)

    zero = jnp.zeros((SUBLANES, LANES), F32)
    one = jnp.ones((SUBLANES, LANES), F32)
    ends = lax.fori_loop(0, sub, scan, tuple((zero, one) for _ in chains), unroll=2)

    for (d, lt), (h, p) in zip(chains, ends):
        cols = slice(lt * LANES, (lt + 1) * LANES)
        cin = h0_ref[d:d + 1, cols] if has_state else jnp.zeros((1, LANES), F32)
        for i in (range(SUBLANES) if d == 0 else reversed(range(SUBLANES))):
            cin_scr[d, lt, i:i + 1, :] = cin
            cin = h[i:i + 1] + p[i:i + 1] * cin
        if emit_state:
            hn_ref[d:d + 1, cols] = cin

    def finish(c, carry):
        t0 = pl.multiple_of(c * tc, tc)
        for lt in range(ntile):
            cols = slice(lt * LANES, (lt + 1) * LANES)
            for p in range(npieces):
                rows, i = scan_rows(t0, p)
                h = (u_scr[0, lt, rows, :] + a_scr[0, lt, rows, :] * cin_scr[0, lt, pl.ds(i, 1), :]
                     + u_scr[1, lt, rows, :] + a_scr[1, lt, rows, :] * cin_scr[1, lt, pl.ds(i, 1), :])
                nat = pl.ds(pl.multiple_of(t0 + p * piece, SUBLANES), piece)
                yb_ref[nat, cols] = (h * _gelu_tanh(xg_ref[nat, cols])).astype(BF16)
        return carry

    lax.fori_loop(0, nchunks, finish, 0)


def _rglru(xr, xg, conv_w, conv_b, wg, bg, lam, *, row0, n_seq, t_len, seg, state=None, emit_state=False):
    rw = xr.shape[1]
    ntile = 2
    cb = ntile * LANES
    assert rw % cb == 0
    blk0 = row0 // t_len
    sub = t_len // SUBLANES
    pitch = sub + SUBLANES
    tc = 256
    tok = lambda s, g: (blk0 + s, g)
    in_specs = [pl.BlockSpec((t_len, cb), tok), pl.BlockSpec((t_len, cb), tok),
                pl.BlockSpec((conv_w.shape[0], cb), lambda s, g: (0, g)),
                pl.BlockSpec((1, cb), lambda s, g: (0, g)),
                pl.BlockSpec((ntile, LANES, 4 * LANES), lambda s, g: (g, 0, 0)),
                pl.BlockSpec((ntile, 1, 4 * LANES), lambda s, g: (g, 0, 0)),
                pl.BlockSpec((2, cb), lambda s, g: (0, g))]
    args = [xr, xg, conv_w, conv_b.reshape(1, rw), wg, bg, lam]
    if state is not None:
        in_specs.append(pl.BlockSpec((None, 2, cb), lambda s, g: (s, 0, g)))
        args.append(state)
    out_specs = [pl.BlockSpec((t_len, cb), lambda s, g: (s, g))]
    out_shape = [jax.ShapeDtypeStruct((n_seq * t_len, rw), BF16)]
    if emit_state:
        out_specs.append(pl.BlockSpec((None, 2, cb), lambda s, g: (s, 0, g)))
        out_shape.append(jax.ShapeDtypeStruct((n_seq, 2, rw), F32))
    kern = functools.partial(_rglru_kernel, t_len=t_len, seg=seg, sub=sub, pitch=pitch, tc=tc,
                             has_state=state is not None, emit_state=emit_state)
    return pl.pallas_call(
        kern,
        grid=(n_seq, rw // cb),
        in_specs=in_specs,
        out_specs=out_specs,
        out_shape=out_shape,
        scratch_shapes=[pltpu.VMEM((2, ntile, SUBLANES * pitch, LANES), F32),
                        pltpu.VMEM((2, ntile, SUBLANES * pitch, LANES), F32),
                        pltpu.VMEM((2, ntile, SUBLANES, LANES), F32)],
        compiler_params=_cparams(("arbitrary", "arbitrary")),
        name="rglru_state" if emit_state else "rglru",
    )(*args)


def _pack_bf16_pair(lo, hi):
    def rounded_bits(x):
        return pltpu.bitcast(x.astype(BF16).astype(F32), U32)
    return (rounded_bits(lo) >> 16) | rounded_bits(hi)


def _unpack_bf16_pair(w):
    lo = pltpu.bitcast(w << 16, F32).astype(BF16)
    hi = pltpu.bitcast(w & jnp.uint32(0xFFFF0000), F32).astype(BF16)
    return lo, hi


def _outproj_kernel(yac_ref, yal_ref, ybc_ref, ybl_ref, xp_ref, xs_ref, mod_ref, n2_ref, wo_ref, wr_ref, br_ref,
                    x1_ref, hp_ref, ridx_ref, rwt_ref,
                    *, nctx_tiles, ntok_tiles, tiles_per_lat, d, mw, n_groups, epg):
    i = pl.program_id(0)

    @pl.when(i == ntok_tiles)
    def _():
        x1_ref[...] = jnp.zeros_like(x1_ref)
        hp_ref[...] = jnp.zeros_like(hp_ref)
        ridx_ref[...] = jnp.zeros_like(ridx_ref)
        rwt_ref[...] = jnp.zeros_like(rwt_ref)

    def token_tile(x_ref, ya_ref, yb_ref):
        x, ya, yb = x_ref[...], ya_ref[...], yb_ref[...]
        row = _mod_row(i, nctx_tiles, tiles_per_lat)
        gate1 = mod_ref[pl.ds(row, 1), 2 * d:3 * d]
        shift2 = mod_ref[pl.ds(row, 1), 3 * d:4 * d]
        scale2 = mod_ref[pl.ds(row, 1), 4 * d:5 * d]
        y = (jnp.dot(ya, wo_ref[0:mw, :], preferred_element_type=F32)
             + jnp.dot(yb, wo_ref[mw:, :], preferred_element_type=F32))
        x1 = x + gate1 * y
        x1_ref[...] = x1
        h2 = _modulated_norm(x1, n2_ref[...], shift2, scale2)
        half = d // 2
        packed = _pack_bf16_pair(h2[:, :half], h2[:, half:])
        trows = half // LANES
        for s in range(trows):
            hp_ref[pl.ds(s, x.shape[0], stride=trows), :] = packed[:, s * LANES:(s + 1) * LANES]

        lt = lax.dot_general(wr_ref[...], h2.astype(BF16), (((1,), (1,)), ((), ())),
                             preferred_element_type=F32) + br_ref[:, 0:1]
        gidx = lax.broadcasted_iota(I32, (SUBLANES, lt.shape[1]), 0)
        gl = jnp.where(gidx < n_groups, lt[0:SUBLANES], -jnp.inf)
        gmax = jnp.max(gl, axis=0, keepdims=True)
        grp = jnp.min(jnp.where(gl == gmax, gidx, n_groups), axis=0, keepdims=True)
        p_grp = 1.0 / jnp.sum(jnp.exp(gl - gmax), axis=0, keepdims=True)
        el = lt[SUBLANES:SUBLANES + epg]
        for g in range(1, n_groups):
            el = jnp.where(grp == g, lt[SUBLANES + g * epg:SUBLANES + (g + 1) * epg], el)
        eidx = lax.broadcasted_iota(I32, el.shape, 0)
        v1 = jnp.max(el, axis=0, keepdims=True)
        i1 = jnp.min(jnp.where(el == v1, eidx, epg), axis=0, keepdims=True)
        el2 = jnp.where(eidx == i1, -jnp.inf, el)
        v2 = jnp.max(el2, axis=0, keepdims=True)
        i2 = jnp.min(jnp.where(el2 == v2, eidx, epg), axis=0, keepdims=True)
        e2 = jnp.exp(v2 - v1)
        w1 = p_grp / (1.0 + e2)
        w2 = p_grp * e2 / (1.0 + e2)
        rid = lax.broadcasted_iota(I32, ridx_ref.shape, 0)
        ridx_ref[...] = jnp.where(rid == 0, grp * epg + i1, jnp.where(rid == 1, grp * epg + i2, 0))
        rwt_ref[...] = jnp.where(rid == 0, w1, jnp.where(rid == 1, w2, 0.0))

    @pl.when(i < nctx_tiles)
    def _():
        token_tile(xp_ref, yac_ref, ybc_ref)

    @pl.when((i >= nctx_tiles) & (i < ntok_tiles))
    def _():
        token_tile(xs_ref, yal_ref, ybl_ref)


def _out_proj(ya_c, ya_l, yb_c, yb_l, xp, xs, mod, norm2_w, w_out, wr, br, *, lat_seq, n_groups, epg):
    nc, d = xp.shape
    nl = xs.shape[0]
    nt = nc + nl
    mw = ya_c.shape[1]
    tm = 256
    trows = (d // 2) // LANES
    nctx_tiles, ntok_tiles = nc // tm, nt // tm
    kern = functools.partial(_outproj_kernel, nctx_tiles=nctx_tiles, ntok_tiles=ntok_tiles,
                             tiles_per_lat=lat_seq // tm, d=d, mw=mw, n_groups=n_groups, epg=epg)
    ctx = lambda i: (jnp.minimum(i, nctx_tiles - 1), 0)
    lat = lambda i: (jnp.clip(i - nctx_tiles, 0, nl // tm - 1), 0)
    row = lambda i: (i, 0)
    const = lambda i: (0, 0)
    return pl.pallas_call(
        kern,
        grid=(ntok_tiles + 1,),
        in_specs=[pl.BlockSpec((tm, mw), ctx), pl.BlockSpec((tm, mw), lat),
                  pl.BlockSpec((tm, yb_c.shape[1]), ctx), pl.BlockSpec((tm, yb_c.shape[1]), lat),
                  pl.BlockSpec((tm, d), ctx), pl.BlockSpec((tm, d), lat),
                  pl.BlockSpec(mod.shape, const),
                  pl.BlockSpec((1, d), const),
                  pl.BlockSpec(w_out.shape, const, pipeline_mode=pl.Buffered(1)),
                  pl.BlockSpec(wr.shape, const),
                  pl.BlockSpec(br.shape, const)],
        out_specs=[pl.BlockSpec((tm, d), row), pl.BlockSpec((tm * trows, LANES), row),
                   pl.BlockSpec((SUBLANES, tm), lambda i: (0, i)), pl.BlockSpec((SUBLANES, tm), lambda i: (0, i))],
        out_shape=[jax.ShapeDtypeStruct((nt + tm, d), F32), jax.ShapeDtypeStruct(((nt + tm) * trows, LANES), U32),
                   jax.ShapeDtypeStruct((SUBLANES, nt + tm), I32), jax.ShapeDtypeStruct((SUBLANES, nt + tm), F32)],
        compiler_params=_cparams(("arbitrary",)),
        name="out_proj",
    )(ya_c, ya_l, yb_c, yb_l, xp, xs, mod, norm2_w.reshape(1, d), w_out, wr, br)


MOE_ROWS = 256


ROUTE_TILE = 512


def _rank_kernel(ridx_ref, tri_ref, rank_ref, cnt_ref, carry_scr, *, n_exp):
    @pl.when(pl.program_id(0) == 0)
    def _():
        carry_scr[...] = jnp.zeros_like(carry_scr)

    e = ridx_ref[...]
    tr = e.shape[1]
    eid = lax.broadcasted_iota(I32, (n_exp, tr), 0)
    carry = carry_scr[:, 0:1]
    ranks = []
    for kk in range(TOP_K):
        hit = eid == e[kk:kk + 1]
        cum = jnp.dot(jnp.where(hit, 1.0, 0.0).astype(BF16), tri_ref[...], preferred_element_type=F32)
        ranks.append(jnp.sum(jnp.where(hit, cum + carry, 0.0), axis=0, keepdims=True) - 1.0)
        carry = carry + cum[:, tr - 1:tr]
    carry_scr[...] = jnp.broadcast_to(carry, carry_scr.shape)
    cnt_ref[...] = jnp.broadcast_to(carry, cnt_ref.shape)
    rid = lax.broadcasted_iota(I32, rank_ref.shape, 0)
    rank_ref[...] = jnp.where(rid == 0, ranks[0], jnp.where(rid == 1, ranks[1], 0.0)).astype(I32)


def _dest_kernel(ridx_ref, rank_ref, pstart_ref, dest_ref, *, n_exp):
    e = ridx_ref[...]
    tr = e.shape[1]
    eid = lax.broadcasted_iota(I32, (n_exp, tr), 0)
    ps = pstart_ref[:, 0:1]
    rows = [jnp.sum(jnp.where(eid == e[kk:kk + 1], ps, 0.0), axis=0, keepdims=True) for kk in range(TOP_K)]
    rid = lax.broadcasted_iota(I32, dest_ref.shape, 0)
    dest_ref[...] = rank_ref[...] + jnp.where(rid == 0, rows[0], jnp.where(rid == 1, rows[1], 0.0)).astype(I32)


def _invert_kernel(dest_ref, fill_ref, rowtok_ref, sem):
    i = pl.program_id(0)
    tr = dest_ref.shape[1]

    @pl.when(i == 0)
    def _():
        fill = pltpu.make_async_copy(fill_ref, rowtok_ref, sem.at[0])
        fill.start()
        fill.wait()

    def body(r, c):
        for kk in range(TOP_K):
            rowtok_ref[dest_ref[kk, r]] = i * tr + r
        return c
    lax.fori_loop(0, tr, body, 0, unroll=16)


def _routing(ridx, n_tok, n_exp, n_blocks, zero_row):
    tr = ROUTE_TILE
    steps = n_tok // tr
    tri = jnp.asarray(np.triu(np.ones((tr, tr), np.float32)), BF16)
    tile = pl.BlockSpec((SUBLANES, tr), lambda i: (0, i))
    cnt_spec = pl.BlockSpec((n_exp, LANES), lambda i: (0, 0))
    rank, cnt = pl.pallas_call(
        functools.partial(_rank_kernel, n_exp=n_exp),
        grid=(steps,),
        in_specs=[tile, pl.BlockSpec((tr, tr), lambda i: (0, 0))],
        out_specs=[tile, cnt_spec],
        out_shape=[jax.ShapeDtypeStruct((SUBLANES, n_tok), I32), jax.ShapeDtypeStruct((n_exp, LANES), F32)],
        scratch_shapes=[pltpu.VMEM((n_exp, LANES), F32)],
        compiler_params=_cparams(("arbitrary",)),
        name="route_rank",
    )(ridx, tri)
    counts = cnt[:, 0].astype(I32)
    padded = (counts + MOE_ROWS - 1) // MOE_ROWS * MOE_ROWS
    pad_end = jnp.cumsum(padded)
    pad_start = pad_end - padded
    dest = pl.pallas_call(
        functools.partial(_dest_kernel, n_exp=n_exp),
        grid=(steps,),
        in_specs=[tile, tile, cnt_spec],
        out_specs=tile,
        out_shape=jax.ShapeDtypeStruct((SUBLANES, n_tok), I32),
        compiler_params=_cparams(("arbitrary",)),
        name="route_dest",
    )(ridx, rank, jnp.broadcast_to(pad_start.astype(F32)[:, None], (n_exp, LANES)))
    row_tok = pl.pallas_call(
        _invert_kernel,
        grid=(steps,),
        in_specs=[pl.BlockSpec((SUBLANES, tr), lambda i: (0, i), memory_space=pltpu.SMEM),
                  pl.BlockSpec(memory_space=pl.ANY)],
        out_specs=pl.BlockSpec(memory_space=pltpu.SMEM),
        out_shape=jax.ShapeDtypeStruct((n_blocks * MOE_ROWS,), I32),
        scratch_shapes=[pltpu.SemaphoreType.DMA((1,))],
        compiler_params=_cparams(("arbitrary",)),
        name="route_invert",
    )(dest, jnp.full((n_blocks * MOE_ROWS,), zero_row, I32))
    blk_row0 = jnp.arange(n_blocks, dtype=I32) * MOE_ROWS
    blk_e = jnp.minimum(jnp.sum((pad_end[None, :] <= blk_row0[:, None]).astype(I32), axis=1), n_exp - 1)
    nused = (pad_end[-1:] // MOE_ROWS).astype(I32)
    return dest, row_tok, blk_e, nused


CAST_ROWS = 256
WEIGHT_DMA_SPLIT = 8


def _expert_kernel(blk_e_ref, nused_ref, tok_ref, nxt_ref, src_ref, wg_hbm, wu_hbm, wd_hbm, y_ref,
                   xbuf, xsem, stage_g, stage_u, stage_d, bf_g, bf_u, bf_d, wsem, *, half, rows, trows):
    b = pl.program_id(0)
    nused = nused_ref[0]
    w_hbm = (wg_hbm, wu_hbm, wd_hbm)
    stage = (stage_g, stage_u, stage_d)
    wbf = (bf_g, bf_u, bf_d)

    def weight_copy(e, j):
        return pltpu.make_async_copy(w_hbm[j].at[e], stage[j], wsem.at[j])

    def start_weights(e, j):
        step = stage[j].shape[0] // WEIGHT_DMA_SPLIT
        for c in range(WEIGHT_DMA_SPLIT):
            sl = pl.ds(c * step, step)
            pltpu.make_async_copy(w_hbm[j].at[e, sl, :], stage[j].at[sl, :], wsem.at[j]).start()

    def row_copy(tok, slot, r):
        src = src_ref.at[pl.ds(pl.multiple_of(tok * trows, trows), trows), :]
        return pltpu.make_async_copy(src, xbuf.at[slot, pl.ds(r * trows, trows), :], xsem.at[slot])

    def start_gather(t_ref, slot):
        for r in range(rows):
            row_copy(t_ref[0, r], slot, r).start()

    @pl.when(b >= nused)
    def _():
        y_ref[...] = jnp.zeros_like(y_ref)

    @pl.when(b < nused)
    def _():
        e = blk_e_ref[b]

        @pl.when(b == 0)
        def _():
            for j in range(3):
                start_weights(e, j)
            def body(r, c):
                row_copy(tok_ref[0, r], 0, r).start()
                return c
            lax.fori_loop(0, rows, body, 0, unroll=8)

        @pl.when((b == 0) | (blk_e_ref[jnp.maximum(b - 1, 0)] != e))
        def _():
            nb = lax.while_loop(lambda k: (k < nused) & (blk_e_ref[jnp.minimum(k, nused - 1)] == e),
                                lambda k: k + 1, b + 1)
            for j in range(3):
                weight_copy(e, j).wait()
                n_steps = stage[j].shape[0] // CAST_ROWS

                def cast(c, carry, j=j):
                    sl = pl.ds(pl.multiple_of(c * CAST_ROWS, CAST_ROWS), CAST_ROWS)
                    wbf[j][sl, :] = stage[j][sl, :].astype(BF16)
                    return carry
                lax.fori_loop(0, n_steps, cast, 0)

                @pl.when(nb < nused)
                def _(j=j):
                    start_weights(blk_e_ref[jnp.minimum(nb, nused - 1)], j)

        slot = b % 2
        pltpu.make_async_copy(src_ref.at[pl.ds(0, rows * trows), :], xbuf.at[slot], xsem.at[slot]).wait()

        def compute(prefetch):
            parts = [_unpack_bf16_pair(xbuf[slot, pl.ds(s, rows, stride=trows), :]) for s in range(trows)]
            lo = jnp.concatenate([p[0] for p in parts], axis=1)
            hi = jnp.concatenate([p[1] for p in parts], axis=1)
            if prefetch:
                start_gather(nxt_ref, 1 - slot)

            def up(w_ref):
                return (jnp.dot(lo, w_ref[0:half, :], preferred_element_type=F32)
                        + jnp.dot(hi, w_ref[half:, :], preferred_element_type=F32))

            g = up(wbf[0])
            h = ((g * _sigmoid(g)) * up(wbf[1])).astype(BF16)
            y = jnp.dot(h, wbf[2][...], preferred_element_type=F32)
            y_ref[...] = _pack_bf16_pair(y[:, :half], y[:, half:])

        @pl.when(b + 1 < nused)
        def _():
            compute(True)

        @pl.when(b + 1 >= nused)
        def _():
            compute(False)


def _experts(blk_e, nused, row_tok, src, wg, wu, wd, n_blocks):
    n_exp, d, ff = wg.shape
    any_spec = pl.BlockSpec(memory_space=pl.ANY)
    tok = row_tok.reshape(n_blocks, 1, MOE_ROWS)
    trows = (d // 2) // LANES
    return pl.pallas_call(
        functools.partial(_expert_kernel, half=d // 2, rows=MOE_ROWS, trows=trows),
        grid_spec=pltpu.PrefetchScalarGridSpec(
            num_scalar_prefetch=2,
            grid=(n_blocks,),
            in_specs=[pl.BlockSpec((None, 1, MOE_ROWS), lambda b, be, nu: (b, 0, 0), memory_space=pltpu.SMEM),
                      pl.BlockSpec((None, 1, MOE_ROWS), lambda b, be, nu: (jnp.minimum(b + 1, n_blocks - 1), 0, 0),
                                   memory_space=pltpu.SMEM),
                      any_spec, any_spec, any_spec, any_spec],
            out_specs=pl.BlockSpec((MOE_ROWS, d // 2), lambda b, be, nu: (b, 0)),
            scratch_shapes=[pltpu.VMEM((2, MOE_ROWS * trows, LANES), U32), pltpu.SemaphoreType.DMA((2,)),
                            pltpu.VMEM((d, ff), F32), pltpu.VMEM((d, ff), F32), pltpu.VMEM((ff, d), F32),
                            pltpu.VMEM((d, ff), BF16), pltpu.VMEM((d, ff), BF16), pltpu.VMEM((ff, d), BF16),
                            pltpu.SemaphoreType.DMA((3,))]),
        out_shape=jax.ShapeDtypeStruct((n_blocks * MOE_ROWS, d // 2), U32),
        compiler_params=_cparams(("arbitrary",)),
        name="experts",
    )(blk_e, nused, tok, tok, src, wg, wu, wd)


def _combine_kernel(dest_ref, nxt_ref, x1_ref, wt_ref, mod_ref, fw_ref, y_ref, o_ref, ybuf_even, ybuf_odd, sem,
                    *, tiles_per_seq, d, lat):
    i = pl.program_id(0)
    last = pl.num_programs(0) - 1
    tm = x1_ref.shape[0]
    bufs = (ybuf_even, ybuf_odd)

    def row_copy(d_ref, par, kk, r):
        return pltpu.make_async_copy(y_ref.at[d_ref[kk, r]], bufs[par].at[kk, r], sem.at[par])

    @pl.when(i == 0)
    def _():
        def body(r, c):
            for kk in range(TOP_K):
                row_copy(dest_ref, 0, kk, r).start()
            return c
        lax.fori_loop(0, tm, body, 0, unroll=8)

    def step(par, prefetch):
        for kk in range(TOP_K):
            pltpu.make_async_copy(y_ref.at[pl.ds(0, tm)], bufs[par].at[kk], sem.at[par]).wait()
        if prefetch:
            for r in range(tm):
                for kk in range(TOP_K):
                    row_copy(nxt_ref, 1 - par, kk, r).start()
        row = (1 + i // tiles_per_seq) if lat else 0
        gate2 = mod_ref[pl.ds(row, 1), 5 * d:6 * d]
        wt = wt_ref[...]
        half = d // 2
        w0 = _lane_tile(_row_to_col(wt[0:1], tm), half // LANES)
        w1 = _lane_tile(_row_to_col(wt[1:2], tm), half // LANES)
        p0, p1 = bufs[par][0], bufs[par][1]
        sides = ((lambda p: pltpu.bitcast(p << 16, F32), slice(0, half)),
                 (lambda p: pltpu.bitcast(p & jnp.uint32(0xFFFF0000), F32), slice(half, d)))
        xs = [x1_ref[:, cols] + gate2[:, cols] * (w0 * unpack(p0) + w1 * unpack(p1)) for unpack, cols in sides]
        ms = (jnp.sum(xs[0] * xs[0], axis=-1, keepdims=True)
              + jnp.sum(xs[1] * xs[1], axis=-1, keepdims=True)) * (1.0 / d)
        scale = lax.rsqrt(ms + EPS)
        for x, (_, cols) in zip(xs, sides):
            o_ref[:, cols] = x * scale * fw_ref[:, cols]

    for par in range(2):
        @pl.when((i % 2 == par) & (i < last))
        def _(par=par):
            step(par, True)

        @pl.when((i % 2 == par) & (i == last))
        def _(par=par):
            step(par, False)


def _combine(dest, x1, rwt, mod, final_w, y_rows, *, row0, n_tok, seq_len, lat):
    d = x1.shape[1]
    tm = 256
    tile0 = row0 // tm
    last = tile0 + n_tok // tm - 1
    kern = functools.partial(_combine_kernel, tiles_per_seq=seq_len // tm, d=d, lat=lat)
    return pl.pallas_call(
        kern,
        grid=(n_tok // tm,),
        in_specs=[pl.BlockSpec((SUBLANES, tm), lambda i: (0, tile0 + i), memory_space=pltpu.SMEM),
                  pl.BlockSpec((SUBLANES, tm), lambda i: (0, jnp.minimum(tile0 + i + 1, last)),
                               memory_space=pltpu.SMEM),
                  pl.BlockSpec((tm, d), lambda i: (tile0 + i, 0)),
                  pl.BlockSpec((SUBLANES, tm), lambda i: (0, tile0 + i)),
                  pl.BlockSpec(mod.shape, lambda i: (0, 0)),
                  pl.BlockSpec((1, d), lambda i: (0, 0)),
                  pl.BlockSpec(memory_space=pl.ANY)],
        out_specs=pl.BlockSpec((tm, d), lambda i: (i, 0)),
        out_shape=jax.ShapeDtypeStruct((n_tok, d), F32),
        scratch_shapes=[pltpu.VMEM((TOP_K, tm, d // 2), U32), pltpu.VMEM((TOP_K, tm, d // 2), U32),
                        pltpu.SemaphoreType.DMA((2,))],
        compiler_params=_cparams(("arbitrary",)),
        name="combine_lat" if lat else "combine_ctx",
    )(dest, dest, x1, rwt, mod, final_w.reshape(1, d), y_rows)


def _gate_layout(w_gates, b_gates, heads):
    d = w_gates.shape[0]
    w = w_gates.reshape(d, 4, heads).transpose(0, 2, 1)
    w = jnp.pad(w, ((0, 0), (0, 0), (0, SUBLANES - 4))).reshape(d, heads * SUBLANES)
    b = b_gates.reshape(4, heads).T
    b = jnp.pad(b, ((0, 0), (0, SUBLANES - 4))).reshape(1, heads * SUBLANES)
    padl = LANES - heads * SUBLANES
    return jnp.pad(w, ((0, 0), (0, padl))), jnp.pad(b, ((0, 0), (0, padl)))


def kernel(x_prompt, x_sample, state_mlstm_c, state_mlstm_n, state_mlstm_m, state_rglru_h, c, c_ctx, w_ada, b_ada,
           norm1_w, w_in, b_gates, conv_w, conv_b, rg_wa, rg_ba, rg_wx, rg_bx, rg_lambda, mlstm_norm_w, w_out,
           norm2_w, router_group_w, router_group_b, router_expert_w, router_expert_b, expert_w_gate, expert_w_up,
           expert_w_down, final_norm_w):
    n_req, seq, d = x_prompt.shape
    n_lat, lat_seq, _ = x_sample.shape
    depth = w_in.shape[0]
    assert depth == 1, "the token-axis plumbing below is written for the single-layer trunk"
    heads, dk, dv = state_mlstm_c.shape[3:]
    rw = state_rglru_h.shape[-1]
    nblk = rg_wa.shape[2]
    assert rw // nblk == LANES
    n_groups, epg = router_expert_w.shape[1], router_expert_w.shape[3]
    n_exp = n_groups * epg
    qk, mw = heads * dk, heads * dv
    nc, nl = n_req * seq, n_lat * lat_seq
    nt = nc + nl
    assert nc % lat_seq == 0 and n_lat + 1 <= SUBLANES
    l = 0

    xp = x_prompt.reshape(nc, d)
    xs = x_sample.reshape(nl, d)
    cvec = jnp.zeros((SUBLANES, d), F32).at[0].set(c_ctx).at[1:1 + n_lat].set(c)
    mod = _ada(cvec, w_ada[l], b_ada[l])

    w = w_in[l]
    g0 = 2 * qk + 2 * mw
    wgate, bgate = _gate_layout(w[:, g0:g0 + 4 * heads], b_gates[l], heads)
    w_cat = jnp.concatenate([w[:, :qk], w[:, 2 * qk:g0], wgate, w[:, g0 + 4 * heads:]], axis=1).astype(BF16)
    w_kt = w[:, qk:2 * qk].T.astype(BF16)
    q, kt, v, o, gt, xr, xg = _in_proj(xp, xs, mod, norm1_w[l], w_cat, w_kt, bgate, lat_seq=lat_seq, heads=heads,
                                       dk=dk, dv=dv, rw=rw)

    mkw = dict(heads=heads, dk=dk, dv=dv)
    ya_c, new_c, new_n, new_m = _mlstm(q, kt, v, o, gt, mlstm_norm_w[l], row0=0, n_seq=n_req, t_len=seq,
                                       emit_state=True, **mkw)
    (ya_l,) = _mlstm(q, kt, v, o, gt, mlstm_norm_w[l], row0=nc, n_seq=n_lat, t_len=lat_seq,
                     state=(state_mlstm_c[:, l], state_mlstm_n[:, l], state_mlstm_m[:, l]), **mkw)

    wg = (0.5 * jnp.concatenate([rg_wa[l, 0], rg_wx[l, 0], rg_wa[l, 1], rg_wx[l, 1]], axis=-1)).astype(BF16)
    bg = 0.5 * jnp.concatenate([rg_ba[l, 0].reshape(nblk, 1, LANES), rg_bx[l, 0].reshape(nblk, 1, LANES),
                                rg_ba[l, 1].reshape(nblk, 1, LANES), rg_bx[l, 1].reshape(nblk, 1, LANES)], axis=-1)
    rargs = (xr, xg, conv_w[l], conv_b[l], wg, bg, rg_lambda[l])
    yb_c, new_h = _rglru(*rargs, row0=0, n_seq=n_req, t_len=seq, seg=seq, emit_state=True)
    (yb_l,) = _rglru(*rargs, row0=nc, n_seq=n_lat, t_len=lat_seq, seg=GRID_W, state=state_rglru_h[:, l])

    r_rows = -(-(SUBLANES + n_exp) // 16) * 16
    wr = jnp.zeros((r_rows, d), F32)
    wr = wr.at[0:n_groups].set(router_group_w[l].T)
    wr = wr.at[SUBLANES:SUBLANES + n_exp].set(router_expert_w[l].transpose(0, 2, 1).reshape(n_exp, d)).astype(BF16)
    br = jnp.zeros((r_rows, LANES), F32)
    br = br.at[0:n_groups, 0].set(router_group_b[l])
    br = br.at[SUBLANES:SUBLANES + n_exp, 0].set(router_expert_b[l].reshape(n_exp))
    x1, hp, ridx, rwt = _out_proj(ya_c, ya_l, yb_c, yb_l, xp, xs, mod, norm2_w[l], w_out[l].astype(BF16), wr, br,
                                  lat_seq=lat_seq, n_groups=n_groups, epg=epg)

    n_blocks = (nt * TOP_K) // MOE_ROWS + n_exp
    dest, row_tok, blk_e, nused = _routing(ridx, nt, n_exp, n_blocks, zero_row=nt)
    only_layer = lambda a: a.reshape(a.shape[1:])
    y_rows = _experts(blk_e, nused, row_tok, hp, only_layer(expert_w_gate), only_layer(expert_w_up),
                      only_layer(expert_w_down), n_blocks)

    y_prompt = _combine(dest, x1, rwt, mod, final_norm_w, y_rows, row0=0, n_tok=nc, seq_len=seq, lat=False)
    y_sample = _combine(dest, x1, rwt, mod, final_norm_w, y_rows, row0=nc, n_tok=nl, seq_len=lat_seq, lat=True)

    return (y_prompt.reshape(n_req, seq, d), y_sample.reshape(n_lat, lat_seq, d),
            new_c[:, None], new_n.reshape(n_req, 1, 2, heads, dk), new_m.reshape(n_req, 1, 2, heads),
            new_h[:, None])
```

```python
import functools

import jax
import jax.numpy as jnp
import numpy as np
from jax import lax
from jax.experimental import pallas as pl
from jax.experimental.pallas import tpu as pltpu

F32 = jnp.float32
BF16 = jnp.bfloat16
I32 = jnp.int32
U32 = jnp.uint32

EPS = 1e-6
GRID_W = 64
CONV_LEFT = 2
RGLRU_C = 8.0
TOP_K = 2
LANES = 128
SUBLANES = 8
MLSTM_L = 256
NEG = -1e30
VMEM_LIMIT = 56 * 1024 * 1024

_HIGHEST = lax.Precision.HIGHEST


def _cparams(sem, vmem=VMEM_LIMIT):
    return pltpu.CompilerParams(dimension_semantics=sem, vmem_limit_bytes=vmem)


def _sigmoid(x):
    return 0.5 * jnp.tanh(0.5 * x) + 0.5


def _row_to_col(r, n):
    return jnp.broadcast_to(r, (LANES, n)).T


def _lane_tile(x, reps):
    return x if reps == 1 else jnp.concatenate([x] * reps, axis=1)


def _ada_kernel(c_ref, w_ref, b_ref, o_ref):
    c = c_ref[...]
    s = (c * _sigmoid(c)).astype(BF16)
    o_ref[...] = jnp.dot(s, w_ref[...].astype(BF16), preferred_element_type=F32) + b_ref[...]


def _ada(cvec, w_ada, b_ada):
    d, n = w_ada.shape
    tn = 1024 if n % 1024 == 0 else 512
    assert n % tn == 0
    return pl.pallas_call(
        _ada_kernel,
        grid=(n // tn,),
        in_specs=[pl.BlockSpec((SUBLANES, d), lambda j: (0, 0)),
                  pl.BlockSpec((d, tn), lambda j: (0, j)),
                  pl.BlockSpec((1, tn), lambda j: (0, j))],
        out_specs=pl.BlockSpec((SUBLANES, tn), lambda j: (0, j)),
        out_shape=jax.ShapeDtypeStruct((SUBLANES, n), F32),
        compiler_params=_cparams(("arbitrary",)),
        name="ada",
    )(cvec, w_ada, b_ada.reshape(1, n))


def _modulated_norm(x, w, shift, scale):
    ms = jnp.mean(x * x, axis=-1, keepdims=True)
    return (x * lax.rsqrt(ms + EPS) * w) * (1.0 + scale) + shift


def _mod_row(i, nctx_tiles, tiles_per_lat):
    return jnp.where(i < nctx_tiles, 0, 1 + (i - nctx_tiles) // tiles_per_lat)


def _inproj_kernel(xp_ref, xs_ref, mod_ref, n1_ref, w_ref, wkt_ref, bg_ref,
                   q_ref, kt_ref, v_ref, o_ref, gt_ref, xr_ref, xg_ref,
                   *, nctx_tiles, tiles_per_lat, d, qk, mw, rw, gh, qscale):
    i = pl.program_id(0)
    x = jnp.where(i < nctx_tiles, xp_ref[...], xs_ref[...])
    row = _mod_row(i, nctx_tiles, tiles_per_lat)
    shift = mod_ref[pl.ds(row, 1), 0:d]
    scale = mod_ref[pl.ds(row, 1), d:2 * d]
    hb = _modulated_norm(x, n1_ref[...], shift, scale).astype(BF16)

    def proj(c0, width):
        return jnp.dot(hb, w_ref[:, c0:c0 + width], preferred_element_type=F32)

    c0 = 0
    q_ref[...] = (proj(c0, qk) * qscale).astype(BF16); c0 += qk
    kt_ref[...] = lax.dot_general(wkt_ref[...], hb, (((1,), (1,)), ((), ())),
                                  preferred_element_type=F32).astype(BF16)
    v_ref[...] = proj(c0, mw).astype(BF16); c0 += mw
    o_ref[...] = proj(c0, mw); c0 += mw
    zg = proj(c0, LANES) + bg_ref[...]; c0 += LANES
    lane = lax.broadcasted_iota(I32, zg.shape, 1)
    log_sig = jnp.minimum(zg, 0.0) - jnp.log1p(jnp.exp(-jnp.abs(zg)))
    zg = jnp.where(lane % 2 == 1, log_sig, zg)
    gt_ref[...] = zg.T[0:gh, :]
    xr_ref[...] = proj(c0, rw); c0 += rw
    xg_ref[...] = proj(c0, rw)


def _in_proj(xp, xs, mod, norm1_w, w_cat, w_kt, bg, *, lat_seq, heads, dk, dv, rw):
    nc, d = xp.shape
    nl = xs.shape[0]
    nt = nc + nl
    tm = 256
    qk, mw, gh = heads * dk, heads * dv, heads * SUBLANES
    nctx_tiles = nc // tm
    kern = functools.partial(_inproj_kernel, nctx_tiles=nctx_tiles, tiles_per_lat=lat_seq // tm, d=d, qk=qk, mw=mw,
                             rw=rw, gh=gh, qscale=dk ** -0.5)
    row = lambda i: (i, 0)
    const = lambda i: (0, 0)
    return pl.pallas_call(
        kern,
        grid=(nt // tm,),
        in_specs=[pl.BlockSpec((tm, d), lambda i: (jnp.minimum(i, nctx_tiles - 1), 0)),
                  pl.BlockSpec((tm, d), lambda i: (jnp.maximum(i - nctx_tiles, 0), 0)),
                  pl.BlockSpec(mod.shape, const),
                  pl.BlockSpec((1, d), const),
                  pl.BlockSpec(w_cat.shape, const, pipeline_mode=pl.Buffered(1)),
                  pl.BlockSpec(w_kt.shape, const, pipeline_mode=pl.Buffered(1)),
                  pl.BlockSpec((1, LANES), const)],
        out_specs=[pl.BlockSpec((tm, qk), row), pl.BlockSpec((qk, tm), lambda i: (0, i)), pl.BlockSpec((tm, mw), row),
                   pl.BlockSpec((tm, mw), row), pl.BlockSpec((gh, tm), lambda i: (0, i)),
                   pl.BlockSpec((tm, rw), row), pl.BlockSpec((tm, rw), row)],
        out_shape=[jax.ShapeDtypeStruct((nt, qk), BF16), jax.ShapeDtypeStruct((qk, nt), BF16),
                   jax.ShapeDtypeStruct((nt, mw), BF16), jax.ShapeDtypeStruct((nt, mw), F32),
                   jax.ShapeDtypeStruct((gh, nt), F32),
                   jax.ShapeDtypeStruct((nt, rw), F32), jax.ShapeDtypeStruct((nt, rw), F32)],
        compiler_params=_cparams(("arbitrary",)),
        name="in_proj",
    )(xp, xs, mod, norm1_w.reshape(1, d), w_cat, w_kt, bg)


def _mlstm_kernel(*refs, t_len, dk, dv, has_state, emit_state):
    it = iter(refs)
    q_ref, kt_ref, v_ref, o_ref, gt_ref, nw_ref, tri_ref = (next(it) for _ in range(7))
    if has_state:
        c0_ref, n0_ref, m0_ref = (next(it) for _ in range(3))
    ya_ref = next(it)
    if emit_state:
        cn_ref, nn_ref, mn_ref = (next(it) for _ in range(3))
    hf_scr, hb_scr, c_scr, ma_scr, p_scr, w_scr, em_scr, kw_scr, dm_scr = (next(it) for _ in range(9))
    ln = MLSTM_L
    nchunks = t_len // ln
    assert ln % LANES == 0 and dk == LANES
    gsz = 2 if nchunks % 2 == 0 else 1
    ngroups = nchunks // gsz
    h_scr = (hf_scr, hb_scr)
    ext = dv + LANES
    lrep = ln // LANES

    for d in range(2):
        if has_state:
            c_scr[d, :, 0:dv] = c0_ref[d]
            c_scr[d, :, dv:ext] = _row_to_col(n0_ref[d], dk)
            ma_scr[d] = m0_ref[d]
        else:
            c_scr[d] = jnp.zeros((dk, ext), F32)
            ma_scr[d] = jnp.zeros((1, 1), F32)

    def chunk_start(d, j):
        return pl.multiple_of((j if d == 0 else nchunks - 1 - j) * ln, ln)

    def stage_a(d, j, slot):
        t0 = chunk_start(d, j)
        q = q_ref[pl.ds(t0, ln), :]
        kt = kt_ref[:, pl.ds(t0, ln)]
        g8 = gt_ref[:, pl.ds(t0, ln)]
        cum8 = jnp.dot(g8, tri_ref[d], precision=_HIGHEST, preferred_element_type=F32)
        valid = tri_ref[1 - d] > 0.5
        li = g8[2 * d:2 * d + 1]
        lf = g8[2 * d + 1:2 * d + 2]
        cum_row = cum8[2 * d + 1:2 * d + 2]
        total = jnp.sum(lf, axis=1, keepdims=True)
        a_row = li - cum_row
        cum_col = _row_to_col(cum_row, ln)
        m_prev = ma_scr[d]
        dmat = jnp.where(valid, _lane_tile(cum_col, lrep) + a_row, NEG)
        inter = cum_col + m_prev
        m_t = jnp.maximum(inter, jnp.max(dmat, axis=1, keepdims=True))
        s = jnp.dot(q, kt, preferred_element_type=F32)
        bank, par = slot
        p_scr[d, bank, par] = (s * jnp.exp(dmat - _lane_tile(m_t, lrep))).astype(BF16)
        w_scr[d, bank, par] = jnp.exp(inter - m_t)
        em_scr[d, bank, par] = jnp.exp(-m_t)
        g_row = total + a_row
        m_new = jnp.maximum(total + m_prev, jnp.max(g_row, axis=1, keepdims=True))
        kw_scr[d, bank, par] = (kt.astype(F32) * jnp.exp(g_row - m_new)).astype(BF16)
        decay = jnp.exp(total + m_prev - m_new)
        rid = lax.broadcasted_iota(I32, (SUBLANES, LANES), 0)
        dm_scr[d, bank, par] = jnp.where(rid == 0, decay, m_new)
        ma_scr[d] = m_new

    def stage_b(d, j, slot):
        t0 = chunk_start(d, j)
        q = q_ref[pl.ds(t0, ln), :]
        v_ext = jnp.concatenate([v_ref[pl.ds(t0, ln), :], jnp.ones((ln, LANES), BF16)], axis=1)
        c_st = c_scr[d]
        bank, par = slot
        full = (jnp.dot(p_scr[d, bank, par], v_ext, preferred_element_type=F32)
                + _lane_tile(w_scr[d, bank, par], ext // LANES) * jnp.dot(q, c_st.astype(BF16),
                                                                         preferred_element_type=F32))
        inv = 1.0 / jnp.maximum(jnp.abs(full[:, dv:ext]), em_scr[d, bank, par])
        h_scr[d][pl.ds(t0, ln), :] = full[:, 0:dv] * _lane_tile(inv, dv // LANES)
        decay = dm_scr[d, bank, par, 0:1, 0:1]
        c_scr[d] = decay * c_st + jnp.dot(kw_scr[d, bank, par], v_ext, preferred_element_type=F32)

    def group_a(jj, bank):
        for par in range(gsz):
            for d in range(2):
                stage_a(d, gsz * jj + par, (bank, par))

    def group_b(jj, bank):
        for par in range(gsz):
            for d in range(2):
                stage_b(d, gsz * jj + par, (bank, par))

    group_a(0, 0)

    def body(jj, carry):
        bank = jj % 2
        group_b(jj, bank)
        group_a(jj + 1, 1 - bank)
        return carry

    lax.fori_loop(0, ngroups - 1, body, 0)
    group_b(ngroups - 1, (ngroups - 1) % 2)

    def finish(j, carry):
        t0 = pl.multiple_of(j * ln, ln)
        hs = hf_scr[pl.ds(t0, ln), :] + hb_scr[pl.ds(t0, ln), :]
        ms = jnp.mean(hs * hs, axis=1, keepdims=True)
        y = hs * lax.rsqrt(ms + EPS) * nw_ref[...]
        ya_ref[pl.ds(t0, ln), :] = (_sigmoid(o_ref[pl.ds(t0, ln), :]) * y).astype(BF16)
        return carry

    lax.fori_loop(0, nchunks, finish, 0)

    if emit_state:
        for d in range(2):
            cn_ref[d] = c_scr[d, :, 0:dv]
            nn_ref[d] = c_scr[d, :, dv:ext].T[0:1, :]
            mn_ref[d] = dm_scr[d, (ngroups - 1) % 2, gsz - 1, 1:2, 0:1]


def _mlstm_tri():
    r = np.arange(MLSTM_L)
    fwd = (r[:, None] <= r[None, :]).astype(np.float32)
    return jnp.asarray(np.stack([fwd, fwd.T]))


def _mlstm(q, kt, v, o, gt, norm_w, *, row0, n_seq, t_len, heads, dk, dv, state=None, emit_state=False):
    blk0 = row0 // t_len
    tok = lambda s, h: (blk0 + s, h)
    in_specs = [pl.BlockSpec((t_len, dk), tok), pl.BlockSpec((dk, t_len), lambda s, h: (h, blk0 + s)),
                pl.BlockSpec((t_len, dv), tok), pl.BlockSpec((t_len, dv), tok),
                pl.BlockSpec((SUBLANES, t_len), lambda s, h: (h, blk0 + s)),
                pl.BlockSpec((None, 1, dv), lambda s, h: (h, 0, 0)),
                pl.BlockSpec((2, MLSTM_L, MLSTM_L), lambda s, h: (0, 0, 0))]
    args = [q, kt, v, o, gt, norm_w.reshape(heads, 1, dv), _mlstm_tri()]
    if state is not None:
        c0, n0, m0 = state
        in_specs += [pl.BlockSpec((None, 2, None, dk, dv), lambda s, h: (s, 0, h, 0, 0)),
                     pl.BlockSpec((None, 2, None, 1, dk), lambda s, h: (s, 0, h, 0, 0)),
                     pl.BlockSpec((None, 2, None, 1, 1), lambda s, h: (s, 0, h, 0, 0))]
        args += [c0, n0.reshape(n_seq, 2, heads, 1, dk), m0.reshape(n_seq, 2, heads, 1, 1)]
    out_specs = [pl.BlockSpec((t_len, dv), lambda s, h: (s, h))]
    out_shape = [jax.ShapeDtypeStruct((n_seq * t_len, heads * dv), BF16)]
    if emit_state:
        out_specs += [pl.BlockSpec((None, 2, None, dk, dv), lambda s, h: (s, 0, h, 0, 0)),
                      pl.BlockSpec((None, 2, None, 1, dk), lambda s, h: (s, 0, h, 0, 0)),
                      pl.BlockSpec((None, 2, None, 1, 1), lambda s, h: (s, 0, h, 0, 0))]
        out_shape += [jax.ShapeDtypeStruct((n_seq, 2, heads, dk, dv), F32),
                      jax.ShapeDtypeStruct((n_seq, 2, heads, 1, dk), F32),
                      jax.ShapeDtypeStruct((n_seq, 2, heads, 1, 1), F32)]
    kern = functools.partial(_mlstm_kernel, t_len=t_len, dk=dk, dv=dv, has_state=state is not None,
                             emit_state=emit_state)
    return pl.pallas_call(
        kern,
        grid=(n_seq, heads),
        in_specs=in_specs,
        out_specs=out_specs,
        out_shape=out_shape,
        scratch_shapes=[pltpu.VMEM((t_len, dv), F32), pltpu.VMEM((t_len, dv), F32),
                        pltpu.VMEM((2, dk, dv + LANES), F32), pltpu.VMEM((2, 1, 1), F32),
                        pltpu.VMEM((2, 2, 2, MLSTM_L, MLSTM_L), BF16), pltpu.VMEM((2, 2, 2, MLSTM_L, LANES), F32),
                        pltpu.VMEM((2, 2, 2, MLSTM_L, LANES), F32), pltpu.VMEM((2, 2, 2, dk, MLSTM_L), BF16),
                        pltpu.VMEM((2, 2, 2, SUBLANES, LANES), F32)],
        compiler_params=_cparams(("arbitrary", "arbitrary")),
        name="mlstm_state" if emit_state else "mlstm",
    )(*args)


def _gelu_tanh(x):
    return x * (0.5 * (1.0 + jnp.tanh(0.7978845608028654 * (x + 0.044715 * (x * x * x)))))


def _softplus(x):
    return jnp.maximum(x, 0.0) + jnp.log1p(jnp.exp(-jnp.abs(x)))


SCAN_SPLIT = 1
SCAN_GROUP = 4


def _rglru_kernel(*refs, t_len, seg, sub, pitch, tc, has_state, emit_state):
    it = iter(refs)
    xr_ref, xg_ref, cw_ref, cb_ref, wg_ref, bg_ref, lam_ref = (next(it) for _ in range(7))
    if has_state:
        h0_ref = next(it)
    yb_ref = next(it)
    if emit_state:
        hn_ref = next(it)
    a_scr, u_scr, cin_scr = (next(it) for _ in range(3))
    nchunks = t_len // tc
    piece = min(tc, sub)
    npieces = tc // piece
    ntile = xr_ref.shape[1] // LANES
    chains = [(d, lt) for d in range(2) for lt in range(ntile)]

    def scan_rows(t0, p):
        t = t0 + p * piece
        i = t // sub
        return pl.ds(pl.multiple_of(i * pitch + (t - i * sub), SUBLANES), piece), i

    ka = (-0.5 * RGLRU_C * 1.4426950408889634) * _softplus(-lam_ref[...])

    def gates(c, carry):
        t0 = pl.multiple_of(c * tc, tc)
        pos = lax.broadcasted_iota(I32, (tc, LANES), 0) % seg
        for lt in range(ntile):
            cols = slice(lt * LANES, (lt + 1) * LANES)
            x = xr_ref[pl.ds(t0, tc), cols]
            xc = cb_ref[:, cols] + cw_ref[CONV_LEFT:CONV_LEFT + 1, cols] * x
            for j in range(cw_ref.shape[0]):
                off = j - CONV_LEFT
                if off == 0:
                    continue
                shifted = pltpu.roll(x, (-off) % tc, 0)
                ok = (pos >= -off) if off < 0 else (pos < seg - off)
                xc = xc + cw_ref[j:j + 1, cols] * jnp.where(ok, shifted, 0.0)
            zh = jnp.dot(xc.astype(BF16), wg_ref[lt], preferred_element_type=F32) + bg_ref[lt]
            hx = 0.5 * xc
            for d in range(2):
                kd = ka[d:d + 1, cols]
                a = jnp.exp2(jnp.tanh(zh[:, (2 * d) * LANES:(2 * d + 1) * LANES]) * kd + kd)
                igx = hx * jnp.tanh(zh[:, (2 * d + 1) * LANES:(2 * d + 2) * LANES]) + hx
                u = jnp.sqrt(1.0 - a * a) * igx
                for p in range(npieces):
                    rows, _ = scan_rows(t0, p)
                    a_scr[d, lt, rows, :] = a[p * piece:(p + 1) * piece]
                    u_scr[d, lt, rows, :] = u[p * piece:(p + 1) * piece]
        return carry

    lax.fori_loop(0, nchunks, gates, 0)

    seg_len = sub // SCAN_SPLIT
    seg_chains = [(d, lt, s) for (d, lt) in chains for s in range(SCAN_SPLIT)]

    def scan(jg, carry):
        def rows(d, s, k):
            j = jg * SCAN_GROUP + k
            return pl.ds(s * seg_len + (j if d == 0 else seg_len - 1 - j), SUBLANES, stride=pitch)

        loaded = [[(a_scr[d, lt, rows(d, s, k), :], u_scr[d, lt, rows(d, s, k), :]) for k in range(SCAN_GROUP)]
                  for (d, lt, s) in seg_chains]
        out = []
        for (d, lt, s), (h, p), steps in zip(seg_chains, carry, loaded):
            for k, (a, u) in enumerate(steps):
                h = a * h + u
                p = a * p
                a_scr[d, lt, rows(d, s, k), :] = p
                u_scr[d, lt, rows(d, s, k), :] = h
            out.append((h, p))
        return tuple(out)

    zero = jnp.zeros((SUBLANES, LANES), F32)
    one = jnp.ones((SUBLANES, LANES), F32)
    assert seg_len % SCAN_GROUP == 0
    ends = dict(zip(seg_chains, lax.fori_loop(0, seg_len // SCAN_GROUP, scan, tuple((zero, one) for _ in seg_chains))))

    order = [(i, s) for i in range(SUBLANES) for s in range(SCAN_SPLIT)]
    for (d, lt) in chains:
        cols = slice(lt * LANES, (lt + 1) * LANES)
        cin = h0_ref[d:d + 1, cols] if has_state else jnp.zeros((1, LANES), F32)
        for (i, s) in (order if d == 0 else reversed(order)):
            h, p = ends[(d, lt, s)]
            cin_scr[d, lt, s * SUBLANES + i:s * SUBLANES + i + 1, :] = cin
            cin = h[i:i + 1] + p[i:i + 1] * cin
        if emit_state:
            hn_ref[d:d + 1, cols] = cin

    fpiece = min(tc, seg_len)

    def finish(c, carry):
        t0 = pl.multiple_of(c * tc, tc)
        for lt in range(ntile):
            cols = slice(lt * LANES, (lt + 1) * LANES)
            for p in range(tc // fpiece):
                t = t0 + p * fpiece
                i = t // sub
                local = t - i * sub
                rows = pl.ds(pl.multiple_of(i * pitch + local, SUBLANES), fpiece)
                crow = pl.ds((local // seg_len) * SUBLANES + i, 1)
                h = (u_scr[0, lt, rows, :] + a_scr[0, lt, rows, :] * cin_scr[0, lt, crow, :]
                     + u_scr[1, lt, rows, :] + a_scr[1, lt, rows, :] * cin_scr[1, lt, crow, :])
                nat = pl.ds(pl.multiple_of(t, SUBLANES), fpiece)
                yb_ref[nat, cols] = (h * _gelu_tanh(xg_ref[nat, cols])).astype(BF16)
        return carry

    lax.fori_loop(0, nchunks, finish, 0)


def _rglru(xr, xg, conv_w, conv_b, wg, bg, lam, *, row0, n_seq, t_len, seg, state=None, emit_state=False):
    rw = xr.shape[1]
    ntile = 2
    cb = ntile * LANES
    assert rw % cb == 0
    blk0 = row0 // t_len
    sub = t_len // SUBLANES
    pitch = sub + SUBLANES
    tc = 256
    tok = lambda s, g: (blk0 + s, g)
    in_specs = [pl.BlockSpec((t_len, cb), tok), pl.BlockSpec((t_len, cb), tok),
                pl.BlockSpec((conv_w.shape[0], cb), lambda s, g: (0, g)),
                pl.BlockSpec((1, cb), lambda s, g: (0, g)),
                pl.BlockSpec((ntile, LANES, 4 * LANES), lambda s, g: (g, 0, 0)),
                pl.BlockSpec((ntile, 1, 4 * LANES), lambda s, g: (g, 0, 0)),
                pl.BlockSpec((2, cb), lambda s, g: (0, g))]
    args = [xr, xg, conv_w, conv_b.reshape(1, rw), wg, bg, lam]
    if state is not None:
        in_specs.append(pl.BlockSpec((None, 2, cb), lambda s, g: (s, 0, g)))
        args.append(state)
    out_specs = [pl.BlockSpec((t_len, cb), lambda s, g: (s, g))]
    out_shape = [jax.ShapeDtypeStruct((n_seq * t_len, rw), BF16)]
    if emit_state:
        out_specs.append(pl.BlockSpec((None, 2, cb), lambda s, g: (s, 0, g)))
        out_shape.append(jax.ShapeDtypeStruct((n_seq, 2, rw), F32))
    kern = functools.partial(_rglru_kernel, t_len=t_len, seg=seg, sub=sub, pitch=pitch, tc=tc,
                             has_state=state is not None, emit_state=emit_state)
    return pl.pallas_call(
        kern,
        grid=(n_seq, rw // cb),
        in_specs=in_specs,
        out_specs=out_specs,
        out_shape=out_shape,
        scratch_shapes=[pltpu.VMEM((2, ntile, SUBLANES * pitch, LANES), F32),
                        pltpu.VMEM((2, ntile, SUBLANES * pitch, LANES), F32),
                        pltpu.VMEM((2, ntile, SCAN_SPLIT * SUBLANES, LANES), F32)],
        compiler_params=_cparams(("arbitrary", "arbitrary")),
        name="rglru_state" if emit_state else "rglru",
    )(*args)


def _pack_bf16_pair(lo, hi):
    def rounded_bits(x):
        return pltpu.bitcast(x.astype(BF16).astype(F32), U32)
    return (rounded_bits(lo) >> 16) | rounded_bits(hi)


def _unpack_bf16_pair(w):
    lo = pltpu.bitcast(w << 16, F32).astype(BF16)
    hi = pltpu.bitcast(w & jnp.uint32(0xFFFF0000), F32).astype(BF16)
    return lo, hi


def _outproj_kernel(yac_ref, yal_ref, ybc_ref, ybl_ref, xp_ref, xs_ref, mod_ref, n2_ref, wo_ref, wr_ref, br_ref,
                    x1_ref, hp_ref, ridx_ref, rwt_ref,
                    *, nctx_tiles, ntok_tiles, tiles_per_lat, d, mw, n_groups, epg):
    i = pl.program_id(0)

    @pl.when(i == ntok_tiles)
    def _():
        x1_ref[...] = jnp.zeros_like(x1_ref)
        hp_ref[...] = jnp.zeros_like(hp_ref)
        ridx_ref[...] = jnp.zeros_like(ridx_ref)
        rwt_ref[...] = jnp.zeros_like(rwt_ref)

    def token_tile(x_ref, ya_ref, yb_ref):
        x, ya, yb = x_ref[...], ya_ref[...], yb_ref[...]
        row = _mod_row(i, nctx_tiles, tiles_per_lat)
        gate1 = mod_ref[pl.ds(row, 1), 2 * d:3 * d]
        shift2 = mod_ref[pl.ds(row, 1), 3 * d:4 * d]
        scale2 = mod_ref[pl.ds(row, 1), 4 * d:5 * d]
        y = (jnp.dot(ya, wo_ref[0:mw, :], preferred_element_type=F32)
             + jnp.dot(yb, wo_ref[mw:, :], preferred_element_type=F32))
        x1 = x + gate1 * y
        x1_ref[...] = x1
        h2 = _modulated_norm(x1, n2_ref[...], shift2, scale2)
        half = d // 2
        packed = _pack_bf16_pair(h2[:, :half], h2[:, half:])
        trows = half // LANES
        for s in range(trows):
            hp_ref[pl.ds(s, x.shape[0], stride=trows), :] = packed[:, s * LANES:(s + 1) * LANES]

        lt = lax.dot_general(wr_ref[...], h2.astype(BF16), (((1,), (1,)), ((), ())),
                             preferred_element_type=F32) + br_ref[:, 0:1]
        gidx = lax.broadcasted_iota(I32, (SUBLANES, lt.shape[1]), 0)
        gl = jnp.where(gidx < n_groups, lt[0:SUBLANES], -jnp.inf)
        gmax = jnp.max(gl, axis=0, keepdims=True)
        grp = jnp.min(jnp.where(gl == gmax, gidx, n_groups), axis=0, keepdims=True)
        p_grp = 1.0 / jnp.sum(jnp.exp(gl - gmax), axis=0, keepdims=True)
        el = lt[SUBLANES:SUBLANES + epg]
        for g in range(1, n_groups):
            el = jnp.where(grp == g, lt[SUBLANES + g * epg:SUBLANES + (g + 1) * epg], el)
        eidx = lax.broadcasted_iota(I32, el.shape, 0)
        v1 = jnp.max(el, axis=0, keepdims=True)
        i1 = jnp.min(jnp.where(el == v1, eidx, epg), axis=0, keepdims=True)
        el2 = jnp.where(eidx == i1, -jnp.inf, el)
        v2 = jnp.max(el2, axis=0, keepdims=True)
        i2 = jnp.min(jnp.where(el2 == v2, eidx, epg), axis=0, keepdims=True)
        e2 = jnp.exp(v2 - v1)
        w1 = p_grp / (1.0 + e2)
        w2 = p_grp * e2 / (1.0 + e2)
        rid = lax.broadcasted_iota(I32, ridx_ref.shape, 0)
        ridx_ref[...] = jnp.where(rid == 0, grp * epg + i1, jnp.where(rid == 1, grp * epg + i2, 0))
        rwt_ref[...] = jnp.where(rid == 0, w1, jnp.where(rid == 1, w2, 0.0))

    @pl.when(i < nctx_tiles)
    def _():
        token_tile(xp_ref, yac_ref, ybc_ref)

    @pl.when((i >= nctx_tiles) & (i < ntok_tiles))
    def _():
        token_tile(xs_ref, yal_ref, ybl_ref)


def _out_proj(ya_c, ya_l, yb_c, yb_l, xp, xs, mod, norm2_w, w_out, wr, br, *, lat_seq, n_groups, epg):
    nc, d = xp.shape
    nl = xs.shape[0]
    nt = nc + nl
    mw = ya_c.shape[1]
    tm = 256
    trows = (d // 2) // LANES
    nctx_tiles, ntok_tiles = nc // tm, nt // tm
    kern = functools.partial(_outproj_kernel, nctx_tiles=nctx_tiles, ntok_tiles=ntok_tiles,
                             tiles_per_lat=lat_seq // tm, d=d, mw=mw, n_groups=n_groups, epg=epg)
    ctx = lambda i: (jnp.minimum(i, nctx_tiles - 1), 0)
    lat = lambda i: (jnp.clip(i - nctx_tiles, 0, nl // tm - 1), 0)
    row = lambda i: (i, 0)
    const = lambda i: (0, 0)
    return pl.pallas_call(
        kern,
        grid=(ntok_tiles + 1,),
        in_specs=[pl.BlockSpec((tm, mw), ctx), pl.BlockSpec((tm, mw), lat),
                  pl.BlockSpec((tm, yb_c.shape[1]), ctx), pl.BlockSpec((tm, yb_c.shape[1]), lat),
                  pl.BlockSpec((tm, d), ctx), pl.BlockSpec((tm, d), lat),
                  pl.BlockSpec(mod.shape, const),
                  pl.BlockSpec((1, d), const),
                  pl.BlockSpec(w_out.shape, const, pipeline_mode=pl.Buffered(1)),
                  pl.BlockSpec(wr.shape, const),
                  pl.BlockSpec(br.shape, const)],
        out_specs=[pl.BlockSpec((tm, d), row), pl.BlockSpec((tm * trows, LANES), row),
                   pl.BlockSpec((SUBLANES, tm), lambda i: (0, i)), pl.BlockSpec((SUBLANES, tm), lambda i: (0, i))],
        out_shape=[jax.ShapeDtypeStruct((nt + tm, d), F32), jax.ShapeDtypeStruct(((nt + tm) * trows, LANES), U32),
                   jax.ShapeDtypeStruct((SUBLANES, nt + tm), I32), jax.ShapeDtypeStruct((SUBLANES, nt + tm), F32)],
        compiler_params=_cparams(("arbitrary",)),
        name="out_proj",
    )(ya_c, ya_l, yb_c, yb_l, xp, xs, mod, norm2_w.reshape(1, d), w_out, wr, br)


MOE_ROWS = 256


ROUTE_TILE = 512


def _rank_kernel(ridx_ref, tri_ref, rank_ref, cnt_ref, carry_scr, *, n_exp):
    @pl.when(pl.program_id(0) == 0)
    def _():
        carry_scr[...] = jnp.zeros_like(carry_scr)

    e = ridx_ref[...]
    tr = e.shape[1]
    eid = lax.broadcasted_iota(I32, (n_exp, tr), 0)
    carry = carry_scr[:, 0:1]
    ranks = []
    for kk in range(TOP_K):
        hit = eid == e[kk:kk + 1]
        cum = jnp.dot(jnp.where(hit, 1.0, 0.0).astype(BF16), tri_ref[...], preferred_element_type=F32)
        ranks.append(jnp.sum(jnp.where(hit, cum + carry, 0.0), axis=0, keepdims=True) - 1.0)
        carry = carry + cum[:, tr - 1:tr]
    carry_scr[...] = jnp.broadcast_to(carry, carry_scr.shape)
    cnt_ref[...] = jnp.broadcast_to(carry, cnt_ref.shape)
    rid = lax.broadcasted_iota(I32, rank_ref.shape, 0)
    rank_ref[...] = jnp.where(rid == 0, ranks[0], jnp.where(rid == 1, ranks[1], 0.0)).astype(I32)


def _dest_kernel(ridx_ref, rank_ref, pstart_ref, dest_ref, *, n_exp):
    e = ridx_ref[...]
    tr = e.shape[1]
    eid = lax.broadcasted_iota(I32, (n_exp, tr), 0)
    ps = pstart_ref[:, 0:1]
    rows = [jnp.sum(jnp.where(eid == e[kk:kk + 1], ps, 0.0), axis=0, keepdims=True) for kk in range(TOP_K)]
    rid = lax.broadcasted_iota(I32, dest_ref.shape, 0)
    dest_ref[...] = rank_ref[...] + jnp.where(rid == 0, rows[0], jnp.where(rid == 1, rows[1], 0.0)).astype(I32)


def _invert_kernel(dest_ref, fill_ref, rowtok_ref, sem):
    i = pl.program_id(0)
    tr = dest_ref.shape[1]

    @pl.when(i == 0)
    def _():
        fill = pltpu.make_async_copy(fill_ref, rowtok_ref, sem.at[0])
        fill.start()
        fill.wait()

    def body(r, c):
        for kk in range(TOP_K):
            rowtok_ref[dest_ref[kk, r]] = i * tr + r
        return c
    lax.fori_loop(0, tr, body, 0, unroll=16)


def _routing(ridx, n_tok, n_exp, n_blocks, zero_row):
    tr = ROUTE_TILE
    steps = n_tok // tr
    tri = jnp.asarray(np.triu(np.ones((tr, tr), np.float32)), BF16)
    tile = pl.BlockSpec((SUBLANES, tr), lambda i: (0, i))
    cnt_spec = pl.BlockSpec((n_exp, LANES), lambda i: (0, 0))
    rank, cnt = pl.pallas_call(
        functools.partial(_rank_kernel, n_exp=n_exp),
        grid=(steps,),
        in_specs=[tile, pl.BlockSpec((tr, tr), lambda i: (0, 0))],
        out_specs=[tile, cnt_spec],
        out_shape=[jax.ShapeDtypeStruct((SUBLANES, n_tok), I32), jax.ShapeDtypeStruct((n_exp, LANES), F32)],
        scratch_shapes=[pltpu.VMEM((n_exp, LANES), F32)],
        compiler_params=_cparams(("arbitrary",)),
        name="route_rank",
    )(ridx, tri)
    counts = cnt[:, 0].astype(I32)
    padded = (counts + MOE_ROWS - 1) // MOE_ROWS * MOE_ROWS
    pad_end = jnp.cumsum(padded)
    pad_start = pad_end - padded
    dest = pl.pallas_call(
        functools.partial(_dest_kernel, n_exp=n_exp),
        grid=(steps,),
        in_specs=[tile, tile, cnt_spec],
        out_specs=tile,
        out_shape=jax.ShapeDtypeStruct((SUBLANES, n_tok), I32),
        compiler_params=_cparams(("arbitrary",)),
        name="route_dest",
    )(ridx, rank, jnp.broadcast_to(pad_start.astype(F32)[:, None], (n_exp, LANES)))
    row_tok = pl.pallas_call(
        _invert_kernel,
        grid=(steps,),
        in_specs=[pl.BlockSpec((SUBLANES, tr), lambda i: (0, i), memory_space=pltpu.SMEM),
                  pl.BlockSpec(memory_space=pl.ANY)],
        out_specs=pl.BlockSpec(memory_space=pltpu.SMEM),
        out_shape=jax.ShapeDtypeStruct((n_blocks * MOE_ROWS,), I32),
        scratch_shapes=[pltpu.SemaphoreType.DMA((1,))],
        compiler_params=_cparams(("arbitrary",)),
        name="route_invert",
    )(dest, jnp.full((n_blocks * MOE_ROWS,), zero_row, I32))
    blk_row0 = jnp.arange(n_blocks, dtype=I32) * MOE_ROWS
    blk_e = jnp.minimum(jnp.sum((pad_end[None, :] <= blk_row0[:, None]).astype(I32), axis=1), n_exp - 1)
    nused = (pad_end[-1:] // MOE_ROWS).astype(I32)
    return dest, row_tok, blk_e, nused


CAST_ROWS = 256
WEIGHT_DMA_SPLIT = 8


def _expert_kernel(blk_e_ref, nused_ref, tok_ref, nxt_ref, src_ref, wg_hbm, wu_hbm, wd_hbm, y_ref,
                   xbuf, xsem, stage_g, stage_u, stage_d, bf_g, bf_u, bf_d, wsem, *, half, rows, trows):
    b = pl.program_id(0)
    nused = nused_ref[0]
    w_hbm = (wg_hbm, wu_hbm, wd_hbm)
    stage = (stage_g, stage_u, stage_d)
    wbf = (bf_g, bf_u, bf_d)

    def weight_copy(e, j):
        return pltpu.make_async_copy(w_hbm[j].at[e], stage[j], wsem.at[j])

    def start_weights(e, j):
        step = stage[j].shape[0] // WEIGHT_DMA_SPLIT
        for c in range(WEIGHT_DMA_SPLIT):
            sl = pl.ds(c * step, step)
            pltpu.make_async_copy(w_hbm[j].at[e, sl, :], stage[j].at[sl, :], wsem.at[j]).start(priority=1)

    def row_copy(tok, slot, r):
        src = src_ref.at[pl.ds(pl.multiple_of(tok * trows, trows), trows), :]
        return pltpu.make_async_copy(src, xbuf.at[slot, pl.ds(r * trows, trows), :], xsem.at[slot])

    def start_gather(t_ref, slot):
        for r in range(rows):
            row_copy(t_ref[0, r], slot, r).start()

    @pl.when(b >= nused)
    def _():
        y_ref[...] = jnp.zeros_like(y_ref)

    @pl.when(b < nused)
    def _():
        e = blk_e_ref[b]

        @pl.when(b == 0)
        def _():
            for j in range(3):
                start_weights(e, j)
            def body(r, c):
                row_copy(tok_ref[0, r], 0, r).start()
                return c
            lax.fori_loop(0, rows, body, 0, unroll=8)

        @pl.when((b == 0) | (blk_e_ref[jnp.maximum(b - 1, 0)] != e))
        def _():
            nb = lax.while_loop(lambda k: (k < nused) & (blk_e_ref[jnp.minimum(k, nused - 1)] == e),
                                lambda k: k + 1, b + 1)
            for j in range(3):
                weight_copy(e, j).wait()
                n_steps = stage[j].shape[0] // CAST_ROWS

                def cast(c, carry, j=j):
                    sl = pl.ds(pl.multiple_of(c * CAST_ROWS, CAST_ROWS), CAST_ROWS)
                    wbf[j][sl, :] = stage[j][sl, :].astype(BF16)
                    return carry
                lax.fori_loop(0, n_steps, cast, 0)

                @pl.when(nb < nused)
                def _(j=j):
                    start_weights(blk_e_ref[jnp.minimum(nb, nused - 1)], j)

        slot = b % 2
        pltpu.make_async_copy(src_ref.at[pl.ds(0, rows * trows), :], xbuf.at[slot], xsem.at[slot]).wait()

        def compute(prefetch):
            parts = [_unpack_bf16_pair(xbuf[slot, pl.ds(s, rows, stride=trows), :]) for s in range(trows)]
            lo = jnp.concatenate([p[0] for p in parts], axis=1)
            hi = jnp.concatenate([p[1] for p in parts], axis=1)
            if prefetch:
                start_gather(nxt_ref, 1 - slot)

            def up(w_ref):
                return (jnp.dot(lo, w_ref[0:half, :], preferred_element_type=F32)
                        + jnp.dot(hi, w_ref[half:, :], preferred_element_type=F32))

            g = up(wbf[0])
            h = ((g * _sigmoid(g)) * up(wbf[1])).astype(BF16)
            y = jnp.dot(h, wbf[2][...], preferred_element_type=F32)
            y_ref[...] = _pack_bf16_pair(y[:, :half], y[:, half:])

        @pl.when(b + 1 < nused)
        def _():
            compute(True)

        @pl.when(b + 1 >= nused)
        def _():
            compute(False)


def _experts(blk_e, nused, row_tok, src, wg, wu, wd, n_blocks):
    n_exp, d, ff = wg.shape
    any_spec = pl.BlockSpec(memory_space=pl.ANY)
    tok = row_tok.reshape(n_blocks, 1, MOE_ROWS)
    trows = (d // 2) // LANES
    return pl.pallas_call(
        functools.partial(_expert_kernel, half=d // 2, rows=MOE_ROWS, trows=trows),
        grid_spec=pltpu.PrefetchScalarGridSpec(
            num_scalar_prefetch=2,
            grid=(n_blocks,),
            in_specs=[pl.BlockSpec((None, 1, MOE_ROWS), lambda b, be, nu: (b, 0, 0), memory_space=pltpu.SMEM),
                      pl.BlockSpec((None, 1, MOE_ROWS), lambda b, be, nu: (jnp.minimum(b + 1, n_blocks - 1), 0, 0),
                                   memory_space=pltpu.SMEM),
                      any_spec, any_spec, any_spec, any_spec],
            out_specs=pl.BlockSpec((MOE_ROWS, d // 2), lambda b, be, nu: (b, 0)),
            scratch_shapes=[pltpu.VMEM((2, MOE_ROWS * trows, LANES), U32), pltpu.SemaphoreType.DMA((2,)),
                            pltpu.VMEM((d, ff), F32), pltpu.VMEM((d, ff), F32), pltpu.VMEM((ff, d), F32),
                            pltpu.VMEM((d, ff), BF16), pltpu.VMEM((d, ff), BF16), pltpu.VMEM((ff, d), BF16),
                            pltpu.SemaphoreType.DMA((3,))]),
        out_shape=jax.ShapeDtypeStruct((n_blocks * MOE_ROWS, d // 2), U32),
        compiler_params=_cparams(("arbitrary",)),
        name="experts",
    )(blk_e, nused, tok, tok, src, wg, wu, wd)


def _combine_kernel(dest_ref, nxt_ref, x1_ref, wt_ref, mod_ref, fw_ref, y_ref, o_ref, ybuf_even, ybuf_odd, sem,
                    *, tiles_per_seq, d, lat):
    i = pl.program_id(0)
    last = pl.num_programs(0) - 1
    tm = x1_ref.shape[0]
    bufs = (ybuf_even, ybuf_odd)

    def row_copy(d_ref, par, kk, r):
        return pltpu.make_async_copy(y_ref.at[d_ref[kk, r]], bufs[par].at[kk, r], sem.at[par])

    @pl.when(i == 0)
    def _():
        def body(r, c):
            for kk in range(TOP_K):
                row_copy(dest_ref, 0, kk, r).start()
            return c
        lax.fori_loop(0, tm, body, 0, unroll=8)

    def step(par, prefetch):
        for kk in range(TOP_K):
            pltpu.make_async_copy(y_ref.at[pl.ds(0, tm)], bufs[par].at[kk], sem.at[par]).wait()
        if prefetch:
            for r in range(tm):
                for kk in range(TOP_K):
                    row_copy(nxt_ref, 1 - par, kk, r).start(priority=kk)
        row = (1 + i // tiles_per_seq) if lat else 0
        gate2 = mod_ref[pl.ds(row, 1), 5 * d:6 * d]
        wt = wt_ref[...]
        half = d // 2
        w0 = _lane_tile(_row_to_col(wt[0:1], tm), half // LANES)
        w1 = _lane_tile(_row_to_col(wt[1:2], tm), half // LANES)
        p0, p1 = bufs[par][0], bufs[par][1]
        sides = ((lambda p: pltpu.bitcast(p << 16, F32), slice(0, half)),
                 (lambda p: pltpu.bitcast(p & jnp.uint32(0xFFFF0000), F32), slice(half, d)))
        xs = [x1_ref[:, cols] + gate2[:, cols] * (w0 * unpack(p0) + w1 * unpack(p1)) for unpack, cols in sides]
        ms = (jnp.sum(xs[0] * xs[0], axis=-1, keepdims=True)
              + jnp.sum(xs[1] * xs[1], axis=-1, keepdims=True)) * (1.0 / d)
        scale = lax.rsqrt(ms + EPS)
        for x, (_, cols) in zip(xs, sides):
            o_ref[:, cols] = x * scale * fw_ref[:, cols]

    for par in range(2):
        @pl.when((i % 2 == par) & (i < last))
        def _(par=par):
            step(par, True)

        @pl.when((i % 2 == par) & (i == last))
        def _(par=par):
            step(par, False)


def _combine(dest, x1, rwt, mod, final_w, y_rows, *, row0, n_tok, seq_len, lat):
    d = x1.shape[1]
    tm = 256
    tile0 = row0 // tm
    last = tile0 + n_tok // tm - 1
    kern = functools.partial(_combine_kernel, tiles_per_seq=seq_len // tm, d=d, lat=lat)
    return pl.pallas_call(
        kern,
        grid=(n_tok // tm,),
        in_specs=[pl.BlockSpec((SUBLANES, tm), lambda i: (0, tile0 + i), memory_space=pltpu.SMEM),
                  pl.BlockSpec((SUBLANES, tm), lambda i: (0, jnp.minimum(tile0 + i + 1, last)),
                               memory_space=pltpu.SMEM),
                  pl.BlockSpec((tm, d), lambda i: (tile0 + i, 0)),
                  pl.BlockSpec((SUBLANES, tm), lambda i: (0, tile0 + i)),
                  pl.BlockSpec(mod.shape, lambda i: (0, 0)),
                  pl.BlockSpec((1, d), lambda i: (0, 0)),
                  pl.BlockSpec(memory_space=pl.ANY)],
        out_specs=pl.BlockSpec((tm, d), lambda i: (i, 0)),
        out_shape=jax.ShapeDtypeStruct((n_tok, d), F32),
        scratch_shapes=[pltpu.VMEM((TOP_K, tm, d // 2), U32), pltpu.VMEM((TOP_K, tm, d // 2), U32),
                        pltpu.SemaphoreType.DMA((2,))],
        compiler_params=_cparams(("arbitrary",)),
        name="combine_lat" if lat else "combine_ctx",
    )(dest, dest, x1, rwt, mod, final_w.reshape(1, d), y_rows)


def _gate_layout(w_gates, b_gates, heads):
    d = w_gates.shape[0]
    w = w_gates.reshape(d, 4, heads).transpose(0, 2, 1)
    w = jnp.pad(w, ((0, 0), (0, 0), (0, SUBLANES - 4))).reshape(d, heads * SUBLANES)
    b = b_gates.reshape(4, heads).T
    b = jnp.pad(b, ((0, 0), (0, SUBLANES - 4))).reshape(1, heads * SUBLANES)
    padl = LANES - heads * SUBLANES
    return jnp.pad(w, ((0, 0), (0, padl))), jnp.pad(b, ((0, 0), (0, padl)))


def kernel(x_prompt, x_sample, state_mlstm_c, state_mlstm_n, state_mlstm_m, state_rglru_h, c, c_ctx, w_ada, b_ada,
           norm1_w, w_in, b_gates, conv_w, conv_b, rg_wa, rg_ba, rg_wx, rg_bx, rg_lambda, mlstm_norm_w, w_out,
           norm2_w, router_group_w, router_group_b, router_expert_w, router_expert_b, expert_w_gate, expert_w_up,
           expert_w_down, final_norm_w):
    n_req, seq, d = x_prompt.shape
    n_lat, lat_seq, _ = x_sample.shape
    depth = w_in.shape[0]
    assert depth == 1, "the token-axis plumbing below is written for the single-layer trunk"
    heads, dk, dv = state_mlstm_c.shape[3:]
    rw = state_rglru_h.shape[-1]
    nblk = rg_wa.shape[2]
    assert rw // nblk == LANES
    n_groups, epg = router_expert_w.shape[1], router_expert_w.shape[3]
    n_exp = n_groups * epg
    qk, mw = heads * dk, heads * dv
    nc, nl = n_req * seq, n_lat * lat_seq
    nt = nc + nl
    assert nc % lat_seq == 0 and n_lat + 1 <= SUBLANES
    l = 0

    xp = x_prompt.reshape(nc, d)
    xs = x_sample.reshape(nl, d)
    cvec = jnp.zeros((SUBLANES, d), F32).at[0].set(c_ctx).at[1:1 + n_lat].set(c)
    mod = _ada(cvec, w_ada[l], b_ada[l])

    w = w_in[l]
    g0 = 2 * qk + 2 * mw
    wgate, bgate = _gate_layout(w[:, g0:g0 + 4 * heads], b_gates[l], heads)
    w_cat = jnp.concatenate([w[:, :qk], w[:, 2 * qk:g0], wgate, w[:, g0 + 4 * heads:]], axis=1).astype(BF16)
    w_kt = w[:, qk:2 * qk].T.astype(BF16)
    q, kt, v, o, gt, xr, xg = _in_proj(xp, xs, mod, norm1_w[l], w_cat, w_kt, bgate, lat_seq=lat_seq, heads=heads,
                                       dk=dk, dv=dv, rw=rw)

    mkw = dict(heads=heads, dk=dk, dv=dv)
    ya_c, new_c, new_n, new_m = _mlstm(q, kt, v, o, gt, mlstm_norm_w[l], row0=0, n_seq=n_req, t_len=seq,
                                       emit_state=True, **mkw)
    (ya_l,) = _mlstm(q, kt, v, o, gt, mlstm_norm_w[l], row0=nc, n_seq=n_lat, t_len=lat_seq,
                     state=(state_mlstm_c[:, l], state_mlstm_n[:, l], state_mlstm_m[:, l]), **mkw)

    wg = (0.5 * jnp.concatenate([rg_wa[l, 0], rg_wx[l, 0], rg_wa[l, 1], rg_wx[l, 1]], axis=-1)).astype(BF16)
    bg = 0.5 * jnp.concatenate([rg_ba[l, 0].reshape(nblk, 1, LANES), rg_bx[l, 0].reshape(nblk, 1, LANES),
                                rg_ba[l, 1].reshape(nblk, 1, LANES), rg_bx[l, 1].reshape(nblk, 1, LANES)], axis=-1)
    rargs = (xr, xg, conv_w[l], conv_b[l], wg, bg, rg_lambda[l])
    yb_c, new_h = _rglru(*rargs, row0=0, n_seq=n_req, t_len=seq, seg=seq, emit_state=True)
    (yb_l,) = _rglru(*rargs, row0=nc, n_seq=n_lat, t_len=lat_seq, seg=GRID_W, state=state_rglru_h[:, l])

    r_rows = -(-(SUBLANES + n_exp) // 16) * 16
    wr = jnp.zeros((r_rows, d), F32)
    wr = wr.at[0:n_groups].set(router_group_w[l].T)
    wr = wr.at[SUBLANES:SUBLANES + n_exp].set(router_expert_w[l].transpose(0, 2, 1).reshape(n_exp, d)).astype(BF16)
    br = jnp.zeros((r_rows, LANES), F32)
    br = br.at[0:n_groups, 0].set(router_group_b[l])
    br = br.at[SUBLANES:SUBLANES + n_exp, 0].set(router_expert_b[l].reshape(n_exp))
    x1, hp, ridx, rwt = _out_proj(ya_c, ya_l, yb_c, yb_l, xp, xs, mod, norm2_w[l], w_out[l].astype(BF16), wr, br,
                                  lat_seq=lat_seq, n_groups=n_groups, epg=epg)

    n_blocks = (nt * TOP_K) // MOE_ROWS + n_exp
    dest, row_tok, blk_e, nused = _routing(ridx, nt, n_exp, n_blocks, zero_row=nt)
    only_layer = lambda a: a.reshape(a.shape[1:])
    y_rows = _experts(blk_e, nused, row_tok, hp, only_layer(expert_w_gate), only_layer(expert_w_up),
                      only_layer(expert_w_down), n_blocks)

    y_prompt = _combine(dest, x1, rwt, mod, final_norm_w, y_rows, row0=0, n_tok=nc, seq_len=seq, lat=False)
    y_sample = _combine(dest, x1, rwt, mod, final_norm_w, y_rows, row0=nc, n_tok=nl, seq_len=lat_seq, lat=True)

    return (y_prompt.reshape(n_req, seq, d), y_sample.reshape(n_lat, lat_seq, d),
            new_c[:, None], new_n.reshape(n_req, 1, 2, heads, dk), new_m.reshape(n_req, 1, 2, heads),
            new_h[:, None])
```

```python
import functools

import jax
import jax.numpy as jnp
import numpy as np
from jax import lax
from jax.experimental import pallas as pl
from jax.experimental.pallas import tpu as pltpu

F32 = jnp.float32
BF16 = jnp.bfloat16
I32 = jnp.int32
U32 = jnp.uint32

EPS = 1e-6
GRID_W = 64
CONV_LEFT = 2
RGLRU_C = 8.0
TOP_K = 2
LANES = 128
SUBLANES = 8
MLSTM_L = 256
NEG = -1e30
VMEM_LIMIT = 56 * 1024 * 1024

_HIGHEST = lax.Precision.HIGHEST


def _cparams(sem, vmem=VMEM_LIMIT):
    return pltpu.CompilerParams(dimension_semantics=sem, vmem_limit_bytes=vmem)


def _sigmoid(x):
    return 0.5 * jnp.tanh(0.5 * x) + 0.5


def _row_to_col(r, n):
    return jnp.broadcast_to(r, (LANES, n)).T


def _lane_tile(x, reps):
    return x if reps == 1 else jnp.concatenate([x] * reps, axis=1)


def _ada_kernel(c_ref, w_ref, b_ref, o_ref):
    c = c_ref[...]
    s = (c * _sigmoid(c)).astype(BF16)
    o_ref[...] = jnp.dot(s, w_ref[...].astype(BF16), preferred_element_type=F32) + b_ref[...]


def _ada(cvec, w_ada, b_ada):
    d, n = w_ada.shape
    tn = 1024 if n % 1024 == 0 else 512
    assert n % tn == 0
    return pl.pallas_call(
        _ada_kernel,
        grid=(n // tn,),
        in_specs=[pl.BlockSpec((SUBLANES, d), lambda j: (0, 0)),
                  pl.BlockSpec((d, tn), lambda j: (0, j)),
                  pl.BlockSpec((1, tn), lambda j: (0, j))],
        out_specs=pl.BlockSpec((SUBLANES, tn), lambda j: (0, j)),
        out_shape=jax.ShapeDtypeStruct((SUBLANES, n), F32),
        compiler_params=_cparams(("arbitrary",)),
        name="ada",
    )(cvec, w_ada, b_ada.reshape(1, n))


def _modulated_norm(x, w, shift, scale):
    ms = jnp.mean(x * x, axis=-1, keepdims=True)
    return (x * lax.rsqrt(ms + EPS) * w) * (1.0 + scale) + shift


def _mod_row(i, nctx_tiles, tiles_per_lat):
    return jnp.where(i < nctx_tiles, 0, 1 + (i - nctx_tiles) // tiles_per_lat)


def _inproj_kernel(xp_ref, xs_ref, mod_ref, n1_ref, w_ref, wkt_ref, bg_ref,
                   q_ref, kt_ref, v_ref, o_ref, gt_ref, xr_ref, xg_ref,
                   *, nctx_tiles, tiles_per_lat, d, qk, mw, rw, gh, qscale):
    i = pl.program_id(0)
    x = jnp.where(i < nctx_tiles, xp_ref[...], xs_ref[...])
    row = _mod_row(i, nctx_tiles, tiles_per_lat)
    shift = mod_ref[pl.ds(row, 1), 0:d]
    scale = mod_ref[pl.ds(row, 1), d:2 * d]
    hb = _modulated_norm(x, n1_ref[...], shift, scale).astype(BF16)

    def proj(c0, width):
        return jnp.dot(hb, w_ref[:, c0:c0 + width], preferred_element_type=F32)

    c0 = 0
    q_ref[...] = (proj(c0, qk) * qscale).astype(BF16); c0 += qk
    kt_ref[...] = lax.dot_general(wkt_ref[...], hb, (((1,), (1,)), ((), ())),
                                  preferred_element_type=F32).astype(BF16)
    v_ref[...] = proj(c0, mw).astype(BF16); c0 += mw
    o_ref[...] = proj(c0, mw); c0 += mw
    zg = proj(c0, LANES) + bg_ref[...]; c0 += LANES
    lane = lax.broadcasted_iota(I32, zg.shape, 1)
    log_sig = jnp.minimum(zg, 0.0) - jnp.log1p(jnp.exp(-jnp.abs(zg)))
    zg = jnp.where(lane % 2 == 1, log_sig, zg)
    gt_ref[...] = zg.T[0:gh, :]
    xr_ref[...] = proj(c0, rw); c0 += rw
    xg_ref[...] = proj(c0, rw)


def _in_proj(xp, xs, mod, norm1_w, w_cat, w_kt, bg, *, lat_seq, heads, dk, dv, rw):
    nc, d = xp.shape
    nl = xs.shape[0]
    nt = nc + nl
    tm = 256
    qk, mw, gh = heads * dk, heads * dv, heads * SUBLANES
    nctx_tiles = nc // tm
    kern = functools.partial(_inproj_kernel, nctx_tiles=nctx_tiles, tiles_per_lat=lat_seq // tm, d=d, qk=qk, mw=mw,
                             rw=rw, gh=gh, qscale=dk ** -0.5)
    row = lambda i: (i, 0)
    const = lambda i: (0, 0)
    return pl.pallas_call(
        kern,
        grid=(nt // tm,),
        in_specs=[pl.BlockSpec((tm, d), lambda i: (jnp.minimum(i, nctx_tiles - 1), 0)),
                  pl.BlockSpec((tm, d), lambda i: (jnp.maximum(i - nctx_tiles, 0), 0)),
                  pl.BlockSpec(mod.shape, const),
                  pl.BlockSpec((1, d), const),
                  pl.BlockSpec(w_cat.shape, const, pipeline_mode=pl.Buffered(1)),
                  pl.BlockSpec(w_kt.shape, const, pipeline_mode=pl.Buffered(1)),
                  pl.BlockSpec((1, LANES), const)],
        out_specs=[pl.BlockSpec((tm, qk), row), pl.BlockSpec((qk, tm), lambda i: (0, i)), pl.BlockSpec((tm, mw), row),
                   pl.BlockSpec((tm, mw), row), pl.BlockSpec((gh, tm), lambda i: (0, i)),
                   pl.BlockSpec((tm, rw), row), pl.BlockSpec((tm, rw), row)],
        out_shape=[jax.ShapeDtypeStruct((nt, qk), BF16), jax.ShapeDtypeStruct((qk, nt), BF16),
                   jax.ShapeDtypeStruct((nt, mw), BF16), jax.ShapeDtypeStruct((nt, mw), F32),
                   jax.ShapeDtypeStruct((gh, nt), F32),
                   jax.ShapeDtypeStruct((nt, rw), F32), jax.ShapeDtypeStruct((nt, rw), F32)],
        compiler_params=_cparams(("arbitrary",)),
        name="in_proj",
    )(xp, xs, mod, norm1_w.reshape(1, d), w_cat, w_kt, bg)


def _mlstm_kernel(*refs, t_len, dk, dv, has_state, emit_state):
    it = iter(refs)
    q_ref, kt_ref, v_ref, o_ref, gt_ref, nw_ref, tri_ref = (next(it) for _ in range(7))
    if has_state:
        c0_ref, n0_ref, m0_ref = (next(it) for _ in range(3))
    ya_ref = next(it)
    if emit_state:
        cn_ref, nn_ref, mn_ref = (next(it) for _ in range(3))
    hf_scr, hb_scr, c_scr, ma_scr, p_scr, w_scr, em_scr, kw_scr, dm_scr = (next(it) for _ in range(9))
    ln = MLSTM_L
    nchunks = t_len // ln
    assert ln % LANES == 0 and dk == LANES
    gsz = 2 if nchunks % 2 == 0 else 1
    ngroups = nchunks // gsz
    h_scr = (hf_scr, hb_scr)
    ext = dv + LANES
    lrep = ln // LANES

    for d in range(2):
        if has_state:
            c_scr[d, :, 0:dv] = c0_ref[d]
            c_scr[d, :, dv:ext] = _row_to_col(n0_ref[d], dk)
            ma_scr[d] = m0_ref[d]
        else:
            c_scr[d] = jnp.zeros((dk, ext), F32)
            ma_scr[d] = jnp.zeros((1, 1), F32)

    def chunk_start(d, j):
        return pl.multiple_of((j if d == 0 else nchunks - 1 - j) * ln, ln)

    def stage_a(d, j, slot):
        t0 = chunk_start(d, j)
        q = q_ref[pl.ds(t0, ln), :]
        kt = kt_ref[:, pl.ds(t0, ln)]
        g8 = gt_ref[:, pl.ds(t0, ln)]
        cum8 = jnp.dot(g8, tri_ref[d], precision=_HIGHEST, preferred_element_type=F32)
        valid = tri_ref[1 - d] > 0.5
        li = g8[2 * d:2 * d + 1]
        lf = g8[2 * d + 1:2 * d + 2]
        cum_row = cum8[2 * d + 1:2 * d + 2]
        total = jnp.sum(lf, axis=1, keepdims=True)
        a_row = li - cum_row
        cum_col = _row_to_col(cum_row, ln)
        m_prev = ma_scr[d]
        dmat = jnp.where(valid, _lane_tile(cum_col, lrep) + a_row, NEG)
        inter = cum_col + m_prev
        m_t = jnp.maximum(inter, jnp.max(dmat, axis=1, keepdims=True))
        s = jnp.dot(q, kt, preferred_element_type=F32)
        bank, par = slot
        p_scr[d, bank, par] = (s * jnp.exp(dmat - _lane_tile(m_t, lrep))).astype(BF16)
        w_scr[d, bank, par] = jnp.exp(inter - m_t)
        em_scr[d, bank, par] = jnp.exp(-m_t)
        g_row = total + a_row
        m_new = jnp.maximum(total + m_prev, jnp.max(g_row, axis=1, keepdims=True))
        kw_scr[d, bank, par] = (kt.astype(F32) * jnp.exp(g_row - m_new)).astype(BF16)
        decay = jnp.exp(total + m_prev - m_new)
        rid = lax.broadcasted_iota(I32, (SUBLANES, LANES), 0)
        dm_scr[d, bank, par] = jnp.where(rid == 0, decay, m_new)
        ma_scr[d] = m_new

    def stage_b(d, j, slot):
        t0 = chunk_start(d, j)
        q = q_ref[pl.ds(t0, ln), :]
        v_ext = jnp.concatenate([v_ref[pl.ds(t0, ln), :], jnp.ones((ln, LANES), BF16)], axis=1)
        c_st = c_scr[d]
        bank, par = slot
        full = (jnp.dot(p_scr[d, bank, par], v_ext, preferred_element_type=F32)
                + _lane_tile(w_scr[d, bank, par], ext // LANES) * jnp.dot(q, c_st.astype(BF16),
                                                                         preferred_element_type=F32))
        inv = 1.0 / jnp.maximum(jnp.abs(full[:, dv:ext]), em_scr[d, bank, par])
        h_scr[d][pl.ds(t0, ln), :] = full[:, 0:dv] * _lane_tile(inv, dv // LANES)
        decay = dm_scr[d, bank, par, 0:1, 0:1]
        c_scr[d] = decay * c_st + jnp.dot(kw_scr[d, bank, par], v_ext, preferred_element_type=F32)

    def group_a(jj, bank):
        for par in range(gsz):
            for d in range(2):
                stage_a(d, gsz * jj + par, (bank, par))

    def group_b(jj, bank):
        for par in range(gsz):
            for d in range(2):
                stage_b(d, gsz * jj + par, (bank, par))

    group_a(0, 0)

    def body(jj, carry):
        bank = jj % 2
        group_b(jj, bank)
        group_a(jj + 1, 1 - bank)
        return carry

    lax.fori_loop(0, ngroups - 1, body, 0)
    group_b(ngroups - 1, (ngroups - 1) % 2)

    def finish(j, carry):
        t0 = pl.multiple_of(j * ln, ln)
        hs = hf_scr[pl.ds(t0, ln), :] + hb_scr[pl.ds(t0, ln), :]
        ms = jnp.mean(hs * hs, axis=1, keepdims=True)
        y = hs * lax.rsqrt(ms + EPS) * nw_ref[...]
        ya_ref[pl.ds(t0, ln), :] = (_sigmoid(o_ref[pl.ds(t0, ln), :]) * y).astype(BF16)
        return carry

    lax.fori_loop(0, nchunks, finish, 0)

    if emit_state:
        for d in range(2):
            cn_ref[d] = c_scr[d, :, 0:dv]
            nn_ref[d] = c_scr[d, :, dv:ext].T[0:1, :]
            mn_ref[d] = dm_scr[d, (ngroups - 1) % 2, gsz - 1, 1:2, 0:1]


def _mlstm_tri():
    r = np.arange(MLSTM_L)
    fwd = (r[:, None] <= r[None, :]).astype(np.float32)
    return jnp.asarray(np.stack([fwd, fwd.T]))


def _mlstm(q, kt, v, o, gt, norm_w, *, row0, n_seq, t_len, heads, dk, dv, state=None, emit_state=False):
    blk0 = row0 // t_len
    tok = lambda s, h: (blk0 + s, h)
    in_specs = [pl.BlockSpec((t_len, dk), tok), pl.BlockSpec((dk, t_len), lambda s, h: (h, blk0 + s)),
                pl.BlockSpec((t_len, dv), tok), pl.BlockSpec((t_len, dv), tok),
                pl.BlockSpec((SUBLANES, t_len), lambda s, h: (h, blk0 + s)),
                pl.BlockSpec((None, 1, dv), lambda s, h: (h, 0, 0)),
                pl.BlockSpec((2, MLSTM_L, MLSTM_L), lambda s, h: (0, 0, 0))]
    args = [q, kt, v, o, gt, norm_w.reshape(heads, 1, dv), _mlstm_tri()]
    if state is not None:
        c0, n0, m0 = state
        in_specs += [pl.BlockSpec((None, 2, None, dk, dv), lambda s, h: (s, 0, h, 0, 0)),
                     pl.BlockSpec((None, 2, None, 1, dk), lambda s, h: (s, 0, h, 0, 0)),
                     pl.BlockSpec((None, 2, None, 1, 1), lambda s, h: (s, 0, h, 0, 0))]
        args += [c0, n0.reshape(n_seq, 2, heads, 1, dk), m0.reshape(n_seq, 2, heads, 1, 1)]
    out_specs = [pl.BlockSpec((t_len, dv), lambda s, h: (s, h))]
    out_shape = [jax.ShapeDtypeStruct((n_seq * t_len, heads * dv), BF16)]
    if emit_state:
        out_specs += [pl.BlockSpec((None, 2, None, dk, dv), lambda s, h: (s, 0, h, 0, 0)),
                      pl.BlockSpec((None, 2, None, 1, dk), lambda s, h: (s, 0, h, 0, 0)),
                      pl.BlockSpec((None, 2, None, 1, 1), lambda s, h: (s, 0, h, 0, 0))]
        out_shape += [jax.ShapeDtypeStruct((n_seq, 2, heads, dk, dv), F32),
                      jax.ShapeDtypeStruct((n_seq, 2, heads, 1, dk), F32),
                      jax.ShapeDtypeStruct((n_seq, 2, heads, 1, 1), F32)]
    kern = functools.partial(_mlstm_kernel, t_len=t_len, dk=dk, dv=dv, has_state=state is not None,
                             emit_state=emit_state)
    return pl.pallas_call(
        kern,
        grid=(n_seq, heads),
        in_specs=in_specs,
        out_specs=out_specs,
        out_shape=out_shape,
        scratch_shapes=[pltpu.VMEM((t_len, dv), F32), pltpu.VMEM((t_len, dv), F32),
                        pltpu.VMEM((2, dk, dv + LANES), F32), pltpu.VMEM((2, 1, 1), F32),
                        pltpu.VMEM((2, 2, 2, MLSTM_L, MLSTM_L), BF16), pltpu.VMEM((2, 2, 2, MLSTM_L, LANES), F32),
                        pltpu.VMEM((2, 2, 2, MLSTM_L, LANES), F32), pltpu.VMEM((2, 2, 2, dk, MLSTM_L), BF16),
                        pltpu.VMEM((2, 2, 2, SUBLANES, LANES), F32)],
        compiler_params=_cparams(("arbitrary", "arbitrary")),
        name="mlstm_state" if emit_state else "mlstm",
    )(*args)


def _gelu_tanh(x):
    return x * (0.5 * (1.0 + jnp.tanh(0.7978845608028654 * (x + 0.044715 * (x * x * x)))))


def _softplus(x):
    return jnp.maximum(x, 0.0) + jnp.log1p(jnp.exp(-jnp.abs(x)))


SCAN_SPLIT = 1
SCAN_GROUP = 4


def _rglru_kernel(*refs, t_len, seg, sub, pitch, tc, has_state, emit_state):
    it = iter(refs)
    xr_ref, xg_ref, cw_ref, cb_ref, wg_ref, bg_ref, lam_ref = (next(it) for _ in range(7))
    if has_state:
        h0_ref = next(it)
    yb_ref = next(it)
    if emit_state:
        hn_ref = next(it)
    a_scr, u_scr, cin_scr = (next(it) for _ in range(3))
    nchunks = t_len // tc
    piece = min(tc, sub)
    npieces = tc // piece
    ntile = xr_ref.shape[1] // LANES
    chains = [(d, lt) for d in range(2) for lt in range(ntile)]

    def scan_rows(t0, p):
        t = t0 + p * piece
        i = t // sub
        return pl.ds(pl.multiple_of(i * pitch + (t - i * sub), SUBLANES), piece), i

    ka = (-0.5 * RGLRU_C * 1.4426950408889634) * _softplus(-lam_ref[...])

    def gates(c, carry):
        t0 = pl.multiple_of(c * tc, tc)
        pos = lax.broadcasted_iota(I32, (tc, LANES), 0) % seg
        for lt in range(ntile):
            cols = slice(lt * LANES, (lt + 1) * LANES)
            x = xr_ref[pl.ds(t0, tc), cols]
            xc = cb_ref[:, cols] + cw_ref[CONV_LEFT:CONV_LEFT + 1, cols] * x
            for j in range(cw_ref.shape[0]):
                off = j - CONV_LEFT
                if off == 0:
                    continue
                shifted = pltpu.roll(x, (-off) % tc, 0)
                ok = (pos >= -off) if off < 0 else (pos < seg - off)
                xc = xc + cw_ref[j:j + 1, cols] * jnp.where(ok, shifted, 0.0)
            zh = jnp.dot(xc.astype(BF16), wg_ref[lt], preferred_element_type=F32) + bg_ref[lt]
            hx = 0.5 * xc
            for d in range(2):
                kd = ka[d:d + 1, cols]
                a = jnp.exp2(jnp.tanh(zh[:, (2 * d) * LANES:(2 * d + 1) * LANES]) * kd + kd)
                igx = hx * jnp.tanh(zh[:, (2 * d + 1) * LANES:(2 * d + 2) * LANES]) + hx
                u = jnp.sqrt(1.0 - a * a) * igx
                for p in range(npieces):
                    rows, _ = scan_rows(t0, p)
                    a_scr[d, lt, rows, :] = a[p * piece:(p + 1) * piece]
                    u_scr[d, lt, rows, :] = u[p * piece:(p + 1) * piece]
        return carry

    lax.fori_loop(0, nchunks, gates, 0)

    seg_len = sub // SCAN_SPLIT
    seg_chains = [(d, lt, s) for (d, lt) in chains for s in range(SCAN_SPLIT)]

    def scan(jg, carry):
        def rows(d, s, k):
            j = jg * SCAN_GROUP + k
            return pl.ds(s * seg_len + (j if d == 0 else seg_len - 1 - j), SUBLANES, stride=pitch)

        loaded = [[(a_scr[d, lt, rows(d, s, k), :], u_scr[d, lt, rows(d, s, k), :]) for k in range(SCAN_GROUP)]
                  for (d, lt, s) in seg_chains]
        out = []
        for (d, lt, s), (h, p), steps in zip(seg_chains, carry, loaded):
            for k, (a, u) in enumerate(steps):
                h = a * h + u
                p = a * p
                a_scr[d, lt, rows(d, s, k), :] = p
                u_scr[d, lt, rows(d, s, k), :] = h
            out.append((h, p))
        return tuple(out)

    zero = jnp.zeros((SUBLANES, LANES), F32)
    one = jnp.ones((SUBLANES, LANES), F32)
    assert seg_len % SCAN_GROUP == 0
    ends = dict(zip(seg_chains, lax.fori_loop(0, seg_len // SCAN_GROUP, scan, tuple((zero, one) for _ in seg_chains))))

    order = [(i, s) for i in range(SUBLANES) for s in range(SCAN_SPLIT)]
    for (d, lt) in chains:
        cols = slice(lt * LANES, (lt + 1) * LANES)
        cin = h0_ref[d:d + 1, cols] if has_state else jnp.zeros((1, LANES), F32)
        for (i, s) in (order if d == 0 else reversed(order)):
            h, p = ends[(d, lt, s)]
            cin_scr[d, lt, s * SUBLANES + i:s * SUBLANES + i + 1, :] = cin
            cin = h[i:i + 1] + p[i:i + 1] * cin
        if emit_state:
            hn_ref[d:d + 1, cols] = cin

    fpiece = min(tc, seg_len)

    def finish(c, carry):
        t0 = pl.multiple_of(c * tc, tc)
        for lt in range(ntile):
            cols = slice(lt * LANES, (lt + 1) * LANES)
            for p in range(tc // fpiece):
                t = t0 + p * fpiece
                i = t // sub
                local = t - i * sub
                rows = pl.ds(pl.multiple_of(i * pitch + local, SUBLANES), fpiece)
                crow = pl.ds((local // seg_len) * SUBLANES + i, 1)
                h = (u_scr[0, lt, rows, :] + a_scr[0, lt, rows, :] * cin_scr[0, lt, crow, :]
                     + u_scr[1, lt, rows, :] + a_scr[1, lt, rows, :] * cin_scr[1, lt, crow, :])
                nat = pl.ds(pl.multiple_of(t, SUBLANES), fpiece)
                yb_ref[nat, cols] = (h * _gelu_tanh(xg_ref[nat, cols])).astype(BF16)
        return carry

    lax.fori_loop(0, nchunks, finish, 0)


def _rglru(xr, xg, conv_w, conv_b, wg, bg, lam, *, row0, n_seq, t_len, seg, state=None, emit_state=False):
    rw = xr.shape[1]
    ntile = 2
    cb = ntile * LANES
    assert rw % cb == 0
    blk0 = row0 // t_len
    sub = t_len // SUBLANES
    pitch = sub + SUBLANES
    tc = 256
    tok = lambda s, g: (blk0 + s, g)
    in_specs = [pl.BlockSpec((t_len, cb), tok), pl.BlockSpec((t_len, cb), tok),
                pl.BlockSpec((conv_w.shape[0], cb), lambda s, g: (0, g)),
                pl.BlockSpec((1, cb), lambda s, g: (0, g)),
                pl.BlockSpec((ntile, LANES, 4 * LANES), lambda s, g: (g, 0, 0)),
                pl.BlockSpec((ntile, 1, 4 * LANES), lambda s, g: (g, 0, 0)),
                pl.BlockSpec((2, cb), lambda s, g: (0, g))]
    args = [xr, xg, conv_w, conv_b.reshape(1, rw), wg, bg, lam]
    if state is not None:
        in_specs.append(pl.BlockSpec((None, 2, cb), lambda s, g: (s, 0, g)))
        args.append(state)
    out_specs = [pl.BlockSpec((t_len, cb), lambda s, g: (s, g))]
    out_shape = [jax.ShapeDtypeStruct((n_seq * t_len, rw), BF16)]
    if emit_state:
        out_specs.append(pl.BlockSpec((None, 2, cb), lambda s, g: (s, 0, g)))
        out_shape.append(jax.ShapeDtypeStruct((n_seq, 2, rw), F32))
    kern = functools.partial(_rglru_kernel, t_len=t_len, seg=seg, sub=sub, pitch=pitch, tc=tc,
                             has_state=state is not None, emit_state=emit_state)
    return pl.pallas_call(
        kern,
        grid=(n_seq, rw // cb),
        in_specs=in_specs,
        out_specs=out_specs,
        out_shape=out_shape,
        scratch_shapes=[pltpu.VMEM((2, ntile, SUBLANES * pitch, LANES), F32),
                        pltpu.VMEM((2, ntile, SUBLANES * pitch, LANES), F32),
                        pltpu.VMEM((2, ntile, SCAN_SPLIT * SUBLANES, LANES), F32)],
        compiler_params=_cparams(("arbitrary", "arbitrary")),
        name="rglru_state" if emit_state else "rglru",
    )(*args)


def _pack_bf16_pair(lo, hi):
    def rounded_bits(x):
        return pltpu.bitcast(x.astype(BF16).astype(F32), U32)
    return (rounded_bits(lo) >> 16) | rounded_bits(hi)


def _unpack_bf16_pair(w):
    lo = pltpu.bitcast(w << 16, F32).astype(BF16)
    hi = pltpu.bitcast(w & jnp.uint32(0xFFFF0000), F32).astype(BF16)
    return lo, hi


def _outproj_kernel(yac_ref, yal_ref, ybc_ref, ybl_ref, xp_ref, xs_ref, mod_ref, n2_ref, wo_ref, wr_ref, br_ref,
                    x1_ref, hp_ref, ridx_ref, rwt_ref,
                    *, nctx_tiles, ntok_tiles, tiles_per_lat, d, mw, n_groups, epg):
    i = pl.program_id(0)

    @pl.when(i == ntok_tiles)
    def _():
        x1_ref[...] = jnp.zeros_like(x1_ref)
        hp_ref[...] = jnp.zeros_like(hp_ref)
        ridx_ref[...] = jnp.zeros_like(ridx_ref)
        rwt_ref[...] = jnp.zeros_like(rwt_ref)

    def token_tile(x_ref, ya_ref, yb_ref):
        x, ya, yb = x_ref[...], ya_ref[...], yb_ref[...]
        row = _mod_row(i, nctx_tiles, tiles_per_lat)
        gate1 = mod_ref[pl.ds(row, 1), 2 * d:3 * d]
        shift2 = mod_ref[pl.ds(row, 1), 3 * d:4 * d]
        scale2 = mod_ref[pl.ds(row, 1), 4 * d:5 * d]
        y = (jnp.dot(ya, wo_ref[0:mw, :], preferred_element_type=F32)
             + jnp.dot(yb, wo_ref[mw:, :], preferred_element_type=F32))
        x1 = x + gate1 * y
        x1_ref[...] = x1
        h2 = _modulated_norm(x1, n2_ref[...], shift2, scale2)
        half = d // 2
        packed = _pack_bf16_pair(h2[:, :half], h2[:, half:])
        trows = half // LANES
        for s in range(trows):
            hp_ref[pl.ds(s, x.shape[0], stride=trows), :] = packed[:, s * LANES:(s + 1) * LANES]

        lt = lax.dot_general(wr_ref[...], h2.astype(BF16), (((1,), (1,)), ((), ())),
                             preferred_element_type=F32) + br_ref[:, 0:1]
        gidx = lax.broadcasted_iota(I32, (SUBLANES, lt.shape[1]), 0)
        gl = jnp.where(gidx < n_groups, lt[0:SUBLANES], -jnp.inf)
        gmax = jnp.max(gl, axis=0, keepdims=True)
        grp = jnp.min(jnp.where(gl == gmax, gidx, n_groups), axis=0, keepdims=True)
        p_grp = 1.0 / jnp.sum(jnp.exp(gl - gmax), axis=0, keepdims=True)
        el = lt[SUBLANES:SUBLANES + epg]
        for g in range(1, n_groups):
            el = jnp.where(grp == g, lt[SUBLANES + g * epg:SUBLANES + (g + 1) * epg], el)
        eidx = lax.broadcasted_iota(I32, el.shape, 0)
        v1 = jnp.max(el, axis=0, keepdims=True)
        i1 = jnp.min(jnp.where(el == v1, eidx, epg), axis=0, keepdims=True)
        el2 = jnp.where(eidx == i1, -jnp.inf, el)
        v2 = jnp.max(el2, axis=0, keepdims=True)
        i2 = jnp.min(jnp.where(el2 == v2, eidx, epg), axis=0, keepdims=True)
        e2 = jnp.exp(v2 - v1)
        w1 = p_grp / (1.0 + e2)
        w2 = p_grp * e2 / (1.0 + e2)
        rid = lax.broadcasted_iota(I32, ridx_ref.shape, 0)
        ridx_ref[...] = jnp.where(rid == 0, grp * epg + i1, jnp.where(rid == 1, grp * epg + i2, 0))
        rwt_ref[...] = jnp.where(rid == 0, w1, jnp.where(rid == 1, w2, 0.0))

    @pl.when(i < nctx_tiles)
    def _():
        token_tile(xp_ref, yac_ref, ybc_ref)

    @pl.when((i >= nctx_tiles) & (i < ntok_tiles))
    def _():
        token_tile(xs_ref, yal_ref, ybl_ref)


def _out_proj(ya_c, ya_l, yb_c, yb_l, xp, xs, mod, norm2_w, w_out, wr, br, *, lat_seq, n_groups, epg):
    nc, d = xp.shape
    nl = xs.shape[0]
    nt = nc + nl
    mw = ya_c.shape[1]
    tm = 256
    trows = (d // 2) // LANES
    nctx_tiles, ntok_tiles = nc // tm, nt // tm
    kern = functools.partial(_outproj_kernel, nctx_tiles=nctx_tiles, ntok_tiles=ntok_tiles,
                             tiles_per_lat=lat_seq // tm, d=d, mw=mw, n_groups=n_groups, epg=epg)
    ctx = lambda i: (jnp.minimum(i, nctx_tiles - 1), 0)
    lat = lambda i: (jnp.clip(i - nctx_tiles, 0, nl // tm - 1), 0)
    row = lambda i: (i, 0)
    const = lambda i: (0, 0)
    return pl.pallas_call(
        kern,
        grid=(ntok_tiles + 1,),
        in_specs=[pl.BlockSpec((tm, mw), ctx), pl.BlockSpec((tm, mw), lat),
                  pl.BlockSpec((tm, yb_c.shape[1]), ctx), pl.BlockSpec((tm, yb_c.shape[1]), lat),
                  pl.BlockSpec((tm, d), ctx), pl.BlockSpec((tm, d), lat),
                  pl.BlockSpec(mod.shape, const),
                  pl.BlockSpec((1, d), const),
                  pl.BlockSpec(w_out.shape, const, pipeline_mode=pl.Buffered(1)),
                  pl.BlockSpec(wr.shape, const),
                  pl.BlockSpec(br.shape, const)],
        out_specs=[pl.BlockSpec((tm, d), row), pl.BlockSpec((tm * trows, LANES), row),
                   pl.BlockSpec((SUBLANES, tm), lambda i: (0, i)), pl.BlockSpec((SUBLANES, tm), lambda i: (0, i))],
        out_shape=[jax.ShapeDtypeStruct((nt + tm, d), F32), jax.ShapeDtypeStruct(((nt + tm) * trows, LANES), U32),
                   jax.ShapeDtypeStruct((SUBLANES, nt + tm), I32), jax.ShapeDtypeStruct((SUBLANES, nt + tm), F32)],
        compiler_params=_cparams(("arbitrary",)),
        name="out_proj",
    )(ya_c, ya_l, yb_c, yb_l, xp, xs, mod, norm2_w.reshape(1, d), w_out, wr, br)


MOE_ROWS = 256


ROUTE_TILE = 512


def _rank_kernel(ridx_ref, tri_ref, rank_ref, cnt_ref, carry_scr, *, n_exp):
    @pl.when(pl.program_id(0) == 0)
    def _():
        carry_scr[...] = jnp.zeros_like(carry_scr)

    e = ridx_ref[...]
    tr = e.shape[1]
    eid = lax.broadcasted_iota(I32, (n_exp, tr), 0)
    carry = carry_scr[:, 0:1]
    ranks = []
    for kk in range(TOP_K):
        hit = eid == e[kk:kk + 1]
        cum = jnp.dot(jnp.where(hit, 1.0, 0.0).astype(BF16), tri_ref[...], preferred_element_type=F32)
        ranks.append(jnp.sum(jnp.where(hit, cum + carry, 0.0), axis=0, keepdims=True) - 1.0)
        carry = carry + cum[:, tr - 1:tr]
    carry_scr[...] = jnp.broadcast_to(carry, carry_scr.shape)
    cnt_ref[...] = jnp.broadcast_to(carry, cnt_ref.shape)
    rid = lax.broadcasted_iota(I32, rank_ref.shape, 0)
    rank_ref[...] = jnp.where(rid == 0, ranks[0], jnp.where(rid == 1, ranks[1], 0.0)).astype(I32)


def _dest_kernel(ridx_ref, rank_ref, pstart_ref, dest_ref, *, n_exp):
    e = ridx_ref[...]
    tr = e.shape[1]
    eid = lax.broadcasted_iota(I32, (n_exp, tr), 0)
    ps = pstart_ref[:, 0:1]
    rows = [jnp.sum(jnp.where(eid == e[kk:kk + 1], ps, 0.0), axis=0, keepdims=True) for kk in range(TOP_K)]
    rid = lax.broadcasted_iota(I32, dest_ref.shape, 0)
    dest_ref[...] = rank_ref[...] + jnp.where(rid == 0, rows[0], jnp.where(rid == 1, rows[1], 0.0)).astype(I32)


def _invert_kernel(dest_ref, fill_ref, rowtok_ref, sem):
    i = pl.program_id(0)
    tr = dest_ref.shape[1]

    @pl.when(i == 0)
    def _():
        fill = pltpu.make_async_copy(fill_ref, rowtok_ref, sem.at[0])
        fill.start()
        fill.wait()

    def body(r, c):
        for kk in range(TOP_K):
            rowtok_ref[dest_ref[kk, r]] = i * tr + r
        return c
    lax.fori_loop(0, tr, body, 0, unroll=16)


def _routing(ridx, n_tok, n_exp, n_blocks, zero_row):
    tr = ROUTE_TILE
    steps = n_tok // tr
    tri = jnp.asarray(np.triu(np.ones((tr, tr), np.float32)), BF16)
    tile = pl.BlockSpec((SUBLANES, tr), lambda i: (0, i))
    cnt_spec = pl.BlockSpec((n_exp, LANES), lambda i: (0, 0))
    rank, cnt = pl.pallas_call(
        functools.partial(_rank_kernel, n_exp=n_exp),
        grid=(steps,),
        in_specs=[tile, pl.BlockSpec((tr, tr), lambda i: (0, 0))],
        out_specs=[tile, cnt_spec],
        out_shape=[jax.ShapeDtypeStruct((SUBLANES, n_tok), I32), jax.ShapeDtypeStruct((n_exp, LANES), F32)],
        scratch_shapes=[pltpu.VMEM((n_exp, LANES), F32)],
        compiler_params=_cparams(("arbitrary",)),
        name="route_rank",
    )(ridx, tri)
    counts = cnt[:, 0].astype(I32)
    padded = (counts + MOE_ROWS - 1) // MOE_ROWS * MOE_ROWS
    pad_end = jnp.cumsum(padded)
    pad_start = pad_end - padded
    dest = pl.pallas_call(
        functools.partial(_dest_kernel, n_exp=n_exp),
        grid=(steps,),
        in_specs=[tile, tile, cnt_spec],
        out_specs=tile,
        out_shape=jax.ShapeDtypeStruct((SUBLANES, n_tok), I32),
        compiler_params=_cparams(("arbitrary",)),
        name="route_dest",
    )(ridx, rank, jnp.broadcast_to(pad_start.astype(F32)[:, None], (n_exp, LANES)))
    row_tok = pl.pallas_call(
        _invert_kernel,
        grid=(steps,),
        in_specs=[pl.BlockSpec((SUBLANES, tr), lambda i: (0, i), memory_space=pltpu.SMEM),
                  pl.BlockSpec(memory_space=pl.ANY)],
        out_specs=pl.BlockSpec(memory_space=pltpu.SMEM),
        out_shape=jax.ShapeDtypeStruct((n_blocks * MOE_ROWS,), I32),
        scratch_shapes=[pltpu.SemaphoreType.DMA((1,))],
        compiler_params=_cparams(("arbitrary",)),
        name="route_invert",
    )(dest, jnp.full((n_blocks * MOE_ROWS,), zero_row, I32))
    blk_row0 = jnp.arange(n_blocks, dtype=I32) * MOE_ROWS
    blk_e = jnp.minimum(jnp.sum((pad_end[None, :] <= blk_row0[:, None]).astype(I32), axis=1), n_exp - 1)
    nused = (pad_end[-1:] // MOE_ROWS).astype(I32)
    return dest, row_tok, blk_e, nused


CAST_ROWS = 256
WEIGHT_DMA_SPLIT = 8


def _expert_kernel(blk_e_ref, nused_ref, tok_ref, src_ref, wg_hbm, wu_hbm, wd_hbm, y_ref,
                   xbuf_even, xbuf_odd, xsem, stage_g, stage_u, stage_d, bf_g, bf_u, bf_d, wsem,
                   *, half, rows, trows):
    b = pl.program_id(0)
    nused = nused_ref[0]
    w_hbm = (wg_hbm, wu_hbm, wd_hbm)
    stage = (stage_g, stage_u, stage_d)
    wbf = (bf_g, bf_u, bf_d)

    def weight_copy(e, j):
        return pltpu.make_async_copy(w_hbm[j].at[e], stage[j], wsem.at[j])

    def start_weights(e, j):
        step = stage[j].shape[0] // WEIGHT_DMA_SPLIT
        for c in range(WEIGHT_DMA_SPLIT):
            sl = pl.ds(c * step, step)
            pltpu.make_async_copy(w_hbm[j].at[e, sl, :], stage[j].at[sl, :], wsem.at[j]).start(priority=1)

    xbufs = (xbuf_even, xbuf_odd)

    def row_copy(blk, par, r):
        tok = tok_ref[blk * rows + r]
        src = src_ref.at[pl.ds(pl.multiple_of(tok * trows, trows), trows), :]
        return pltpu.make_async_copy(src, xbufs[par].at[pl.ds(r * trows, trows), :], xsem.at[par])

    @pl.when(b >= nused)
    def _():
        y_ref[...] = jnp.zeros_like(y_ref)

    @pl.when(b < nused)
    def _():
        e = blk_e_ref[b]

        @pl.when(b == 0)
        def _():
            for j in range(3):
                start_weights(e, j)
            def body(r, c):
                row_copy(0, 0, r).start()
                return c
            lax.fori_loop(0, rows, body, 0, unroll=8)

        @pl.when((b == 0) | (blk_e_ref[jnp.maximum(b - 1, 0)] != e))
        def _():
            nb = lax.while_loop(lambda k: (k < nused) & (blk_e_ref[jnp.minimum(k, nused - 1)] == e),
                                lambda k: k + 1, b + 1)
            for j in range(3):
                weight_copy(e, j).wait()
                n_steps = stage[j].shape[0] // CAST_ROWS

                def cast(c, carry, j=j):
                    sl = pl.ds(pl.multiple_of(c * CAST_ROWS, CAST_ROWS), CAST_ROWS)
                    wbf[j][sl, :] = stage[j][sl, :].astype(BF16)
                    return carry
                lax.fori_loop(0, n_steps, cast, 0)

                @pl.when(nb < nused)
                def _(j=j):
                    start_weights(blk_e_ref[jnp.minimum(nb, nused - 1)], j)

        def compute(par, prefetch):
            xbuf = xbufs[par]
            pltpu.make_async_copy(src_ref.at[pl.ds(0, rows * trows), :], xbuf, xsem.at[par]).wait()
            if prefetch:
                for r in range(rows):
                    row_copy(b + 1, 1 - par, r).start()
            parts = [_unpack_bf16_pair(xbuf[pl.ds(s, rows, stride=trows), :]) for s in range(trows)]
            lo = jnp.concatenate([p[0] for p in parts], axis=1)
            hi = jnp.concatenate([p[1] for p in parts], axis=1)

            def up(w_ref):
                return (jnp.dot(lo, w_ref[0:half, :], preferred_element_type=F32)
                        + jnp.dot(hi, w_ref[half:, :], preferred_element_type=F32))

            g = up(wbf[0])
            h = ((g * _sigmoid(g)) * up(wbf[1])).astype(BF16)
            y = jnp.dot(h, wbf[2][...], preferred_element_type=F32)
            y_ref[...] = _pack_bf16_pair(y[:, :half], y[:, half:])

        for par in range(2):
            @pl.when((b % 2 == par) & (b + 1 < nused))
            def _(par=par):
                compute(par, True)

            @pl.when((b % 2 == par) & (b + 1 >= nused))
            def _(par=par):
                compute(par, False)


def _experts(blk_e, nused, row_tok, src, wg, wu, wd, n_blocks):
    n_exp, d, ff = wg.shape
    any_spec = pl.BlockSpec(memory_space=pl.ANY)
    trows = (d // 2) // LANES
    return pl.pallas_call(
        functools.partial(_expert_kernel, half=d // 2, rows=MOE_ROWS, trows=trows),
        grid_spec=pltpu.PrefetchScalarGridSpec(
            num_scalar_prefetch=3,
            grid=(n_blocks,),
            in_specs=[any_spec, any_spec, any_spec, any_spec],
            out_specs=pl.BlockSpec((MOE_ROWS, d // 2), lambda b, be, nu, rt: (b, 0)),
            scratch_shapes=[pltpu.VMEM((MOE_ROWS * trows, LANES), U32), pltpu.VMEM((MOE_ROWS * trows, LANES), U32),
                            pltpu.SemaphoreType.DMA((2,)),
                            pltpu.VMEM((d, ff), F32), pltpu.VMEM((d, ff), F32), pltpu.VMEM((ff, d), F32),
                            pltpu.VMEM((d, ff), BF16), pltpu.VMEM((d, ff), BF16), pltpu.VMEM((ff, d), BF16),
                            pltpu.SemaphoreType.DMA((3,))]),
        out_shape=jax.ShapeDtypeStruct((n_blocks * MOE_ROWS, d // 2), U32),
        compiler_params=_cparams(("arbitrary",)),
        name="experts",
    )(blk_e, nused, row_tok, src, wg, wu, wd)


def _combine_kernel(dest_ref, x1_ref, wt_ref, mod_ref, fw_ref, y_ref, o_ref, ybuf_even, ybuf_odd, sem,
                    *, tile0, n_all, tiles_per_seq, d, lat):
    i = pl.program_id(0)
    last = pl.num_programs(0) - 1
    tm = x1_ref.shape[0]
    bufs = (ybuf_even, ybuf_odd)

    def row_copy(tile, par, kk, r):
        row = dest_ref[kk * n_all + (tile0 + tile) * tm + r]
        return pltpu.make_async_copy(y_ref.at[row], bufs[par].at[kk, r], sem.at[par])

    @pl.when(i == 0)
    def _():
        def body(r, c):
            for kk in range(TOP_K):
                row_copy(0, 0, kk, r).start()
            return c
        lax.fori_loop(0, tm, body, 0, unroll=8)

    def step(par, prefetch):
        for kk in range(TOP_K):
            pltpu.make_async_copy(y_ref.at[pl.ds(0, tm)], bufs[par].at[kk], sem.at[par]).wait()
        if prefetch:
            for r in range(tm):
                for kk in range(TOP_K):
                    row_copy(i + 1, 1 - par, kk, r).start(priority=kk)
        row = (1 + i // tiles_per_seq) if lat else 0
        gate2 = mod_ref[pl.ds(row, 1), 5 * d:6 * d]
        wt = wt_ref[...]
        half = d // 2
        w0 = _lane_tile(_row_to_col(wt[0:1], tm), half // LANES)
        w1 = _lane_tile(_row_to_col(wt[1:2], tm), half // LANES)
        p0, p1 = bufs[par][0], bufs[par][1]
        sides = ((lambda p: pltpu.bitcast(p << 16, F32), slice(0, half)),
                 (lambda p: pltpu.bitcast(p & jnp.uint32(0xFFFF0000), F32), slice(half, d)))
        xs = [x1_ref[:, cols] + gate2[:, cols] * (w0 * unpack(p0) + w1 * unpack(p1)) for unpack, cols in sides]
        ms = (jnp.sum(xs[0] * xs[0], axis=-1, keepdims=True)
              + jnp.sum(xs[1] * xs[1], axis=-1, keepdims=True)) * (1.0 / d)
        scale = lax.rsqrt(ms + EPS)
        for x, (_, cols) in zip(xs, sides):
            o_ref[:, cols] = x * scale * fw_ref[:, cols]

    for par in range(2):
        @pl.when((i % 2 == par) & (i < last))
        def _(par=par):
            step(par, True)

        @pl.when((i % 2 == par) & (i == last))
        def _(par=par):
            step(par, False)


def _combine(dest_flat, x1, rwt, mod, final_w, y_rows, *, row0, n_tok, seq_len, lat):
    d = x1.shape[1]
    tm = 256
    tile0 = row0 // tm
    n_all = dest_flat.shape[0] // TOP_K
    kern = functools.partial(_combine_kernel, tile0=tile0, n_all=n_all, tiles_per_seq=seq_len // tm, d=d, lat=lat)
    return pl.pallas_call(
        kern,
        grid_spec=pltpu.PrefetchScalarGridSpec(
            num_scalar_prefetch=1,
            grid=(n_tok // tm,),
            in_specs=[pl.BlockSpec((tm, d), lambda i, dr: (tile0 + i, 0)),
                      pl.BlockSpec((SUBLANES, tm), lambda i, dr: (0, tile0 + i)),
                      pl.BlockSpec(mod.shape, lambda i, dr: (0, 0)),
                      pl.BlockSpec((1, d), lambda i, dr: (0, 0)),
                      pl.BlockSpec(memory_space=pl.ANY)],
            out_specs=pl.BlockSpec((tm, d), lambda i, dr: (i, 0)),
            scratch_shapes=[pltpu.VMEM((TOP_K, tm, d // 2), U32), pltpu.VMEM((TOP_K, tm, d // 2), U32),
                            pltpu.SemaphoreType.DMA((2,))]),
        out_shape=jax.ShapeDtypeStruct((n_tok, d), F32),
        compiler_params=_cparams(("arbitrary",)),
        name="combine_lat" if lat else "combine_ctx",
    )(dest_flat, x1, rwt, mod, final_w.reshape(1, d), y_rows)


def _gate_layout(w_gates, b_gates, heads):
    d = w_gates.shape[0]
    w = w_gates.reshape(d, 4, heads).transpose(0, 2, 1)
    w = jnp.pad(w, ((0, 0), (0, 0), (0, SUBLANES - 4))).reshape(d, heads * SUBLANES)
    b = b_gates.reshape(4, heads).T
    b = jnp.pad(b, ((0, 0), (0, SUBLANES - 4))).reshape(1, heads * SUBLANES)
    padl = LANES - heads * SUBLANES
    return jnp.pad(w, ((0, 0), (0, padl))), jnp.pad(b, ((0, 0), (0, padl)))


def kernel(x_prompt, x_sample, state_mlstm_c, state_mlstm_n, state_mlstm_m, state_rglru_h, c, c_ctx, w_ada, b_ada,
           norm1_w, w_in, b_gates, conv_w, conv_b, rg_wa, rg_ba, rg_wx, rg_bx, rg_lambda, mlstm_norm_w, w_out,
           norm2_w, router_group_w, router_group_b, router_expert_w, router_expert_b, expert_w_gate, expert_w_up,
           expert_w_down, final_norm_w):
    n_req, seq, d = x_prompt.shape
    n_lat, lat_seq, _ = x_sample.shape
    depth = w_in.shape[0]
    assert depth == 1, "the token-axis plumbing below is written for the single-layer trunk"
    heads, dk, dv = state_mlstm_c.shape[3:]
    rw = state_rglru_h.shape[-1]
    nblk = rg_wa.shape[2]
    assert rw // nblk == LANES
    n_groups, epg = router_expert_w.shape[1], router_expert_w.shape[3]
    n_exp = n_groups * epg
    qk, mw = heads * dk, heads * dv
    nc, nl = n_req * seq, n_lat * lat_seq
    nt = nc + nl
    assert nc % lat_seq == 0 and n_lat + 1 <= SUBLANES
    l = 0

    xp = x_prompt.reshape(nc, d)
    xs = x_sample.reshape(nl, d)
    cvec = jnp.zeros((SUBLANES, d), F32).at[0].set(c_ctx).at[1:1 + n_lat].set(c)
    mod = _ada(cvec, w_ada[l], b_ada[l])

    w = w_in[l]
    g0 = 2 * qk + 2 * mw
    wgate, bgate = _gate_layout(w[:, g0:g0 + 4 * heads], b_gates[l], heads)
    w_cat = jnp.concatenate([w[:, :qk], w[:, 2 * qk:g0], wgate, w[:, g0 + 4 * heads:]], axis=1).astype(BF16)
    w_kt = w[:, qk:2 * qk].T.astype(BF16)
    q, kt, v, o, gt, xr, xg = _in_proj(xp, xs, mod, norm1_w[l], w_cat, w_kt, bgate, lat_seq=lat_seq, heads=heads,
                                       dk=dk, dv=dv, rw=rw)

    mkw = dict(heads=heads, dk=dk, dv=dv)
    ya_c, new_c, new_n, new_m = _mlstm(q, kt, v, o, gt, mlstm_norm_w[l], row0=0, n_seq=n_req, t_len=seq,
                                       emit_state=True, **mkw)
    (ya_l,) = _mlstm(q, kt, v, o, gt, mlstm_norm_w[l], row0=nc, n_seq=n_lat, t_len=lat_seq,
                     state=(state_mlstm_c[:, l], state_mlstm_n[:, l], state_mlstm_m[:, l]), **mkw)

    wg = (0.5 * jnp.concatenate([rg_wa[l, 0], rg_wx[l, 0], rg_wa[l, 1], rg_wx[l, 1]], axis=-1)).astype(BF16)
    bg = 0.5 * jnp.concatenate([rg_ba[l, 0].reshape(nblk, 1, LANES), rg_bx[l, 0].reshape(nblk, 1, LANES),
                                rg_ba[l, 1].reshape(nblk, 1, LANES), rg_bx[l, 1].reshape(nblk, 1, LANES)], axis=-1)
    rargs = (xr, xg, conv_w[l], conv_b[l], wg, bg, rg_lambda[l])
    yb_c, new_h = _rglru(*rargs, row0=0, n_seq=n_req, t_len=seq, seg=seq, emit_state=True)
    (yb_l,) = _rglru(*rargs, row0=nc, n_seq=n_lat, t_len=lat_seq, seg=GRID_W, state=state_rglru_h[:, l])

    r_rows = -(-(SUBLANES + n_exp) // 16) * 16
    wr = jnp.zeros((r_rows, d), F32)
    wr = wr.at[0:n_groups].set(router_group_w[l].T)
    wr = wr.at[SUBLANES:SUBLANES + n_exp].set(router_expert_w[l].transpose(0, 2, 1).reshape(n_exp, d)).astype(BF16)
    br = jnp.zeros((r_rows, LANES), F32)
    br = br.at[0:n_groups, 0].set(router_group_b[l])
    br = br.at[SUBLANES:SUBLANES + n_exp, 0].set(router_expert_b[l].reshape(n_exp))
    x1, hp, ridx, rwt = _out_proj(ya_c, ya_l, yb_c, yb_l, xp, xs, mod, norm2_w[l], w_out[l].astype(BF16), wr, br,
                                  lat_seq=lat_seq, n_groups=n_groups, epg=epg)

    n_blocks = (nt * TOP_K) // MOE_ROWS + n_exp
    dest, row_tok, blk_e, nused = _routing(ridx, nt, n_exp, n_blocks, zero_row=nt)
    only_layer = lambda a: a.reshape(a.shape[1:])
    y_rows = _experts(blk_e, nused, row_tok, hp, only_layer(expert_w_gate), only_layer(expert_w_up),
                      only_layer(expert_w_down), n_blocks)

    dest_flat = dest[0:TOP_K].reshape(-1)
    y_prompt = _combine(dest_flat, x1, rwt, mod, final_norm_w, y_rows, row0=0, n_tok=nc, seq_len=seq, lat=False)
    y_sample = _combine(dest_flat, x1, rwt, mod, final_norm_w, y_rows, row0=nc, n_tok=nl, seq_len=lat_seq,
                        lat=True)

    return (y_prompt.reshape(n_req, seq, d), y_sample.reshape(n_lat, lat_seq, d),
            new_c[:, None], new_n.reshape(n_req, 1, 2, heads, dk), new_m.reshape(n_req, 1, 2, heads),
            new_h[:, None])
```

```python
import functools

import jax
import jax.numpy as jnp
import numpy as np
from jax import lax
from jax.experimental import pallas as pl
from jax.experimental.pallas import tpu as pltpu

F32 = jnp.float32
BF16 = jnp.bfloat16
I32 = jnp.int32
U32 = jnp.uint32

EPS = 1e-6
GRID_W = 64
CONV_LEFT = 2
RGLRU_C = 8.0
TOP_K = 2
LANES = 128
SUBLANES = 8
MLSTM_L = 256
NEG = -1e30
VMEM_LIMIT = 56 * 1024 * 1024

_HIGHEST = lax.Precision.HIGHEST


def _cparams(sem, vmem=VMEM_LIMIT):
    return pltpu.CompilerParams(dimension_semantics=sem, vmem_limit_bytes=vmem)


def _sigmoid(x):
    return 0.5 * jnp.tanh(0.5 * x) + 0.5


def _row_to_col(r, n):
    return jnp.broadcast_to(r, (LANES, n)).T


def _lane_tile(x, reps):
    return x if reps == 1 else jnp.concatenate([x] * reps, axis=1)


def _ada_kernel(c_ref, w_ref, b_ref, o_ref):
    c = c_ref[...]
    s = (c * _sigmoid(c)).astype(BF16)
    o_ref[...] = jnp.dot(s, w_ref[...].astype(BF16), preferred_element_type=F32) + b_ref[...]


def _ada(cvec, w_ada, b_ada):
    d, n = w_ada.shape
    tn = 1024 if n % 1024 == 0 else 512
    assert n % tn == 0
    return pl.pallas_call(
        _ada_kernel,
        grid=(n // tn,),
        in_specs=[pl.BlockSpec((SUBLANES, d), lambda j: (0, 0)),
                  pl.BlockSpec((d, tn), lambda j: (0, j)),
                  pl.BlockSpec((1, tn), lambda j: (0, j))],
        out_specs=pl.BlockSpec((SUBLANES, tn), lambda j: (0, j)),
        out_shape=jax.ShapeDtypeStruct((SUBLANES, n), F32),
        compiler_params=_cparams(("arbitrary",)),
        name="ada",
    )(cvec, w_ada, b_ada.reshape(1, n))


def _modulated_norm(x, w, shift, scale):
    ms = jnp.mean(x * x, axis=-1, keepdims=True)
    return (x * lax.rsqrt(ms + EPS) * w) * (1.0 + scale) + shift


def _mod_row(i, nctx_tiles, tiles_per_lat):
    return jnp.where(i < nctx_tiles, 0, 1 + (i - nctx_tiles) // tiles_per_lat)


def _inproj_kernel(xp_ref, xs_ref, mod_ref, n1_ref, w_ref, wkt_ref, bg_ref,
                   q_ref, kt_ref, v_ref, o_ref, gt_ref, xr_ref, xg_ref,
                   *, nctx_tiles, tiles_per_lat, d, qk, mw, rw, gh, qscale):
    i = pl.program_id(0)
    x = jnp.where(i < nctx_tiles, xp_ref[...], xs_ref[...])
    row = _mod_row(i, nctx_tiles, tiles_per_lat)
    shift = mod_ref[pl.ds(row, 1), 0:d]
    scale = mod_ref[pl.ds(row, 1), d:2 * d]
    hb = _modulated_norm(x, n1_ref[...], shift, scale).astype(BF16)

    def proj(c0, width):
        return jnp.dot(hb, w_ref[:, c0:c0 + width], preferred_element_type=F32)

    c0 = 0
    q_ref[...] = (proj(c0, qk) * qscale).astype(BF16); c0 += qk
    kt_ref[...] = lax.dot_general(wkt_ref[...], hb, (((1,), (1,)), ((), ())),
                                  preferred_element_type=F32).astype(BF16)
    v_ref[...] = proj(c0, mw).astype(BF16); c0 += mw
    o_ref[...] = proj(c0, mw); c0 += mw
    zg = proj(c0, LANES) + bg_ref[...]; c0 += LANES
    lane = lax.broadcasted_iota(I32, zg.shape, 1)
    log_sig = jnp.minimum(zg, 0.0) - jnp.log1p(jnp.exp(-jnp.abs(zg)))
    zg = jnp.where(lane % 2 == 1, log_sig, zg)
    gt_ref[...] = zg.T[0:gh, :]
    xr_ref[...] = proj(c0, rw); c0 += rw
    xg_ref[...] = proj(c0, rw)


def _in_proj(xp, xs, mod, norm1_w, w_cat, w_kt, bg, *, lat_seq, heads, dk, dv, rw):
    nc, d = xp.shape
    nl = xs.shape[0]
    nt = nc + nl
    tm = 256
    qk, mw, gh = heads * dk, heads * dv, heads * SUBLANES
    nctx_tiles = nc // tm
    kern = functools.partial(_inproj_kernel, nctx_tiles=nctx_tiles, tiles_per_lat=lat_seq // tm, d=d, qk=qk, mw=mw,
                             rw=rw, gh=gh, qscale=dk ** -0.5)
    row = lambda i: (i, 0)
    const = lambda i: (0, 0)
    return pl.pallas_call(
        kern,
        grid=(nt // tm,),
        in_specs=[pl.BlockSpec((tm, d), lambda i: (jnp.minimum(i, nctx_tiles - 1), 0)),
                  pl.BlockSpec((tm, d), lambda i: (jnp.maximum(i - nctx_tiles, 0), 0)),
                  pl.BlockSpec(mod.shape, const),
                  pl.BlockSpec((1, d), const),
                  pl.BlockSpec(w_cat.shape, const, pipeline_mode=pl.Buffered(1)),
                  pl.BlockSpec(w_kt.shape, const, pipeline_mode=pl.Buffered(1)),
                  pl.BlockSpec((1, LANES), const)],
        out_specs=[pl.BlockSpec((tm, qk), row), pl.BlockSpec((qk, tm), lambda i: (0, i)), pl.BlockSpec((tm, mw), row),
                   pl.BlockSpec((tm, mw), row), pl.BlockSpec((gh, tm), lambda i: (0, i)),
                   pl.BlockSpec((tm, rw), row), pl.BlockSpec((tm, rw), row)],
        out_shape=[jax.ShapeDtypeStruct((nt, qk), BF16), jax.ShapeDtypeStruct((qk, nt), BF16),
                   jax.ShapeDtypeStruct((nt, mw), BF16), jax.ShapeDtypeStruct((nt, mw), F32),
                   jax.ShapeDtypeStruct((gh, nt), F32),
                   jax.ShapeDtypeStruct((nt, rw), F32), jax.ShapeDtypeStruct((nt, rw), F32)],
        compiler_params=_cparams(("arbitrary",)),
        name="in_proj",
    )(xp, xs, mod, norm1_w.reshape(1, d), w_cat, w_kt, bg)


def _mlstm_kernel(*refs, t_len, dk, dv, has_state, emit_state):
    it = iter(refs)
    q_ref, kt_ref, v_ref, o_ref, gt_ref, nw_ref, tri_ref = (next(it) for _ in range(7))
    if has_state:
        c0_ref, n0_ref, m0_ref = (next(it) for _ in range(3))
    ya_ref = next(it)
    if emit_state:
        cn_ref, nn_ref, mn_ref = (next(it) for _ in range(3))
    hf_scr, hb_scr, c_scr, ma_scr, p_scr, w_scr, em_scr, kw_scr, dm_scr = (next(it) for _ in range(9))
    ln = MLSTM_L
    nchunks = t_len // ln
    assert ln % LANES == 0 and dk == LANES
    gsz = 2 if nchunks % 2 == 0 else 1
    ngroups = nchunks // gsz
    h_scr = (hf_scr, hb_scr)
    ext = dv + LANES
    lrep = ln // LANES

    for d in range(2):
        if has_state:
            c_scr[d, :, 0:dv] = c0_ref[d]
            c_scr[d, :, dv:ext] = _row_to_col(n0_ref[d], dk)
            ma_scr[d] = m0_ref[d]
        else:
            c_scr[d] = jnp.zeros((dk, ext), F32)
            ma_scr[d] = jnp.zeros((1, 1), F32)

    def chunk_start(d, j):
        return pl.multiple_of((j if d == 0 else nchunks - 1 - j) * ln, ln)

    def stage_a(d, j, slot):
        t0 = chunk_start(d, j)
        q = q_ref[pl.ds(t0, ln), :]
        kt = kt_ref[:, pl.ds(t0, ln)]
        g8 = gt_ref[:, pl.ds(t0, ln)]
        cum8 = jnp.dot(g8, tri_ref[d], precision=_HIGHEST, preferred_element_type=F32)
        valid = tri_ref[1 - d] > 0.5
        li = g8[2 * d:2 * d + 1]
        lf = g8[2 * d + 1:2 * d + 2]
        cum_row = cum8[2 * d + 1:2 * d + 2]
        total = jnp.sum(lf, axis=1, keepdims=True)
        a_row = li - cum_row
        cum_col = _row_to_col(cum_row, ln)
        m_prev = ma_scr[d]
        dmat = jnp.where(valid, _lane_tile(cum_col, lrep) + a_row, NEG)
        inter = cum_col + m_prev
        m_t = jnp.maximum(inter, jnp.max(dmat, axis=1, keepdims=True))
        s = jnp.dot(q, kt, preferred_element_type=F32)
        bank, par = slot
        p_scr[d, bank, par] = (s * jnp.exp(dmat - _lane_tile(m_t, lrep))).astype(BF16)
        w_scr[d, bank, par] = jnp.exp(inter - m_t)
        em_scr[d, bank, par] = jnp.exp(-m_t)
        g_row = total + a_row
        m_new = jnp.maximum(total + m_prev, jnp.max(g_row, axis=1, keepdims=True))
        kw_scr[d, bank, par] = (kt.astype(F32) * jnp.exp(g_row - m_new)).astype(BF16)
        decay = jnp.exp(total + m_prev - m_new)
        rid = lax.broadcasted_iota(I32, (SUBLANES, LANES), 0)
        dm_scr[d, bank, par] = jnp.where(rid == 0, decay, m_new)
        ma_scr[d] = m_new

    def stage_b(d, j, slot):
        t0 = chunk_start(d, j)
        q = q_ref[pl.ds(t0, ln), :]
        v_ext = jnp.concatenate([v_ref[pl.ds(t0, ln), :], jnp.ones((ln, LANES), BF16)], axis=1)
        c_st = c_scr[d]
        bank, par = slot
        full = (jnp.dot(p_scr[d, bank, par], v_ext, preferred_element_type=F32)
                + _lane_tile(w_scr[d, bank, par], ext // LANES) * jnp.dot(q, c_st.astype(BF16),
                                                                         preferred_element_type=F32))
        inv = 1.0 / jnp.maximum(jnp.abs(full[:, dv:ext]), em_scr[d, bank, par])
        h_scr[d][pl.ds(t0, ln), :] = full[:, 0:dv] * _lane_tile(inv, dv // LANES)
        decay = dm_scr[d, bank, par, 0:1, 0:1]
        c_scr[d] = decay * c_st + jnp.dot(kw_scr[d, bank, par], v_ext, preferred_element_type=F32)

    def group_a(jj, bank):
        for par in range(gsz):
            for d in range(2):
                stage_a(d, gsz * jj + par, (bank, par))

    def group_b(jj, bank):
        for par in range(gsz):
            for d in range(2):
                stage_b(d, gsz * jj + par, (bank, par))

    group_a(0, 0)

    def body(jj, carry):
        bank = jj % 2
        group_b(jj, bank)
        group_a(jj + 1, 1 - bank)
        return carry

    lax.fori_loop(0, ngroups - 1, body, 0)
    group_b(ngroups - 1, (ngroups - 1) % 2)

    def finish(j, carry):
        t0 = pl.multiple_of(j * ln, ln)
        hs = hf_scr[pl.ds(t0, ln), :] + hb_scr[pl.ds(t0, ln), :]
        ms = jnp.mean(hs * hs, axis=1, keepdims=True)
        y = hs * lax.rsqrt(ms + EPS) * nw_ref[...]
        ya_ref[pl.ds(t0, ln), :] = (_sigmoid(o_ref[pl.ds(t0, ln), :]) * y).astype(BF16)
        return carry

    lax.fori_loop(0, nchunks, finish, 0)

    if emit_state:
        for d in range(2):
            cn_ref[d] = c_scr[d, :, 0:dv]
            nn_ref[d] = c_scr[d, :, dv:ext].T[0:1, :]
            mn_ref[d] = dm_scr[d, (ngroups - 1) % 2, gsz - 1, 1:2, 0:1]


def _mlstm_tri():
    r = np.arange(MLSTM_L)
    fwd = (r[:, None] <= r[None, :]).astype(np.float32)
    return jnp.asarray(np.stack([fwd, fwd.T]))


def _mlstm(q, kt, v, o, gt, norm_w, *, row0, n_seq, t_len, heads, dk, dv, state=None, emit_state=False):
    blk0 = row0 // t_len
    tok = lambda s, h: (blk0 + s, h)
    in_specs = [pl.BlockSpec((t_len, dk), tok), pl.BlockSpec((dk, t_len), lambda s, h: (h, blk0 + s)),
                pl.BlockSpec((t_len, dv), tok), pl.BlockSpec((t_len, dv), tok),
                pl.BlockSpec((SUBLANES, t_len), lambda s, h: (h, blk0 + s)),
                pl.BlockSpec((None, 1, dv), lambda s, h: (h, 0, 0)),
                pl.BlockSpec((2, MLSTM_L, MLSTM_L), lambda s, h: (0, 0, 0))]
    args = [q, kt, v, o, gt, norm_w.reshape(heads, 1, dv), _mlstm_tri()]
    if state is not None:
        c0, n0, m0 = state
        in_specs += [pl.BlockSpec((None, 2, None, dk, dv), lambda s, h: (s, 0, h, 0, 0)),
                     pl.BlockSpec((None, 2, None, 1, dk), lambda s, h: (s, 0, h, 0, 0)),
                     pl.BlockSpec((None, 2, None, 1, 1), lambda s, h: (s, 0, h, 0, 0))]
        args += [c0, n0.reshape(n_seq, 2, heads, 1, dk), m0.reshape(n_seq, 2, heads, 1, 1)]
    out_specs = [pl.BlockSpec((t_len, dv), lambda s, h: (s, h))]
    out_shape = [jax.ShapeDtypeStruct((n_seq * t_len, heads * dv), BF16)]
    if emit_state:
        out_specs += [pl.BlockSpec((None, 2, None, dk, dv), lambda s, h: (s, 0, h, 0, 0)),
                      pl.BlockSpec((None, 2, None, 1, dk), lambda s, h: (s, 0, h, 0, 0)),
                      pl.BlockSpec((None, 2, None, 1, 1), lambda s, h: (s, 0, h, 0, 0))]
        out_shape += [jax.ShapeDtypeStruct((n_seq, 2, heads, dk, dv), F32),
                      jax.ShapeDtypeStruct((n_seq, 2, heads, 1, dk), F32),
                      jax.ShapeDtypeStruct((n_seq, 2, heads, 1, 1), F32)]
    kern = functools.partial(_mlstm_kernel, t_len=t_len, dk=dk, dv=dv, has_state=state is not None,
                             emit_state=emit_state)
    return pl.pallas_call(
        kern,
        grid=(n_seq, heads),
        in_specs=in_specs,
        out_specs=out_specs,
        out_shape=out_shape,
        scratch_shapes=[pltpu.VMEM((t_len, dv), F32), pltpu.VMEM((t_len, dv), F32),
                        pltpu.VMEM((2, dk, dv + LANES), F32), pltpu.VMEM((2, 1, 1), F32),
                        pltpu.VMEM((2, 2, 2, MLSTM_L, MLSTM_L), BF16), pltpu.VMEM((2, 2, 2, MLSTM_L, LANES), F32),
                        pltpu.VMEM((2, 2, 2, MLSTM_L, LANES), F32), pltpu.VMEM((2, 2, 2, dk, MLSTM_L), BF16),
                        pltpu.VMEM((2, 2, 2, SUBLANES, LANES), F32)],
        compiler_params=_cparams(("arbitrary", "arbitrary")),
        name="mlstm_state" if emit_state else "mlstm",
    )(*args)


def _gelu_tanh(x):
    return x * (0.5 * (1.0 + jnp.tanh(0.7978845608028654 * (x + 0.044715 * (x * x * x)))))


def _softplus(x):
    return jnp.maximum(x, 0.0) + jnp.log1p(jnp.exp(-jnp.abs(x)))


SCAN_SPLIT = 1
SCAN_GROUP = 4


def _rglru_kernel(*refs, t_len, seg, sub, pitch, tc, has_state, emit_state):
    it = iter(refs)
    xr_ref, xg_ref, cw_ref, cb_ref, wg_ref, bg_ref, lam_ref = (next(it) for _ in range(7))
    if has_state:
        h0_ref = next(it)
    yb_ref = next(it)
    if emit_state:
        hn_ref = next(it)
    a_scr, u_scr, cin_scr = (next(it) for _ in range(3))
    nchunks = t_len // tc
    piece = min(tc, sub)
    npieces = tc // piece
    ntile = xr_ref.shape[1] // LANES
    chains = [(d, lt) for d in range(2) for lt in range(ntile)]

    def scan_rows(t0, p):
        t = t0 + p * piece
        i = t // sub
        return pl.ds(pl.multiple_of(i * pitch + (t - i * sub), SUBLANES), piece), i

    ka = (-0.5 * RGLRU_C * 1.4426950408889634) * _softplus(-lam_ref[...])

    def gates(c, carry):
        t0 = pl.multiple_of(c * tc, tc)
        pos = lax.broadcasted_iota(I32, (tc, LANES), 0) % seg
        for lt in range(ntile):
            cols = slice(lt * LANES, (lt + 1) * LANES)
            x = xr_ref[pl.ds(t0, tc), cols]
            xc = cb_ref[:, cols] + cw_ref[CONV_LEFT:CONV_LEFT + 1, cols] * x
            for j in range(cw_ref.shape[0]):
                off = j - CONV_LEFT
                if off == 0:
                    continue
                shifted = pltpu.roll(x, (-off) % tc, 0)
                ok = (pos >= -off) if off < 0 else (pos < seg - off)
                xc = xc + cw_ref[j:j + 1, cols] * jnp.where(ok, shifted, 0.0)
            zh = jnp.dot(xc.astype(BF16), wg_ref[lt], preferred_element_type=F32) + bg_ref[lt]
            hx = 0.5 * xc
            for d in range(2):
                kd = ka[d:d + 1, cols]
                a = jnp.exp2(jnp.tanh(zh[:, (2 * d) * LANES:(2 * d + 1) * LANES]) * kd + kd)
                igx = hx * jnp.tanh(zh[:, (2 * d + 1) * LANES:(2 * d + 2) * LANES]) + hx
                u = jnp.sqrt(1.0 - a * a) * igx
                for p in range(npieces):
                    rows, _ = scan_rows(t0, p)
                    a_scr[d, lt, rows, :] = a[p * piece:(p + 1) * piece]
                    u_scr[d, lt, rows, :] = u[p * piece:(p + 1) * piece]
        return carry

    lax.fori_loop(0, nchunks, gates, 0)

    seg_len = sub // SCAN_SPLIT
    seg_chains = [(d, lt, s) for (d, lt) in chains for s in range(SCAN_SPLIT)]

    def scan(jg, carry):
        def rows(d, s, k):
            j = jg * SCAN_GROUP + k
            return pl.ds(s * seg_len + (j if d == 0 else seg_len - 1 - j), SUBLANES, stride=pitch)

        loaded = [[(a_scr[d, lt, rows(d, s, k), :], u_scr[d, lt, rows(d, s, k), :]) for k in range(SCAN_GROUP)]
                  for (d, lt, s) in seg_chains]
        out = []
        for (d, lt, s), (h, p), steps in zip(seg_chains, carry, loaded):
            for k, (a, u) in enumerate(steps):
                h = a * h + u
                p = a * p
                a_scr[d, lt, rows(d, s, k), :] = p
                u_scr[d, lt, rows(d, s, k), :] = h
            out.append((h, p))
        return tuple(out)

    zero = jnp.zeros((SUBLANES, LANES), F32)
    one = jnp.ones((SUBLANES, LANES), F32)
    assert seg_len % SCAN_GROUP == 0
    ends = dict(zip(seg_chains, lax.fori_loop(0, seg_len // SCAN_GROUP, scan, tuple((zero, one) for _ in seg_chains))))

    order = [(i, s) for i in range(SUBLANES) for s in range(SCAN_SPLIT)]
    for (d, lt) in chains:
        cols = slice(lt * LANES, (lt + 1) * LANES)
        cin = h0_ref[d:d + 1, cols] if has_state else jnp.zeros((1, LANES), F32)
        for (i, s) in (order if d == 0 else reversed(order)):
            h, p = ends[(d, lt, s)]
            cin_scr[d, lt, s * SUBLANES + i:s * SUBLANES + i + 1, :] = cin
            cin = h[i:i + 1] + p[i:i + 1] * cin
        if emit_state:
            hn_ref[d:d + 1, cols] = cin

    fpiece = min(tc, seg_len)

    def finish(c, carry):
        t0 = pl.multiple_of(c * tc, tc)
        for lt in range(ntile):
            cols = slice(lt * LANES, (lt + 1) * LANES)
            for p in range(tc // fpiece):
                t = t0 + p * fpiece
                i = t // sub
                local = t - i * sub
                rows = pl.ds(pl.multiple_of(i * pitch + local, SUBLANES), fpiece)
                crow = pl.ds((local // seg_len) * SUBLANES + i, 1)
                h = (u_scr[0, lt, rows, :] + a_scr[0, lt, rows, :] * cin_scr[0, lt, crow, :]
                     + u_scr[1, lt, rows, :] + a_scr[1, lt, rows, :] * cin_scr[1, lt, crow, :])
                nat = pl.ds(pl.multiple_of(t, SUBLANES), fpiece)
                yb_ref[nat, cols] = (h * _gelu_tanh(xg_ref[nat, cols])).astype(BF16)
        return carry

    lax.fori_loop(0, nchunks, finish, 0)


def _rglru(xr, xg, conv_w, conv_b, wg, bg, lam, *, row0, n_seq, t_len, seg, state=None, emit_state=False):
    rw = xr.shape[1]
    ntile = 2
    cb = ntile * LANES
    assert rw % cb == 0
    blk0 = row0 // t_len
    sub = t_len // SUBLANES
    pitch = sub + SUBLANES
    tc = 256
    tok = lambda s, g: (blk0 + s, g)
    in_specs = [pl.BlockSpec((t_len, cb), tok), pl.BlockSpec((t_len, cb), tok),
                pl.BlockSpec((conv_w.shape[0], cb), lambda s, g: (0, g)),
                pl.BlockSpec((1, cb), lambda s, g: (0, g)),
                pl.BlockSpec((ntile, LANES, 4 * LANES), lambda s, g: (g, 0, 0)),
                pl.BlockSpec((ntile, 1, 4 * LANES), lambda s, g: (g, 0, 0)),
                pl.BlockSpec((2, cb), lambda s, g: (0, g))]
    args = [xr, xg, conv_w, conv_b.reshape(1, rw), wg, bg, lam]
    if state is not None:
        in_specs.append(pl.BlockSpec((None, 2, cb), lambda s, g: (s, 0, g)))
        args.append(state)
    out_specs = [pl.BlockSpec((t_len, cb), lambda s, g: (s, g))]
    out_shape = [jax.ShapeDtypeStruct((n_seq * t_len, rw), BF16)]
    if emit_state:
        out_specs.append(pl.BlockSpec((None, 2, cb), lambda s, g: (s, 0, g)))
        out_shape.append(jax.ShapeDtypeStruct((n_seq, 2, rw), F32))
    kern = functools.partial(_rglru_kernel, t_len=t_len, seg=seg, sub=sub, pitch=pitch, tc=tc,
                             has_state=state is not None, emit_state=emit_state)
    return pl.pallas_call(
        kern,
        grid=(n_seq, rw // cb),
        in_specs=in_specs,
        out_specs=out_specs,
        out_shape=out_shape,
        scratch_shapes=[pltpu.VMEM((2, ntile, SUBLANES * pitch, LANES), F32),
                        pltpu.VMEM((2, ntile, SUBLANES * pitch, LANES), F32),
                        pltpu.VMEM((2, ntile, SCAN_SPLIT * SUBLANES, LANES), F32)],
        compiler_params=_cparams(("arbitrary", "arbitrary")),
        name="rglru_state" if emit_state else "rglru",
    )(*args)


def _pack_bf16_pair(lo, hi):
    def rounded_bits(x):
        return pltpu.bitcast(x.astype(BF16).astype(F32), U32)
    return (rounded_bits(lo) >> 16) | rounded_bits(hi)


def _unpack_bf16_pair(w):
    lo = pltpu.bitcast(w << 16, F32).astype(BF16)
    hi = pltpu.bitcast(w & jnp.uint32(0xFFFF0000), F32).astype(BF16)
    return lo, hi


def _outproj_kernel(yac_ref, yal_ref, ybc_ref, ybl_ref, xp_ref, xs_ref, mod_ref, n2_ref, wo_ref, wr_ref, br_ref,
                    x1_ref, hp_ref, ridx_ref, rwt_ref,
                    *, nctx_tiles, ntok_tiles, tiles_per_lat, d, mw, n_groups, epg):
    i = pl.program_id(0)

    @pl.when(i == ntok_tiles)
    def _():
        x1_ref[...] = jnp.zeros_like(x1_ref)
        hp_ref[...] = jnp.zeros_like(hp_ref)
        ridx_ref[...] = jnp.zeros_like(ridx_ref)
        rwt_ref[...] = jnp.zeros_like(rwt_ref)

    def token_tile(x_ref, ya_ref, yb_ref):
        x, ya, yb = x_ref[...], ya_ref[...], yb_ref[...]
        row = _mod_row(i, nctx_tiles, tiles_per_lat)
        gate1 = mod_ref[pl.ds(row, 1), 2 * d:3 * d]
        shift2 = mod_ref[pl.ds(row, 1), 3 * d:4 * d]
        scale2 = mod_ref[pl.ds(row, 1), 4 * d:5 * d]
        y = (jnp.dot(ya, wo_ref[0:mw, :], preferred_element_type=F32)
             + jnp.dot(yb, wo_ref[mw:, :], preferred_element_type=F32))
        x1 = x + gate1 * y
        x1_ref[...] = x1
        h2 = _modulated_norm(x1, n2_ref[...], shift2, scale2)
        half = d // 2
        packed = _pack_bf16_pair(h2[:, :half], h2[:, half:])
        trows = half // LANES
        for s in range(trows):
            hp_ref[pl.ds(s, x.shape[0], stride=trows), :] = packed[:, s * LANES:(s + 1) * LANES]

        lt = lax.dot_general(wr_ref[...], h2.astype(BF16), (((1,), (1,)), ((), ())),
                             preferred_element_type=F32) + br_ref[:, 0:1]
        gidx = lax.broadcasted_iota(I32, (SUBLANES, lt.shape[1]), 0)
        gl = jnp.where(gidx < n_groups, lt[0:SUBLANES], -jnp.inf)
        gmax = jnp.max(gl, axis=0, keepdims=True)
        grp = jnp.min(jnp.where(gl == gmax, gidx, n_groups), axis=0, keepdims=True)
        p_grp = 1.0 / jnp.sum(jnp.exp(gl - gmax), axis=0, keepdims=True)
        el = lt[SUBLANES:SUBLANES + epg]
        for g in range(1, n_groups):
            el = jnp.where(grp == g, lt[SUBLANES + g * epg:SUBLANES + (g + 1) * epg], el)
        eidx = lax.broadcasted_iota(I32, el.shape, 0)
        v1 = jnp.max(el, axis=0, keepdims=True)
        i1 = jnp.min(jnp.where(el == v1, eidx, epg), axis=0, keepdims=True)
        el2 = jnp.where(eidx == i1, -jnp.inf, el)
        v2 = jnp.max(el2, axis=0, keepdims=True)
        i2 = jnp.min(jnp.where(el2 == v2, eidx, epg), axis=0, keepdims=True)
        e2 = jnp.exp(v2 - v1)
        w1 = p_grp / (1.0 + e2)
        w2 = p_grp * e2 / (1.0 + e2)
        rid = lax.broadcasted_iota(I32, ridx_ref.shape, 0)
        ridx_ref[...] = jnp.where(rid == 0, grp * epg + i1, jnp.where(rid == 1, grp * epg + i2, 0))
        rwt_ref[...] = jnp.where(rid == 0, w1, jnp.where(rid == 1, w2, 0.0))

    @pl.when(i < nctx_tiles)
    def _():
        token_tile(xp_ref, yac_ref, ybc_ref)

    @pl.when((i >= nctx_tiles) & (i < ntok_tiles))
    def _():
        token_tile(xs_ref, yal_ref, ybl_ref)


def _out_proj(ya_c, ya_l, yb_c, yb_l, xp, xs, mod, norm2_w, w_out, wr, br, *, lat_seq, n_groups, epg):
    nc, d = xp.shape
    nl = xs.shape[0]
    nt = nc + nl
    mw = ya_c.shape[1]
    tm = 256
    trows = (d // 2) // LANES
    nctx_tiles, ntok_tiles = nc // tm, nt // tm
    kern = functools.partial(_outproj_kernel, nctx_tiles=nctx_tiles, ntok_tiles=ntok_tiles,
                             tiles_per_lat=lat_seq // tm, d=d, mw=mw, n_groups=n_groups, epg=epg)
    ctx = lambda i: (jnp.minimum(i, nctx_tiles - 1), 0)
    lat = lambda i: (jnp.clip(i - nctx_tiles, 0, nl // tm - 1), 0)
    row = lambda i: (i, 0)
    const = lambda i: (0, 0)
    return pl.pallas_call(
        kern,
        grid=(ntok_tiles + 1,),
        in_specs=[pl.BlockSpec((tm, mw), ctx), pl.BlockSpec((tm, mw), lat),
                  pl.BlockSpec((tm, yb_c.shape[1]), ctx), pl.BlockSpec((tm, yb_c.shape[1]), lat),
                  pl.BlockSpec((tm, d), ctx), pl.BlockSpec((tm, d), lat),
                  pl.BlockSpec(mod.shape, const),
                  pl.BlockSpec((1, d), const),
                  pl.BlockSpec(w_out.shape, const, pipeline_mode=pl.Buffered(1)),
                  pl.BlockSpec(wr.shape, const),
                  pl.BlockSpec(br.shape, const)],
        out_specs=[pl.BlockSpec((tm, d), row), pl.BlockSpec((tm * trows, LANES), row),
                   pl.BlockSpec((SUBLANES, tm), lambda i: (0, i)), pl.BlockSpec((SUBLANES, tm), lambda i: (0, i))],
        out_shape=[jax.ShapeDtypeStruct((nt + tm, d), F32), jax.ShapeDtypeStruct(((nt + tm) * trows, LANES), U32),
                   jax.ShapeDtypeStruct((SUBLANES, nt + tm), I32), jax.ShapeDtypeStruct((SUBLANES, nt + tm), F32)],
        compiler_params=_cparams(("arbitrary",)),
        name="out_proj",
    )(ya_c, ya_l, yb_c, yb_l, xp, xs, mod, norm2_w.reshape(1, d), w_out, wr, br)


MOE_ROWS = 256


ROUTE_TILE = 512


def _rank_kernel(ridx_ref, tri_ref, rank_ref, cnt_ref, carry_scr, *, n_exp):
    @pl.when(pl.program_id(0) == 0)
    def _():
        carry_scr[...] = jnp.zeros_like(carry_scr)

    e = ridx_ref[...]
    tr = e.shape[1]
    eid = lax.broadcasted_iota(I32, (n_exp, tr), 0)
    carry = carry_scr[:, 0:1]
    ranks = []
    for kk in range(TOP_K):
        hit = eid == e[kk:kk + 1]
        cum = jnp.dot(jnp.where(hit, 1.0, 0.0).astype(BF16), tri_ref[...], preferred_element_type=F32)
        ranks.append(jnp.sum(jnp.where(hit, cum + carry, 0.0), axis=0, keepdims=True) - 1.0)
        carry = carry + cum[:, tr - 1:tr]
    carry_scr[...] = jnp.broadcast_to(carry, carry_scr.shape)
    cnt_ref[...] = jnp.broadcast_to(carry, cnt_ref.shape)
    rid = lax.broadcasted_iota(I32, rank_ref.shape, 0)
    rank_ref[...] = jnp.where(rid == 0, ranks[0], jnp.where(rid == 1, ranks[1], 0.0)).astype(I32)


def _dest_kernel(ridx_ref, rank_ref, pstart_ref, dest_ref, *, n_exp):
    e = ridx_ref[...]
    tr = e.shape[1]
    eid = lax.broadcasted_iota(I32, (n_exp, tr), 0)
    ps = pstart_ref[:, 0:1]
    rows = [jnp.sum(jnp.where(eid == e[kk:kk + 1], ps, 0.0), axis=0, keepdims=True) for kk in range(TOP_K)]
    rid = lax.broadcasted_iota(I32, dest_ref.shape, 0)
    dest_ref[...] = rank_ref[...] + jnp.where(rid == 0, rows[0], jnp.where(rid == 1, rows[1], 0.0)).astype(I32)


def _invert_kernel(dest_ref, fill_ref, rowtok_ref, sem):
    i = pl.program_id(0)
    tr = dest_ref.shape[1]

    @pl.when(i == 0)
    def _():
        fill = pltpu.make_async_copy(fill_ref, rowtok_ref, sem.at[0])
        fill.start()
        fill.wait()

    def body(r, c):
        for kk in range(TOP_K):
            rowtok_ref[dest_ref[kk, r]] = i * tr + r
        return c
    lax.fori_loop(0, tr, body, 0, unroll=16)


def _routing(ridx, n_tok, n_exp, n_blocks, zero_row):
    tr = ROUTE_TILE
    steps = n_tok // tr
    tri = jnp.asarray(np.triu(np.ones((tr, tr), np.float32)), BF16)
    tile = pl.BlockSpec((SUBLANES, tr), lambda i: (0, i))
    cnt_spec = pl.BlockSpec((n_exp, LANES), lambda i: (0, 0))
    rank, cnt = pl.pallas_call(
        functools.partial(_rank_kernel, n_exp=n_exp),
        grid=(steps,),
        in_specs=[tile, pl.BlockSpec((tr, tr), lambda i: (0, 0))],
        out_specs=[tile, cnt_spec],
        out_shape=[jax.ShapeDtypeStruct((SUBLANES, n_tok), I32), jax.ShapeDtypeStruct((n_exp, LANES), F32)],
        scratch_shapes=[pltpu.VMEM((n_exp, LANES), F32)],
        compiler_params=_cparams(("arbitrary",)),
        name="route_rank",
    )(ridx, tri)
    counts = cnt[:, 0].astype(I32)
    padded = (counts + MOE_ROWS - 1) // MOE_ROWS * MOE_ROWS
    pad_end = jnp.cumsum(padded)
    pad_start = pad_end - padded
    dest = pl.pallas_call(
        functools.partial(_dest_kernel, n_exp=n_exp),
        grid=(steps,),
        in_specs=[tile, tile, cnt_spec],
        out_specs=tile,
        out_shape=jax.ShapeDtypeStruct((SUBLANES, n_tok), I32),
        compiler_params=_cparams(("arbitrary",)),
        name="route_dest",
    )(ridx, rank, jnp.broadcast_to(pad_start.astype(F32)[:, None], (n_exp, LANES)))
    row_tok = pl.pallas_call(
        _invert_kernel,
        grid=(steps,),
        in_specs=[pl.BlockSpec((SUBLANES, tr), lambda i: (0, i), memory_space=pltpu.SMEM),
                  pl.BlockSpec(memory_space=pl.ANY)],
        out_specs=pl.BlockSpec(memory_space=pltpu.SMEM),
        out_shape=jax.ShapeDtypeStruct((n_blocks * MOE_ROWS,), I32),
        scratch_shapes=[pltpu.SemaphoreType.DMA((1,))],
        compiler_params=_cparams(("arbitrary",)),
        name="route_invert",
    )(dest, jnp.full((n_blocks * MOE_ROWS,), zero_row, I32))
    blk_row0 = jnp.arange(n_blocks, dtype=I32) * MOE_ROWS
    blk_e = jnp.minimum(jnp.sum((pad_end[None, :] <= blk_row0[:, None]).astype(I32), axis=1), n_exp - 1)
    nused = (pad_end[-1:] // MOE_ROWS).astype(I32)
    return dest, row_tok, blk_e, nused


CAST_ROWS = 256
WEIGHT_DMA_SPLIT = 8


def _expert_kernel(blk_e_ref, nused_ref, tok_ref, nxt_ref, src_ref, wg_hbm, wu_hbm, wd_hbm, y_ref,
                   xbuf, xsem, stage_g, stage_u, stage_d, bf_g, bf_u, bf_d, wsem, *, half, rows, trows):
    b = pl.program_id(0)
    nused = nused_ref[0]
    w_hbm = (wg_hbm, wu_hbm, wd_hbm)
    stage = (stage_g, stage_u, stage_d)
    wbf = (bf_g, bf_u, bf_d)

    def weight_copy(e, j):
        return pltpu.make_async_copy(w_hbm[j].at[e], stage[j], wsem.at[j])

    def start_weights(e, j):
        step = stage[j].shape[0] // WEIGHT_DMA_SPLIT
        for c in range(WEIGHT_DMA_SPLIT):
            sl = pl.ds(c * step, step)
            pltpu.make_async_copy(w_hbm[j].at[e, sl, :], stage[j].at[sl, :], wsem.at[j]).start(priority=1)

    def row_copy(tok, slot, r):
        src = src_ref.at[pl.ds(pl.multiple_of(tok * trows, trows), trows), :]
        return pltpu.make_async_copy(src, xbuf.at[slot, pl.ds(r * trows, trows), :], xsem.at[slot])

    @pl.when(b >= nused)
    def _():
        y_ref[...] = jnp.zeros_like(y_ref)

    @pl.when(b < nused)
    def _():
        e = blk_e_ref[b]

        @pl.when(b == 0)
        def _():
            for j in range(3):
                start_weights(e, j)
            def body(r, c):
                row_copy(tok_ref[0, r], 0, r).start()
                return c
            lax.fori_loop(0, rows, body, 0, unroll=8)

        @pl.when((b == 0) | (blk_e_ref[jnp.maximum(b - 1, 0)] != e))
        def _():
            nb = lax.while_loop(lambda k: (k < nused) & (blk_e_ref[jnp.minimum(k, nused - 1)] == e),
                                lambda k: k + 1, b + 1)
            for j in range(3):
                weight_copy(e, j).wait()
                n_steps = stage[j].shape[0] // CAST_ROWS

                def cast(c, carry, j=j):
                    sl = pl.ds(pl.multiple_of(c * CAST_ROWS, CAST_ROWS), CAST_ROWS)
                    wbf[j][sl, :] = stage[j][sl, :].astype(BF16)
                    return carry
                lax.fori_loop(0, n_steps, cast, 0)

                @pl.when(nb < nused)
                def _(j=j):
                    start_weights(blk_e_ref[jnp.minimum(nb, nused - 1)], j)

        slot = b % 2
        pltpu.make_async_copy(src_ref.at[pl.ds(0, rows * trows), :], xbuf.at[slot], xsem.at[slot]).wait()

        def compute(prefetch):
            parts = [_unpack_bf16_pair(xbuf[slot, pl.ds(s, rows, stride=trows), :]) for s in range(trows)]
            lo = jnp.concatenate([p[0] for p in parts], axis=1)
            hi = jnp.concatenate([p[1] for p in parts], axis=1)
            if prefetch:
                for r in range(rows):
                    row_copy(nxt_ref[0, r], 1 - slot, r).start()

            def up(w_ref):
                return (jnp.dot(lo, w_ref[0:half, :], preferred_element_type=F32)
                        + jnp.dot(hi, w_ref[half:, :], preferred_element_type=F32))

            g = up(wbf[0])
            h = ((g * _sigmoid(g)) * up(wbf[1])).astype(BF16)
            y = jnp.dot(h, wbf[2][...], preferred_element_type=F32)
            packed = _pack_bf16_pair(y[:, :half], y[:, half:])
            for s in range(trows):
                y_ref[pl.ds(s, rows, stride=trows), :] = packed[:, s * LANES:(s + 1) * LANES]

        @pl.when(b + 1 < nused)
        def _():
            compute(True)

        @pl.when(b + 1 >= nused)
        def _():
            compute(False)


def _experts(blk_e, nused, row_tok, src, wg, wu, wd, n_blocks):
    n_exp, d, ff = wg.shape
    any_spec = pl.BlockSpec(memory_space=pl.ANY)
    trows = (d // 2) // LANES
    tok = row_tok.reshape(n_blocks, 1, MOE_ROWS)
    return pl.pallas_call(
        functools.partial(_expert_kernel, half=d // 2, rows=MOE_ROWS, trows=trows),
        grid_spec=pltpu.PrefetchScalarGridSpec(
            num_scalar_prefetch=2,
            grid=(n_blocks,),
            in_specs=[pl.BlockSpec((None, 1, MOE_ROWS), lambda b, be, nu: (b, 0, 0), memory_space=pltpu.SMEM),
                      pl.BlockSpec((None, 1, MOE_ROWS), lambda b, be, nu: (jnp.minimum(b + 1, n_blocks - 1), 0, 0),
                                   memory_space=pltpu.SMEM),
                      any_spec, any_spec, any_spec, any_spec],
            out_specs=pl.BlockSpec((MOE_ROWS * trows, LANES), lambda b, be, nu: (b, 0)),
            scratch_shapes=[pltpu.VMEM((2, MOE_ROWS * trows, LANES), U32), pltpu.SemaphoreType.DMA((2,)),
                            pltpu.VMEM((d, ff), F32), pltpu.VMEM((d, ff), F32), pltpu.VMEM((ff, d), F32),
                            pltpu.VMEM((d, ff), BF16), pltpu.VMEM((d, ff), BF16), pltpu.VMEM((ff, d), BF16),
                            pltpu.SemaphoreType.DMA((3,))]),
        out_shape=jax.ShapeDtypeStruct((n_blocks * MOE_ROWS * trows, LANES), U32),
        compiler_params=_cparams(("arbitrary",)),
        name="experts",
    )(blk_e, nused, tok, tok, src, wg, wu, wd)


def _combine_kernel(dest_ref, x1_ref, wt_ref, mod_ref, fw_ref, y_ref, o_ref, ybuf_even, ybuf_odd, sem,
                    *, tile0, n_all, tiles_per_seq, d, lat):
    i = pl.program_id(0)
    last = pl.num_programs(0) - 1
    tm = x1_ref.shape[0]
    bufs = (ybuf_even, ybuf_odd)

    trows = (d // 2) // LANES

    def row_copy(tile, par, kk, r):
        row = dest_ref[kk * n_all + (tile0 + tile) * tm + r]
        src = y_ref.at[pl.ds(pl.multiple_of(row * trows, trows), trows), :]
        return pltpu.make_async_copy(src, bufs[par].at[kk, pl.ds(r * trows, trows), :], sem.at[par])

    @pl.when(i == 0)
    def _():
        def body(r, c):
            for kk in range(TOP_K):
                row_copy(0, 0, kk, r).start()
            return c
        lax.fori_loop(0, tm, body, 0, unroll=8)

    def step(par, prefetch):
        for kk in range(TOP_K):
            pltpu.make_async_copy(y_ref.at[pl.ds(0, tm * trows), :], bufs[par].at[kk], sem.at[par]).wait()
        if prefetch:
            for r in range(tm):
                for kk in range(TOP_K):
                    row_copy(i + 1, 1 - par, kk, r).start(priority=kk)
        row = (1 + i // tiles_per_seq) if lat else 0
        gate2 = mod_ref[pl.ds(row, 1), 5 * d:6 * d]
        wt = wt_ref[...]
        half = d // 2
        w0 = _row_to_col(wt[0:1], tm)
        w1 = _row_to_col(wt[1:2], tm)
        unpack = (lambda p: pltpu.bitcast(p << 16, F32),
                  lambda p: pltpu.bitcast(p & jnp.uint32(0xFFFF0000), F32))
        ssq = jnp.zeros((tm, 1), F32)
        for s in range(trows):
            p0 = bufs[par][0, pl.ds(s, tm, stride=trows), :]
            p1 = bufs[par][1, pl.ds(s, tm, stride=trows), :]
            for side in range(2):
                cols = slice(side * half + s * LANES, side * half + (s + 1) * LANES)
                x = x1_ref[:, cols] + gate2[:, cols] * (w0 * unpack[side](p0) + w1 * unpack[side](p1))
                ssq = ssq + jnp.sum(x * x, axis=-1, keepdims=True)
                o_ref[:, cols] = x
        scale = lax.rsqrt(ssq * (1.0 / d) + EPS)
        o_ref[...] = o_ref[...] * scale * fw_ref[...]

    for par in range(2):
        @pl.when((i % 2 == par) & (i < last))
        def _(par=par):
            step(par, True)

        @pl.when((i % 2 == par) & (i == last))
        def _(par=par):
            step(par, False)


def _combine(dest_flat, x1, rwt, mod, final_w, y_rows, *, row0, n_tok, seq_len, lat):
    d = x1.shape[1]
    tm = 256
    tile0 = row0 // tm
    trows = (d // 2) // LANES
    n_all = dest_flat.shape[0] // TOP_K
    kern = functools.partial(_combine_kernel, tile0=tile0, n_all=n_all, tiles_per_seq=seq_len // tm, d=d, lat=lat)
    return pl.pallas_call(
        kern,
        grid_spec=pltpu.PrefetchScalarGridSpec(
            num_scalar_prefetch=1,
            grid=(n_tok // tm,),
            in_specs=[pl.BlockSpec((tm, d), lambda i, dr: (tile0 + i, 0)),
                      pl.BlockSpec((SUBLANES, tm), lambda i, dr: (0, tile0 + i)),
                      pl.BlockSpec(mod.shape, lambda i, dr: (0, 0)),
                      pl.BlockSpec((1, d), lambda i, dr: (0, 0)),
                      pl.BlockSpec(memory_space=pl.ANY)],
            out_specs=pl.BlockSpec((tm, d), lambda i, dr: (i, 0)),
            scratch_shapes=[pltpu.VMEM((TOP_K, tm * trows, LANES), U32), pltpu.VMEM((TOP_K, tm * trows, LANES), U32),
                            pltpu.SemaphoreType.DMA((2,))]),
        out_shape=jax.ShapeDtypeStruct((n_tok, d), F32),
        compiler_params=_cparams(("arbitrary",)),
        name="combine_lat" if lat else "combine_ctx",
    )(dest_flat, x1, rwt, mod, final_w.reshape(1, d), y_rows)


def _gate_layout(w_gates, b_gates, heads):
    d = w_gates.shape[0]
    w = w_gates.reshape(d, 4, heads).transpose(0, 2, 1)
    w = jnp.pad(w, ((0, 0), (0, 0), (0, SUBLANES - 4))).reshape(d, heads * SUBLANES)
    b = b_gates.reshape(4, heads).T
    b = jnp.pad(b, ((0, 0), (0, SUBLANES - 4))).reshape(1, heads * SUBLANES)
    padl = LANES - heads * SUBLANES
    return jnp.pad(w, ((0, 0), (0, padl))), jnp.pad(b, ((0, 0), (0, padl)))


def kernel(x_prompt, x_sample, state_mlstm_c, state_mlstm_n, state_mlstm_m, state_rglru_h, c, c_ctx, w_ada, b_ada,
           norm1_w, w_in, b_gates, conv_w, conv_b, rg_wa, rg_ba, rg_wx, rg_bx, rg_lambda, mlstm_norm_w, w_out,
           norm2_w, router_group_w, router_group_b, router_expert_w, router_expert_b, expert_w_gate, expert_w_up,
           expert_w_down, final_norm_w):
    n_req, seq, d = x_prompt.shape
    n_lat, lat_seq, _ = x_sample.shape
    depth = w_in.shape[0]
    assert depth == 1, "the token-axis plumbing below is written for the single-layer trunk"
    heads, dk, dv = state_mlstm_c.shape[3:]
    rw = state_rglru_h.shape[-1]
    nblk = rg_wa.shape[2]
    assert rw // nblk == LANES
    n_groups, epg = router_expert_w.shape[1], router_expert_w.shape[3]
    n_exp = n_groups * epg
    qk, mw = heads * dk, heads * dv
    nc, nl = n_req * seq, n_lat * lat_seq
    nt = nc + nl
    assert nc % lat_seq == 0 and n_lat + 1 <= SUBLANES
    l = 0

    xp = x_prompt.reshape(nc, d)
    xs = x_sample.reshape(nl, d)
    cvec = jnp.zeros((SUBLANES, d), F32).at[0].set(c_ctx).at[1:1 + n_lat].set(c)
    mod = _ada(cvec, w_ada[l], b_ada[l])

    w = w_in[l]
    g0 = 2 * qk + 2 * mw
    wgate, bgate = _gate_layout(w[:, g0:g0 + 4 * heads], b_gates[l], heads)
    w_cat = jnp.concatenate([w[:, :qk], w[:, 2 * qk:g0], wgate, w[:, g0 + 4 * heads:]], axis=1).astype(BF16)
    w_kt = w[:, qk:2 * qk].T.astype(BF16)
    q, kt, v, o, gt, xr, xg = _in_proj(xp, xs, mod, norm1_w[l], w_cat, w_kt, bgate, lat_seq=lat_seq, heads=heads,
                                       dk=dk, dv=dv, rw=rw)

    mkw = dict(heads=heads, dk=dk, dv=dv)
    ya_c, new_c, new_n, new_m = _mlstm(q, kt, v, o, gt, mlstm_norm_w[l], row0=0, n_seq=n_req, t_len=seq,
                                       emit_state=True, **mkw)
    (ya_l,) = _mlstm(q, kt, v, o, gt, mlstm_norm_w[l], row0=nc, n_seq=n_lat, t_len=lat_seq,
                     state=(state_mlstm_c[:, l], state_mlstm_n[:, l], state_mlstm_m[:, l]), **mkw)

    wg = (0.5 * jnp.concatenate([rg_wa[l, 0], rg_wx[l, 0], rg_wa[l, 1], rg_wx[l, 1]], axis=-1)).astype(BF16)
    bg = 0.5 * jnp.concatenate([rg_ba[l, 0].reshape(nblk, 1, LANES), rg_bx[l, 0].reshape(nblk, 1, LANES),
                                rg_ba[l, 1].reshape(nblk, 1, LANES), rg_bx[l, 1].reshape(nblk, 1, LANES)], axis=-1)
    rargs = (xr, xg, conv_w[l], conv_b[l], wg, bg, rg_lambda[l])
    yb_c, new_h = _rglru(*rargs, row0=0, n_seq=n_req, t_len=seq, seg=seq, emit_state=True)
    (yb_l,) = _rglru(*rargs, row0=nc, n_seq=n_lat, t_len=lat_seq, seg=GRID_W, state=state_rglru_h[:, l])

    r_rows = -(-(SUBLANES + n_exp) // 16) * 16
    wr = jnp.zeros((r_rows, d), F32)
    wr = wr.at[0:n_groups].set(router_group_w[l].T)
    wr = wr.at[SUBLANES:SUBLANES + n_exp].set(router_expert_w[l].transpose(0, 2, 1).reshape(n_exp, d)).astype(BF16)
    br = jnp.zeros((r_rows, LANES), F32)
    br = br.at[0:n_groups, 0].set(router_group_b[l])
    br = br.at[SUBLANES:SUBLANES + n_exp, 0].set(router_expert_b[l].reshape(n_exp))
    x1, hp, ridx, rwt = _out_proj(ya_c, ya_l, yb_c, yb_l, xp, xs, mod, norm2_w[l], w_out[l].astype(BF16), wr, br,
                                  lat_seq=lat_seq, n_groups=n_groups, epg=epg)

    n_blocks = (nt * TOP_K) // MOE_ROWS + n_exp
    dest, row_tok, blk_e, nused = _routing(ridx, nt, n_exp, n_blocks, zero_row=nt)
    only_layer = lambda a: a.reshape(a.shape[1:])
    y_rows = _experts(blk_e, nused, row_tok, hp, only_layer(expert_w_gate), only_layer(expert_w_up),
                      only_layer(expert_w_down), n_blocks)

    dest_flat = dest[0:TOP_K].reshape(-1)
    y_prompt = _combine(dest_flat, x1, rwt, mod, final_norm_w, y_rows, row0=0, n_tok=nc, seq_len=seq, lat=False)
    y_sample = _combine(dest_flat, x1, rwt, mod, final_norm_w, y_rows, row0=nc, n_tok=nl, seq_len=lat_seq,
                        lat=True)

    return (y_prompt.reshape(n_req, seq, d), y_sample.reshape(n_lat, lat_seq, d),
            new_c[:, None], new_n.reshape(n_req, 1, 2, heads, dk), new_m.reshape(n_req, 1, 2, heads),
            new_h[:, None])
```

```python
import functools

import jax
import jax.numpy as jnp
import numpy as np
from jax import lax
from jax.experimental import pallas as pl
from jax.experimental.pallas import tpu as pltpu

F32 = jnp.float32
BF16 = jnp.bfloat16
I32 = jnp.int32
U32 = jnp.uint32

EPS = 1e-6
GRID_W = 64
CONV_LEFT = 2
RGLRU_C = 8.0
TOP_K = 2
LANES = 128
SUBLANES = 8
MLSTM_L = 256
NEG = -1e30
VMEM_LIMIT = 56 * 1024 * 1024

_HIGHEST = lax.Precision.HIGHEST


def _cparams(sem, vmem=VMEM_LIMIT):
    return pltpu.CompilerParams(dimension_semantics=sem, vmem_limit_bytes=vmem)


def _sigmoid(x):
    return 0.5 * jnp.tanh(0.5 * x) + 0.5


def _row_to_col(r, n):
    return jnp.broadcast_to(r, (LANES, n)).T


def _lane_tile(x, reps):
    return x if reps == 1 else jnp.concatenate([x] * reps, axis=1)


def _ada_kernel(c_ref, w_ref, b_ref, o_ref):
    c = c_ref[...]
    s = (c * _sigmoid(c)).astype(BF16)
    o_ref[...] = jnp.dot(s, w_ref[...].astype(BF16), preferred_element_type=F32) + b_ref[...]


def _ada(cvec, w_ada, b_ada):
    d, n = w_ada.shape
    tn = 1024 if n % 1024 == 0 else 512
    assert n % tn == 0
    return pl.pallas_call(
        _ada_kernel,
        grid=(n // tn,),
        in_specs=[pl.BlockSpec((SUBLANES, d), lambda j: (0, 0)),
                  pl.BlockSpec((d, tn), lambda j: (0, j)),
                  pl.BlockSpec((1, tn), lambda j: (0, j))],
        out_specs=pl.BlockSpec((SUBLANES, tn), lambda j: (0, j)),
        out_shape=jax.ShapeDtypeStruct((SUBLANES, n), F32),
        compiler_params=_cparams(("arbitrary",)),
        name="ada",
    )(cvec, w_ada, b_ada.reshape(1, n))


def _modulated_norm(x, w, shift, scale):
    ms = jnp.mean(x * x, axis=-1, keepdims=True)
    return (x * lax.rsqrt(ms + EPS) * w) * (1.0 + scale) + shift


def _mod_row(i, nctx_tiles, tiles_per_lat):
    return jnp.where(i < nctx_tiles, 0, 1 + (i - nctx_tiles) // tiles_per_lat)


def _inproj_kernel(xp_ref, xs_ref, mod_ref, n1_ref, w_ref, wkt_ref, bg_ref,
                   q_ref, kt_ref, v_ref, o_ref, gt_ref, xr_ref, xg_ref,
                   *, nctx_tiles, tiles_per_lat, d, qk, mw, rw, gh, qscale):
    i = pl.program_id(0)
    x = jnp.where(i < nctx_tiles, xp_ref[...], xs_ref[...])
    row = _mod_row(i, nctx_tiles, tiles_per_lat)
    shift = mod_ref[pl.ds(row, 1), 0:d]
    scale = mod_ref[pl.ds(row, 1), d:2 * d]
    hb = _modulated_norm(x, n1_ref[...], shift, scale).astype(BF16)

    def proj(c0, width):
        return jnp.dot(hb, w_ref[:, c0:c0 + width], preferred_element_type=F32)

    c0 = 0
    q_ref[...] = (proj(c0, qk) * qscale).astype(BF16); c0 += qk
    kt_ref[...] = lax.dot_general(wkt_ref[...], hb, (((1,), (1,)), ((), ())),
                                  preferred_element_type=F32).astype(BF16)
    v_ref[...] = proj(c0, mw).astype(BF16); c0 += mw
    o_ref[...] = proj(c0, mw); c0 += mw
    zg = proj(c0, LANES) + bg_ref[...]; c0 += LANES
    lane = lax.broadcasted_iota(I32, zg.shape, 1)
    log_sig = jnp.minimum(zg, 0.0) - jnp.log1p(jnp.exp(-jnp.abs(zg)))
    zg = jnp.where(lane % 2 == 1, log_sig, zg)
    gt_ref[...] = zg.T[0:gh, :]
    xr_ref[...] = proj(c0, rw); c0 += rw
    xg_ref[...] = proj(c0, rw)


def _in_proj(xp, xs, mod, norm1_w, w_cat, w_kt, bg, *, lat_seq, heads, dk, dv, rw):
    nc, d = xp.shape
    nl = xs.shape[0]
    nt = nc + nl
    tm = 256
    qk, mw, gh = heads * dk, heads * dv, heads * SUBLANES
    nctx_tiles = nc // tm
    kern = functools.partial(_inproj_kernel, nctx_tiles=nctx_tiles, tiles_per_lat=lat_seq // tm, d=d, qk=qk, mw=mw,
                             rw=rw, gh=gh, qscale=dk ** -0.5)
    row = lambda i: (i, 0)
    const = lambda i: (0, 0)
    return pl.pallas_call(
        kern,
        grid=(nt // tm,),
        in_specs=[pl.BlockSpec((tm, d), lambda i: (jnp.minimum(i, nctx_tiles - 1), 0)),
                  pl.BlockSpec((tm, d), lambda i: (jnp.maximum(i - nctx_tiles, 0), 0)),
                  pl.BlockSpec(mod.shape, const),
                  pl.BlockSpec((1, d), const),
                  pl.BlockSpec(w_cat.shape, const, pipeline_mode=pl.Buffered(1)),
                  pl.BlockSpec(w_kt.shape, const, pipeline_mode=pl.Buffered(1)),
                  pl.BlockSpec((1, LANES), const)],
        out_specs=[pl.BlockSpec((tm, qk), row), pl.BlockSpec((qk, tm), lambda i: (0, i)), pl.BlockSpec((tm, mw), row),
                   pl.BlockSpec((tm, mw), row), pl.BlockSpec((gh, tm), lambda i: (0, i)),
                   pl.BlockSpec((tm, rw), row), pl.BlockSpec((tm, rw), row)],
        out_shape=[jax.ShapeDtypeStruct((nt, qk), BF16), jax.ShapeDtypeStruct((qk, nt), BF16),
                   jax.ShapeDtypeStruct((nt, mw), BF16), jax.ShapeDtypeStruct((nt, mw), F32),
                   jax.ShapeDtypeStruct((gh, nt), F32),
                   jax.ShapeDtypeStruct((nt, rw), F32), jax.ShapeDtypeStruct((nt, rw), F32)],
        compiler_params=_cparams(("arbitrary",)),
        name="in_proj",
    )(xp, xs, mod, norm1_w.reshape(1, d), w_cat, w_kt, bg)


def _mlstm_kernel(*refs, t_len, dk, dv, has_state, emit_state):
    it = iter(refs)
    q_ref, kt_ref, v_ref, o_ref, gt_ref, nw_ref, tri_ref = (next(it) for _ in range(7))
    if has_state:
        c0_ref, n0_ref, m0_ref = (next(it) for _ in range(3))
    ya_ref = next(it)
    if emit_state:
        cn_ref, nn_ref, mn_ref = (next(it) for _ in range(3))
    hf_scr, hb_scr, c_scr, ma_scr, p_scr, w_scr, em_scr, kw_scr, dm_scr = (next(it) for _ in range(9))
    ln = MLSTM_L
    nchunks = t_len // ln
    assert ln % LANES == 0 and dk == LANES
    gsz = 2 if nchunks % 2 == 0 else 1
    ngroups = nchunks // gsz
    h_scr = (hf_scr, hb_scr)
    ext = dv + LANES
    lrep = ln // LANES

    for d in range(2):
        if has_state:
            c_scr[d, :, 0:dv] = c0_ref[d]
            c_scr[d, :, dv:ext] = _row_to_col(n0_ref[d], dk)
            ma_scr[d] = m0_ref[d]
        else:
            c_scr[d] = jnp.zeros((dk, ext), F32)
            ma_scr[d] = jnp.zeros((1, 1), F32)

    def chunk_start(d, j):
        return pl.multiple_of((j if d == 0 else nchunks - 1 - j) * ln, ln)

    def stage_a(d, j, slot):
        t0 = chunk_start(d, j)
        q = q_ref[pl.ds(t0, ln), :]
        kt = kt_ref[:, pl.ds(t0, ln)]
        g8 = gt_ref[:, pl.ds(t0, ln)]
        cum8 = jnp.dot(g8, tri_ref[d], precision=_HIGHEST, preferred_element_type=F32)
        valid = tri_ref[1 - d] > 0.5
        li = g8[2 * d:2 * d + 1]
        lf = g8[2 * d + 1:2 * d + 2]
        cum_row = cum8[2 * d + 1:2 * d + 2]
        total = jnp.sum(lf, axis=1, keepdims=True)
        a_row = li - cum_row
        cum_col = _row_to_col(cum_row, ln)
        m_prev = ma_scr[d]
        dmat = jnp.where(valid, _lane_tile(cum_col, lrep) + a_row, NEG)
        inter = cum_col + m_prev
        m_t = jnp.maximum(inter, jnp.max(dmat, axis=1, keepdims=True))
        s = jnp.dot(q, kt, preferred_element_type=F32)
        bank, par = slot
        p_scr[d, bank, par] = (s * jnp.exp(dmat - _lane_tile(m_t, lrep))).astype(BF16)
        w_scr[d, bank, par] = jnp.exp(inter - m_t)
        em_scr[d, bank, par] = jnp.exp(-m_t)
        g_row = total + a_row
        m_new = jnp.maximum(total + m_prev, jnp.max(g_row, axis=1, keepdims=True))
        kw_scr[d, bank, par] = (kt.astype(F32) * jnp.exp(g_row - m_new)).astype(BF16)
        decay = jnp.exp(total + m_prev - m_new)
        rid = lax.broadcasted_iota(I32, (SUBLANES, LANES), 0)
        dm_scr[d, bank, par] = jnp.where(rid == 0, decay, m_new)
        ma_scr[d] = m_new

    def stage_b(d, j, slot):
        t0 = chunk_start(d, j)
        q = q_ref[pl.ds(t0, ln), :]
        v_ext = jnp.concatenate([v_ref[pl.ds(t0, ln), :], jnp.ones((ln, LANES), BF16)], axis=1)
        c_st = c_scr[d]
        bank, par = slot
        full = (jnp.dot(p_scr[d, bank, par], v_ext, preferred_element_type=F32)
                + _lane_tile(w_scr[d, bank, par], ext // LANES) * jnp.dot(q, c_st.astype(BF16),
                                                                         preferred_element_type=F32))
        inv = 1.0 / jnp.maximum(jnp.abs(full[:, dv:ext]), em_scr[d, bank, par])
        h_scr[d][pl.ds(t0, ln), :] = full[:, 0:dv] * _lane_tile(inv, dv // LANES)
        decay = dm_scr[d, bank, par, 0:1, 0:1]
        c_scr[d] = decay * c_st + jnp.dot(kw_scr[d, bank, par], v_ext, preferred_element_type=F32)

    def group_a(jj, bank):
        for par in range(gsz):
            for d in range(2):
                stage_a(d, gsz * jj + par, (bank, par))

    def group_b(jj, bank):
        for par in range(gsz):
            for d in range(2):
                stage_b(d, gsz * jj + par, (bank, par))

    group_a(0, 0)

    def body(jj, carry):
        bank = jj % 2
        group_b(jj, bank)
        group_a(jj + 1, 1 - bank)
        return carry

    lax.fori_loop(0, ngroups - 1, body, 0)
    group_b(ngroups - 1, (ngroups - 1) % 2)

    def finish(j, carry):
        t0 = pl.multiple_of(j * ln, ln)
        hs = hf_scr[pl.ds(t0, ln), :] + hb_scr[pl.ds(t0, ln), :]
        ms = jnp.mean(hs * hs, axis=1, keepdims=True)
        y = hs * lax.rsqrt(ms + EPS) * nw_ref[...]
        ya_ref[pl.ds(t0, ln), :] = (_sigmoid(o_ref[pl.ds(t0, ln), :]) * y).astype(BF16)
        return carry

    lax.fori_loop(0, nchunks, finish, 0)

    if emit_state:
        for d in range(2):
            cn_ref[d] = c_scr[d, :, 0:dv]
            nn_ref[d] = c_scr[d, :, dv:ext].T[0:1, :]
            mn_ref[d] = dm_scr[d, (ngroups - 1) % 2, gsz - 1, 1:2, 0:1]


def _mlstm_tri():
    r = np.arange(MLSTM_L)
    fwd = (r[:, None] <= r[None, :]).astype(np.float32)
    return jnp.asarray(np.stack([fwd, fwd.T]))


def _mlstm(q, kt, v, o, gt, norm_w, *, row0, n_seq, t_len, heads, dk, dv, state=None, emit_state=False):
    blk0 = row0 // t_len
    tok = lambda s, h: (blk0 + s, h)
    in_specs = [pl.BlockSpec((t_len, dk), tok), pl.BlockSpec((dk, t_len), lambda s, h: (h, blk0 + s)),
                pl.BlockSpec((t_len, dv), tok), pl.BlockSpec((t_len, dv), tok),
                pl.BlockSpec((SUBLANES, t_len), lambda s, h: (h, blk0 + s)),
                pl.BlockSpec((None, 1, dv), lambda s, h: (h, 0, 0)),
                pl.BlockSpec((2, MLSTM_L, MLSTM_L), lambda s, h: (0, 0, 0))]
    args = [q, kt, v, o, gt, norm_w.reshape(heads, 1, dv), _mlstm_tri()]
    if state is not None:
        c0, n0, m0 = state
        in_specs += [pl.BlockSpec((None, 2, None, dk, dv), lambda s, h: (s, 0, h, 0, 0)),
                     pl.BlockSpec((None, 2, None, 1, dk), lambda s, h: (s, 0, h, 0, 0)),
                     pl.BlockSpec((None, 2, None, 1, 1), lambda s, h: (s, 0, h, 0, 0))]
        args += [c0, n0.reshape(n_seq, 2, heads, 1, dk), m0.reshape(n_seq, 2, heads, 1, 1)]
    out_specs = [pl.BlockSpec((t_len, dv), lambda s, h: (s, h))]
    out_shape = [jax.ShapeDtypeStruct((n_seq * t_len, heads * dv), BF16)]
    if emit_state:
        out_specs += [pl.BlockSpec((None, 2, None, dk, dv), lambda s, h: (s, 0, h, 0, 0)),
                      pl.BlockSpec((None, 2, None, 1, dk), lambda s, h: (s, 0, h, 0, 0)),
                      pl.BlockSpec((None, 2, None, 1, 1), lambda s, h: (s, 0, h, 0, 0))]
        out_shape += [jax.ShapeDtypeStruct((n_seq, 2, heads, dk, dv), F32),
                      jax.ShapeDtypeStruct((n_seq, 2, heads, 1, dk), F32),
                      jax.ShapeDtypeStruct((n_seq, 2, heads, 1, 1), F32)]
    kern = functools.partial(_mlstm_kernel, t_len=t_len, dk=dk, dv=dv, has_state=state is not None,
                             emit_state=emit_state)
    return pl.pallas_call(
        kern,
        grid=(n_seq, heads),
        in_specs=in_specs,
        out_specs=out_specs,
        out_shape=out_shape,
        scratch_shapes=[pltpu.VMEM((t_len, dv), F32), pltpu.VMEM((t_len, dv), F32),
                        pltpu.VMEM((2, dk, dv + LANES), F32), pltpu.VMEM((2, 1, 1), F32),
                        pltpu.VMEM((2, 2, 2, MLSTM_L, MLSTM_L), BF16), pltpu.VMEM((2, 2, 2, MLSTM_L, LANES), F32),
                        pltpu.VMEM((2, 2, 2, MLSTM_L, LANES), F32), pltpu.VMEM((2, 2, 2, dk, MLSTM_L), BF16),
                        pltpu.VMEM((2, 2, 2, SUBLANES, LANES), F32)],
        compiler_params=_cparams(("arbitrary", "arbitrary")),
        name="mlstm_state" if emit_state else "mlstm",
    )(*args)


def _gelu_tanh(x):
    return x * (0.5 * (1.0 + jnp.tanh(0.7978845608028654 * (x + 0.044715 * (x * x * x)))))


def _softplus(x):
    return jnp.maximum(x, 0.0) + jnp.log1p(jnp.exp(-jnp.abs(x)))


SCAN_SPLIT = 1
SCAN_GROUP = 4


def _rglru_kernel(*refs, t_len, seg, sub, pitch, tc, has_state, emit_state):
    it = iter(refs)
    xr_ref, xg_ref, cw_ref, cb_ref, wg_ref, bg_ref, lam_ref = (next(it) for _ in range(7))
    if has_state:
        h0_ref = next(it)
    yb_ref = next(it)
    if emit_state:
        hn_ref = next(it)
    a_scr, u_scr, cin_scr = (next(it) for _ in range(3))
    nchunks = t_len // tc
    piece = min(tc, sub)
    npieces = tc // piece
    ntile = xr_ref.shape[1] // LANES
    chains = [(d, lt) for d in range(2) for lt in range(ntile)]

    def scan_rows(t0, p):
        t = t0 + p * piece
        i = t // sub
        return pl.ds(pl.multiple_of(i * pitch + (t - i * sub), SUBLANES), piece), i

    ka = (-0.5 * RGLRU_C * 1.4426950408889634) * _softplus(-lam_ref[...])

    def gates(c, carry):
        t0 = pl.multiple_of(c * tc, tc)
        pos = lax.broadcasted_iota(I32, (tc, LANES), 0) % seg
        for lt in range(ntile):
            cols = slice(lt * LANES, (lt + 1) * LANES)
            x = xr_ref[pl.ds(t0, tc), cols]
            xc = cb_ref[:, cols] + cw_ref[CONV_LEFT:CONV_LEFT + 1, cols] * x
            for j in range(cw_ref.shape[0]):
                off = j - CONV_LEFT
                if off == 0:
                    continue
                shifted = pltpu.roll(x, (-off) % tc, 0)
                ok = (pos >= -off) if off < 0 else (pos < seg - off)
                xc = xc + cw_ref[j:j + 1, cols] * jnp.where(ok, shifted, 0.0)
            zh = jnp.dot(xc.astype(BF16), wg_ref[lt], preferred_element_type=F32) + bg_ref[lt]
            hx = 0.5 * xc
            for d in range(2):
                kd = ka[d:d + 1, cols]
                a = jnp.exp2(jnp.tanh(zh[:, (2 * d) * LANES:(2 * d + 1) * LANES]) * kd + kd)
                igx = hx * jnp.tanh(zh[:, (2 * d + 1) * LANES:(2 * d + 2) * LANES]) + hx
                u = jnp.sqrt(1.0 - a * a) * igx
                for p in range(npieces):
                    rows, _ = scan_rows(t0, p)
                    a_scr[d, lt, rows, :] = a[p * piece:(p + 1) * piece]
                    u_scr[d, lt, rows, :] = u[p * piece:(p + 1) * piece]
        return carry

    lax.fori_loop(0, nchunks, gates, 0)

    seg_len = sub // SCAN_SPLIT
    seg_chains = [(d, lt, s) for (d, lt) in chains for s in range(SCAN_SPLIT)]

    def scan(jg, carry):
        def rows(d, s, k):
            j = jg * SCAN_GROUP + k
            return pl.ds(s * seg_len + (j if d == 0 else seg_len - 1 - j), SUBLANES, stride=pitch)

        loaded = [[(a_scr[d, lt, rows(d, s, k), :], u_scr[d, lt, rows(d, s, k), :]) for k in range(SCAN_GROUP)]
                  for (d, lt, s) in seg_chains]
        out = []
        for (d, lt, s), (h, p), steps in zip(seg_chains, carry, loaded):
            for k, (a, u) in enumerate(steps):
                h = a * h + u
                p = a * p
                a_scr[d, lt, rows(d, s, k), :] = p
                u_scr[d, lt, rows(d, s, k), :] = h
            out.append((h, p))
        return tuple(out)

    zero = jnp.zeros((SUBLANES, LANES), F32)
    one = jnp.ones((SUBLANES, LANES), F32)
    assert seg_len % SCAN_GROUP == 0
    ends = dict(zip(seg_chains, lax.fori_loop(0, seg_len // SCAN_GROUP, scan, tuple((zero, one) for _ in seg_chains))))

    order = [(i, s) for i in range(SUBLANES) for s in range(SCAN_SPLIT)]
    for (d, lt) in chains:
        cols = slice(lt * LANES, (lt + 1) * LANES)
        cin = h0_ref[d:d + 1, cols] if has_state else jnp.zeros((1, LANES), F32)
        for (i, s) in (order if d == 0 else reversed(order)):
            h, p = ends[(d, lt, s)]
            cin_scr[d, lt, s * SUBLANES + i:s * SUBLANES + i + 1, :] = cin
            cin = h[i:i + 1] + p[i:i + 1] * cin
        if emit_state:
            hn_ref[d:d + 1, cols] = cin

    fpiece = min(tc, seg_len)

    def finish(c, carry):
        t0 = pl.multiple_of(c * tc, tc)
        for lt in range(ntile):
            cols = slice(lt * LANES, (lt + 1) * LANES)
            for p in range(tc // fpiece):
                t = t0 + p * fpiece
                i = t // sub
                local = t - i * sub
                rows = pl.ds(pl.multiple_of(i * pitch + local, SUBLANES), fpiece)
                crow = pl.ds((local // seg_len) * SUBLANES + i, 1)
                h = (u_scr[0, lt, rows, :] + a_scr[0, lt, rows, :] * cin_scr[0, lt, crow, :]
                     + u_scr[1, lt, rows, :] + a_scr[1, lt, rows, :] * cin_scr[1, lt, crow, :])
                nat = pl.ds(pl.multiple_of(t, SUBLANES), fpiece)
                yb_ref[nat, cols] = (h * _gelu_tanh(xg_ref[nat, cols])).astype(BF16)
        return carry

    lax.fori_loop(0, nchunks, finish, 0)


def _rglru(xr, xg, conv_w, conv_b, wg, bg, lam, *, row0, n_seq, t_len, seg, state=None, emit_state=False):
    rw = xr.shape[1]
    ntile = 2
    cb = ntile * LANES
    assert rw % cb == 0
    blk0 = row0 // t_len
    sub = t_len // SUBLANES
    pitch = sub + SUBLANES
    tc = 256
    tok = lambda s, g: (blk0 + s, g)
    in_specs = [pl.BlockSpec((t_len, cb), tok), pl.BlockSpec((t_len, cb), tok),
                pl.BlockSpec((conv_w.shape[0], cb), lambda s, g: (0, g)),
                pl.BlockSpec((1, cb), lambda s, g: (0, g)),
                pl.BlockSpec((ntile, LANES, 4 * LANES), lambda s, g: (g, 0, 0)),
                pl.BlockSpec((ntile, 1, 4 * LANES), lambda s, g: (g, 0, 0)),
                pl.BlockSpec((2, cb), lambda s, g: (0, g))]
    args = [xr, xg, conv_w, conv_b.reshape(1, rw), wg, bg, lam]
    if state is not None:
        in_specs.append(pl.BlockSpec((None, 2, cb), lambda s, g: (s, 0, g)))
        args.append(state)
    out_specs = [pl.BlockSpec((t_len, cb), lambda s, g: (s, g))]
    out_shape = [jax.ShapeDtypeStruct((n_seq * t_len, rw), BF16)]
    if emit_state:
        out_specs.append(pl.BlockSpec((None, 2, cb), lambda s, g: (s, 0, g)))
        out_shape.append(jax.ShapeDtypeStruct((n_seq, 2, rw), F32))
    kern = functools.partial(_rglru_kernel, t_len=t_len, seg=seg, sub=sub, pitch=pitch, tc=tc,
                             has_state=state is not None, emit_state=emit_state)
    return pl.pallas_call(
        kern,
        grid=(n_seq, rw // cb),
        in_specs=in_specs,
        out_specs=out_specs,
        out_shape=out_shape,
        scratch_shapes=[pltpu.VMEM((2, ntile, SUBLANES * pitch, LANES), F32),
                        pltpu.VMEM((2, ntile, SUBLANES * pitch, LANES), F32),
                        pltpu.VMEM((2, ntile, SCAN_SPLIT * SUBLANES, LANES), F32)],
        compiler_params=_cparams(("arbitrary", "arbitrary")),
        name="rglru_state" if emit_state else "rglru",
    )(*args)


def _pack_bf16_pair(lo, hi):
    def rounded_bits(x):
        return pltpu.bitcast(x.astype(BF16).astype(F32), U32)
    return (rounded_bits(lo) >> 16) | rounded_bits(hi)


def _unpack_bf16_pair(w):
    lo = pltpu.bitcast(w << 16, F32).astype(BF16)
    hi = pltpu.bitcast(w & jnp.uint32(0xFFFF0000), F32).astype(BF16)
    return lo, hi


def _outproj_kernel(yac_ref, yal_ref, ybc_ref, ybl_ref, xp_ref, xs_ref, mod_ref, n2_ref, wo_ref, wr_ref, br_ref,
                    x1_ref, hp_ref, ridx_ref, rwt_ref,
                    *, nctx_tiles, ntok_tiles, tiles_per_lat, d, mw, n_groups, epg, sub_rows):
    i = pl.program_id(0)

    @pl.when(i == ntok_tiles)
    def _():
        x1_ref[...] = jnp.zeros_like(x1_ref)
        hp_ref[...] = jnp.zeros_like(hp_ref)
        ridx_ref[...] = jnp.zeros_like(ridx_ref)
        rwt_ref[...] = jnp.zeros_like(rwt_ref)

    def token_tile(x_ref, ya_ref, yb_ref):
        for r0 in range(0, x_ref.shape[0], sub_rows):
            token_rows(x_ref, ya_ref, yb_ref, r0)

    def token_rows(x_ref, ya_ref, yb_ref, r0):
        rows = slice(r0, r0 + sub_rows)
        x, ya, yb = x_ref[rows, :], ya_ref[rows, :], yb_ref[rows, :]
        row = _mod_row(i, nctx_tiles, tiles_per_lat)
        gate1 = mod_ref[pl.ds(row, 1), 2 * d:3 * d]
        shift2 = mod_ref[pl.ds(row, 1), 3 * d:4 * d]
        scale2 = mod_ref[pl.ds(row, 1), 4 * d:5 * d]
        y = (jnp.dot(ya, wo_ref[0:mw, :], preferred_element_type=F32)
             + jnp.dot(yb, wo_ref[mw:, :], preferred_element_type=F32))
        x1 = x + gate1 * y
        x1_ref[rows, :] = x1
        h2 = _modulated_norm(x1, n2_ref[...], shift2, scale2)
        half = d // 2
        packed = _pack_bf16_pair(h2[:, :half], h2[:, half:])
        trows = half // LANES
        for s in range(trows):
            hp_ref[pl.ds(r0 * trows + s, sub_rows, stride=trows), :] = packed[:, s * LANES:(s + 1) * LANES]

        lt = lax.dot_general(wr_ref[...], h2.astype(BF16), (((1,), (1,)), ((), ())),
                             preferred_element_type=F32) + br_ref[:, 0:1]
        gidx = lax.broadcasted_iota(I32, (SUBLANES, lt.shape[1]), 0)
        gl = jnp.where(gidx < n_groups, lt[0:SUBLANES], -jnp.inf)
        gmax = jnp.max(gl, axis=0, keepdims=True)
        grp = jnp.min(jnp.where(gl == gmax, gidx, n_groups), axis=0, keepdims=True)
        p_grp = 1.0 / jnp.sum(jnp.exp(gl - gmax), axis=0, keepdims=True)
        el = lt[SUBLANES:SUBLANES + epg]
        for g in range(1, n_groups):
            el = jnp.where(grp == g, lt[SUBLANES + g * epg:SUBLANES + (g + 1) * epg], el)
        eidx = lax.broadcasted_iota(I32, el.shape, 0)
        v1 = jnp.max(el, axis=0, keepdims=True)
        i1 = jnp.min(jnp.where(el == v1, eidx, epg), axis=0, keepdims=True)
        el2 = jnp.where(eidx == i1, -jnp.inf, el)
        v2 = jnp.max(el2, axis=0, keepdims=True)
        i2 = jnp.min(jnp.where(el2 == v2, eidx, epg), axis=0, keepdims=True)
        e2 = jnp.exp(v2 - v1)
        w1 = p_grp / (1.0 + e2)
        w2 = p_grp * e2 / (1.0 + e2)
        rid = lax.broadcasted_iota(I32, (SUBLANES, sub_rows), 0)
        ridx_ref[:, rows] = jnp.where(rid == 0, grp * epg + i1, jnp.where(rid == 1, grp * epg + i2, 0))
        rwt_ref[:, rows] = jnp.where(rid == 0, w1, jnp.where(rid == 1, w2, 0.0))

    @pl.when(i < nctx_tiles)
    def _():
        token_tile(xp_ref, yac_ref, ybc_ref)

    @pl.when((i >= nctx_tiles) & (i < ntok_tiles))
    def _():
        token_tile(xs_ref, yal_ref, ybl_ref)


def _out_proj(ya_c, ya_l, yb_c, yb_l, xp, xs, mod, norm2_w, w_out, wr, br, *, lat_seq, n_groups, epg):
    nc, d = xp.shape
    nl = xs.shape[0]
    nt = nc + nl
    mw = ya_c.shape[1]
    tm, sub_rows = 512, 256
    trows = (d // 2) // LANES
    nctx_tiles, ntok_tiles = nc // tm, nt // tm
    kern = functools.partial(_outproj_kernel, nctx_tiles=nctx_tiles, ntok_tiles=ntok_tiles,
                             tiles_per_lat=lat_seq // tm, d=d, mw=mw, n_groups=n_groups, epg=epg, sub_rows=sub_rows)
    ctx = lambda i: (jnp.minimum(i, nctx_tiles - 1), 0)
    lat = lambda i: (jnp.clip(i - nctx_tiles, 0, nl // tm - 1), 0)
    row = lambda i: (i, 0)
    const = lambda i: (0, 0)
    return pl.pallas_call(
        kern,
        grid=(ntok_tiles + 1,),
        in_specs=[pl.BlockSpec((tm, mw), ctx), pl.BlockSpec((tm, mw), lat),
                  pl.BlockSpec((tm, yb_c.shape[1]), ctx), pl.BlockSpec((tm, yb_c.shape[1]), lat),
                  pl.BlockSpec((tm, d), ctx), pl.BlockSpec((tm, d), lat),
                  pl.BlockSpec(mod.shape, const),
                  pl.BlockSpec((1, d), const),
                  pl.BlockSpec(w_out.shape, const, pipeline_mode=pl.Buffered(1)),
                  pl.BlockSpec(wr.shape, const),
                  pl.BlockSpec(br.shape, const)],
        out_specs=[pl.BlockSpec((tm, d), row), pl.BlockSpec((tm * trows, LANES), row),
                   pl.BlockSpec((SUBLANES, tm), lambda i: (0, i)), pl.BlockSpec((SUBLANES, tm), lambda i: (0, i))],
        out_shape=[jax.ShapeDtypeStruct((nt + tm, d), F32), jax.ShapeDtypeStruct(((nt + tm) * trows, LANES), U32),
                   jax.ShapeDtypeStruct((SUBLANES, nt + tm), I32), jax.ShapeDtypeStruct((SUBLANES, nt + tm), F32)],
        compiler_params=_cparams(("arbitrary",)),
        name="out_proj",
    )(ya_c, ya_l, yb_c, yb_l, xp, xs, mod, norm2_w.reshape(1, d), w_out, wr, br)


MOE_ROWS = 256


ROUTE_TILE = 512


def _rank_kernel(ridx_ref, tri_ref, rank_ref, cnt_ref, carry_scr, *, n_exp):
    @pl.when(pl.program_id(0) == 0)
    def _():
        carry_scr[...] = jnp.zeros_like(carry_scr)

    e = ridx_ref[...]
    tr = e.shape[1]
    eid = lax.broadcasted_iota(I32, (n_exp, tr), 0)
    carry = carry_scr[:, 0:1]
    ranks = []
    for kk in range(TOP_K):
        hit = eid == e[kk:kk + 1]
        cum = jnp.dot(jnp.where(hit, 1.0, 0.0).astype(BF16), tri_ref[...], preferred_element_type=F32)
        ranks.append(jnp.sum(jnp.where(hit, cum + carry, 0.0), axis=0, keepdims=True) - 1.0)
        carry = carry + cum[:, tr - 1:tr]
    carry_scr[...] = jnp.broadcast_to(carry, carry_scr.shape)
    cnt_ref[...] = jnp.broadcast_to(carry, cnt_ref.shape)
    rid = lax.broadcasted_iota(I32, rank_ref.shape, 0)
    rank_ref[...] = jnp.where(rid == 0, ranks[0], jnp.where(rid == 1, ranks[1], 0.0)).astype(I32)


def _dest_kernel(ridx_ref, rank_ref, pstart_ref, dest_ref, *, n_exp):
    e = ridx_ref[...]
    tr = e.shape[1]
    eid = lax.broadcasted_iota(I32, (n_exp, tr), 0)
    ps = pstart_ref[:, 0:1]
    rows = [jnp.sum(jnp.where(eid == e[kk:kk + 1], ps, 0.0), axis=0, keepdims=True) for kk in range(TOP_K)]
    rid = lax.broadcasted_iota(I32, dest_ref.shape, 0)
    dest_ref[...] = rank_ref[...] + jnp.where(rid == 0, rows[0], jnp.where(rid == 1, rows[1], 0.0)).astype(I32)


def _invert_kernel(dest_ref, fill_ref, rowtok_ref, sem):
    i = pl.program_id(0)
    tr = dest_ref.shape[1]

    @pl.when(i == 0)
    def _():
        fill = pltpu.make_async_copy(fill_ref, rowtok_ref, sem.at[0])
        fill.start()
        fill.wait()

    def body(r, c):
        for kk in range(TOP_K):
            rowtok_ref[dest_ref[kk, r]] = i * tr + r
        return c
    lax.fori_loop(0, tr, body, 0, unroll=16)


def _routing(ridx, n_tok, n_exp, n_blocks, zero_row):
    tr = ROUTE_TILE
    steps = n_tok // tr
    tri = jnp.asarray(np.triu(np.ones((tr, tr), np.float32)), BF16)
    tile = pl.BlockSpec((SUBLANES, tr), lambda i: (0, i))
    cnt_spec = pl.BlockSpec((n_exp, LANES), lambda i: (0, 0))
    rank, cnt = pl.pallas_call(
        functools.partial(_rank_kernel, n_exp=n_exp),
        grid=(steps,),
        in_specs=[tile, pl.BlockSpec((tr, tr), lambda i: (0, 0))],
        out_specs=[tile, cnt_spec],
        out_shape=[jax.ShapeDtypeStruct((SUBLANES, n_tok), I32), jax.ShapeDtypeStruct((n_exp, LANES), F32)],
        scratch_shapes=[pltpu.VMEM((n_exp, LANES), F32)],
        compiler_params=_cparams(("arbitrary",)),
        name="route_rank",
    )(ridx, tri)
    counts = cnt[:, 0].astype(I32)
    padded = (counts + MOE_ROWS - 1) // MOE_ROWS * MOE_ROWS
    pad_end = jnp.cumsum(padded)
    pad_start = pad_end - padded
    dest = pl.pallas_call(
        functools.partial(_dest_kernel, n_exp=n_exp),
        grid=(steps,),
        in_specs=[tile, tile, cnt_spec],
        out_specs=tile,
        out_shape=jax.ShapeDtypeStruct((SUBLANES, n_tok), I32),
        compiler_params=_cparams(("arbitrary",)),
        name="route_dest",
    )(ridx, rank, jnp.broadcast_to(pad_start.astype(F32)[:, None], (n_exp, LANES)))
    row_tok = pl.pallas_call(
        _invert_kernel,
        grid=(steps,),
        in_specs=[pl.BlockSpec((SUBLANES, tr), lambda i: (0, i), memory_space=pltpu.SMEM),
                  pl.BlockSpec(memory_space=pl.ANY)],
        out_specs=pl.BlockSpec(memory_space=pltpu.SMEM),
        out_shape=jax.ShapeDtypeStruct((n_blocks * MOE_ROWS,), I32),
        scratch_shapes=[pltpu.SemaphoreType.DMA((1,))],
        compiler_params=_cparams(("arbitrary",)),
        name="route_invert",
    )(dest, jnp.full((n_blocks * MOE_ROWS,), zero_row, I32))
    blk_row0 = jnp.arange(n_blocks, dtype=I32) * MOE_ROWS
    blk_e = jnp.minimum(jnp.sum((pad_end[None, :] <= blk_row0[:, None]).astype(I32), axis=1), n_exp - 1)
    nused = (pad_end[-1:] // MOE_ROWS).astype(I32)
    return dest, row_tok, blk_e, nused


CAST_ROWS = 256
WEIGHT_DMA_SPLIT = 8


GATHER_DEPTH = 3


def _expert_kernel(blk_e_ref, nused_ref, tok_ref, nx1_ref, nx2_ref, src_ref, wg_hbm, wu_hbm, wd_hbm, y_ref,
                   xbuf, xsem, stage_g, stage_u, stage_d, bf_g, bf_u, bf_d, wsem, *, half, rows, trows):
    b = pl.program_id(0)
    nused = nused_ref[0]
    w_hbm = (wg_hbm, wu_hbm, wd_hbm)
    stage = (stage_g, stage_u, stage_d)
    wbf = (bf_g, bf_u, bf_d)

    def weight_copy(e, j):
        return pltpu.make_async_copy(w_hbm[j].at[e], stage[j], wsem.at[j])

    def start_weights(e, j):
        step = stage[j].shape[0] // WEIGHT_DMA_SPLIT
        for c in range(WEIGHT_DMA_SPLIT):
            sl = pl.ds(c * step, step)
            pltpu.make_async_copy(w_hbm[j].at[e, sl, :], stage[j].at[sl, :], wsem.at[j]).start(priority=1)

    def row_copy(tok, slot, r):
        src = src_ref.at[pl.ds(pl.multiple_of(tok * trows, trows), trows), :]
        return pltpu.make_async_copy(src, xbuf.at[slot, pl.ds(r * trows, trows), :], xsem.at[slot])

    @pl.when(b >= nused)
    def _():
        y_ref[...] = jnp.zeros_like(y_ref)

    @pl.when(b < nused)
    def _():
        e = blk_e_ref[b]

        @pl.when(b == 0)
        def _():
            for j in range(3):
                start_weights(e, j)
            def body(r, c):
                row_copy(tok_ref[0, r], 0, r).start()
                return c
            lax.fori_loop(0, rows, body, 0, unroll=8)

        @pl.when((b == 0) & (nused > 1))
        def _():
            def body(r, c):
                row_copy(nx1_ref[0, r], 1, r).start()
                return c
            lax.fori_loop(0, rows, body, 0, unroll=8)

        @pl.when((b == 0) | (blk_e_ref[jnp.maximum(b - 1, 0)] != e))
        def _():
            nb = lax.while_loop(lambda k: (k < nused) & (blk_e_ref[jnp.minimum(k, nused - 1)] == e),
                                lambda k: k + 1, b + 1)
            for j in range(3):
                weight_copy(e, j).wait()
                n_steps = stage[j].shape[0] // CAST_ROWS

                def cast(c, carry, j=j):
                    sl = pl.ds(pl.multiple_of(c * CAST_ROWS, CAST_ROWS), CAST_ROWS)
                    wbf[j][sl, :] = stage[j][sl, :].astype(BF16)
                    return carry
                lax.fori_loop(0, n_steps, cast, 0)

                @pl.when(nb < nused)
                def _(j=j):
                    start_weights(blk_e_ref[jnp.minimum(nb, nused - 1)], j)

        slot = b % GATHER_DEPTH
        pltpu.make_async_copy(src_ref.at[pl.ds(0, rows * trows), :], xbuf.at[slot], xsem.at[slot]).wait()

        def compute(prefetch):
            parts = [_unpack_bf16_pair(xbuf[slot, pl.ds(s, rows, stride=trows), :]) for s in range(trows)]
            lo = jnp.concatenate([p[0] for p in parts], axis=1)
            hi = jnp.concatenate([p[1] for p in parts], axis=1)
            if prefetch:
                ahead = (b + GATHER_DEPTH - 1) % GATHER_DEPTH
                for r in range(rows):
                    row_copy(nx2_ref[0, r], ahead, r).start()

            def up(w_ref):
                return (jnp.dot(lo, w_ref[0:half, :], preferred_element_type=F32)
                        + jnp.dot(hi, w_ref[half:, :], preferred_element_type=F32))

            g = up(wbf[0])
            h = ((g * _sigmoid(g)) * up(wbf[1])).astype(BF16)
            y = jnp.dot(h, wbf[2][...], preferred_element_type=F32)
            packed = _pack_bf16_pair(y[:, :half], y[:, half:])
            for s in range(trows):
                y_ref[pl.ds(s, rows, stride=trows), :] = packed[:, s * LANES:(s + 1) * LANES]

        @pl.when(b + GATHER_DEPTH - 1 < nused)
        def _():
            compute(True)

        @pl.when(b + GATHER_DEPTH - 1 >= nused)
        def _():
            compute(False)


def _experts(blk_e, nused, row_tok, src, wg, wu, wd, n_blocks):
    n_exp, d, ff = wg.shape
    any_spec = pl.BlockSpec(memory_space=pl.ANY)
    trows = (d // 2) // LANES
    tok = row_tok.reshape(n_blocks, 1, MOE_ROWS)
    return pl.pallas_call(
        functools.partial(_expert_kernel, half=d // 2, rows=MOE_ROWS, trows=trows),
        grid_spec=pltpu.PrefetchScalarGridSpec(
            num_scalar_prefetch=2,
            grid=(n_blocks,),
            in_specs=[pl.BlockSpec((None, 1, MOE_ROWS), lambda b, be, nu: (b, 0, 0), memory_space=pltpu.SMEM),
                      pl.BlockSpec((None, 1, MOE_ROWS), lambda b, be, nu: (jnp.minimum(b + 1, n_blocks - 1), 0, 0),
                                   memory_space=pltpu.SMEM),
                      pl.BlockSpec((None, 1, MOE_ROWS), lambda b, be, nu: (jnp.minimum(b + 2, n_blocks - 1), 0, 0),
                                   memory_space=pltpu.SMEM),
                      any_spec, any_spec, any_spec, any_spec],
            out_specs=pl.BlockSpec((MOE_ROWS * trows, LANES), lambda b, be, nu: (b, 0)),
            scratch_shapes=[pltpu.VMEM((GATHER_DEPTH, MOE_ROWS * trows, LANES), U32),
                            pltpu.SemaphoreType.DMA((GATHER_DEPTH,)),
                            pltpu.VMEM((d, ff), F32), pltpu.VMEM((d, ff), F32), pltpu.VMEM((ff, d), F32),
                            pltpu.VMEM((d, ff), BF16), pltpu.VMEM((d, ff), BF16), pltpu.VMEM((ff, d), BF16),
                            pltpu.SemaphoreType.DMA((3,))]),
        out_shape=jax.ShapeDtypeStruct((n_blocks * MOE_ROWS * trows, LANES), U32),
        compiler_params=_cparams(("arbitrary",)),
        name="experts",
    )(blk_e, nused, tok, tok, tok, src, wg, wu, wd)


def _combine_kernel(dest_ref, x1_ref, wt_ref, mod_ref, fw_ref, y_ref, o_ref, ybuf_even, ybuf_odd, sem,
                    *, tile0, n_all, tiles_per_seq, d, lat):
    i = pl.program_id(0)
    last = pl.num_programs(0) - 1
    tm = x1_ref.shape[0]
    bufs = (ybuf_even, ybuf_odd)

    trows = (d // 2) // LANES

    def row_copy(tile, par, kk, r):
        row = dest_ref[kk * n_all + (tile0 + tile) * tm + r]
        src = y_ref.at[pl.ds(pl.multiple_of(row * trows, trows), trows), :]
        return pltpu.make_async_copy(src, bufs[par].at[kk, pl.ds(r * trows, trows), :], sem.at[par])

    @pl.when(i == 0)
    def _():
        def body(r, c):
            for kk in range(TOP_K):
                row_copy(0, 0, kk, r).start()
            return c
        lax.fori_loop(0, tm, body, 0, unroll=8)

    def step(par, prefetch):
        for kk in range(TOP_K):
            pltpu.make_async_copy(y_ref.at[pl.ds(0, tm * trows), :], bufs[par].at[kk], sem.at[par]).wait()
        if prefetch:
            for r in range(tm):
                for kk in range(TOP_K):
                    row_copy(i + 1, 1 - par, kk, r).start(priority=kk)
        row = (1 + i // tiles_per_seq) if lat else 0
        gate2 = mod_ref[pl.ds(row, 1), 5 * d:6 * d]
        wt = wt_ref[...]
        half = d // 2
        w0 = _row_to_col(wt[0:1], tm)
        w1 = _row_to_col(wt[1:2], tm)
        unpack = (lambda p: pltpu.bitcast(p << 16, F32),
                  lambda p: pltpu.bitcast(p & jnp.uint32(0xFFFF0000), F32))
        ssq = jnp.zeros((tm, 1), F32)
        for s in range(trows):
            p0 = bufs[par][0, pl.ds(s, tm, stride=trows), :]
            p1 = bufs[par][1, pl.ds(s, tm, stride=trows), :]
            for side in range(2):
                cols = slice(side * half + s * LANES, side * half + (s + 1) * LANES)
                x = x1_ref[:, cols] + gate2[:, cols] * (w0 * unpack[side](p0) + w1 * unpack[side](p1))
                ssq = ssq + jnp.sum(x * x, axis=-1, keepdims=True)
                o_ref[:, cols] = x
        scale = lax.rsqrt(ssq * (1.0 / d) + EPS)
        o_ref[...] = o_ref[...] * scale * fw_ref[...]

    for par in range(2):
        @pl.when((i % 2 == par) & (i < last))
        def _(par=par):
            step(par, True)

        @pl.when((i % 2 == par) & (i == last))
        def _(par=par):
            step(par, False)


def _combine(dest_flat, x1, rwt, mod, final_w, y_rows, *, row0, n_tok, seq_len, lat):
    d = x1.shape[1]
    tm = 256
    tile0 = row0 // tm
    trows = (d // 2) // LANES
    n_all = dest_flat.shape[0] // TOP_K
    kern = functools.partial(_combine_kernel, tile0=tile0, n_all=n_all, tiles_per_seq=seq_len // tm, d=d, lat=lat)
    return pl.pallas_call(
        kern,
        grid_spec=pltpu.PrefetchScalarGridSpec(
            num_scalar_prefetch=1,
            grid=(n_tok // tm,),
            in_specs=[pl.BlockSpec((tm, d), lambda i, dr: (tile0 + i, 0)),
                      pl.BlockSpec((SUBLANES, tm), lambda i, dr: (0, tile0 + i)),
                      pl.BlockSpec(mod.shape, lambda i, dr: (0, 0)),
                      pl.BlockSpec((1, d), lambda i, dr: (0, 0)),
                      pl.BlockSpec(memory_space=pl.ANY)],
            out_specs=pl.BlockSpec((tm, d), lambda i, dr: (i, 0)),
            scratch_shapes=[pltpu.VMEM((TOP_K, tm * trows, LANES), U32), pltpu.VMEM((TOP_K, tm * trows, LANES), U32),
                            pltpu.SemaphoreType.DMA((2,))]),
        out_shape=jax.ShapeDtypeStruct((n_tok, d), F32),
        compiler_params=_cparams(("arbitrary",)),
        name="combine_lat" if lat else "combine_ctx",
    )(dest_flat, x1, rwt, mod, final_w.reshape(1, d), y_rows)


def _gate_layout(w_gates, b_gates, heads):
    d = w_gates.shape[0]
    w = w_gates.reshape(d, 4, heads).transpose(0, 2, 1)
    w = jnp.pad(w, ((0, 0), (0, 0), (0, SUBLANES - 4))).reshape(d, heads * SUBLANES)
    b = b_gates.reshape(4, heads).T
    b = jnp.pad(b, ((0, 0), (0, SUBLANES - 4))).reshape(1, heads * SUBLANES)
    padl = LANES - heads * SUBLANES
    return jnp.pad(w, ((0, 0), (0, padl))), jnp.pad(b, ((0, 0), (0, padl)))


def kernel(x_prompt, x_sample, state_mlstm_c, state_mlstm_n, state_mlstm_m, state_rglru_h, c, c_ctx, w_ada, b_ada,
           norm1_w, w_in, b_gates, conv_w, conv_b, rg_wa, rg_ba, rg_wx, rg_bx, rg_lambda, mlstm_norm_w, w_out,
           norm2_w, router_group_w, router_group_b, router_expert_w, router_expert_b, expert_w_gate, expert_w_up,
           expert_w_down, final_norm_w):
    n_req, seq, d = x_prompt.shape
    n_lat, lat_seq, _ = x_sample.shape
    depth = w_in.shape[0]
    assert depth == 1, "the token-axis plumbing below is written for the single-layer trunk"
    heads, dk, dv = state_mlstm_c.shape[3:]
    rw = state_rglru_h.shape[-1]
    nblk = rg_wa.shape[2]
    assert rw // nblk == LANES
    n_groups, epg = router_expert_w.shape[1], router_expert_w.shape[3]
    n_exp = n_groups * epg
    qk, mw = heads * dk, heads * dv
    nc, nl = n_req * seq, n_lat * lat_seq
    nt = nc + nl
    assert nc % lat_seq == 0 and n_lat + 1 <= SUBLANES
    l = 0

    xp = x_prompt.reshape(nc, d)
    xs = x_sample.reshape(nl, d)
    cvec = jnp.zeros((SUBLANES, d), F32).at[0].set(c_ctx).at[1:1 + n_lat].set(c)
    mod = _ada(cvec, w_ada[l], b_ada[l])

    w = w_in[l]
    g0 = 2 * qk + 2 * mw
    wgate, bgate = _gate_layout(w[:, g0:g0 + 4 * heads], b_gates[l], heads)
    w_cat = jnp.concatenate([w[:, :qk], w[:, 2 * qk:g0], wgate, w[:, g0 + 4 * heads:]], axis=1).astype(BF16)
    w_kt = w[:, qk:2 * qk].T.astype(BF16)
    q, kt, v, o, gt, xr, xg = _in_proj(xp, xs, mod, norm1_w[l], w_cat, w_kt, bgate, lat_seq=lat_seq, heads=heads,
                                       dk=dk, dv=dv, rw=rw)

    mkw = dict(heads=heads, dk=dk, dv=dv)
    ya_c, new_c, new_n, new_m = _mlstm(q, kt, v, o, gt, mlstm_norm_w[l], row0=0, n_seq=n_req, t_len=seq,
                                       emit_state=True, **mkw)
    (ya_l,) = _mlstm(q, kt, v, o, gt, mlstm_norm_w[l], row0=nc, n_seq=n_lat, t_len=lat_seq,
                     state=(state_mlstm_c[:, l], state_mlstm_n[:, l], state_mlstm_m[:, l]), **mkw)

    wg = (0.5 * jnp.concatenate([rg_wa[l, 0], rg_wx[l, 0], rg_wa[l, 1], rg_wx[l, 1]], axis=-1)).astype(BF16)
    bg = 0.5 * jnp.concatenate([rg_ba[l, 0].reshape(nblk, 1, LANES), rg_bx[l, 0].reshape(nblk, 1, LANES),
                                rg_ba[l, 1].reshape(nblk, 1, LANES), rg_bx[l, 1].reshape(nblk, 1, LANES)], axis=-1)
    rargs = (xr, xg, conv_w[l], conv_b[l], wg, bg, rg_lambda[l])
    yb_c, new_h = _rglru(*rargs, row0=0, n_seq=n_req, t_len=seq, seg=seq, emit_state=True)
    (yb_l,) = _rglru(*rargs, row0=nc, n_seq=n_lat, t_len=lat_seq, seg=GRID_W, state=state_rglru_h[:, l])

    r_rows = -(-(SUBLANES + n_exp) // 16) * 16
    wr = jnp.zeros((r_rows, d), F32)
    wr = wr.at[0:n_groups].set(router_group_w[l].T)
    wr = wr.at[SUBLANES:SUBLANES + n_exp].set(router_expert_w[l].transpose(0, 2, 1).reshape(n_exp, d)).astype(BF16)
    br = jnp.zeros((r_rows, LANES), F32)
    br = br.at[0:n_groups, 0].set(router_group_b[l])
    br = br.at[SUBLANES:SUBLANES + n_exp, 0].set(router_expert_b[l].reshape(n_exp))
    x1, hp, ridx, rwt = _out_proj(ya_c, ya_l, yb_c, yb_l, xp, xs, mod, norm2_w[l], w_out[l].astype(BF16), wr, br,
                                  lat_seq=lat_seq, n_groups=n_groups, epg=epg)

    n_blocks = (nt * TOP_K) // MOE_ROWS + n_exp
    dest, row_tok, blk_e, nused = _routing(ridx, nt, n_exp, n_blocks, zero_row=nt)
    only_layer = lambda a: a.reshape(a.shape[1:])
    y_rows = _experts(blk_e, nused, row_tok, hp, only_layer(expert_w_gate), only_layer(expert_w_up),
                      only_layer(expert_w_down), n_blocks)

    dest_flat = dest[0:TOP_K].reshape(-1)
    y_prompt = _combine(dest_flat, x1, rwt, mod, final_norm_w, y_rows, row0=0, n_tok=nc, seq_len=seq, lat=False)
    y_sample = _combine(dest_flat, x1, rwt, mod, final_norm_w, y_rows, row0=nc, n_tok=nl, seq_len=lat_seq,
                        lat=True)

    return (y_prompt.reshape(n_req, seq, d), y_sample.reshape(n_lat, lat_seq, d),
            new_c[:, None], new_n.reshape(n_req, 1, 2, heads, dk), new_m.reshape(n_req, 1, 2, heads),
            new_h[:, None])
```

```python
import functools

import jax
import jax.numpy as jnp
import numpy as np
from jax import lax
from jax.experimental import pallas as pl
from jax.experimental.pallas import tpu as pltpu

F32 = jnp.float32
BF16 = jnp.bfloat16
I32 = jnp.int32
U32 = jnp.uint32

EPS = 1e-6
GRID_W = 64
CONV_LEFT = 2
RGLRU_C = 8.0
TOP_K = 2
LANES = 128
SUBLANES = 8
MLSTM_L = 256
NEG = -1e30
VMEM_LIMIT = 56 * 1024 * 1024

_HIGHEST = lax.Precision.HIGHEST


def _cparams(sem, vmem=VMEM_LIMIT):
    return pltpu.CompilerParams(dimension_semantics=sem, vmem_limit_bytes=vmem)


def _sigmoid(x):
    return 0.5 * jnp.tanh(0.5 * x) + 0.5


def _row_to_col(r, n):
    return jnp.broadcast_to(r, (LANES, n)).T


def _lane_tile(x, reps):
    return x if reps == 1 else jnp.concatenate([x] * reps, axis=1)


def _ada_kernel(c_ref, w_ref, b_ref, o_ref):
    c = c_ref[...]
    s = (c * _sigmoid(c)).astype(BF16)
    o_ref[...] = jnp.dot(s, w_ref[...].astype(BF16), preferred_element_type=F32) + b_ref[...]


def _ada(cvec, w_ada, b_ada):
    d, n = w_ada.shape
    tn = 1024 if n % 1024 == 0 else 512
    assert n % tn == 0
    return pl.pallas_call(
        _ada_kernel,
        grid=(n // tn,),
        in_specs=[pl.BlockSpec((SUBLANES, d), lambda j: (0, 0)),
                  pl.BlockSpec((d, tn), lambda j: (0, j)),
                  pl.BlockSpec((1, tn), lambda j: (0, j))],
        out_specs=pl.BlockSpec((SUBLANES, tn), lambda j: (0, j)),
        out_shape=jax.ShapeDtypeStruct((SUBLANES, n), F32),
        compiler_params=_cparams(("arbitrary",)),
        name="ada",
    )(cvec, w_ada, b_ada.reshape(1, n))


def _modulated_norm(x, w, shift, scale):
    ms = jnp.mean(x * x, axis=-1, keepdims=True)
    return (x * lax.rsqrt(ms + EPS) * w) * (1.0 + scale) + shift


def _mod_row(i, nctx_tiles, tiles_per_lat):
    return jnp.where(i < nctx_tiles, 0, 1 + (i - nctx_tiles) // tiles_per_lat)


def _inproj_kernel(xp_ref, xs_ref, mod_ref, n1_ref, w_ref, wkt_ref, bg_ref,
                   q_ref, kt_ref, v_ref, o_ref, gt_ref, xr_ref, xg_ref,
                   *, nctx_tiles, tiles_per_lat, d, qk, mw, rw, gh, qscale):
    i = pl.program_id(0)
    x = jnp.where(i < nctx_tiles, xp_ref[...], xs_ref[...])
    row = _mod_row(i, nctx_tiles, tiles_per_lat)
    shift = mod_ref[pl.ds(row, 1), 0:d]
    scale = mod_ref[pl.ds(row, 1), d:2 * d]
    hb = _modulated_norm(x, n1_ref[...], shift, scale).astype(BF16)

    def proj(c0, width):
        return jnp.dot(hb, w_ref[:, c0:c0 + width], preferred_element_type=F32)

    c0 = 0
    q_ref[...] = (proj(c0, qk) * qscale).astype(BF16); c0 += qk
    kt_ref[...] = lax.dot_general(wkt_ref[...], hb, (((1,), (1,)), ((), ())),
                                  preferred_element_type=F32).astype(BF16)
    v_ref[...] = proj(c0, mw).astype(BF16); c0 += mw
    o_ref[...] = proj(c0, mw); c0 += mw
    zg = proj(c0, LANES) + bg_ref[...]; c0 += LANES
    lane = lax.broadcasted_iota(I32, zg.shape, 1)
    log_sig = jnp.minimum(zg, 0.0) - jnp.log1p(jnp.exp(-jnp.abs(zg)))
    zg = jnp.where(lane % 2 == 1, log_sig, zg)
    gt_ref[...] = zg.T[0:gh, :]
    xr_ref[...] = proj(c0, rw); c0 += rw
    xg_ref[...] = proj(c0, rw)


def _in_proj(xp, xs, mod, norm1_w, w_cat, w_kt, bg, *, lat_seq, heads, dk, dv, rw):
    nc, d = xp.shape
    nl = xs.shape[0]
    nt = nc + nl
    tm = 256
    qk, mw, gh = heads * dk, heads * dv, heads * SUBLANES
    nctx_tiles = nc // tm
    kern = functools.partial(_inproj_kernel, nctx_tiles=nctx_tiles, tiles_per_lat=lat_seq // tm, d=d, qk=qk, mw=mw,
                             rw=rw, gh=gh, qscale=dk ** -0.5)
    row = lambda i: (i, 0)
    const = lambda i: (0, 0)
    return pl.pallas_call(
        kern,
        grid=(nt // tm,),
        in_specs=[pl.BlockSpec((tm, d), lambda i: (jnp.minimum(i, nctx_tiles - 1), 0)),
                  pl.BlockSpec((tm, d), lambda i: (jnp.maximum(i - nctx_tiles, 0), 0)),
                  pl.BlockSpec(mod.shape, const),
                  pl.BlockSpec((1, d), const),
                  pl.BlockSpec(w_cat.shape, const, pipeline_mode=pl.Buffered(1)),
                  pl.BlockSpec(w_kt.shape, const, pipeline_mode=pl.Buffered(1)),
                  pl.BlockSpec((1, LANES), const)],
        out_specs=[pl.BlockSpec((tm, qk), row), pl.BlockSpec((qk, tm), lambda i: (0, i)), pl.BlockSpec((tm, mw), row),
                   pl.BlockSpec((tm, mw), row), pl.BlockSpec((gh, tm), lambda i: (0, i)),
                   pl.BlockSpec((tm, rw), row), pl.BlockSpec((tm, rw), row)],
        out_shape=[jax.ShapeDtypeStruct((nt, qk), BF16), jax.ShapeDtypeStruct((qk, nt), BF16),
                   jax.ShapeDtypeStruct((nt, mw), BF16), jax.ShapeDtypeStruct((nt, mw), F32),
                   jax.ShapeDtypeStruct((gh, nt), F32),
                   jax.ShapeDtypeStruct((nt, rw), F32), jax.ShapeDtypeStruct((nt, rw), F32)],
        compiler_params=_cparams(("arbitrary",)),
        name="in_proj",
    )(xp, xs, mod, norm1_w.reshape(1, d), w_cat, w_kt, bg)


def _mlstm_kernel(*refs, t_len, dk, dv, has_state, emit_state):
    it = iter(refs)
    q_ref, kt_ref, v_ref, o_ref, gt_ref, nw_ref, tri_ref = (next(it) for _ in range(7))
    if has_state:
        c0_ref, n0_ref, m0_ref = (next(it) for _ in range(3))
    ya_ref = next(it)
    if emit_state:
        cn_ref, nn_ref, mn_ref = (next(it) for _ in range(3))
    hf_scr, hb_scr, c_scr, ma_scr, p_scr, w_scr, em_scr, kw_scr, dm_scr = (next(it) for _ in range(9))
    ln = MLSTM_L
    nchunks = t_len // ln
    assert ln % LANES == 0 and dk == LANES
    gsz = 2 if nchunks % 2 == 0 else 1
    ngroups = nchunks // gsz
    h_scr = (hf_scr, hb_scr)
    ext = dv + LANES
    lrep = ln // LANES

    for d in range(2):
        if has_state:
            c_scr[d, :, 0:dv] = c0_ref[d]
            c_scr[d, :, dv:ext] = _row_to_col(n0_ref[d], dk)
            ma_scr[d] = m0_ref[d]
        else:
            c_scr[d] = jnp.zeros((dk, ext), F32)
            ma_scr[d] = jnp.zeros((1, 1), F32)

    def chunk_start(d, j):
        return pl.multiple_of((j if d == 0 else nchunks - 1 - j) * ln, ln)

    def stage_a(d, j, slot):
        t0 = chunk_start(d, j)
        q = q_ref[pl.ds(t0, ln), :]
        kt = kt_ref[:, pl.ds(t0, ln)]
        g8 = gt_ref[:, pl.ds(t0, ln)]
        cum8 = jnp.dot(g8, tri_ref[d], precision=_HIGHEST, preferred_element_type=F32)
        valid = tri_ref[1 - d] > 0.5
        li = g8[2 * d:2 * d + 1]
        lf = g8[2 * d + 1:2 * d + 2]
        cum_row = cum8[2 * d + 1:2 * d + 2]
        total = jnp.sum(lf, axis=1, keepdims=True)
        a_row = li - cum_row
        cum_col = _row_to_col(cum_row, ln)
        m_prev = ma_scr[d]
        dmat = jnp.where(valid, _lane_tile(cum_col, lrep) + a_row, NEG)
        inter = cum_col + m_prev
        m_t = jnp.maximum(inter, jnp.max(dmat, axis=1, keepdims=True))
        s = jnp.dot(q, kt, preferred_element_type=F32)
        bank, par = slot
        p_scr[d, bank, par] = (s * jnp.exp(dmat - _lane_tile(m_t, lrep))).astype(BF16)
        w_scr[d, bank, par] = jnp.exp(inter - m_t)
        em_scr[d, bank, par] = jnp.exp(-m_t)
        g_row = total + a_row
        m_new = jnp.maximum(total + m_prev, jnp.max(g_row, axis=1, keepdims=True))
        kw_scr[d, bank, par] = (kt.astype(F32) * jnp.exp(g_row - m_new)).astype(BF16)
        decay = jnp.exp(total + m_prev - m_new)
        rid = lax.broadcasted_iota(I32, (SUBLANES, LANES), 0)
        dm_scr[d, bank, par] = jnp.where(rid == 0, decay, m_new)
        ma_scr[d] = m_new

    def stage_b(d, j, slot):
        t0 = chunk_start(d, j)
        q = q_ref[pl.ds(t0, ln), :]
        v_ext = jnp.concatenate([v_ref[pl.ds(t0, ln), :], jnp.ones((ln, LANES), BF16)], axis=1)
        c_st = c_scr[d]
        bank, par = slot
        full = (jnp.dot(p_scr[d, bank, par], v_ext, preferred_element_type=F32)
                + _lane_tile(w_scr[d, bank, par], ext // LANES) * jnp.dot(q, c_st.astype(BF16),
                                                                         preferred_element_type=F32))
        inv = 1.0 / jnp.maximum(jnp.abs(full[:, dv:ext]), em_scr[d, bank, par])
        h_scr[d][pl.ds(t0, ln), :] = full[:, 0:dv] * _lane_tile(inv, dv // LANES)
        decay = dm_scr[d, bank, par, 0:1, 0:1]
        c_scr[d] = decay * c_st + jnp.dot(kw_scr[d, bank, par], v_ext, preferred_element_type=F32)

    def group_a(jj, bank):
        for par in range(gsz):
            for d in range(2):
                stage_a(d, gsz * jj + par, (bank, par))

    def group_b(jj, bank):
        for par in range(gsz):
            for d in range(2):
                stage_b(d, gsz * jj + par, (bank, par))

    group_a(0, 0)

    def body(jj, carry):
        bank = jj % 2
        group_b(jj, bank)
        group_a(jj + 1, 1 - bank)
        return carry

    lax.fori_loop(0, ngroups - 1, body, 0)
    group_b(ngroups - 1, (ngroups - 1) % 2)

    def finish(j, carry):
        t0 = pl.multiple_of(j * ln, ln)
        hs = hf_scr[pl.ds(t0, ln), :] + hb_scr[pl.ds(t0, ln), :]
        ms = jnp.mean(hs * hs, axis=1, keepdims=True)
        y = hs * lax.rsqrt(ms + EPS) * nw_ref[...]
        ya_ref[pl.ds(t0, ln), :] = (_sigmoid(o_ref[pl.ds(t0, ln), :]) * y).astype(BF16)
        return carry

    lax.fori_loop(0, nchunks, finish, 0)

    if emit_state:
        for d in range(2):
            cn_ref[d] = c_scr[d, :, 0:dv]
            nn_ref[d] = c_scr[d, :, dv:ext].T[0:1, :]
            mn_ref[d] = dm_scr[d, (ngroups - 1) % 2, gsz - 1, 1:2, 0:1]


def _mlstm_tri():
    r = np.arange(MLSTM_L)
    fwd = (r[:, None] <= r[None, :]).astype(np.float32)
    return jnp.asarray(np.stack([fwd, fwd.T]))


def _mlstm(q, kt, v, o, gt, norm_w, *, row0, n_seq, t_len, heads, dk, dv, state=None, emit_state=False):
    blk0 = row0 // t_len
    tok = lambda s, h: (blk0 + s, h)
    in_specs = [pl.BlockSpec((t_len, dk), tok), pl.BlockSpec((dk, t_len), lambda s, h: (h, blk0 + s)),
                pl.BlockSpec((t_len, dv), tok), pl.BlockSpec((t_len, dv), tok),
                pl.BlockSpec((SUBLANES, t_len), lambda s, h: (h, blk0 + s)),
                pl.BlockSpec((None, 1, dv), lambda s, h: (h, 0, 0)),
                pl.BlockSpec((2, MLSTM_L, MLSTM_L), lambda s, h: (0, 0, 0))]
    args = [q, kt, v, o, gt, norm_w.reshape(heads, 1, dv), _mlstm_tri()]
    if state is not None:
        c0, n0, m0 = state
        in_specs += [pl.BlockSpec((None, 2, None, dk, dv), lambda s, h: (s, 0, h, 0, 0)),
                     pl.BlockSpec((None, 2, None, 1, dk), lambda s, h: (s, 0, h, 0, 0)),
                     pl.BlockSpec((None, 2, None, 1, 1), lambda s, h: (s, 0, h, 0, 0))]
        args += [c0, n0.reshape(n_seq, 2, heads, 1, dk), m0.reshape(n_seq, 2, heads, 1, 1)]
    out_specs = [pl.BlockSpec((t_len, dv), lambda s, h: (s, h))]
    out_shape = [jax.ShapeDtypeStruct((n_seq * t_len, heads * dv), BF16)]
    if emit_state:
        out_specs += [pl.BlockSpec((None, 2, None, dk, dv), lambda s, h: (s, 0, h, 0, 0)),
                      pl.BlockSpec((None, 2, None, 1, dk), lambda s, h: (s, 0, h, 0, 0)),
                      pl.BlockSpec((None, 2, None, 1, 1), lambda s, h: (s, 0, h, 0, 0))]
        out_shape += [jax.ShapeDtypeStruct((n_seq, 2, heads, dk, dv), F32),
                      jax.ShapeDtypeStruct((n_seq, 2, heads, 1, dk), F32),
                      jax.ShapeDtypeStruct((n_seq, 2, heads, 1, 1), F32)]
    kern = functools.partial(_mlstm_kernel, t_len=t_len, dk=dk, dv=dv, has_state=state is not None,
                             emit_state=emit_state)
    return pl.pallas_call(
        kern,
        grid=(n_seq, heads),
        in_specs=in_specs,
        out_specs=out_specs,
        out_shape=out_shape,
        scratch_shapes=[pltpu.VMEM((t_len, dv), F32), pltpu.VMEM((t_len, dv), F32),
                        pltpu.VMEM((2, dk, dv + LANES), F32), pltpu.VMEM((2, 1, 1), F32),
                        pltpu.VMEM((2, 2, 2, MLSTM_L, MLSTM_L), BF16), pltpu.VMEM((2, 2, 2, MLSTM_L, LANES), F32),
                        pltpu.VMEM((2, 2, 2, MLSTM_L, LANES), F32), pltpu.VMEM((2, 2, 2, dk, MLSTM_L), BF16),
                        pltpu.VMEM((2, 2, 2, SUBLANES, LANES), F32)],
        compiler_params=_cparams(("arbitrary", "arbitrary")),
        name="mlstm_state" if emit_state else "mlstm",
    )(*args)


def _gelu_tanh(x):
    return x * (0.5 * (1.0 + jnp.tanh(0.7978845608028654 * (x + 0.044715 * (x * x * x)))))


def _softplus(x):
    return jnp.maximum(x, 0.0) + jnp.log1p(jnp.exp(-jnp.abs(x)))


SCAN_SPLIT = 1
SCAN_GROUP = 4


def _rglru_kernel(*refs, t_len, seg, sub, pitch, tc, has_state, emit_state):
    it = iter(refs)
    xr_ref, xg_ref, cw_ref, cb_ref, wg_ref, bg_ref, lam_ref = (next(it) for _ in range(7))
    if has_state:
        h0_ref = next(it)
    yb_ref = next(it)
    if emit_state:
        hn_ref = next(it)
    a_scr, u_scr, cin_scr = (next(it) for _ in range(3))
    nchunks = t_len // tc
    piece = min(tc, sub)
    npieces = tc // piece
    ntile = xr_ref.shape[1] // LANES
    chains = [(d, lt) for d in range(2) for lt in range(ntile)]

    def scan_rows(t0, p):
        t = t0 + p * piece
        i = t // sub
        return pl.ds(pl.multiple_of(i * pitch + (t - i * sub), SUBLANES), piece), i

    ka = (-0.5 * RGLRU_C * 1.4426950408889634) * _softplus(-lam_ref[...])

    def gates(c, carry):
        t0 = pl.multiple_of(c * tc, tc)
        pos = lax.broadcasted_iota(I32, (tc, LANES), 0) % seg
        for lt in range(ntile):
            cols = slice(lt * LANES, (lt + 1) * LANES)
            x = xr_ref[pl.ds(t0, tc), cols]
            xc = cb_ref[:, cols] + cw_ref[CONV_LEFT:CONV_LEFT + 1, cols] * x
            for j in range(cw_ref.shape[0]):
                off = j - CONV_LEFT
                if off == 0:
                    continue
                shifted = pltpu.roll(x, (-off) % tc, 0)
                ok = (pos >= -off) if off < 0 else (pos < seg - off)
                xc = xc + cw_ref[j:j + 1, cols] * jnp.where(ok, shifted, 0.0)
            zh = jnp.dot(xc.astype(BF16), wg_ref[lt], preferred_element_type=F32) + bg_ref[lt]
            hx = 0.5 * xc
            for d in range(2):
                kd = ka[d:d + 1, cols]
                a = jnp.exp2(jnp.tanh(zh[:, (2 * d) * LANES:(2 * d + 1) * LANES]) * kd + kd)
                igx = hx * jnp.tanh(zh[:, (2 * d + 1) * LANES:(2 * d + 2) * LANES]) + hx
                u = jnp.sqrt(1.0 - a * a) * igx
                for p in range(npieces):
                    rows, _ = scan_rows(t0, p)
                    a_scr[d, lt, rows, :] = a[p * piece:(p + 1) * piece]
                    u_scr[d, lt, rows, :] = u[p * piece:(p + 1) * piece]
        return carry

    lax.fori_loop(0, nchunks, gates, 0)

    seg_len = sub // SCAN_SPLIT
    seg_chains = [(d, lt, s) for (d, lt) in chains for s in range(SCAN_SPLIT)]

    def scan(jg, carry):
        def rows(d, s, k):
            j = jg * SCAN_GROUP + k
            return pl.ds(s * seg_len + (j if d == 0 else seg_len - 1 - j), SUBLANES, stride=pitch)

        loaded = [[(a_scr[d, lt, rows(d, s, k), :], u_scr[d, lt, rows(d, s, k), :]) for k in range(SCAN_GROUP)]
                  for (d, lt, s) in seg_chains]
        out = []
        for (d, lt, s), (h, p), steps in zip(seg_chains, carry, loaded):
            for k, (a, u) in enumerate(steps):
                h = a * h + u
                p = a * p
                a_scr[d, lt, rows(d, s, k), :] = p
                u_scr[d, lt, rows(d, s, k), :] = h
            out.append((h, p))
        return tuple(out)

    zero = jnp.zeros((SUBLANES, LANES), F32)
    one = jnp.ones((SUBLANES, LANES), F32)
    assert seg_len % SCAN_GROUP == 0
    ends = dict(zip(seg_chains, lax.fori_loop(0, seg_len // SCAN_GROUP, scan, tuple((zero, one) for _ in seg_chains))))

    order = [(i, s) for i in range(SUBLANES) for s in range(SCAN_SPLIT)]
    for (d, lt) in chains:
        cols = slice(lt * LANES, (lt + 1) * LANES)
        cin = h0_ref[d:d + 1, cols] if has_state else jnp.zeros((1, LANES), F32)
        for (i, s) in (order if d == 0 else reversed(order)):
            h, p = ends[(d, lt, s)]
            cin_scr[d, lt, s * SUBLANES + i:s * SUBLANES + i + 1, :] = cin
            cin = h[i:i + 1] + p[i:i + 1] * cin
        if emit_state:
            hn_ref[d:d + 1, cols] = cin

    fpiece = min(tc, seg_len)

    def finish(c, carry):
        t0 = pl.multiple_of(c * tc, tc)
        for lt in range(ntile):
            cols = slice(lt * LANES, (lt + 1) * LANES)
            for p in range(tc // fpiece):
                t = t0 + p * fpiece
                i = t // sub
                local = t - i * sub
                rows = pl.ds(pl.multiple_of(i * pitch + local, SUBLANES), fpiece)
                crow = pl.ds((local // seg_len) * SUBLANES + i, 1)
                h = (u_scr[0, lt, rows, :] + a_scr[0, lt, rows, :] * cin_scr[0, lt, crow, :]
                     + u_scr[1, lt, rows, :] + a_scr[1, lt, rows, :] * cin_scr[1, lt, crow, :])
                nat = pl.ds(pl.multiple_of(t, SUBLANES), fpiece)
                yb_ref[nat, cols] = (h * _gelu_tanh(xg_ref[nat, cols])).astype(BF16)
        return carry

    lax.fori_loop(0, nchunks, finish, 0)


def _rglru(xr, xg, conv_w, conv_b, wg, bg, lam, *, row0, n_seq, t_len, seg, state=None, emit_state=False):
    rw = xr.shape[1]
    ntile = 2
    cb = ntile * LANES
    assert rw % cb == 0
    blk0 = row0 // t_len
    sub = t_len // SUBLANES
    pitch = sub + SUBLANES
    tc = 256
    tok = lambda s, g: (blk0 + s, g)
    in_specs = [pl.BlockSpec((t_len, cb), tok), pl.BlockSpec((t_len, cb), tok),
                pl.BlockSpec((conv_w.shape[0], cb), lambda s, g: (0, g)),
                pl.BlockSpec((1, cb), lambda s, g: (0, g)),
                pl.BlockSpec((ntile, LANES, 4 * LANES), lambda s, g: (g, 0, 0)),
                pl.BlockSpec((ntile, 1, 4 * LANES), lambda s, g: (g, 0, 0)),
                pl.BlockSpec((2, cb), lambda s, g: (0, g))]
    args = [xr, xg, conv_w, conv_b.reshape(1, rw), wg, bg, lam]
    if state is not None:
        in_specs.append(pl.BlockSpec((None, 2, cb), lambda s, g: (s, 0, g)))
        args.append(state)
    out_specs = [pl.BlockSpec((t_len, cb), lambda s, g: (s, g))]
    out_shape = [jax.ShapeDtypeStruct((n_seq * t_len, rw), BF16)]
    if emit_state:
        out_specs.append(pl.BlockSpec((None, 2, cb), lambda s, g: (s, 0, g)))
        out_shape.append(jax.ShapeDtypeStruct((n_seq, 2, rw), F32))
    kern = functools.partial(_rglru_kernel, t_len=t_len, seg=seg, sub=sub, pitch=pitch, tc=tc,
                             has_state=state is not None, emit_state=emit_state)
    return pl.pallas_call(
        kern,
        grid=(n_seq, rw // cb),
        in_specs=in_specs,
        out_specs=out_specs,
        out_shape=out_shape,
        scratch_shapes=[pltpu.VMEM((2, ntile, SUBLANES * pitch, LANES), F32),
                        pltpu.VMEM((2, ntile, SUBLANES * pitch, LANES), F32),
                        pltpu.VMEM((2, ntile, SCAN_SPLIT * SUBLANES, LANES), F32)],
        compiler_params=_cparams(("arbitrary", "arbitrary")),
        name="rglru_state" if emit_state else "rglru",
    )(*args)


def _pack_bf16_pair(lo, hi):
    def rounded_bits(x):
        return pltpu.bitcast(x.astype(BF16).astype(F32), U32)
    return (rounded_bits(lo) >> 16) | rounded_bits(hi)


def _unpack_bf16_pair(w):
    lo = pltpu.bitcast(w << 16, F32).astype(BF16)
    hi = pltpu.bitcast(w & jnp.uint32(0xFFFF0000), F32).astype(BF16)
    return lo, hi


def _outproj_kernel(yac_ref, yal_ref, ybc_ref, ybl_ref, xp_ref, xs_ref, mod_ref, n2_ref, wo_ref, wr_ref, br_ref,
                    x1_ref, hp_ref, ridx_ref, rwt_ref,
                    *, nctx_tiles, ntok_tiles, tiles_per_lat, d, mw, n_groups, epg, sub_rows):
    i = pl.program_id(0)

    @pl.when(i == ntok_tiles)
    def _():
        x1_ref[...] = jnp.zeros_like(x1_ref)
        hp_ref[...] = jnp.zeros_like(hp_ref)
        ridx_ref[...] = jnp.zeros_like(ridx_ref)
        rwt_ref[...] = jnp.zeros_like(rwt_ref)

    def token_tile(x_ref, ya_ref, yb_ref):
        for r0 in range(0, x_ref.shape[0], sub_rows):
            token_rows(x_ref, ya_ref, yb_ref, r0)

    def token_rows(x_ref, ya_ref, yb_ref, r0):
        rows = slice(r0, r0 + sub_rows)
        x, ya, yb = x_ref[rows, :], ya_ref[rows, :], yb_ref[rows, :]
        row = _mod_row(i, nctx_tiles, tiles_per_lat)
        gate1 = mod_ref[pl.ds(row, 1), 2 * d:3 * d]
        shift2 = mod_ref[pl.ds(row, 1), 3 * d:4 * d]
        scale2 = mod_ref[pl.ds(row, 1), 4 * d:5 * d]
        y = (jnp.dot(ya, wo_ref[0:mw, :], preferred_element_type=F32)
             + jnp.dot(yb, wo_ref[mw:, :], preferred_element_type=F32))
        x1 = x + gate1 * y
        x1_ref[rows, :] = x1
        h2 = _modulated_norm(x1, n2_ref[...], shift2, scale2)
        half = d // 2
        packed = _pack_bf16_pair(h2[:, :half], h2[:, half:])
        trows = half // LANES
        for s in range(trows):
            hp_ref[pl.ds(r0 * trows + s, sub_rows, stride=trows), :] = packed[:, s * LANES:(s + 1) * LANES]

        lt = lax.dot_general(wr_ref[...], h2.astype(BF16), (((1,), (1,)), ((), ())),
                             preferred_element_type=F32) + br_ref[:, 0:1]
        gidx = lax.broadcasted_iota(I32, (SUBLANES, lt.shape[1]), 0)
        gl = jnp.where(gidx < n_groups, lt[0:SUBLANES], -jnp.inf)
        gmax = jnp.max(gl, axis=0, keepdims=True)
        grp = jnp.min(jnp.where(gl == gmax, gidx, n_groups), axis=0, keepdims=True)
        p_grp = 1.0 / jnp.sum(jnp.exp(gl - gmax), axis=0, keepdims=True)
        el = lt[SUBLANES:SUBLANES + epg]
        for g in range(1, n_groups):
            el = jnp.where(grp == g, lt[SUBLANES + g * epg:SUBLANES + (g + 1) * epg], el)
        eidx = lax.broadcasted_iota(I32, el.shape, 0)
        v1 = jnp.max(el, axis=0, keepdims=True)
        i1 = jnp.min(jnp.where(el == v1, eidx, epg), axis=0, keepdims=True)
        el2 = jnp.where(eidx == i1, -jnp.inf, el)
        v2 = jnp.max(el2, axis=0, keepdims=True)
        i2 = jnp.min(jnp.where(el2 == v2, eidx, epg), axis=0, keepdims=True)
        e2 = jnp.exp(v2 - v1)
        w1 = p_grp / (1.0 + e2)
        w2 = p_grp * e2 / (1.0 + e2)
        rid = lax.broadcasted_iota(I32, (SUBLANES, sub_rows), 0)
        ridx_ref[:, rows] = jnp.where(rid == 0, grp * epg + i1, jnp.where(rid == 1, grp * epg + i2, 0))
        rwt_ref[:, rows] = jnp.where(rid == 0, w1, jnp.where(rid == 1, w2, 0.0))

    @pl.when(i < nctx_tiles)
    def _():
        token_tile(xp_ref, yac_ref, ybc_ref)

    @pl.when((i >= nctx_tiles) & (i < ntok_tiles))
    def _():
        token_tile(xs_ref, yal_ref, ybl_ref)


def _out_proj(ya_c, ya_l, yb_c, yb_l, xp, xs, mod, norm2_w, w_out, wr, br, *, lat_seq, n_groups, epg):
    nc, d = xp.shape
    nl = xs.shape[0]
    nt = nc + nl
    mw = ya_c.shape[1]
    tm, sub_rows = 512, 256
    trows = (d // 2) // LANES
    nctx_tiles, ntok_tiles = nc // tm, nt // tm
    kern = functools.partial(_outproj_kernel, nctx_tiles=nctx_tiles, ntok_tiles=ntok_tiles,
                             tiles_per_lat=lat_seq // tm, d=d, mw=mw, n_groups=n_groups, epg=epg, sub_rows=sub_rows)
    ctx = lambda i: (jnp.minimum(i, nctx_tiles - 1), 0)
    lat = lambda i: (jnp.clip(i - nctx_tiles, 0, nl // tm - 1), 0)
    row = lambda i: (i, 0)
    const = lambda i: (0, 0)
    return pl.pallas_call(
        kern,
        grid=(ntok_tiles + 1,),
        in_specs=[pl.BlockSpec((tm, mw), ctx), pl.BlockSpec((tm, mw), lat),
                  pl.BlockSpec((tm, yb_c.shape[1]), ctx), pl.BlockSpec((tm, yb_c.shape[1]), lat),
                  pl.BlockSpec((tm, d), ctx), pl.BlockSpec((tm, d), lat),
                  pl.BlockSpec(mod.shape, const),
                  pl.BlockSpec((1, d), const),
                  pl.BlockSpec(w_out.shape, const, pipeline_mode=pl.Buffered(1)),
                  pl.BlockSpec(wr.shape, const),
                  pl.BlockSpec(br.shape, const)],
        out_specs=[pl.BlockSpec((tm, d), row), pl.BlockSpec((tm * trows, LANES), row),
                   pl.BlockSpec((SUBLANES, tm), lambda i: (0, i)), pl.BlockSpec((SUBLANES, tm), lambda i: (0, i))],
        out_shape=[jax.ShapeDtypeStruct((nt + tm, d), F32), jax.ShapeDtypeStruct(((nt + tm) * trows, LANES), U32),
                   jax.ShapeDtypeStruct((SUBLANES, nt + tm), I32), jax.ShapeDtypeStruct((SUBLANES, nt + tm), F32)],
        compiler_params=_cparams(("arbitrary",)),
        name="out_proj",
    )(ya_c, ya_l, yb_c, yb_l, xp, xs, mod, norm2_w.reshape(1, d), w_out, wr, br)


MOE_ROWS = 256


ROUTE_TILE = 512


def _rank_kernel(ridx_ref, tri_ref, rank_ref, cnt_ref, carry_scr, *, n_exp):
    @pl.when(pl.program_id(0) == 0)
    def _():
        carry_scr[...] = jnp.zeros_like(carry_scr)

    e = ridx_ref[...]
    tr = e.shape[1]
    eid = lax.broadcasted_iota(I32, (n_exp, tr), 0)
    carry = carry_scr[:, 0:1]
    ranks = []
    for kk in range(TOP_K):
        hit = eid == e[kk:kk + 1]
        cum = jnp.dot(jnp.where(hit, 1.0, 0.0).astype(BF16), tri_ref[...], preferred_element_type=F32)
        ranks.append(jnp.sum(jnp.where(hit, cum + carry, 0.0), axis=0, keepdims=True) - 1.0)
        carry = carry + cum[:, tr - 1:tr]
    carry_scr[...] = jnp.broadcast_to(carry, carry_scr.shape)
    cnt_ref[...] = jnp.broadcast_to(carry, cnt_ref.shape)
    rid = lax.broadcasted_iota(I32, rank_ref.shape, 0)
    rank_ref[...] = jnp.where(rid == 0, ranks[0], jnp.where(rid == 1, ranks[1], 0.0)).astype(I32)


def _dest_kernel(ridx_ref, rank_ref, pstart_ref, dest_ref, *, n_exp):
    e = ridx_ref[...]
    tr = e.shape[1]
    eid = lax.broadcasted_iota(I32, (n_exp, tr), 0)
    ps = pstart_ref[:, 0:1]
    rows = [jnp.sum(jnp.where(eid == e[kk:kk + 1], ps, 0.0), axis=0, keepdims=True) for kk in range(TOP_K)]
    rid = lax.broadcasted_iota(I32, dest_ref.shape, 0)
    dest_ref[...] = rank_ref[...] + jnp.where(rid == 0, rows[0], jnp.where(rid == 1, rows[1], 0.0)).astype(I32)


def _invert_kernel(dest_ref, fill_ref, rowtok_ref, sem):
    i = pl.program_id(0)
    tr = dest_ref.shape[1]

    @pl.when(i == 0)
    def _():
        fill = pltpu.make_async_copy(fill_ref, rowtok_ref, sem.at[0])
        fill.start()
        fill.wait()

    def body(r, c):
        for kk in range(TOP_K):
            rowtok_ref[dest_ref[kk, r]] = i * tr + r
        return c
    lax.fori_loop(0, tr, body, 0, unroll=16)


def _routing(ridx, n_tok, n_exp, n_blocks, zero_row):
    tr = ROUTE_TILE
    steps = n_tok // tr
    tri = jnp.asarray(np.triu(np.ones((tr, tr), np.float32)), BF16)
    tile = pl.BlockSpec((SUBLANES, tr), lambda i: (0, i))
    cnt_spec = pl.BlockSpec((n_exp, LANES), lambda i: (0, 0))
    rank, cnt = pl.pallas_call(
        functools.partial(_rank_kernel, n_exp=n_exp),
        grid=(steps,),
        in_specs=[tile, pl.BlockSpec((tr, tr), lambda i: (0, 0))],
        out_specs=[tile, cnt_spec],
        out_shape=[jax.ShapeDtypeStruct((SUBLANES, n_tok), I32), jax.ShapeDtypeStruct((n_exp, LANES), F32)],
        scratch_shapes=[pltpu.VMEM((n_exp, LANES), F32)],
        compiler_params=_cparams(("arbitrary",)),
        name="route_rank",
    )(ridx, tri)
    counts = cnt[:, 0].astype(I32)
    padded = (counts + MOE_ROWS - 1) // MOE_ROWS * MOE_ROWS
    pad_end = jnp.cumsum(padded)
    pad_start = pad_end - padded
    dest = pl.pallas_call(
        functools.partial(_dest_kernel, n_exp=n_exp),
        grid=(steps,),
        in_specs=[tile, tile, cnt_spec],
        out_specs=tile,
        out_shape=jax.ShapeDtypeStruct((SUBLANES, n_tok), I32),
        compiler_params=_cparams(("arbitrary",)),
        name="route_dest",
    )(ridx, rank, jnp.broadcast_to(pad_start.astype(F32)[:, None], (n_exp, LANES)))
    row_tok = pl.pallas_call(
        _invert_kernel,
        grid=(steps,),
        in_specs=[pl.BlockSpec((SUBLANES, tr), lambda i: (0, i), memory_space=pltpu.SMEM),
                  pl.BlockSpec(memory_space=pl.ANY)],
        out_specs=pl.BlockSpec(memory_space=pltpu.SMEM),
        out_shape=jax.ShapeDtypeStruct((n_blocks * MOE_ROWS,), I32),
        scratch_shapes=[pltpu.SemaphoreType.DMA((1,))],
        compiler_params=_cparams(("arbitrary",)),
        name="route_invert",
    )(dest, jnp.full((n_blocks * MOE_ROWS,), zero_row, I32))
    blk_row0 = jnp.arange(n_blocks, dtype=I32) * MOE_ROWS
    blk_e = jnp.minimum(jnp.sum((pad_end[None, :] <= blk_row0[:, None]).astype(I32), axis=1), n_exp - 1)
    nused = (pad_end[-1:] // MOE_ROWS).astype(I32)
    return dest, row_tok, blk_e, nused


CAST_ROWS = 256
WEIGHT_DMA_SPLIT = 8


GATHER_DEPTH = 3


def _expert_kernel(blk_e_ref, nused_ref, tok_ref, nx1_ref, nx2_ref, src_ref, wg_hbm, wu_hbm, wd_hbm, y_ref,
                   xbuf, xsem, stage_g, stage_u, stage_d, bf_g, bf_u, bf_d, wsem, *, half, rows, trows):
    b = pl.program_id(0)
    nused = nused_ref[0]
    w_hbm = (wg_hbm, wu_hbm, wd_hbm)
    stage = (stage_g, stage_u, stage_d)
    wbf = (bf_g, bf_u, bf_d)

    def weight_copy(e, j):
        return pltpu.make_async_copy(w_hbm[j].at[e], stage[j], wsem.at[j])

    def start_weights(e, j):
        step = stage[j].shape[0] // WEIGHT_DMA_SPLIT
        for c in range(WEIGHT_DMA_SPLIT):
            sl = pl.ds(c * step, step)
            pltpu.make_async_copy(w_hbm[j].at[e, sl, :], stage[j].at[sl, :], wsem.at[j]).start(priority=1)

    def row_copy(tok, slot, r):
        src = src_ref.at[pl.ds(pl.multiple_of(tok * trows, trows), trows), :]
        return pltpu.make_async_copy(src, xbuf.at[slot, pl.ds(r * trows, trows), :], xsem.at[slot])

    @pl.when(b >= nused)
    def _():
        y_ref[...] = jnp.zeros_like(y_ref)

    @pl.when(b < nused)
    def _():
        e = blk_e_ref[b]

        @pl.when(b == 0)
        def _():
            for j in range(3):
                start_weights(e, j)
            def body(r, c):
                row_copy(tok_ref[0, r], 0, r).start()
                return c
            lax.fori_loop(0, rows, body, 0, unroll=8)

        @pl.when((b == 0) & (nused > 1))
        def _():
            def body(r, c):
                row_copy(nx1_ref[0, r], 1, r).start()
                return c
            lax.fori_loop(0, rows, body, 0, unroll=8)

        @pl.when((b == 0) | (blk_e_ref[jnp.maximum(b - 1, 0)] != e))
        def _():
            nb = lax.while_loop(lambda k: (k < nused) & (blk_e_ref[jnp.minimum(k, nused - 1)] == e),
                                lambda k: k + 1, b + 1)
            for j in range(3):
                weight_copy(e, j).wait()
                n_steps = stage[j].shape[0] // CAST_ROWS

                def cast(c, carry, j=j):
                    sl = pl.ds(pl.multiple_of(c * CAST_ROWS, CAST_ROWS), CAST_ROWS)
                    wbf[j][sl, :] = stage[j][sl, :].astype(BF16)
                    return carry
                lax.fori_loop(0, n_steps, cast, 0)

                @pl.when(nb < nused)
                def _(j=j):
                    start_weights(blk_e_ref[jnp.minimum(nb, nused - 1)], j)

        slot = b % GATHER_DEPTH
        pltpu.make_async_copy(src_ref.at[pl.ds(0, rows * trows), :], xbuf.at[slot], xsem.at[slot]).wait()

        def compute(prefetch):
            parts = [_unpack_bf16_pair(xbuf[slot, pl.ds(s, rows, stride=trows), :]) for s in range(trows)]
            lo = jnp.concatenate([p[0] for p in parts], axis=1)
            hi = jnp.concatenate([p[1] for p in parts], axis=1)
            if prefetch:
                ahead = (b + GATHER_DEPTH - 1) % GATHER_DEPTH
                for r in range(rows):
                    row_copy(nx2_ref[0, r], ahead, r).start()

            def up(w_ref):
                return (jnp.dot(lo, w_ref[0:half, :], preferred_element_type=F32)
                        + jnp.dot(hi, w_ref[half:, :], preferred_element_type=F32))

            g = up(wbf[0])
            h = ((g * _sigmoid(g)) * up(wbf[1])).astype(BF16)
            y = jnp.dot(h, wbf[2][...], preferred_element_type=F32)
            packed = _pack_bf16_pair(y[:, :half], y[:, half:])
            for s in range(trows):
                y_ref[pl.ds(s, rows, stride=trows), :] = packed[:, s * LANES:(s + 1) * LANES]

        @pl.when(b + GATHER_DEPTH - 1 < nused)
        def _():
            compute(True)

        @pl.when(b + GATHER_DEPTH - 1 >= nused)
        def _():
            compute(False)


def _experts(blk_e, nused, row_tok, src, wg, wu, wd, n_blocks):
    n_exp, d, ff = wg.shape
    any_spec = pl.BlockSpec(memory_space=pl.ANY)
    trows = (d // 2) // LANES
    tok = row_tok.reshape(n_blocks, 1, MOE_ROWS)
    return pl.pallas_call(
        functools.partial(_expert_kernel, half=d // 2, rows=MOE_ROWS, trows=trows),
        grid_spec=pltpu.PrefetchScalarGridSpec(
            num_scalar_prefetch=2,
            grid=(n_blocks,),
            in_specs=[pl.BlockSpec((None, 1, MOE_ROWS), lambda b, be, nu: (b, 0, 0), memory_space=pltpu.SMEM),
                      pl.BlockSpec((None, 1, MOE_ROWS), lambda b, be, nu: (jnp.minimum(b + 1, n_blocks - 1), 0, 0),
                                   memory_space=pltpu.SMEM),
                      pl.BlockSpec((None, 1, MOE_ROWS), lambda b, be, nu: (jnp.minimum(b + 2, n_blocks - 1), 0, 0),
                                   memory_space=pltpu.SMEM),
                      any_spec, any_spec, any_spec, any_spec],
            out_specs=pl.BlockSpec((MOE_ROWS * trows, LANES), lambda b, be, nu: (b, 0)),
            scratch_shapes=[pltpu.VMEM((GATHER_DEPTH, MOE_ROWS * trows, LANES), U32),
                            pltpu.SemaphoreType.DMA((GATHER_DEPTH,)),
                            pltpu.VMEM((d, ff), F32), pltpu.VMEM((d, ff), F32), pltpu.VMEM((ff, d), F32),
                            pltpu.VMEM((d, ff), BF16), pltpu.VMEM((d, ff), BF16), pltpu.VMEM((ff, d), BF16),
                            pltpu.SemaphoreType.DMA((3,))]),
        out_shape=jax.ShapeDtypeStruct((n_blocks * MOE_ROWS * trows, LANES), U32),
        compiler_params=_cparams(("arbitrary",)),
        name="experts",
    )(blk_e, nused, tok, tok, tok, src, wg, wu, wd)


COMBINE_BUFS = 3


def _combine_kernel(dest_ref, x1_ref, wt_ref, mod_ref, fw_ref, y_ref, o_ref, *scratch,
                    tile0, n_all, tiles_per_seq, d, lat):
    i = pl.program_id(0)
    last = pl.num_programs(0) - 1
    tm = x1_ref.shape[0]
    bufs, sem = scratch[:-1], scratch[-1]
    nbuf = len(bufs)

    trows = (d // 2) // LANES

    def row_copy(tile, par, kk, r):
        row = dest_ref[kk * n_all + (tile0 + tile) * tm + r]
        src = y_ref.at[pl.ds(pl.multiple_of(row * trows, trows), trows), :]
        return pltpu.make_async_copy(src, bufs[par].at[kk, pl.ds(r * trows, trows), :], sem.at[par])

    for first in range(nbuf - 1):
        @pl.when((i == 0) & (first <= last))
        def _(first=first):
            def body(r, c):
                for kk in range(TOP_K):
                    row_copy(first, first, kk, r).start()
                return c
            lax.fori_loop(0, tm, body, 0, unroll=8)

    def step(par, prefetch):
        for kk in range(TOP_K):
            pltpu.make_async_copy(y_ref.at[pl.ds(0, tm * trows), :], bufs[par].at[kk], sem.at[par]).wait()
        if prefetch:
            for r in range(tm):
                for kk in range(TOP_K):
                    row_copy(i + nbuf - 1, (par + nbuf - 1) % nbuf, kk, r).start(priority=kk)
        row = (1 + i // tiles_per_seq) if lat else 0
        gate2 = mod_ref[pl.ds(row, 1), 5 * d:6 * d]
        wt = wt_ref[...]
        half = d // 2
        w0 = _row_to_col(wt[0:1], tm)
        w1 = _row_to_col(wt[1:2], tm)
        unpack = (lambda p: pltpu.bitcast(p << 16, F32),
                  lambda p: pltpu.bitcast(p & jnp.uint32(0xFFFF0000), F32))
        ssq = jnp.zeros((tm, 1), F32)
        for s in range(trows):
            p0 = bufs[par][0, pl.ds(s, tm, stride=trows), :]
            p1 = bufs[par][1, pl.ds(s, tm, stride=trows), :]
            for side in range(2):
                cols = slice(side * half + s * LANES, side * half + (s + 1) * LANES)
                x = x1_ref[:, cols] + gate2[:, cols] * (w0 * unpack[side](p0) + w1 * unpack[side](p1))
                ssq = ssq + jnp.sum(x * x, axis=-1, keepdims=True)
                o_ref[:, cols] = x
        scale = lax.rsqrt(ssq * (1.0 / d) + EPS)
        o_ref[...] = o_ref[...] * scale * fw_ref[...]

    for par in range(nbuf):
        @pl.when((i % nbuf == par) & (i + nbuf - 1 <= last))
        def _(par=par):
            step(par, True)

        @pl.when((i % nbuf == par) & (i + nbuf - 1 > last))
        def _(par=par):
            step(par, False)


def _combine(dest_flat, x1, rwt, mod, final_w, y_rows, *, row0, n_tok, seq_len, lat):
    d = x1.shape[1]
    tm = 256
    tile0 = row0 // tm
    trows = (d // 2) // LANES
    n_all = dest_flat.shape[0] // TOP_K
    kern = functools.partial(_combine_kernel, tile0=tile0, n_all=n_all, tiles_per_seq=seq_len // tm, d=d, lat=lat)
    return pl.pallas_call(
        kern,
        grid_spec=pltpu.PrefetchScalarGridSpec(
            num_scalar_prefetch=1,
            grid=(n_tok // tm,),
            in_specs=[pl.BlockSpec((tm, d), lambda i, dr: (tile0 + i, 0)),
                      pl.BlockSpec((SUBLANES, tm), lambda i, dr: (0, tile0 + i)),
                      pl.BlockSpec(mod.shape, lambda i, dr: (0, 0)),
                      pl.BlockSpec((1, d), lambda i, dr: (0, 0)),
                      pl.BlockSpec(memory_space=pl.ANY)],
            out_specs=pl.BlockSpec((tm, d), lambda i, dr: (i, 0)),
            scratch_shapes=[pltpu.VMEM((TOP_K, tm * trows, LANES), U32) for _ in range(COMBINE_BUFS)]
            + [pltpu.SemaphoreType.DMA((COMBINE_BUFS,))]),
        out_shape=jax.ShapeDtypeStruct((n_tok, d), F32),
        compiler_params=_cparams(("arbitrary",)),
        name="combine_lat" if lat else "combine_ctx",
    )(dest_flat, x1, rwt, mod, final_w.reshape(1, d), y_rows)


def _gate_layout(w_gates, b_gates, heads):
    d = w_gates.shape[0]
    w = w_gates.reshape(d, 4, heads).transpose(0, 2, 1)
    w = jnp.pad(w, ((0, 0), (0, 0), (0, SUBLANES - 4))).reshape(d, heads * SUBLANES)
    b = b_gates.reshape(4, heads).T
    b = jnp.pad(b, ((0, 0), (0, SUBLANES - 4))).reshape(1, heads * SUBLANES)
    padl = LANES - heads * SUBLANES
    return jnp.pad(w, ((0, 0), (0, padl))), jnp.pad(b, ((0, 0), (0, padl)))


def kernel(x_prompt, x_sample, state_mlstm_c, state_mlstm_n, state_mlstm_m, state_rglru_h, c, c_ctx, w_ada, b_ada,
           norm1_w, w_in, b_gates, conv_w, conv_b, rg_wa, rg_ba, rg_wx, rg_bx, rg_lambda, mlstm_norm_w, w_out,
           norm2_w, router_group_w, router_group_b, router_expert_w, router_expert_b, expert_w_gate, expert_w_up,
           expert_w_down, final_norm_w):
    n_req, seq, d = x_prompt.shape
    n_lat, lat_seq, _ = x_sample.shape
    depth = w_in.shape[0]
    assert depth == 1, "the token-axis plumbing below is written for the single-layer trunk"
    heads, dk, dv = state_mlstm_c.shape[3:]
    rw = state_rglru_h.shape[-1]
    nblk = rg_wa.shape[2]
    assert rw // nblk == LANES
    n_groups, epg = router_expert_w.shape[1], router_expert_w.shape[3]
    n_exp = n_groups * epg
    qk, mw = heads * dk, heads * dv
    nc, nl = n_req * seq, n_lat * lat_seq
    nt = nc + nl
    assert nc % lat_seq == 0 and n_lat + 1 <= SUBLANES
    l = 0

    xp = x_prompt.reshape(nc, d)
    xs = x_sample.reshape(nl, d)
    cvec = jnp.zeros((SUBLANES, d), F32).at[0].set(c_ctx).at[1:1 + n_lat].set(c)
    mod = _ada(cvec, w_ada[l], b_ada[l])

    w = w_in[l]
    g0 = 2 * qk + 2 * mw
    wgate, bgate = _gate_layout(w[:, g0:g0 + 4 * heads], b_gates[l], heads)
    w_cat = jnp.concatenate([w[:, :qk], w[:, 2 * qk:g0], wgate, w[:, g0 + 4 * heads:]], axis=1).astype(BF16)
    w_kt = w[:, qk:2 * qk].T.astype(BF16)
    q, kt, v, o, gt, xr, xg = _in_proj(xp, xs, mod, norm1_w[l], w_cat, w_kt, bgate, lat_seq=lat_seq, heads=heads,
                                       dk=dk, dv=dv, rw=rw)

    mkw = dict(heads=heads, dk=dk, dv=dv)
    ya_c, new_c, new_n, new_m = _mlstm(q, kt, v, o, gt, mlstm_norm_w[l], row0=0, n_seq=n_req, t_len=seq,
                                       emit_state=True, **mkw)
    (ya_l,) = _mlstm(q, kt, v, o, gt, mlstm_norm_w[l], row0=nc, n_seq=n_lat, t_len=lat_seq,
                     state=(state_mlstm_c[:, l], state_mlstm_n[:, l], state_mlstm_m[:, l]), **mkw)

    wg = (0.5 * jnp.concatenate([rg_wa[l, 0], rg_wx[l, 0], rg_wa[l, 1], rg_wx[l, 1]], axis=-1)).astype(BF16)
    bg = 0.5 * jnp.concatenate([rg_ba[l, 0].reshape(nblk, 1, LANES), rg_bx[l, 0].reshape(nblk, 1, LANES),
                                rg_ba[l, 1].reshape(nblk, 1, LANES), rg_bx[l, 1].reshape(nblk, 1, LANES)], axis=-1)
    rargs = (xr, xg, conv_w[l], conv_b[l], wg, bg, rg_lambda[l])
    yb_c, new_h = _rglru(*rargs, row0=0, n_seq=n_req, t_len=seq, seg=seq, emit_state=True)
    (yb_l,) = _rglru(*rargs, row0=nc, n_seq=n_lat, t_len=lat_seq, seg=GRID_W, state=state_rglru_h[:, l])

    r_rows = -(-(SUBLANES + n_exp) // 16) * 16
    wr = jnp.zeros((r_rows, d), F32)
    wr = wr.at[0:n_groups].set(router_group_w[l].T)
    wr = wr.at[SUBLANES:SUBLANES + n_exp].set(router_expert_w[l].transpose(0, 2, 1).reshape(n_exp, d)).astype(BF16)
    br = jnp.zeros((r_rows, LANES), F32)
    br = br.at[0:n_groups, 0].set(router_group_b[l])
    br = br.at[SUBLANES:SUBLANES + n_exp, 0].set(router_expert_b[l].reshape(n_exp))
    x1, hp, ridx, rwt = _out_proj(ya_c, ya_l, yb_c, yb_l, xp, xs, mod, norm2_w[l], w_out[l].astype(BF16), wr, br,
                                  lat_seq=lat_seq, n_groups=n_groups, epg=epg)

    n_blocks = (nt * TOP_K) // MOE_ROWS + n_exp
    dest, row_tok, blk_e, nused = _routing(ridx, nt, n_exp, n_blocks, zero_row=nt)
    only_layer = lambda a: a.reshape(a.shape[1:])
    y_rows = _experts(blk_e, nused, row_tok, hp, only_layer(expert_w_gate), only_layer(expert_w_up),
                      only_layer(expert_w_down), n_blocks)

    dest_flat = dest[0:TOP_K].reshape(-1)
    y_prompt = _combine(dest_flat, x1, rwt, mod, final_norm_w, y_rows, row0=0, n_tok=nc, seq_len=seq, lat=False)
    y_sample = _combine(dest_flat, x1, rwt, mod, final_norm_w, y_rows, row0=nc, n_tok=nl, seq_len=lat_seq,
                        lat=True)

    return (y_prompt.reshape(n_req, seq, d), y_sample.reshape(n_lat, lat_seq, d),
            new_c[:, None], new_n.reshape(n_req, 1, 2, heads, dk), new_m.reshape(n_req, 1, 2, heads),
            new_h[:, None])
```

```python
import functools

import jax
import jax.numpy as jnp
import numpy as np
from jax import lax
from jax.experimental import pallas as pl
from jax.experimental.pallas import tpu as pltpu

F32 = jnp.float32
BF16 = jnp.bfloat16
I32 = jnp.int32
U32 = jnp.uint32

EPS = 1e-6
GRID_W = 64
CONV_LEFT = 2
RGLRU_C = 8.0
TOP_K = 2
LANES = 128
SUBLANES = 8
MLSTM_L = 256
NEG = -1e30
VMEM_LIMIT = 56 * 1024 * 1024

_HIGHEST = lax.Precision.HIGHEST


def _cparams(sem, vmem=VMEM_LIMIT):
    return pltpu.CompilerParams(dimension_semantics=sem, vmem_limit_bytes=vmem)


def _sigmoid(x):
    return 0.5 * jnp.tanh(0.5 * x) + 0.5


def _row_to_col(r, n):
    return jnp.broadcast_to(r, (LANES, n)).T


def _lane_tile(x, reps):
    return x if reps == 1 else jnp.concatenate([x] * reps, axis=1)


def _ada_kernel(c_ref, w_ref, b_ref, o_ref):
    c = c_ref[...]
    s = (c * _sigmoid(c)).astype(BF16)
    o_ref[...] = jnp.dot(s, w_ref[...].astype(BF16), preferred_element_type=F32) + b_ref[...]


def _ada(cvec, w_ada, b_ada):
    d, n = w_ada.shape
    tn = 1024 if n % 1024 == 0 else 512
    assert n % tn == 0
    return pl.pallas_call(
        _ada_kernel,
        grid=(n // tn,),
        in_specs=[pl.BlockSpec((SUBLANES, d), lambda j: (0, 0)),
                  pl.BlockSpec((d, tn), lambda j: (0, j)),
                  pl.BlockSpec((1, tn), lambda j: (0, j))],
        out_specs=pl.BlockSpec((SUBLANES, tn), lambda j: (0, j)),
        out_shape=jax.ShapeDtypeStruct((SUBLANES, n), F32),
        compiler_params=_cparams(("arbitrary",)),
        name="ada",
    )(cvec, w_ada, b_ada.reshape(1, n))


def _modulated_norm(x, w, shift, scale):
    ms = jnp.mean(x * x, axis=-1, keepdims=True)
    return (x * lax.rsqrt(ms + EPS) * w) * (1.0 + scale) + shift


def _mod_row(i, nctx_tiles, tiles_per_lat):
    return jnp.where(i < nctx_tiles, 0, 1 + (i - nctx_tiles) // tiles_per_lat)


def _inproj_kernel(xp_ref, xs_ref, mod_ref, n1_ref, w_ref, wk_ref, bg_ref,
                   q_ref, kt_ref, v_ref, o_ref, gt_ref, xr_ref, xg_ref, wkt_scr,
                   *, nctx_tiles, tiles_per_lat, d, qk, mw, rw, gh, qscale):
    i = pl.program_id(0)

    @pl.when(i == 0)
    def _():
        wkt_scr[...] = wk_ref[...].T.astype(BF16)

    x = jnp.where(i < nctx_tiles, xp_ref[...], xs_ref[...])
    row = _mod_row(i, nctx_tiles, tiles_per_lat)
    shift = mod_ref[pl.ds(row, 1), 0:d]
    scale = mod_ref[pl.ds(row, 1), d:2 * d]
    hb = _modulated_norm(x, n1_ref[...], shift, scale).astype(BF16)

    def proj(c0, width):
        return jnp.dot(hb, w_ref[:, c0:c0 + width], preferred_element_type=F32)

    c0 = 0
    q_ref[...] = (proj(c0, qk) * qscale).astype(BF16); c0 += qk
    kt_ref[...] = lax.dot_general(wkt_scr[...], hb, (((1,), (1,)), ((), ())),
                                  preferred_element_type=F32).astype(BF16)
    v_ref[...] = proj(c0, mw).astype(BF16); c0 += mw
    o_ref[...] = proj(c0, mw); c0 += mw
    zg = proj(c0, LANES) + bg_ref[...]; c0 += LANES
    lane = lax.broadcasted_iota(I32, zg.shape, 1)
    log_sig = jnp.minimum(zg, 0.0) - jnp.log1p(jnp.exp(-jnp.abs(zg)))
    zg = jnp.where(lane % 2 == 1, log_sig, zg)
    gt_ref[...] = zg.T[0:gh, :]
    xr_ref[...] = proj(c0, rw); c0 += rw
    xg_ref[...] = proj(c0, rw)


def _in_proj(xp, xs, mod, norm1_w, w_cat, w_k, bg, *, lat_seq, heads, dk, dv, rw):
    nc, d = xp.shape
    nl = xs.shape[0]
    nt = nc + nl
    tm = 256
    qk, mw, gh = heads * dk, heads * dv, heads * SUBLANES
    nctx_tiles = nc // tm
    kern = functools.partial(_inproj_kernel, nctx_tiles=nctx_tiles, tiles_per_lat=lat_seq // tm, d=d, qk=qk, mw=mw,
                             rw=rw, gh=gh, qscale=dk ** -0.5)
    row = lambda i: (i, 0)
    const = lambda i: (0, 0)
    return pl.pallas_call(
        kern,
        grid=(nt // tm,),
        in_specs=[pl.BlockSpec((tm, d), lambda i: (jnp.minimum(i, nctx_tiles - 1), 0)),
                  pl.BlockSpec((tm, d), lambda i: (jnp.maximum(i - nctx_tiles, 0), 0)),
                  pl.BlockSpec(mod.shape, const),
                  pl.BlockSpec((1, d), const),
                  pl.BlockSpec(w_cat.shape, const, pipeline_mode=pl.Buffered(1)),
                  pl.BlockSpec(w_k.shape, const, pipeline_mode=pl.Buffered(1)),
                  pl.BlockSpec((1, LANES), const)],
        out_specs=[pl.BlockSpec((tm, qk), row), pl.BlockSpec((qk, tm), lambda i: (0, i)), pl.BlockSpec((tm, mw), row),
                   pl.BlockSpec((tm, mw), row), pl.BlockSpec((gh, tm), lambda i: (0, i)),
                   pl.BlockSpec((tm, rw), row), pl.BlockSpec((tm, rw), row)],
        out_shape=[jax.ShapeDtypeStruct((nt, qk), BF16), jax.ShapeDtypeStruct((qk, nt), BF16),
                   jax.ShapeDtypeStruct((nt, mw), BF16), jax.ShapeDtypeStruct((nt, mw), F32),
                   jax.ShapeDtypeStruct((gh, nt), F32),
                   jax.ShapeDtypeStruct((nt, rw), F32), jax.ShapeDtypeStruct((nt, rw), F32)],
        scratch_shapes=[pltpu.VMEM((qk, d), BF16)],
        compiler_params=_cparams(("arbitrary",)),
        name="in_proj",
    )(xp, xs, mod, norm1_w.reshape(1, d), w_cat, w_k, bg)


def _mlstm_kernel(*refs, t_len, dk, dv, has_state, emit_state):
    it = iter(refs)
    q_ref, kt_ref, v_ref, o_ref, gt_ref, nw_ref, tri_ref = (next(it) for _ in range(7))
    if has_state:
        c0_ref, n0_ref, m0_ref = (next(it) for _ in range(3))
    ya_ref = next(it)
    if emit_state:
        cn_ref, nn_ref, mn_ref = (next(it) for _ in range(3))
    hf_scr, hb_scr, c_scr, ma_scr, p_scr, w_scr, em_scr, kw_scr, dm_scr = (next(it) for _ in range(9))
    ln = MLSTM_L
    nchunks = t_len // ln
    assert ln % LANES == 0 and dk == LANES
    gsz = 2 if nchunks % 2 == 0 else 1
    ngroups = nchunks // gsz
    h_scr = (hf_scr, hb_scr)
    ext = dv + LANES
    lrep = ln // LANES

    for d in range(2):
        if has_state:
            c_scr[d, :, 0:dv] = c0_ref[d]
            c_scr[d, :, dv:ext] = _row_to_col(n0_ref[d], dk)
            ma_scr[d] = m0_ref[d]
        else:
            c_scr[d] = jnp.zeros((dk, ext), F32)
            ma_scr[d] = jnp.zeros((1, 1), F32)

    def chunk_start(d, j):
        return pl.multiple_of((j if d == 0 else nchunks - 1 - j) * ln, ln)

    def stage_a(d, j, slot):
        t0 = chunk_start(d, j)
        q = q_ref[pl.ds(t0, ln), :]
        kt = kt_ref[:, pl.ds(t0, ln)]
        g8 = gt_ref[:, pl.ds(t0, ln)]
        cum8 = jnp.dot(g8, tri_ref[d], precision=_HIGHEST, preferred_element_type=F32)
        valid = tri_ref[1 - d] > 0.5
        li = g8[2 * d:2 * d + 1]
        lf = g8[2 * d + 1:2 * d + 2]
        cum_row = cum8[2 * d + 1:2 * d + 2]
        total = jnp.sum(lf, axis=1, keepdims=True)
        a_row = li - cum_row
        cum_col = _row_to_col(cum_row, ln)
        m_prev = ma_scr[d]
        dmat = jnp.where(valid, _lane_tile(cum_col, lrep) + a_row, NEG)
        inter = cum_col + m_prev
        m_t = jnp.maximum(inter, jnp.max(dmat, axis=1, keepdims=True))
        s = jnp.dot(q, kt, preferred_element_type=F32)
        bank, par = slot
        p_scr[d, bank, par] = (s * jnp.exp(dmat - _lane_tile(m_t, lrep))).astype(BF16)
        w_scr[d, bank, par] = jnp.exp(inter - m_t)
        em_scr[d, bank, par] = jnp.exp(-m_t)
        g_row = total + a_row
        m_new = jnp.maximum(total + m_prev, jnp.max(g_row, axis=1, keepdims=True))
        kw_scr[d, bank, par] = (kt.astype(F32) * jnp.exp(g_row - m_new)).astype(BF16)
        decay = jnp.exp(total + m_prev - m_new)
        rid = lax.broadcasted_iota(I32, (SUBLANES, LANES), 0)
        dm_scr[d, bank, par] = jnp.where(rid == 0, decay, m_new)
        ma_scr[d] = m_new

    def stage_b(d, j, slot):
        t0 = chunk_start(d, j)
        q = q_ref[pl.ds(t0, ln), :]
        v_ext = jnp.concatenate([v_ref[pl.ds(t0, ln), :], jnp.ones((ln, LANES), BF16)], axis=1)
        c_st = c_scr[d]
        bank, par = slot
        full = (jnp.dot(p_scr[d, bank, par], v_ext, preferred_element_type=F32)
                + _lane_tile(w_scr[d, bank, par], ext // LANES) * jnp.dot(q, c_st.astype(BF16),
                                                                         preferred_element_type=F32))
        inv = 1.0 / jnp.maximum(jnp.abs(full[:, dv:ext]), em_scr[d, bank, par])
        h_scr[d][pl.ds(t0, ln), :] = full[:, 0:dv] * _lane_tile(inv, dv // LANES)
        decay = dm_scr[d, bank, par, 0:1, 0:1]
        c_scr[d] = decay * c_st + jnp.dot(kw_scr[d, bank, par], v_ext, preferred_element_type=F32)

    def group_a(jj, bank):
        for par in range(gsz):
            for d in range(2):
                stage_a(d, gsz * jj + par, (bank, par))

    def group_b(jj, bank):
        for par in range(gsz):
            for d in range(2):
                stage_b(d, gsz * jj + par, (bank, par))

    group_a(0, 0)

    def body(jj, carry):
        bank = jj % 2
        group_b(jj, bank)
        group_a(jj + 1, 1 - bank)
        return carry

    lax.fori_loop(0, ngroups - 1, body, 0)
    group_b(ngroups - 1, (ngroups - 1) % 2)

    def finish(j, carry):
        t0 = pl.multiple_of(j * ln, ln)
        hs = hf_scr[pl.ds(t0, ln), :] + hb_scr[pl.ds(t0, ln), :]
        ms = jnp.mean(hs * hs, axis=1, keepdims=True)
        y = hs * lax.rsqrt(ms + EPS) * nw_ref[...]
        ya_ref[pl.ds(t0, ln), :] = (_sigmoid(o_ref[pl.ds(t0, ln), :]) * y).astype(BF16)
        return carry

    lax.fori_loop(0, nchunks, finish, 0)

    if emit_state:
        for d in range(2):
            cn_ref[d] = c_scr[d, :, 0:dv]
            nn_ref[d] = c_scr[d, :, dv:ext].T[0:1, :]
            mn_ref[d] = dm_scr[d, (ngroups - 1) % 2, gsz - 1, 1:2, 0:1]


def _mlstm_tri():
    r = np.arange(MLSTM_L)
    fwd = (r[:, None] <= r[None, :]).astype(np.float32)
    return jnp.asarray(np.stack([fwd, fwd.T]))


def _mlstm(q, kt, v, o, gt, norm_w, *, row0, n_seq, t_len, heads, dk, dv, state=None, emit_state=False):
    blk0 = row0 // t_len
    tok = lambda s, h: (blk0 + s, h)
    in_specs = [pl.BlockSpec((t_len, dk), tok), pl.BlockSpec((dk, t_len), lambda s, h: (h, blk0 + s)),
                pl.BlockSpec((t_len, dv), tok), pl.BlockSpec((t_len, dv), tok),
                pl.BlockSpec((SUBLANES, t_len), lambda s, h: (h, blk0 + s)),
                pl.BlockSpec((None, 1, dv), lambda s, h: (h, 0, 0)),
                pl.BlockSpec((2, MLSTM_L, MLSTM_L), lambda s, h: (0, 0, 0))]
    args = [q, kt, v, o, gt, norm_w.reshape(heads, 1, dv), _mlstm_tri()]
    if state is not None:
        c0, n0, m0 = state
        in_specs += [pl.BlockSpec((None, 2, None, dk, dv), lambda s, h: (s, 0, h, 0, 0)),
                     pl.BlockSpec((None, 2, None, 1, dk), lambda s, h: (s, 0, h, 0, 0)),
                     pl.BlockSpec((None, 2, None, 1, 1), lambda s, h: (s, 0, h, 0, 0))]
        args += [c0, n0.reshape(n_seq, 2, heads, 1, dk), m0.reshape(n_seq, 2, heads, 1, 1)]
    out_specs = [pl.BlockSpec((t_len, dv), lambda s, h: (s, h))]
    out_shape = [jax.ShapeDtypeStruct((n_seq * t_len, heads * dv), BF16)]
    if emit_state:
        out_specs += [pl.BlockSpec((None, 2, None, dk, dv), lambda s, h: (s, 0, h, 0, 0)),
                      pl.BlockSpec((None, 2, None, 1, dk), lambda s, h: (s, 0, h, 0, 0)),
                      pl.BlockSpec((None, 2, None, 1, 1), lambda s, h: (s, 0, h, 0, 0))]
        out_shape += [jax.ShapeDtypeStruct((n_seq, 2, heads, dk, dv), F32),
                      jax.ShapeDtypeStruct((n_seq, 2, heads, 1, dk), F32),
                      jax.ShapeDtypeStruct((n_seq, 2, heads, 1, 1), F32)]
    kern = functools.partial(_mlstm_kernel, t_len=t_len, dk=dk, dv=dv, has_state=state is not None,
                             emit_state=emit_state)
    return pl.pallas_call(
        kern,
        grid=(n_seq, heads),
        in_specs=in_specs,
        out_specs=out_specs,
        out_shape=out_shape,
        scratch_shapes=[pltpu.VMEM((t_len, dv), F32), pltpu.VMEM((t_len, dv), F32),
                        pltpu.VMEM((2, dk, dv + LANES), F32), pltpu.VMEM((2, 1, 1), F32),
                        pltpu.VMEM((2, 2, 2, MLSTM_L, MLSTM_L), BF16), pltpu.VMEM((2, 2, 2, MLSTM_L, LANES), F32),
                        pltpu.VMEM((2, 2, 2, MLSTM_L, LANES), F32), pltpu.VMEM((2, 2, 2, dk, MLSTM_L), BF16),
                        pltpu.VMEM((2, 2, 2, SUBLANES, LANES), F32)],
        compiler_params=_cparams(("arbitrary", "arbitrary")),
        name="mlstm_state" if emit_state else "mlstm",
    )(*args)


def _gelu_tanh(x):
    return x * (0.5 * (1.0 + jnp.tanh(0.7978845608028654 * (x + 0.044715 * (x * x * x)))))


def _softplus(x):
    return jnp.maximum(x, 0.0) + jnp.log1p(jnp.exp(-jnp.abs(x)))


SCAN_SPLIT = 1
SCAN_GROUP = 4


def _rglru_kernel(*refs, t_len, seg, sub, pitch, tc, has_state, emit_state):
    it = iter(refs)
    xr_ref, xg_ref, cw_ref, cb_ref, wg_ref, bg_ref, lam_ref = (next(it) for _ in range(7))
    if has_state:
        h0_ref = next(it)
    yb_ref = next(it)
    if emit_state:
        hn_ref = next(it)
    a_scr, u_scr, cin_scr = (next(it) for _ in range(3))
    nchunks = t_len // tc
    piece = min(tc, sub)
    npieces = tc // piece
    ntile = xr_ref.shape[1] // LANES
    chains = [(d, lt) for d in range(2) for lt in range(ntile)]

    def scan_rows(t0, p):
        t = t0 + p * piece
        i = t // sub
        return pl.ds(pl.multiple_of(i * pitch + (t - i * sub), SUBLANES), piece), i

    ka = (-0.5 * RGLRU_C * 1.4426950408889634) * _softplus(-lam_ref[...])

    def gates(c, carry):
        t0 = pl.multiple_of(c * tc, tc)
        pos = lax.broadcasted_iota(I32, (tc, LANES), 0) % seg
        for lt in range(ntile):
            cols = slice(lt * LANES, (lt + 1) * LANES)
            x = xr_ref[pl.ds(t0, tc), cols]
            xc = cb_ref[:, cols] + cw_ref[CONV_LEFT:CONV_LEFT + 1, cols] * x
            for j in range(cw_ref.shape[0]):
                off = j - CONV_LEFT
                if off == 0:
                    continue
                shifted = pltpu.roll(x, (-off) % tc, 0)
                ok = (pos >= -off) if off < 0 else (pos < seg - off)
                xc = xc + cw_ref[j:j + 1, cols] * jnp.where(ok, shifted, 0.0)
            zh = jnp.dot(xc.astype(BF16), wg_ref[lt], preferred_element_type=F32) + bg_ref[lt]
            hx = 0.5 * xc
            for d in range(2):
                kd = ka[d:d + 1, cols]
                a = jnp.exp2(jnp.tanh(zh[:, (2 * d) * LANES:(2 * d + 1) * LANES]) * kd + kd)
                igx = hx * jnp.tanh(zh[:, (2 * d + 1) * LANES:(2 * d + 2) * LANES]) + hx
                u = jnp.sqrt(1.0 - a * a) * igx
                for p in range(npieces):
                    rows, _ = scan_rows(t0, p)
                    a_scr[d, lt, rows, :] = a[p * piece:(p + 1) * piece]
                    u_scr[d, lt, rows, :] = u[p * piece:(p + 1) * piece]
        return carry

    lax.fori_loop(0, nchunks, gates, 0)

    seg_len = sub // SCAN_SPLIT
    seg_chains = [(d, lt, s) for (d, lt) in chains for s in range(SCAN_SPLIT)]

    def scan(jg, carry):
        def rows(d, s, k):
            j = jg * SCAN_GROUP + k
            return pl.ds(s * seg_len + (j if d == 0 else seg_len - 1 - j), SUBLANES, stride=pitch)

        loaded = [[(a_scr[d, lt, rows(d, s, k), :], u_scr[d, lt, rows(d, s, k), :]) for k in range(SCAN_GROUP)]
                  for (d, lt, s) in seg_chains]
        out = []
        for (d, lt, s), (h, p), steps in zip(seg_chains, carry, loaded):
            for k, (a, u) in enumerate(steps):
                h = a * h + u
                p = a * p
                a_scr[d, lt, rows(d, s, k), :] = p
                u_scr[d, lt, rows(d, s, k), :] = h
            out.append((h, p))
        return tuple(out)

    zero = jnp.zeros((SUBLANES, LANES), F32)
    one = jnp.ones((SUBLANES, LANES), F32)
    assert seg_len % SCAN_GROUP == 0
    ends = dict(zip(seg_chains, lax.fori_loop(0, seg_len // SCAN_GROUP, scan, tuple((zero, one) for _ in seg_chains))))

    order = [(i, s) for i in range(SUBLANES) for s in range(SCAN_SPLIT)]
    for (d, lt) in chains:
        cols = slice(lt * LANES, (lt + 1) * LANES)
        cin = h0_ref[d:d + 1, cols] if has_state else jnp.zeros((1, LANES), F32)
        for (i, s) in (order if d == 0 else reversed(order)):
            h, p = ends[(d, lt, s)]
            cin_scr[d, lt, s * SUBLANES + i:s * SUBLANES + i + 1, :] = cin
            cin = h[i:i + 1] + p[i:i + 1] * cin
        if emit_state:
            hn_ref[d:d + 1, cols] = cin

    fpiece = min(tc, seg_len)

    def finish(c, carry):
        t0 = pl.multiple_of(c * tc, tc)
        for lt in range(ntile):
            cols = slice(lt * LANES, (lt + 1) * LANES)
            for p in range(tc // fpiece):
                t = t0 + p * fpiece
                i = t // sub
                local = t - i * sub
                rows = pl.ds(pl.multiple_of(i * pitch + local, SUBLANES), fpiece)
                crow = pl.ds((local // seg_len) * SUBLANES + i, 1)
                h = (u_scr[0, lt, rows, :] + a_scr[0, lt, rows, :] * cin_scr[0, lt, crow, :]
                     + u_scr[1, lt, rows, :] + a_scr[1, lt, rows, :] * cin_scr[1, lt, crow, :])
                nat = pl.ds(pl.multiple_of(t, SUBLANES), fpiece)
                yb_ref[nat, cols] = (h * _gelu_tanh(xg_ref[nat, cols])).astype(BF16)
        return carry

    lax.fori_loop(0, nchunks, finish, 0)


def _rglru(xr, xg, conv_w, conv_b, wg, bg, lam, *, row0, n_seq, t_len, seg, state=None, emit_state=False):
    rw = xr.shape[1]
    ntile = 2
    cb = ntile * LANES
    assert rw % cb == 0
    blk0 = row0 // t_len
    sub = t_len // SUBLANES
    pitch = sub + SUBLANES
    tc = 256
    tok = lambda s, g: (blk0 + s, g)
    in_specs = [pl.BlockSpec((t_len, cb), tok), pl.BlockSpec((t_len, cb), tok),
                pl.BlockSpec((conv_w.shape[0], cb), lambda s, g: (0, g)),
                pl.BlockSpec((1, cb), lambda s, g: (0, g)),
                pl.BlockSpec((ntile, LANES, 4 * LANES), lambda s, g: (g, 0, 0)),
                pl.BlockSpec((ntile, 1, 4 * LANES), lambda s, g: (g, 0, 0)),
                pl.BlockSpec((2, cb), lambda s, g: (0, g))]
    args = [xr, xg, conv_w, conv_b.reshape(1, rw), wg, bg, lam]
    if state is not None:
        in_specs.append(pl.BlockSpec((None, 2, cb), lambda s, g: (s, 0, g)))
        args.append(state)
    out_specs = [pl.BlockSpec((t_len, cb), lambda s, g: (s, g))]
    out_shape = [jax.ShapeDtypeStruct((n_seq * t_len, rw), BF16)]
    if emit_state:
        out_specs.append(pl.BlockSpec((None, 2, cb), lambda s, g: (s, 0, g)))
        out_shape.append(jax.ShapeDtypeStruct((n_seq, 2, rw), F32))
    kern = functools.partial(_rglru_kernel, t_len=t_len, seg=seg, sub=sub, pitch=pitch, tc=tc,
                             has_state=state is not None, emit_state=emit_state)
    return pl.pallas_call(
        kern,
        grid=(n_seq, rw // cb),
        in_specs=in_specs,
        out_specs=out_specs,
        out_shape=out_shape,
        scratch_shapes=[pltpu.VMEM((2, ntile, SUBLANES * pitch, LANES), F32),
                        pltpu.VMEM((2, ntile, SUBLANES * pitch, LANES), F32),
                        pltpu.VMEM((2, ntile, SCAN_SPLIT * SUBLANES, LANES), F32)],
        compiler_params=_cparams(("arbitrary", "arbitrary")),
        name="rglru_state" if emit_state else "rglru",
    )(*args)


def _pack_bf16_pair(lo, hi):
    def rounded_bits(x):
        return pltpu.bitcast(x.astype(BF16).astype(F32), U32)
    return (rounded_bits(lo) >> 16) | rounded_bits(hi)


def _unpack_bf16_pair(w):
    lo = pltpu.bitcast(w << 16, F32).astype(BF16)
    hi = pltpu.bitcast(w & jnp.uint32(0xFFFF0000), F32).astype(BF16)
    return lo, hi


def _outproj_kernel(yac_ref, yal_ref, ybc_ref, ybl_ref, xp_ref, xs_ref, mod_ref, n2_ref, wo_ref, wr_ref, br_ref,
                    x1_ref, hp_ref, ridx_ref, rwt_ref,
                    *, nctx_tiles, ntok_tiles, tiles_per_lat, d, mw, n_groups, epg, sub_rows):
    i = pl.program_id(0)

    @pl.when(i == ntok_tiles)
    def _():
        x1_ref[...] = jnp.zeros_like(x1_ref)
        hp_ref[...] = jnp.zeros_like(hp_ref)
        ridx_ref[...] = jnp.zeros_like(ridx_ref)
        rwt_ref[...] = jnp.zeros_like(rwt_ref)

    def token_tile(x_ref, ya_ref, yb_ref):
        for r0 in range(0, x_ref.shape[0], sub_rows):
            token_rows(x_ref, ya_ref, yb_ref, r0)

    def token_rows(x_ref, ya_ref, yb_ref, r0):
        rows = slice(r0, r0 + sub_rows)
        x, ya, yb = x_ref[rows, :], ya_ref[rows, :], yb_ref[rows, :]
        row = _mod_row(i, nctx_tiles, tiles_per_lat)
        gate1 = mod_ref[pl.ds(row, 1), 2 * d:3 * d]
        shift2 = mod_ref[pl.ds(row, 1), 3 * d:4 * d]
        scale2 = mod_ref[pl.ds(row, 1), 4 * d:5 * d]
        y = (jnp.dot(ya, wo_ref[0:mw, :], preferred_element_type=F32)
             + jnp.dot(yb, wo_ref[mw:, :], preferred_element_type=F32))
        x1 = x + gate1 * y
        x1_ref[rows, :] = x1
        h2 = _modulated_norm(x1, n2_ref[...], shift2, scale2)
        half = d // 2
        packed = _pack_bf16_pair(h2[:, :half], h2[:, half:])
        trows = half // LANES
        for s in range(trows):
            hp_ref[pl.ds(r0 * trows + s, sub_rows, stride=trows), :] = packed[:, s * LANES:(s + 1) * LANES]

        lt = lax.dot_general(wr_ref[...], h2.astype(BF16), (((1,), (1,)), ((), ())),
                             preferred_element_type=F32) + br_ref[:, 0:1]
        gidx = lax.broadcasted_iota(I32, (SUBLANES, lt.shape[1]), 0)
        gl = jnp.where(gidx < n_groups, lt[0:SUBLANES], -jnp.inf)
        gmax = jnp.max(gl, axis=0, keepdims=True)
        grp = jnp.min(jnp.where(gl == gmax, gidx, n_groups), axis=0, keepdims=True)
        p_grp = 1.0 / jnp.sum(jnp.exp(gl - gmax), axis=0, keepdims=True)
        el = lt[SUBLANES:SUBLANES + epg]
        for g in range(1, n_groups):
            el = jnp.where(grp == g, lt[SUBLANES + g * epg:SUBLANES + (g + 1) * epg], el)
        eidx = lax.broadcasted_iota(I32, el.shape, 0)
        v1 = jnp.max(el, axis=0, keepdims=True)
        i1 = jnp.min(jnp.where(el == v1, eidx, epg), axis=0, keepdims=True)
        el2 = jnp.where(eidx == i1, -jnp.inf, el)
        v2 = jnp.max(el2, axis=0, keepdims=True)
        i2 = jnp.min(jnp.where(el2 == v2, eidx, epg), axis=0, keepdims=True)
        e2 = jnp.exp(v2 - v1)
        w1 = p_grp / (1.0 + e2)
        w2 = p_grp * e2 / (1.0 + e2)
        rid = lax.broadcasted_iota(I32, (SUBLANES, sub_rows), 0)
        ridx_ref[:, rows] = jnp.where(rid == 0, grp * epg + i1, jnp.where(rid == 1, grp * epg + i2, 0))
        rwt_ref[:, rows] = jnp.where(rid == 0, w1, jnp.where(rid == 1, w2, 0.0))

    @pl.when(i < nctx_tiles)
    def _():
        token_tile(xp_ref, yac_ref, ybc_ref)

    @pl.when((i >= nctx_tiles) & (i < ntok_tiles))
    def _():
        token_tile(xs_ref, yal_ref, ybl_ref)


def _out_proj(ya_c, ya_l, yb_c, yb_l, xp, xs, mod, norm2_w, w_out, wr, br, *, lat_seq, n_groups, epg):
    nc, d = xp.shape
    nl = xs.shape[0]
    nt = nc + nl
    mw = ya_c.shape[1]
    tm, sub_rows = 512, 256
    trows = (d // 2) // LANES
    nctx_tiles, ntok_tiles = nc // tm, nt // tm
    kern = functools.partial(_outproj_kernel, nctx_tiles=nctx_tiles, ntok_tiles=ntok_tiles,
                             tiles_per_lat=lat_seq // tm, d=d, mw=mw, n_groups=n_groups, epg=epg, sub_rows=sub_rows)
    ctx = lambda i: (jnp.minimum(i, nctx_tiles - 1), 0)
    lat = lambda i: (jnp.clip(i - nctx_tiles, 0, nl // tm - 1), 0)
    row = lambda i: (i, 0)
    const = lambda i: (0, 0)
    return pl.pallas_call(
        kern,
        grid=(ntok_tiles + 1,),
        in_specs=[pl.BlockSpec((tm, mw), ctx), pl.BlockSpec((tm, mw), lat),
                  pl.BlockSpec((tm, yb_c.shape[1]), ctx), pl.BlockSpec((tm, yb_c.shape[1]), lat),
                  pl.BlockSpec((tm, d), ctx), pl.BlockSpec((tm, d), lat),
                  pl.BlockSpec(mod.shape, const),
                  pl.BlockSpec((1, d), const),
                  pl.BlockSpec(w_out.shape, const, pipeline_mode=pl.Buffered(1)),
                  pl.BlockSpec(wr.shape, const),
                  pl.BlockSpec(br.shape, const)],
        out_specs=[pl.BlockSpec((tm, d), row), pl.BlockSpec((tm * trows, LANES), row),
                   pl.BlockSpec((SUBLANES, tm), lambda i: (0, i)), pl.BlockSpec((SUBLANES, tm), lambda i: (0, i))],
        out_shape=[jax.ShapeDtypeStruct((nt + tm, d), F32), jax.ShapeDtypeStruct(((nt + tm) * trows, LANES), U32),
                   jax.ShapeDtypeStruct((SUBLANES, nt + tm), I32), jax.ShapeDtypeStruct((SUBLANES, nt + tm), F32)],
        compiler_params=_cparams(("arbitrary",)),
        name="out_proj",
    )(ya_c, ya_l, yb_c, yb_l, xp, xs, mod, norm2_w.reshape(1, d), w_out, wr, br)


MOE_ROWS = 256


ROUTE_TILE = 512


def _rank_kernel(ridx_ref, tri_ref, rank_ref, cnt_ref, carry_scr, *, n_exp):
    @pl.when(pl.program_id(0) == 0)
    def _():
        carry_scr[...] = jnp.zeros_like(carry_scr)

    e = ridx_ref[...]
    tr = e.shape[1]
    eid = lax.broadcasted_iota(I32, (n_exp, tr), 0)
    carry = carry_scr[:, 0:1]
    ranks = []
    for kk in range(TOP_K):
        hit = eid == e[kk:kk + 1]
        cum = jnp.dot(jnp.where(hit, 1.0, 0.0).astype(BF16), tri_ref[...], preferred_element_type=F32)
        ranks.append(jnp.sum(jnp.where(hit, cum + carry, 0.0), axis=0, keepdims=True) - 1.0)
        carry = carry + cum[:, tr - 1:tr]
    carry_scr[...] = jnp.broadcast_to(carry, carry_scr.shape)
    cnt_ref[...] = jnp.broadcast_to(carry, cnt_ref.shape)
    rid = lax.broadcasted_iota(I32, rank_ref.shape, 0)
    rank_ref[...] = jnp.where(rid == 0, ranks[0], jnp.where(rid == 1, ranks[1], 0.0)).astype(I32)


def _dest_kernel(ridx_ref, rank_ref, pstart_ref, dest_ref, *, n_exp):
    e = ridx_ref[...]
    tr = e.shape[1]
    eid = lax.broadcasted_iota(I32, (n_exp, tr), 0)
    ps = pstart_ref[:, 0:1]
    rows = [jnp.sum(jnp.where(eid == e[kk:kk + 1], ps, 0.0), axis=0, keepdims=True) for kk in range(TOP_K)]
    rid = lax.broadcasted_iota(I32, dest_ref.shape, 0)
    dest_ref[...] = rank_ref[...] + jnp.where(rid == 0, rows[0], jnp.where(rid == 1, rows[1], 0.0)).astype(I32)


def _invert_kernel(dest_ref, fill_ref, rowtok_ref, sem):
    i = pl.program_id(0)
    tr = dest_ref.shape[0] // TOP_K

    @pl.when(i == 0)
    def _():
        fill = pltpu.make_async_copy(fill_ref, rowtok_ref, sem.at[0])
        fill.start()
        fill.wait()

    def body(r, c):
        for kk in range(TOP_K):
            rowtok_ref[dest_ref[kk * tr + r]] = i * tr + r
        return c
    lax.fori_loop(0, tr, body, 0, unroll=16)


def _routing(ridx, n_tok, n_exp, n_blocks, zero_row):
    tr = ROUTE_TILE
    steps = n_tok // tr
    tri = jnp.asarray(np.triu(np.ones((tr, tr), np.float32)), BF16)
    tile = pl.BlockSpec((SUBLANES, tr), lambda i: (0, i))
    cnt_spec = pl.BlockSpec((n_exp, LANES), lambda i: (0, 0))
    rank, cnt = pl.pallas_call(
        functools.partial(_rank_kernel, n_exp=n_exp),
        grid=(steps,),
        in_specs=[tile, pl.BlockSpec((tr, tr), lambda i: (0, 0))],
        out_specs=[tile, cnt_spec],
        out_shape=[jax.ShapeDtypeStruct((SUBLANES, n_tok), I32), jax.ShapeDtypeStruct((n_exp, LANES), F32)],
        scratch_shapes=[pltpu.VMEM((n_exp, LANES), F32)],
        compiler_params=_cparams(("arbitrary",)),
        name="route_rank",
    )(ridx, tri)
    counts = cnt[:, 0].astype(I32)
    padded = (counts + MOE_ROWS - 1) // MOE_ROWS * MOE_ROWS
    pad_end = jnp.cumsum(padded)
    pad_start = pad_end - padded
    dest = pl.pallas_call(
        functools.partial(_dest_kernel, n_exp=n_exp),
        grid=(steps,),
        in_specs=[tile, tile, cnt_spec],
        out_specs=tile,
        out_shape=jax.ShapeDtypeStruct((SUBLANES, n_tok), I32),
        compiler_params=_cparams(("arbitrary",)),
        name="route_dest",
    )(ridx, rank, jnp.broadcast_to(pad_start.astype(F32)[:, None], (n_exp, LANES)))
    row_tok = pl.pallas_call(
        _invert_kernel,
        grid=(steps,),
        in_specs=[pl.BlockSpec((TOP_K * tr,), lambda i: (i,), memory_space=pltpu.SMEM),
                  pl.BlockSpec(memory_space=pl.ANY)],
        out_specs=pl.BlockSpec(memory_space=pltpu.SMEM),
        out_shape=jax.ShapeDtypeStruct((n_blocks * MOE_ROWS,), I32),
        scratch_shapes=[pltpu.SemaphoreType.DMA((1,))],
        compiler_params=_cparams(("arbitrary",)),
        name="route_invert",
    )(dest[0:TOP_K].reshape(TOP_K, steps, tr).transpose(1, 0, 2).reshape(-1),
      jnp.full((n_blocks * MOE_ROWS,), zero_row, I32))
    blk_row0 = jnp.arange(n_blocks, dtype=I32) * MOE_ROWS
    blk_e = jnp.minimum(jnp.sum((pad_end[None, :] <= blk_row0[:, None]).astype(I32), axis=1), n_exp - 1)
    nused = (pad_end[-1:] // MOE_ROWS).astype(I32)
    return dest, row_tok, blk_e, nused


CAST_ROWS = 256
WEIGHT_DMA_SPLIT = 8


GATHER_DEPTH = 3


def _expert_kernel(blk_e_ref, nused_ref, tok_ref, nx1_ref, nx2_ref, src_ref, wg_hbm, wu_hbm, wd_hbm, y_ref,
                   xbuf, xsem, stage_g, stage_u, stage_d, bf_g, bf_u, bf_d, wsem, *, half, rows, trows):
    b = pl.program_id(0)
    nused = nused_ref[0]
    w_hbm = (wg_hbm, wu_hbm, wd_hbm)
    stage = (stage_g, stage_u, stage_d)
    wbf = (bf_g, bf_u, bf_d)

    def weight_copy(e, j):
        return pltpu.make_async_copy(w_hbm[j].at[e], stage[j], wsem.at[j])

    def start_weights(e, j):
        step = stage[j].shape[0] // WEIGHT_DMA_SPLIT
        for c in range(WEIGHT_DMA_SPLIT):
            sl = pl.ds(c * step, step)
            pltpu.make_async_copy(w_hbm[j].at[e, sl, :], stage[j].at[sl, :], wsem.at[j]).start(priority=1)

    def row_copy(tok, slot, r):
        src = src_ref.at[pl.ds(pl.multiple_of(tok * trows, trows), trows), :]
        return pltpu.make_async_copy(src, xbuf.at[slot, pl.ds(r * trows, trows), :], xsem.at[slot])

    @pl.when(b >= nused)
    def _():
        y_ref[...] = jnp.zeros_like(y_ref)

    @pl.when(b < nused)
    def _():
        e = blk_e_ref[b]

        @pl.when(b == 0)
        def _():
            for j in range(3):
                start_weights(e, j)
            def body(r, c):
                row_copy(tok_ref[r], 0, r).start()
                return c
            lax.fori_loop(0, rows, body, 0, unroll=8)

        @pl.when((b == 0) & (nused > 1))
        def _():
            def body(r, c):
                row_copy(nx1_ref[r], 1, r).start()
                return c
            lax.fori_loop(0, rows, body, 0, unroll=8)

        @pl.when((b == 0) | (blk_e_ref[jnp.maximum(b - 1, 0)] != e))
        def _():
            nb = lax.while_loop(lambda k: (k < nused) & (blk_e_ref[jnp.minimum(k, nused - 1)] == e),
                                lambda k: k + 1, b + 1)
            for j in range(3):
                weight_copy(e, j).wait()
                n_steps = stage[j].shape[0] // CAST_ROWS

                def cast(c, carry, j=j):
                    sl = pl.ds(pl.multiple_of(c * CAST_ROWS, CAST_ROWS), CAST_ROWS)
                    wbf[j][sl, :] = stage[j][sl, :].astype(BF16)
                    return carry
                lax.fori_loop(0, n_steps, cast, 0)

                @pl.when(nb < nused)
                def _(j=j):
                    start_weights(blk_e_ref[jnp.minimum(nb, nused - 1)], j)

        slot = b % GATHER_DEPTH
        pltpu.make_async_copy(src_ref.at[pl.ds(0, rows * trows), :], xbuf.at[slot], xsem.at[slot]).wait()

        def compute(prefetch):
            parts = [_unpack_bf16_pair(xbuf[slot, pl.ds(s, rows, stride=trows), :]) for s in range(trows)]
            lo = jnp.concatenate([p[0] for p in parts], axis=1)
            hi = jnp.concatenate([p[1] for p in parts], axis=1)
            if prefetch:
                ahead = (b + GATHER_DEPTH - 1) % GATHER_DEPTH
                for r in range(rows):
                    row_copy(nx2_ref[r], ahead, r).start()

            def up(w_ref):
                return (jnp.dot(lo, w_ref[0:half, :], preferred_element_type=F32)
                        + jnp.dot(hi, w_ref[half:, :], preferred_element_type=F32))

            g = up(wbf[0])
            h = ((g * _sigmoid(g)) * up(wbf[1])).astype(BF16)
            y = jnp.dot(h, wbf[2][...], preferred_element_type=F32)
            packed = _pack_bf16_pair(y[:, :half], y[:, half:])
            for s in range(trows):
                y_ref[pl.ds(s, rows, stride=trows), :] = packed[:, s * LANES:(s + 1) * LANES]

        @pl.when(b + GATHER_DEPTH - 1 < nused)
        def _():
            compute(True)

        @pl.when(b + GATHER_DEPTH - 1 >= nused)
        def _():
            compute(False)


def _experts(blk_e, nused, row_tok, src, wg, wu, wd, n_blocks):
    n_exp, d, ff = wg.shape
    any_spec = pl.BlockSpec(memory_space=pl.ANY)
    trows = (d // 2) // LANES
    tok = row_tok

    def tok_spec(ahead):
        return pl.BlockSpec((MOE_ROWS,), lambda b, be, nu: (jnp.minimum(b + ahead, n_blocks - 1),),
                            memory_space=pltpu.SMEM)

    return pl.pallas_call(
        functools.partial(_expert_kernel, half=d // 2, rows=MOE_ROWS, trows=trows),
        grid_spec=pltpu.PrefetchScalarGridSpec(
            num_scalar_prefetch=2,
            grid=(n_blocks,),
            in_specs=[tok_spec(0), tok_spec(1), tok_spec(2), any_spec, any_spec, any_spec, any_spec],
            out_specs=pl.BlockSpec((MOE_ROWS * trows, LANES), lambda b, be, nu: (b, 0)),
            scratch_shapes=[pltpu.VMEM((GATHER_DEPTH, MOE_ROWS * trows, LANES), U32),
                            pltpu.SemaphoreType.DMA((GATHER_DEPTH,)),
                            pltpu.VMEM((d, ff), F32), pltpu.VMEM((d, ff), F32), pltpu.VMEM((ff, d), F32),
                            pltpu.VMEM((d, ff), BF16), pltpu.VMEM((d, ff), BF16), pltpu.VMEM((ff, d), BF16),
                            pltpu.SemaphoreType.DMA((3,))]),
        out_shape=jax.ShapeDtypeStruct((n_blocks * MOE_ROWS * trows, LANES), U32),
        compiler_params=_cparams(("arbitrary",)),
        name="experts",
    )(blk_e, nused, tok, tok, tok, src, wg, wu, wd)


COMBINE_BUFS = 3


def _combine_kernel(dest_ref, x1_ref, wt_ref, mod_ref, fw_ref, y_ref, o_ref, *scratch,
                    tile0, n_all, tiles_per_seq, d, lat):
    i = pl.program_id(0)
    last = pl.num_programs(0) - 1
    tm = x1_ref.shape[0]
    bufs, sem = scratch[:-1], scratch[-1]
    nbuf = len(bufs)

    trows = (d // 2) // LANES

    def row_copy(tile, par, kk, r):
        row = dest_ref[kk * n_all + (tile0 + tile) * tm + r]
        src = y_ref.at[pl.ds(pl.multiple_of(row * trows, trows), trows), :]
        return pltpu.make_async_copy(src, bufs[par].at[kk, pl.ds(r * trows, trows), :], sem.at[par])

    for first in range(nbuf - 1):
        @pl.when((i == 0) & (first <= last))
        def _(first=first):
            def body(r, c):
                for kk in range(TOP_K):
                    row_copy(first, first, kk, r).start()
                return c
            lax.fori_loop(0, tm, body, 0, unroll=8)

    def step(par, prefetch):
        for kk in range(TOP_K):
            pltpu.make_async_copy(y_ref.at[pl.ds(0, tm * trows), :], bufs[par].at[kk], sem.at[par]).wait()
        if prefetch:
            for r in range(tm):
                for kk in range(TOP_K):
                    row_copy(i + nbuf - 1, (par + nbuf - 1) % nbuf, kk, r).start(priority=kk)
        row = (1 + i // tiles_per_seq) if lat else 0
        gate2 = mod_ref[pl.ds(row, 1), 5 * d:6 * d]
        wt = wt_ref[...]
        half = d // 2
        w0 = _row_to_col(wt[0:1], tm)
        w1 = _row_to_col(wt[1:2], tm)
        unpack = (lambda p: pltpu.bitcast(p << 16, F32),
                  lambda p: pltpu.bitcast(p & jnp.uint32(0xFFFF0000), F32))
        ssq = jnp.zeros((tm, 1), F32)
        for s in range(trows):
            p0 = bufs[par][0, pl.ds(s, tm, stride=trows), :]
            p1 = bufs[par][1, pl.ds(s, tm, stride=trows), :]
            for side in range(2):
                cols = slice(side * half + s * LANES, side * half + (s + 1) * LANES)
                x = x1_ref[:, cols] + gate2[:, cols] * (w0 * unpack[side](p0) + w1 * unpack[side](p1))
                ssq = ssq + jnp.sum(x * x, axis=-1, keepdims=True)
                o_ref[:, cols] = x
        scale = lax.rsqrt(ssq * (1.0 / d) + EPS)
        o_ref[...] = o_ref[...] * scale * fw_ref[...]

    for par in range(nbuf):
        @pl.when((i % nbuf == par) & (i + nbuf - 1 <= last))
        def _(par=par):
            step(par, True)

        @pl.when((i % nbuf == par) & (i + nbuf - 1 > last))
        def _(par=par):
            step(par, False)


def _combine(dest_flat, x1, rwt, mod, final_w, y_rows, *, row0, n_tok, seq_len, lat):
    d = x1.shape[1]
    tm = 256
    tile0 = row0 // tm
    trows = (d // 2) // LANES
    n_all = dest_flat.shape[0] // TOP_K
    kern = functools.partial(_combine_kernel, tile0=tile0, n_all=n_all, tiles_per_seq=seq_len // tm, d=d, lat=lat)
    return pl.pallas_call(
        kern,
        grid_spec=pltpu.PrefetchScalarGridSpec(
            num_scalar_prefetch=1,
            grid=(n_tok // tm,),
            in_specs=[pl.BlockSpec((tm, d), lambda i, dr: (tile0 + i, 0)),
                      pl.BlockSpec((SUBLANES, tm), lambda i, dr: (0, tile0 + i)),
                      pl.BlockSpec(mod.shape, lambda i, dr: (0, 0)),
                      pl.BlockSpec((1, d), lambda i, dr: (0, 0)),
                      pl.BlockSpec(memory_space=pl.ANY)],
            out_specs=pl.BlockSpec((tm, d), lambda i, dr: (i, 0)),
            scratch_shapes=[pltpu.VMEM((TOP_K, tm * trows, LANES), U32) for _ in range(COMBINE_BUFS)]
            + [pltpu.SemaphoreType.DMA((COMBINE_BUFS,))]),
        out_shape=jax.ShapeDtypeStruct((n_tok, d), F32),
        compiler_params=_cparams(("arbitrary",)),
        name="combine_lat" if lat else "combine_ctx",
    )(dest_flat, x1, rwt, mod, final_w.reshape(1, d), y_rows)


def _gate_layout(w_gates, b_gates, heads):
    d = w_gates.shape[0]
    w = w_gates.reshape(d, 4, heads).transpose(0, 2, 1)
    w = jnp.pad(w, ((0, 0), (0, 0), (0, SUBLANES - 4))).reshape(d, heads * SUBLANES)
    b = b_gates.reshape(4, heads).T
    b = jnp.pad(b, ((0, 0), (0, SUBLANES - 4))).reshape(1, heads * SUBLANES)
    padl = LANES - heads * SUBLANES
    return jnp.pad(w, ((0, 0), (0, padl))), jnp.pad(b, ((0, 0), (0, padl)))


def kernel(x_prompt, x_sample, state_mlstm_c, state_mlstm_n, state_mlstm_m, state_rglru_h, c, c_ctx, w_ada, b_ada,
           norm1_w, w_in, b_gates, conv_w, conv_b, rg_wa, rg_ba, rg_wx, rg_bx, rg_lambda, mlstm_norm_w, w_out,
           norm2_w, router_group_w, router_group_b, router_expert_w, router_expert_b, expert_w_gate, expert_w_up,
           expert_w_down, final_norm_w):
    n_req, seq, d = x_prompt.shape
    n_lat, lat_seq, _ = x_sample.shape
    depth = w_in.shape[0]
    assert depth == 1, "the token-axis plumbing below is written for the single-layer trunk"
    heads, dk, dv = state_mlstm_c.shape[3:]
    rw = state_rglru_h.shape[-1]
    nblk = rg_wa.shape[2]
    assert rw // nblk == LANES
    n_groups, epg = router_expert_w.shape[1], router_expert_w.shape[3]
    n_exp = n_groups * epg
    qk, mw = heads * dk, heads * dv
    nc, nl = n_req * seq, n_lat * lat_seq
    nt = nc + nl
    assert nc % lat_seq == 0 and n_lat + 1 <= SUBLANES
    l = 0

    xp = x_prompt.reshape(nc, d)
    xs = x_sample.reshape(nl, d)
    cvec = jnp.zeros((SUBLANES, d), F32).at[0].set(c_ctx).at[1:1 + n_lat].set(c)
    mod = _ada(cvec, w_ada[l], b_ada[l])

    w = w_in[l]
    g0 = 2 * qk + 2 * mw
    wgate, bgate = _gate_layout(w[:, g0:g0 + 4 * heads], b_gates[l], heads)
    w_cat = jnp.concatenate([w[:, :qk], w[:, 2 * qk:g0], wgate, w[:, g0 + 4 * heads:]], axis=1).astype(BF16)
    q, kt, v, o, gt, xr, xg = _in_proj(xp, xs, mod, norm1_w[l], w_cat, w[:, qk:2 * qk], bgate, lat_seq=lat_seq,
                                       heads=heads, dk=dk, dv=dv, rw=rw)

    mkw = dict(heads=heads, dk=dk, dv=dv)
    ya_c, new_c, new_n, new_m = _mlstm(q, kt, v, o, gt, mlstm_norm_w[l], row0=0, n_seq=n_req, t_len=seq,
                                       emit_state=True, **mkw)
    (ya_l,) = _mlstm(q, kt, v, o, gt, mlstm_norm_w[l], row0=nc, n_seq=n_lat, t_len=lat_seq,
                     state=(state_mlstm_c[:, l], state_mlstm_n[:, l], state_mlstm_m[:, l]), **mkw)

    wg = (0.5 * jnp.concatenate([rg_wa[l, 0], rg_wx[l, 0], rg_wa[l, 1], rg_wx[l, 1]], axis=-1)).astype(BF16)
    bg = 0.5 * jnp.concatenate([rg_ba[l, 0].reshape(nblk, 1, LANES), rg_bx[l, 0].reshape(nblk, 1, LANES),
                                rg_ba[l, 1].reshape(nblk, 1, LANES), rg_bx[l, 1].reshape(nblk, 1, LANES)], axis=-1)
    rargs = (xr, xg, conv_w[l], conv_b[l], wg, bg, rg_lambda[l])
    yb_c, new_h = _rglru(*rargs, row0=0, n_seq=n_req, t_len=seq, seg=seq, emit_state=True)
    (yb_l,) = _rglru(*rargs, row0=nc, n_seq=n_lat, t_len=lat_seq, seg=GRID_W, state=state_rglru_h[:, l])

    r_rows = -(-(SUBLANES + n_exp) // 16) * 16
    wr = jnp.zeros((r_rows, d), F32)
    wr = wr.at[0:n_groups].set(router_group_w[l].T)
    wr = wr.at[SUBLANES:SUBLANES + n_exp].set(router_expert_w[l].transpose(0, 2, 1).reshape(n_exp, d)).astype(BF16)
    br = jnp.zeros((r_rows, LANES), F32)
    br = br.at[0:n_groups, 0].set(router_group_b[l])
    br = br.at[SUBLANES:SUBLANES + n_exp, 0].set(router_expert_b[l].reshape(n_exp))
    x1, hp, ridx, rwt = _out_proj(ya_c, ya_l, yb_c, yb_l, xp, xs, mod, norm2_w[l], w_out[l].astype(BF16), wr, br,
                                  lat_seq=lat_seq, n_groups=n_groups, epg=epg)

    n_blocks = (nt * TOP_K) // MOE_ROWS + n_exp
    dest, row_tok, blk_e, nused = _routing(ridx, nt, n_exp, n_blocks, zero_row=nt)
    only_layer = lambda a: a.reshape(a.shape[1:])
    y_rows = _experts(blk_e, nused, row_tok, hp, only_layer(expert_w_gate), only_layer(expert_w_up),
                      only_layer(expert_w_down), n_blocks)

    dest_flat = dest[0:TOP_K].reshape(-1)
    y_prompt = _combine(dest_flat, x1, rwt, mod, final_norm_w, y_rows, row0=0, n_tok=nc, seq_len=seq, lat=False)
    y_sample = _combine(dest_flat, x1, rwt, mod, final_norm_w, y_rows, row0=nc, n_tok=nl, seq_len=lat_seq,
                        lat=True)

    return (y_prompt.reshape(n_req, seq, d), y_sample.reshape(n_lat, lat_seq, d),
            new_c[:, None], new_n.reshape(n_req, 1, 2, heads, dk), new_m.reshape(n_req, 1, 2, heads),
            new_h[:, None])
```

```python
import functools

import jax
import jax.numpy as jnp
import numpy as np
from jax import lax
from jax.experimental import pallas as pl
from jax.experimental.pallas import tpu as pltpu

F32 = jnp.float32
BF16 = jnp.bfloat16
I32 = jnp.int32
U32 = jnp.uint32

EPS = 1e-6
GRID_W = 64
CONV_LEFT = 2
RGLRU_C = 8.0
TOP_K = 2
LANES = 128
SUBLANES = 8
MLSTM_L = 256
NEG = -1e30
VMEM_LIMIT = 56 * 1024 * 1024

_HIGHEST = lax.Precision.HIGHEST


def _cparams(sem, vmem=VMEM_LIMIT):
    return pltpu.CompilerParams(dimension_semantics=sem, vmem_limit_bytes=vmem)


def _sigmoid(x):
    return 0.5 * jnp.tanh(0.5 * x) + 0.5


def _row_to_col(r, n):
    return jnp.broadcast_to(r, (LANES, n)).T


def _lane_tile(x, reps):
    return x if reps == 1 else jnp.concatenate([x] * reps, axis=1)


def _ada_kernel(c_ref, w_ref, b_ref, o_ref):
    c = c_ref[...]
    s = (c * _sigmoid(c)).astype(BF16)
    o_ref[...] = jnp.dot(s, w_ref[...].astype(BF16), preferred_element_type=F32) + b_ref[...]


def _ada(cvec, w_ada, b_ada):
    d, n = w_ada.shape
    tn = 1024 if n % 1024 == 0 else 512
    assert n % tn == 0
    return pl.pallas_call(
        _ada_kernel,
        grid=(n // tn,),
        in_specs=[pl.BlockSpec((SUBLANES, d), lambda j: (0, 0)),
                  pl.BlockSpec((d, tn), lambda j: (0, j)),
                  pl.BlockSpec((1, tn), lambda j: (0, j))],
        out_specs=pl.BlockSpec((SUBLANES, tn), lambda j: (0, j)),
        out_shape=jax.ShapeDtypeStruct((SUBLANES, n), F32),
        compiler_params=_cparams(("arbitrary",)),
        name="ada",
    )(cvec, w_ada, b_ada.reshape(1, n))


def _modulated_norm(x, w, shift, scale):
    ms = jnp.mean(x * x, axis=-1, keepdims=True)
    return (x * lax.rsqrt(ms + EPS) * w) * (1.0 + scale) + shift


def _mod_row(i, nctx_tiles, tiles_per_lat):
    return jnp.where(i < nctx_tiles, 0, 1 + (i - nctx_tiles) // tiles_per_lat)


def _inproj_kernel(xp_ref, xs_ref, mod_ref, n1_ref, w_ref, bg_ref,
                   q_ref, kt_ref, v_ref, o_ref, gt_ref, xr_ref, xg_ref, wkt_scr,
                   *, nctx_tiles, tiles_per_lat, d, qk, mw, rw, gh, qscale):
    i = pl.program_id(0)

    @pl.when(i == 0)
    def _():
        wkt_scr[...] = w_ref[:, qk:2 * qk].astype(F32).T.astype(BF16)

    x = jnp.where(i < nctx_tiles, xp_ref[...], xs_ref[...])
    row = _mod_row(i, nctx_tiles, tiles_per_lat)
    shift = mod_ref[pl.ds(row, 1), 0:d]
    scale = mod_ref[pl.ds(row, 1), d:2 * d]
    hb = _modulated_norm(x, n1_ref[...], shift, scale).astype(BF16)

    def proj(c0, width):
        return jnp.dot(hb, w_ref[:, c0:c0 + width], preferred_element_type=F32)

    c0 = 0
    q_ref[...] = (proj(c0, qk) * qscale).astype(BF16); c0 += qk
    kt_ref[...] = lax.dot_general(wkt_scr[...], hb, (((1,), (1,)), ((), ())),
                                  preferred_element_type=F32).astype(BF16)
    c0 += qk
    v_ref[...] = proj(c0, mw).astype(BF16); c0 += mw
    o_ref[...] = proj(c0, mw); c0 += mw
    zg = proj(c0, LANES) + bg_ref[...]; c0 += LANES
    lane = lax.broadcasted_iota(I32, zg.shape, 1)
    log_sig = jnp.minimum(zg, 0.0) - jnp.log1p(jnp.exp(-jnp.abs(zg)))
    zg = jnp.where(lane % 2 == 1, log_sig, zg)
    gt_ref[...] = zg.T[0:gh, :]
    xr_ref[...] = proj(c0, rw); c0 += rw
    xg_ref[...] = proj(c0, rw)


def _in_proj(xp, xs, mod, norm1_w, w_cat, bg, *, lat_seq, heads, dk, dv, rw):
    nc, d = xp.shape
    nl = xs.shape[0]
    nt = nc + nl
    tm = 256
    qk, mw, gh = heads * dk, heads * dv, heads * SUBLANES
    nctx_tiles = nc // tm
    kern = functools.partial(_inproj_kernel, nctx_tiles=nctx_tiles, tiles_per_lat=lat_seq // tm, d=d, qk=qk, mw=mw,
                             rw=rw, gh=gh, qscale=dk ** -0.5)
    row = lambda i: (i, 0)
    const = lambda i: (0, 0)
    return pl.pallas_call(
        kern,
        grid=(nt // tm,),
        in_specs=[pl.BlockSpec((tm, d), lambda i: (jnp.minimum(i, nctx_tiles - 1), 0)),
                  pl.BlockSpec((tm, d), lambda i: (jnp.maximum(i - nctx_tiles, 0), 0)),
                  pl.BlockSpec(mod.shape, const),
                  pl.BlockSpec((1, d), const),
                  pl.BlockSpec(w_cat.shape, const, pipeline_mode=pl.Buffered(1)),
                  pl.BlockSpec((1, LANES), const)],
        out_specs=[pl.BlockSpec((tm, qk), row), pl.BlockSpec((qk, tm), lambda i: (0, i)), pl.BlockSpec((tm, mw), row),
                   pl.BlockSpec((tm, mw), row), pl.BlockSpec((gh, tm), lambda i: (0, i)),
                   pl.BlockSpec((tm, rw), row), pl.BlockSpec((tm, rw), row)],
        out_shape=[jax.ShapeDtypeStruct((nt, qk), BF16), jax.ShapeDtypeStruct((qk, nt), BF16),
                   jax.ShapeDtypeStruct((nt, mw), BF16), jax.ShapeDtypeStruct((nt, mw), F32),
                   jax.ShapeDtypeStruct((gh, nt), F32),
                   jax.ShapeDtypeStruct((nt, rw), F32), jax.ShapeDtypeStruct((nt, rw), F32)],
        scratch_shapes=[pltpu.VMEM((qk, d), BF16)],
        compiler_params=_cparams(("arbitrary",)),
        name="in_proj",
    )(xp, xs, mod, norm1_w.reshape(1, d), w_cat, bg)


def _mlstm_kernel(*refs, t_len, dk, dv, has_state, emit_state):
    it = iter(refs)
    q_ref, kt_ref, v_ref, o_ref, gt_ref, nw_ref, tri_ref = (next(it) for _ in range(7))
    if has_state:
        c0_ref, n0_ref, m0_ref = (next(it) for _ in range(3))
    ya_ref = next(it)
    if emit_state:
        cn_ref, nn_ref, mn_ref = (next(it) for _ in range(3))
    hf_scr, hb_scr, c_scr, ma_scr, p_scr, w_scr, em_scr, kw_scr, dm_scr = (next(it) for _ in range(9))
    ln = MLSTM_L
    nchunks = t_len // ln
    assert ln % LANES == 0 and dk == LANES
    gsz = 2 if nchunks % 2 == 0 else 1
    ngroups = nchunks // gsz
    h_scr = (hf_scr, hb_scr)
    ext = dv + LANES
    lrep = ln // LANES

    for d in range(2):
        if has_state:
            c_scr[d, :, 0:dv] = c0_ref[d]
            c_scr[d, :, dv:ext] = _row_to_col(n0_ref[d], dk)
            ma_scr[d] = m0_ref[d]
        else:
            c_scr[d] = jnp.zeros((dk, ext), F32)
            ma_scr[d] = jnp.zeros((1, 1), F32)

    def chunk_start(d, j):
        return pl.multiple_of((j if d == 0 else nchunks - 1 - j) * ln, ln)

    def stage_a(d, j, slot):
        t0 = chunk_start(d, j)
        q = q_ref[pl.ds(t0, ln), :]
        kt = kt_ref[:, pl.ds(t0, ln)]
        g8 = gt_ref[:, pl.ds(t0, ln)]
        cum8 = jnp.dot(g8, tri_ref[d], precision=_HIGHEST, preferred_element_type=F32)
        valid = tri_ref[1 - d] > 0.5
        li = g8[2 * d:2 * d + 1]
        lf = g8[2 * d + 1:2 * d + 2]
        cum_row = cum8[2 * d + 1:2 * d + 2]
        total = jnp.sum(lf, axis=1, keepdims=True)
        a_row = li - cum_row
        cum_col = _row_to_col(cum_row, ln)
        m_prev = ma_scr[d]
        dmat = jnp.where(valid, _lane_tile(cum_col, lrep) + a_row, NEG)
        inter = cum_col + m_prev
        m_t = jnp.maximum(inter, jnp.max(dmat, axis=1, keepdims=True))
        s = jnp.dot(q, kt, preferred_element_type=F32)
        bank, par = slot
        p_scr[d, bank, par] = (s * jnp.exp(dmat - _lane_tile(m_t, lrep))).astype(BF16)
        w_scr[d, bank, par] = jnp.exp(inter - m_t)
        em_scr[d, bank, par] = jnp.exp(-m_t)
        g_row = total + a_row
        m_new = jnp.maximum(total + m_prev, jnp.max(g_row, axis=1, keepdims=True))
        kw_scr[d, bank, par] = (kt.astype(F32) * jnp.exp(g_row - m_new)).astype(BF16)
        decay = jnp.exp(total + m_prev - m_new)
        rid = lax.broadcasted_iota(I32, (SUBLANES, LANES), 0)
        dm_scr[d, bank, par] = jnp.where(rid == 0, decay, m_new)
        ma_scr[d] = m_new

    def stage_b(d, j, slot):
        t0 = chunk_start(d, j)
        q = q_ref[pl.ds(t0, ln), :]
        v_ext = jnp.concatenate([v_ref[pl.ds(t0, ln), :], jnp.ones((ln, LANES), BF16)], axis=1)
        c_st = c_scr[d]
        bank, par = slot
        full = (jnp.dot(p_scr[d, bank, par], v_ext, preferred_element_type=F32)
                + _lane_tile(w_scr[d, bank, par], ext // LANES) * jnp.dot(q, c_st.astype(BF16),
                                                                         preferred_element_type=F32))
        inv = 1.0 / jnp.maximum(jnp.abs(full[:, dv:ext]), em_scr[d, bank, par])
        h_scr[d][pl.ds(t0, ln), :] = full[:, 0:dv] * _lane_tile(inv, dv // LANES)
        decay = dm_scr[d, bank, par, 0:1, 0:1]
        c_scr[d] = decay * c_st + jnp.dot(kw_scr[d, bank, par], v_ext, preferred_element_type=F32)

    def group_a(jj, bank):
        for par in range(gsz):
            for d in range(2):
                stage_a(d, gsz * jj + par, (bank, par))

    def group_b(jj, bank):
        for par in range(gsz):
            for d in range(2):
                stage_b(d, gsz * jj + par, (bank, par))

    group_a(0, 0)

    def body(jj, carry):
        bank = jj % 2
        group_b(jj, bank)
        group_a(jj + 1, 1 - bank)
        return carry

    lax.fori_loop(0, ngroups - 1, body, 0)
    group_b(ngroups - 1, (ngroups - 1) % 2)

    def finish(j, carry):
        t0 = pl.multiple_of(j * ln, ln)
        hs = hf_scr[pl.ds(t0, ln), :] + hb_scr[pl.ds(t0, ln), :]
        ms = jnp.mean(hs * hs, axis=1, keepdims=True)
        y = hs * lax.rsqrt(ms + EPS) * nw_ref[...]
        ya_ref[pl.ds(t0, ln), :] = (_sigmoid(o_ref[pl.ds(t0, ln), :]) * y).astype(BF16)
        return carry

    lax.fori_loop(0, nchunks, finish, 0)

    if emit_state:
        for d in range(2):
            cn_ref[d] = c_scr[d, :, 0:dv]
            nn_ref[d] = c_scr[d, :, dv:ext].T[0:1, :]
            mn_ref[d] = dm_scr[d, (ngroups - 1) % 2, gsz - 1, 1:2, 0:1]


def _mlstm_tri():
    r = np.arange(MLSTM_L)
    fwd = (r[:, None] <= r[None, :]).astype(np.float32)
    return jnp.asarray(np.stack([fwd, fwd.T]))


def _mlstm(q, kt, v, o, gt, norm_w, *, row0, n_seq, t_len, heads, dk, dv, state=None, emit_state=False):
    blk0 = row0 // t_len
    tok = lambda s, h: (blk0 + s, h)
    in_specs = [pl.BlockSpec((t_len, dk), tok), pl.BlockSpec((dk, t_len), lambda s, h: (h, blk0 + s)),
                pl.BlockSpec((t_len, dv), tok), pl.BlockSpec((t_len, dv), tok),
                pl.BlockSpec((SUBLANES, t_len), lambda s, h: (h, blk0 + s)),
                pl.BlockSpec((None, 1, dv), lambda s, h: (h, 0, 0)),
                pl.BlockSpec((2, MLSTM_L, MLSTM_L), lambda s, h: (0, 0, 0))]
    args = [q, kt, v, o, gt, norm_w.reshape(heads, 1, dv), _mlstm_tri()]
    if state is not None:
        c0, n0, m0 = state
        in_specs += [pl.BlockSpec((None, 2, None, dk, dv), lambda s, h: (s, 0, h, 0, 0)),
                     pl.BlockSpec((None, 2, None, 1, dk), lambda s, h: (s, 0, h, 0, 0)),
                     pl.BlockSpec((None, 2, None, 1, 1), lambda s, h: (s, 0, h, 0, 0))]
        args += [c0, n0.reshape(n_seq, 2, heads, 1, dk), m0.reshape(n_seq, 2, heads, 1, 1)]
    out_specs = [pl.BlockSpec((t_len, dv), lambda s, h: (s, h))]
    out_shape = [jax.ShapeDtypeStruct((n_seq * t_len, heads * dv), BF16)]
    if emit_state:
        out_specs += [pl.BlockSpec((None, 2, None, dk, dv), lambda s, h: (s, 0, h, 0, 0)),
                      pl.BlockSpec((None, 2, None, 1, dk), lambda s, h: (s, 0, h, 0, 0)),
                      pl.BlockSpec((None, 2, None, 1, 1), lambda s, h: (s, 0, h, 0, 0))]
        out_shape += [jax.ShapeDtypeStruct((n_seq, 2, heads, dk, dv), F32),
                      jax.ShapeDtypeStruct((n_seq, 2, heads, 1, dk), F32),
                      jax.ShapeDtypeStruct((n_seq, 2, heads, 1, 1), F32)]
    kern = functools.partial(_mlstm_kernel, t_len=t_len, dk=dk, dv=dv, has_state=state is not None,
                             emit_state=emit_state)
    return pl.pallas_call(
        kern,
        grid=(n_seq, heads),
        in_specs=in_specs,
        out_specs=out_specs,
        out_shape=out_shape,
        scratch_shapes=[pltpu.VMEM((t_len, dv), F32), pltpu.VMEM((t_len, dv), F32),
                        pltpu.VMEM((2, dk, dv + LANES), F32), pltpu.VMEM((2, 1, 1), F32),
                        pltpu.VMEM((2, 2, 2, MLSTM_L, MLSTM_L), BF16), pltpu.VMEM((2, 2, 2, MLSTM_L, LANES), F32),
                        pltpu.VMEM((2, 2, 2, MLSTM_L, LANES), F32), pltpu.VMEM((2, 2, 2, dk, MLSTM_L), BF16),
                        pltpu.VMEM((2, 2, 2, SUBLANES, LANES), F32)],
        compiler_params=_cparams(("arbitrary", "arbitrary")),
        name="mlstm_state" if emit_state else "mlstm",
    )(*args)


def _gelu_tanh(x):
    return x * (0.5 * (1.0 + jnp.tanh(0.7978845608028654 * (x + 0.044715 * (x * x * x)))))


def _softplus(x):
    return jnp.maximum(x, 0.0) + jnp.log1p(jnp.exp(-jnp.abs(x)))


SCAN_SPLIT = 1
SCAN_GROUP = 4


def _rglru_kernel(*refs, t_len, seg, sub, pitch, tc, has_state, emit_state):
    it = iter(refs)
    xr_ref, xg_ref, cw_ref, cb_ref, wg_ref, bg_ref, lam_ref = (next(it) for _ in range(7))
    if has_state:
        h0_ref = next(it)
    yb_ref = next(it)
    if emit_state:
        hn_ref = next(it)
    a_scr, u_scr, cin_scr = (next(it) for _ in range(3))
    nchunks = t_len // tc
    piece = min(tc, sub)
    npieces = tc // piece
    ntile = xr_ref.shape[1] // LANES
    chains = [(d, lt) for d in range(2) for lt in range(ntile)]

    def scan_rows(t0, p):
        t = t0 + p * piece
        i = t // sub
        return pl.ds(pl.multiple_of(i * pitch + (t - i * sub), SUBLANES), piece), i

    ka = (-0.5 * RGLRU_C * 1.4426950408889634) * _softplus(-lam_ref[...])

    def gates(c, carry):
        t0 = pl.multiple_of(c * tc, tc)
        pos = lax.broadcasted_iota(I32, (tc, LANES), 0) % seg
        for lt in range(ntile):
            cols = slice(lt * LANES, (lt + 1) * LANES)
            x = xr_ref[pl.ds(t0, tc), cols]
            xc = cb_ref[:, cols] + cw_ref[CONV_LEFT:CONV_LEFT + 1, cols] * x
            for j in range(cw_ref.shape[0]):
                off = j - CONV_LEFT
                if off == 0:
                    continue
                shifted = pltpu.roll(x, (-off) % tc, 0)
                ok = (pos >= -off) if off < 0 else (pos < seg - off)
                xc = xc + cw_ref[j:j + 1, cols] * jnp.where(ok, shifted, 0.0)
            zh = jnp.dot(xc.astype(BF16), wg_ref[lt], preferred_element_type=F32) + bg_ref[lt]
            hx = 0.5 * xc
            for d in range(2):
                kd = ka[d:d + 1, cols]
                a = jnp.exp2(jnp.tanh(zh[:, (2 * d) * LANES:(2 * d + 1) * LANES]) * kd + kd)
                igx = hx * jnp.tanh(zh[:, (2 * d + 1) * LANES:(2 * d + 2) * LANES]) + hx
                u = jnp.sqrt(1.0 - a * a) * igx
                for p in range(npieces):
                    rows, _ = scan_rows(t0, p)
                    a_scr[d, lt, rows, :] = a[p * piece:(p + 1) * piece]
                    u_scr[d, lt, rows, :] = u[p * piece:(p + 1) * piece]
        return carry

    lax.fori_loop(0, nchunks, gates, 0)

    seg_len = sub // SCAN_SPLIT
    seg_chains = [(d, lt, s) for (d, lt) in chains for s in range(SCAN_SPLIT)]

    def scan(jg, carry):
        def rows(d, s, k):
            j = jg * SCAN_GROUP + k
            return pl.ds(s * seg_len + (j if d == 0 else seg_len - 1 - j), SUBLANES, stride=pitch)

        loaded = [[(a_scr[d, lt, rows(d, s, k), :], u_scr[d, lt, rows(d, s, k), :]) for k in range(SCAN_GROUP)]
                  for (d, lt, s) in seg_chains]
        out = []
        for (d, lt, s), (h, p), steps in zip(seg_chains, carry, loaded):
            for k, (a, u) in enumerate(steps):
                h = a * h + u
                p = a * p
                a_scr[d, lt, rows(d, s, k), :] = p
                u_scr[d, lt, rows(d, s, k), :] = h
            out.append((h, p))
        return tuple(out)

    zero = jnp.zeros((SUBLANES, LANES), F32)
    one = jnp.ones((SUBLANES, LANES), F32)
    assert seg_len % SCAN_GROUP == 0
    ends = dict(zip(seg_chains, lax.fori_loop(0, seg_len // SCAN_GROUP, scan, tuple((zero, one) for _ in seg_chains))))

    order = [(i, s) for i in range(SUBLANES) for s in range(SCAN_SPLIT)]
    for (d, lt) in chains:
        cols = slice(lt * LANES, (lt + 1) * LANES)
        cin = h0_ref[d:d + 1, cols] if has_state else jnp.zeros((1, LANES), F32)
        for (i, s) in (order if d == 0 else reversed(order)):
            h, p = ends[(d, lt, s)]
            cin_scr[d, lt, s * SUBLANES + i:s * SUBLANES + i + 1, :] = cin
            cin = h[i:i + 1] + p[i:i + 1] * cin
        if emit_state:
            hn_ref[d:d + 1, cols] = cin

    fpiece = min(tc, seg_len)

    def finish(c, carry):
        t0 = pl.multiple_of(c * tc, tc)
        for lt in range(ntile):
            cols = slice(lt * LANES, (lt + 1) * LANES)
            for p in range(tc // fpiece):
                t = t0 + p * fpiece
                i = t // sub
                local = t - i * sub
                rows = pl.ds(pl.multiple_of(i * pitch + local, SUBLANES), fpiece)
                crow = pl.ds((local // seg_len) * SUBLANES + i, 1)
                h = (u_scr[0, lt, rows, :] + a_scr[0, lt, rows, :] * cin_scr[0, lt, crow, :]
                     + u_scr[1, lt, rows, :] + a_scr[1, lt, rows, :] * cin_scr[1, lt, crow, :])
                nat = pl.ds(pl.multiple_of(t, SUBLANES), fpiece)
                yb_ref[nat, cols] = (h * _gelu_tanh(xg_ref[nat, cols])).astype(BF16)
        return carry

    lax.fori_loop(0, nchunks, finish, 0)


def _rglru(xr, xg, conv_w, conv_b, wg, bg, lam, *, row0, n_seq, t_len, seg, state=None, emit_state=False):
    rw = xr.shape[1]
    ntile = 2
    cb = ntile * LANES
    assert rw % cb == 0
    blk0 = row0 // t_len
    sub = t_len // SUBLANES
    pitch = sub + SUBLANES
    tc = 256
    tok = lambda s, g: (blk0 + s, g)
    in_specs = [pl.BlockSpec((t_len, cb), tok), pl.BlockSpec((t_len, cb), tok),
                pl.BlockSpec((conv_w.shape[0], cb), lambda s, g: (0, g)),
                pl.BlockSpec((1, cb), lambda s, g: (0, g)),
                pl.BlockSpec((ntile, LANES, 4 * LANES), lambda s, g: (g, 0, 0)),
                pl.BlockSpec((ntile, 1, 4 * LANES), lambda s, g: (g, 0, 0)),
                pl.BlockSpec((2, cb), lambda s, g: (0, g))]
    args = [xr, xg, conv_w, conv_b.reshape(1, rw), wg, bg, lam]
    if state is not None:
        in_specs.append(pl.BlockSpec((None, 2, cb), lambda s, g: (s, 0, g)))
        args.append(state)
    out_specs = [pl.BlockSpec((t_len, cb), lambda s, g: (s, g))]
    out_shape = [jax.ShapeDtypeStruct((n_seq * t_len, rw), BF16)]
    if emit_state:
        out_specs.append(pl.BlockSpec((None, 2, cb), lambda s, g: (s, 0, g)))
        out_shape.append(jax.ShapeDtypeStruct((n_seq, 2, rw), F32))
    kern = functools.partial(_rglru_kernel, t_len=t_len, seg=seg, sub=sub, pitch=pitch, tc=tc,
                             has_state=state is not None, emit_state=emit_state)
    return pl.pallas_call(
        kern,
        grid=(n_seq, rw // cb),
        in_specs=in_specs,
        out_specs=out_specs,
        out_shape=out_shape,
        scratch_shapes=[pltpu.VMEM((2, ntile, SUBLANES * pitch, LANES), F32),
                        pltpu.VMEM((2, ntile, SUBLANES * pitch, LANES), F32),
                        pltpu.VMEM((2, ntile, SCAN_SPLIT * SUBLANES, LANES), F32)],
        compiler_params=_cparams(("arbitrary", "arbitrary")),
        name="rglru_state" if emit_state else "rglru",
    )(*args)


def _pack_bf16_pair(lo, hi):
    def rounded_bits(x):
        return pltpu.bitcast(x.astype(BF16).astype(F32), U32)
    return (rounded_bits(lo) >> 16) | rounded_bits(hi)


def _unpack_bf16_pair(w):
    lo = pltpu.bitcast(w << 16, F32).astype(BF16)
    hi = pltpu.bitcast(w & jnp.uint32(0xFFFF0000), F32).astype(BF16)
    return lo, hi


def _outproj_kernel(yac_ref, yal_ref, ybc_ref, ybl_ref, xp_ref, xs_ref, mod_ref, n2_ref, wo_ref, wr_ref, br_ref,
                    x1_ref, hp_ref, ridx_ref, rwt_ref,
                    *, nctx_tiles, ntok_tiles, tiles_per_lat, d, mw, n_groups, epg, sub_rows):
    i = pl.program_id(0)

    @pl.when(i == ntok_tiles)
    def _():
        x1_ref[...] = jnp.zeros_like(x1_ref)
        hp_ref[...] = jnp.zeros_like(hp_ref)
        ridx_ref[...] = jnp.zeros_like(ridx_ref)
        rwt_ref[...] = jnp.zeros_like(rwt_ref)

    def token_tile(x_ref, ya_ref, yb_ref):
        for r0 in range(0, x_ref.shape[0], sub_rows):
            token_rows(x_ref, ya_ref, yb_ref, r0)

    def token_rows(x_ref, ya_ref, yb_ref, r0):
        rows = slice(r0, r0 + sub_rows)
        x, ya, yb = x_ref[rows, :], ya_ref[rows, :], yb_ref[rows, :]
        row = _mod_row(i, nctx_tiles, tiles_per_lat)
        gate1 = mod_ref[pl.ds(row, 1), 2 * d:3 * d]
        shift2 = mod_ref[pl.ds(row, 1), 3 * d:4 * d]
        scale2 = mod_ref[pl.ds(row, 1), 4 * d:5 * d]
        y = (jnp.dot(ya, wo_ref[0:mw, :], preferred_element_type=F32)
             + jnp.dot(yb, wo_ref[mw:, :], preferred_element_type=F32))
        x1 = x + gate1 * y
        x1_ref[rows, :] = x1
        h2 = _modulated_norm(x1, n2_ref[...], shift2, scale2)
        half = d // 2
        packed = _pack_bf16_pair(h2[:, :half], h2[:, half:])
        trows = half // LANES
        for s in range(trows):
            hp_ref[pl.ds(r0 * trows + s, sub_rows, stride=trows), :] = packed[:, s * LANES:(s + 1) * LANES]

        lt = lax.dot_general(wr_ref[...], h2.astype(BF16), (((1,), (1,)), ((), ())),
                             preferred_element_type=F32) + br_ref[:, 0:1]
        gidx = lax.broadcasted_iota(I32, (SUBLANES, lt.shape[1]), 0)
        gl = jnp.where(gidx < n_groups, lt[0:SUBLANES], -jnp.inf)
        gmax = jnp.max(gl, axis=0, keepdims=True)
        grp = jnp.min(jnp.where(gl == gmax, gidx, n_groups), axis=0, keepdims=True)
        p_grp = 1.0 / jnp.sum(jnp.exp(gl - gmax), axis=0, keepdims=True)
        el = lt[SUBLANES:SUBLANES + epg]
        for g in range(1, n_groups):
            el = jnp.where(grp == g, lt[SUBLANES + g * epg:SUBLANES + (g + 1) * epg], el)
        eidx = lax.broadcasted_iota(I32, el.shape, 0)
        v1 = jnp.max(el, axis=0, keepdims=True)
        i1 = jnp.min(jnp.where(el == v1, eidx, epg), axis=0, keepdims=True)
        el2 = jnp.where(eidx == i1, -jnp.inf, el)
        v2 = jnp.max(el2, axis=0, keepdims=True)
        i2 = jnp.min(jnp.where(el2 == v2, eidx, epg), axis=0, keepdims=True)
        e2 = jnp.exp(v2 - v1)
        w1 = p_grp / (1.0 + e2)
        w2 = p_grp * e2 / (1.0 + e2)
        rid = lax.broadcasted_iota(I32, (SUBLANES, sub_rows), 0)
        ridx_ref[:, rows] = jnp.where(rid == 0, grp * epg + i1, jnp.where(rid == 1, grp * epg + i2, 0))
        rwt_ref[:, rows] = jnp.where(rid == 0, w1, jnp.where(rid == 1, w2, 0.0))

    @pl.when(i < nctx_tiles)
    def _():
        token_tile(xp_ref, yac_ref, ybc_ref)

    @pl.when((i >= nctx_tiles) & (i < ntok_tiles))
    def _():
        token_tile(xs_ref, yal_ref, ybl_ref)


def _out_proj(ya_c, ya_l, yb_c, yb_l, xp, xs, mod, norm2_w, w_out, wr, br, *, lat_seq, n_groups, epg):
    nc, d = xp.shape
    nl = xs.shape[0]
    nt = nc + nl
    mw = ya_c.shape[1]
    tm, sub_rows = 512, 256
    trows = (d // 2) // LANES
    nctx_tiles, ntok_tiles = nc // tm, nt // tm
    kern = functools.partial(_outproj_kernel, nctx_tiles=nctx_tiles, ntok_tiles=ntok_tiles,
                             tiles_per_lat=lat_seq // tm, d=d, mw=mw, n_groups=n_groups, epg=epg, sub_rows=sub_rows)
    ctx = lambda i: (jnp.minimum(i, nctx_tiles - 1), 0)
    lat = lambda i: (jnp.clip(i - nctx_tiles, 0, nl // tm - 1), 0)
    row = lambda i: (i, 0)
    const = lambda i: (0, 0)
    return pl.pallas_call(
        kern,
        grid=(ntok_tiles + 1,),
        in_specs=[pl.BlockSpec((tm, mw), ctx), pl.BlockSpec((tm, mw), lat),
                  pl.BlockSpec((tm, yb_c.shape[1]), ctx), pl.BlockSpec((tm, yb_c.shape[1]), lat),
                  pl.BlockSpec((tm, d), ctx), pl.BlockSpec((tm, d), lat),
                  pl.BlockSpec(mod.shape, const),
                  pl.BlockSpec((1, d), const),
                  pl.BlockSpec(w_out.shape, const, pipeline_mode=pl.Buffered(1)),
                  pl.BlockSpec(wr.shape, const),
                  pl.BlockSpec(br.shape, const)],
        out_specs=[pl.BlockSpec((tm, d), row), pl.BlockSpec((tm * trows, LANES), row),
                   pl.BlockSpec((SUBLANES, tm), lambda i: (0, i)), pl.BlockSpec((SUBLANES, tm), lambda i: (0, i))],
        out_shape=[jax.ShapeDtypeStruct((nt + tm, d), F32), jax.ShapeDtypeStruct(((nt + tm) * trows, LANES), U32),
                   jax.ShapeDtypeStruct((SUBLANES, nt + tm), I32), jax.ShapeDtypeStruct((SUBLANES, nt + tm), F32)],
        compiler_params=_cparams(("arbitrary",)),
        name="out_proj",
    )(ya_c, ya_l, yb_c, yb_l, xp, xs, mod, norm2_w.reshape(1, d), w_out, wr, br)


MOE_ROWS = 256


ROUTE_TILE = 512


def _rank_kernel(ridx_ref, tri_ref, rank_ref, cnt_ref, carry_scr, *, n_exp):
    @pl.when(pl.program_id(0) == 0)
    def _():
        carry_scr[...] = jnp.zeros_like(carry_scr)

    e = ridx_ref[...]
    tr = e.shape[1]
    eid = lax.broadcasted_iota(I32, (n_exp, tr), 0)
    carry = carry_scr[:, 0:1]
    ranks = []
    for kk in range(TOP_K):
        hit = eid == e[kk:kk + 1]
        cum = jnp.dot(jnp.where(hit, 1.0, 0.0).astype(BF16), tri_ref[...], preferred_element_type=F32)
        ranks.append(jnp.sum(jnp.where(hit, cum + carry, 0.0), axis=0, keepdims=True) - 1.0)
        carry = carry + cum[:, tr - 1:tr]
    carry_scr[...] = jnp.broadcast_to(carry, carry_scr.shape)
    cnt_ref[...] = jnp.broadcast_to(carry, cnt_ref.shape)
    rid = lax.broadcasted_iota(I32, rank_ref.shape, 0)
    rank_ref[...] = jnp.where(rid == 0, ranks[0], jnp.where(rid == 1, ranks[1], 0.0)).astype(I32)


def _dest_kernel(ridx_ref, rank_ref, pstart_ref, dest_ref, *, n_exp):
    e = ridx_ref[...]
    tr = e.shape[1]
    eid = lax.broadcasted_iota(I32, (n_exp, tr), 0)
    ps = pstart_ref[:, 0:1]
    rows = [jnp.sum(jnp.where(eid == e[kk:kk + 1], ps, 0.0), axis=0, keepdims=True) for kk in range(TOP_K)]
    rid = lax.broadcasted_iota(I32, dest_ref.shape, 0)
    dest_ref[...] = rank_ref[...] + jnp.where(rid == 0, rows[0], jnp.where(rid == 1, rows[1], 0.0)).astype(I32)


def _invert_kernel(dest_ref, fill_ref, rowtok_ref, sem):
    i = pl.program_id(0)
    tr = dest_ref.shape[0] // TOP_K

    @pl.when(i == 0)
    def _():
        fill = pltpu.make_async_copy(fill_ref, rowtok_ref, sem.at[0])
        fill.start()
        fill.wait()

    def body(r, c):
        for kk in range(TOP_K):
            rowtok_ref[dest_ref[kk * tr + r]] = i * tr + r
        return c
    lax.fori_loop(0, tr, body, 0, unroll=16)


def _routing(ridx, n_tok, n_exp, n_blocks, zero_row):
    tr = ROUTE_TILE
    steps = n_tok // tr
    tri = jnp.asarray(np.triu(np.ones((tr, tr), np.float32)), BF16)
    tile = pl.BlockSpec((SUBLANES, tr), lambda i: (0, i))
    cnt_spec = pl.BlockSpec((n_exp, LANES), lambda i: (0, 0))
    rank, cnt = pl.pallas_call(
        functools.partial(_rank_kernel, n_exp=n_exp),
        grid=(steps,),
        in_specs=[tile, pl.BlockSpec((tr, tr), lambda i: (0, 0))],
        out_specs=[tile, cnt_spec],
        out_shape=[jax.ShapeDtypeStruct((SUBLANES, n_tok), I32), jax.ShapeDtypeStruct((n_exp, LANES), F32)],
        scratch_shapes=[pltpu.VMEM((n_exp, LANES), F32)],
        compiler_params=_cparams(("arbitrary",)),
        name="route_rank",
    )(ridx, tri)
    counts = cnt[:, 0].astype(I32)
    padded = (counts + MOE_ROWS - 1) // MOE_ROWS * MOE_ROWS
    pad_end = jnp.cumsum(padded)
    pad_start = pad_end - padded
    dest = pl.pallas_call(
        functools.partial(_dest_kernel, n_exp=n_exp),
        grid=(steps,),
        in_specs=[tile, tile, cnt_spec],
        out_specs=tile,
        out_shape=jax.ShapeDtypeStruct((SUBLANES, n_tok), I32),
        compiler_params=_cparams(("arbitrary",)),
        name="route_dest",
    )(ridx, rank, jnp.broadcast_to(pad_start.astype(F32)[:, None], (n_exp, LANES)))
    row_tok = pl.pallas_call(
        _invert_kernel,
        grid=(steps,),
        in_specs=[pl.BlockSpec((TOP_K * tr,), lambda i: (i,), memory_space=pltpu.SMEM),
                  pl.BlockSpec(memory_space=pl.ANY)],
        out_specs=pl.BlockSpec(memory_space=pltpu.SMEM),
        out_shape=jax.ShapeDtypeStruct((n_blocks * MOE_ROWS,), I32),
        scratch_shapes=[pltpu.SemaphoreType.DMA((1,))],
        compiler_params=_cparams(("arbitrary",)),
        name="route_invert",
    )(dest[0:TOP_K].reshape(TOP_K, steps, tr).transpose(1, 0, 2).reshape(-1),
      jnp.full((n_blocks * MOE_ROWS,), zero_row, I32))
    blk_row0 = jnp.arange(n_blocks, dtype=I32) * MOE_ROWS
    blk_e = jnp.minimum(jnp.sum((pad_end[None, :] <= blk_row0[:, None]).astype(I32), axis=1), n_exp - 1)
    nused = (pad_end[-1:] // MOE_ROWS).astype(I32)
    return dest, row_tok, blk_e, nused


CAST_ROWS = 256
WEIGHT_DMA_SPLIT = 8


GATHER_DEPTH = 4


def _expert_kernel(blk_e_ref, nused_ref, *refs, half, rows, trows):
    tok_refs = refs[:GATHER_DEPTH]
    (src_ref, wg_hbm, wu_hbm, wd_hbm, y_ref, xbuf, xsem, stage_g, stage_u, stage_d, bf_g, bf_u, bf_d,
     wsem) = refs[GATHER_DEPTH:]
    b = pl.program_id(0)
    nused = nused_ref[0]
    w_hbm = (wg_hbm, wu_hbm, wd_hbm)
    stage = (stage_g, stage_u, stage_d)
    wbf = (bf_g, bf_u, bf_d)

    def weight_copy(e, j):
        return pltpu.make_async_copy(w_hbm[j].at[e], stage[j], wsem.at[j])

    def start_weights(e, j):
        step = stage[j].shape[0] // WEIGHT_DMA_SPLIT
        for c in range(WEIGHT_DMA_SPLIT):
            sl = pl.ds(c * step, step)
            pltpu.make_async_copy(w_hbm[j].at[e, sl, :], stage[j].at[sl, :], wsem.at[j]).start(priority=1)

    def row_copy(tok, slot, r):
        src = src_ref.at[pl.ds(pl.multiple_of(tok * trows, trows), trows), :]
        return pltpu.make_async_copy(src, xbuf.at[slot, pl.ds(r * trows, trows), :], xsem.at[slot])

    @pl.when(b >= nused)
    def _():
        y_ref[...] = jnp.zeros_like(y_ref)

    @pl.when(b < nused)
    def _():
        e = blk_e_ref[b]

        @pl.when(b == 0)
        def _():
            for j in range(3):
                start_weights(e, j)

        for first in range(GATHER_DEPTH - 1):
            @pl.when((b == 0) & (nused > first))
            def _(first=first):
                def body(r, c):
                    row_copy(tok_refs[first][r], first, r).start()
                    return c
                lax.fori_loop(0, rows, body, 0, unroll=8)

        @pl.when((b == 0) | (blk_e_ref[jnp.maximum(b - 1, 0)] != e))
        def _():
            nb = lax.while_loop(lambda k: (k < nused) & (blk_e_ref[jnp.minimum(k, nused - 1)] == e),
                                lambda k: k + 1, b + 1)
            for j in range(3):
                weight_copy(e, j).wait()
                n_steps = stage[j].shape[0] // CAST_ROWS

                def cast(c, carry, j=j):
                    sl = pl.ds(pl.multiple_of(c * CAST_ROWS, CAST_ROWS), CAST_ROWS)
                    wbf[j][sl, :] = stage[j][sl, :].astype(BF16)
                    return carry
                lax.fori_loop(0, n_steps, cast, 0)

                @pl.when(nb < nused)
                def _(j=j):
                    start_weights(blk_e_ref[jnp.minimum(nb, nused - 1)], j)

        slot = b % GATHER_DEPTH
        pltpu.make_async_copy(src_ref.at[pl.ds(0, rows * trows), :], xbuf.at[slot], xsem.at[slot]).wait()

        def compute(prefetch):
            parts = [_unpack_bf16_pair(xbuf[slot, pl.ds(s, rows, stride=trows), :]) for s in range(trows)]
            lo = jnp.concatenate([p[0] for p in parts], axis=1)
            hi = jnp.concatenate([p[1] for p in parts], axis=1)
            if prefetch:
                ahead = (b + GATHER_DEPTH - 1) % GATHER_DEPTH
                for r in range(rows):
                    row_copy(tok_refs[GATHER_DEPTH - 1][r], ahead, r).start()

            def up(w_ref):
                return (jnp.dot(lo, w_ref[0:half, :], preferred_element_type=F32)
                        + jnp.dot(hi, w_ref[half:, :], preferred_element_type=F32))

            g = up(wbf[0])
            h = ((g * _sigmoid(g)) * up(wbf[1])).astype(BF16)
            y = jnp.dot(h, wbf[2][...], preferred_element_type=F32)
            packed = _pack_bf16_pair(y[:, :half], y[:, half:])
            for s in range(trows):
                y_ref[pl.ds(s, rows, stride=trows), :] = packed[:, s * LANES:(s + 1) * LANES]

        @pl.when(b + GATHER_DEPTH - 1 < nused)
        def _():
            compute(True)

        @pl.when(b + GATHER_DEPTH - 1 >= nused)
        def _():
            compute(False)


def _experts(blk_e, nused, row_tok, src, wg, wu, wd, n_blocks):
    n_exp, d, ff = wg.shape
    any_spec = pl.BlockSpec(memory_space=pl.ANY)
    trows = (d // 2) // LANES
    def tok_spec(ahead):
        return pl.BlockSpec((MOE_ROWS,), lambda b, be, nu: (jnp.minimum(b + ahead, n_blocks - 1),),
                            memory_space=pltpu.SMEM)

    return pl.pallas_call(
        functools.partial(_expert_kernel, half=d // 2, rows=MOE_ROWS, trows=trows),
        grid_spec=pltpu.PrefetchScalarGridSpec(
            num_scalar_prefetch=2,
            grid=(n_blocks,),
            in_specs=[tok_spec(a) for a in range(GATHER_DEPTH)] + [any_spec, any_spec, any_spec, any_spec],
            out_specs=pl.BlockSpec((MOE_ROWS * trows, LANES), lambda b, be, nu: (b, 0)),
            scratch_shapes=[pltpu.VMEM((GATHER_DEPTH, MOE_ROWS * trows, LANES), U32),
                            pltpu.SemaphoreType.DMA((GATHER_DEPTH,)),
                            pltpu.VMEM((d, ff), F32), pltpu.VMEM((d, ff), F32), pltpu.VMEM((ff, d), F32),
                            pltpu.VMEM((d, ff), BF16), pltpu.VMEM((d, ff), BF16), pltpu.VMEM((ff, d), BF16),
                            pltpu.SemaphoreType.DMA((3,))]),
        out_shape=jax.ShapeDtypeStruct((n_blocks * MOE_ROWS * trows, LANES), U32),
        compiler_params=_cparams(("arbitrary",)),
        name="experts",
    )(blk_e, nused, *([row_tok] * GATHER_DEPTH), src, wg, wu, wd)


COMBINE_BUFS = 3


def _combine_kernel(dest_ref, x1_ref, wt_ref, mod_ref, fw_ref, y_ref, o_ref, *scratch,
                    tile0, n_all, tiles_per_seq, d, lat):
    i = pl.program_id(0)
    last = pl.num_programs(0) - 1
    tm = x1_ref.shape[0]
    bufs, sem = scratch[:-1], scratch[-1]
    nbuf = len(bufs)

    trows = (d // 2) // LANES

    def row_copy(tile, par, kk, r):
        row = dest_ref[kk * n_all + (tile0 + tile) * tm + r]
        src = y_ref.at[pl.ds(pl.multiple_of(row * trows, trows), trows), :]
        return pltpu.make_async_copy(src, bufs[par].at[kk, pl.ds(r * trows, trows), :], sem.at[par])

    for first in range(nbuf - 1):
        @pl.when((i == 0) & (first <= last))
        def _(first=first):
            def body(r, c):
                for kk in range(TOP_K):
                    row_copy(first, first, kk, r).start()
                return c
            lax.fori_loop(0, tm, body, 0, unroll=8)

    def step(par, prefetch):
        for kk in range(TOP_K):
            pltpu.make_async_copy(y_ref.at[pl.ds(0, tm * trows), :], bufs[par].at[kk], sem.at[par]).wait()
        if prefetch:
            for r in range(tm):
                for kk in range(TOP_K):
                    row_copy(i + nbuf - 1, (par + nbuf - 1) % nbuf, kk, r).start(priority=kk)
        row = (1 + i // tiles_per_seq) if lat else 0
        gate2 = mod_ref[pl.ds(row, 1), 5 * d:6 * d]
        wt = wt_ref[...]
        half = d // 2
        w0 = _row_to_col(wt[0:1], tm)
        w1 = _row_to_col(wt[1:2], tm)
        unpack = (lambda p: pltpu.bitcast(p << 16, F32),
                  lambda p: pltpu.bitcast(p & jnp.uint32(0xFFFF0000), F32))
        ssq = jnp.zeros((tm, 1), F32)
        for s in range(trows):
            p0 = bufs[par][0, pl.ds(s, tm, stride=trows), :]
            p1 = bufs[par][1, pl.ds(s, tm, stride=trows), :]
            for side in range(2):
                cols = slice(side * half + s * LANES, side * half + (s + 1) * LANES)
                x = x1_ref[:, cols] + gate2[:, cols] * (w0 * unpack[side](p0) + w1 * unpack[side](p1))
                ssq = ssq + jnp.sum(x * x, axis=-1, keepdims=True)
                o_ref[:, cols] = x
        scale = lax.rsqrt(ssq * (1.0 / d) + EPS)
        o_ref[...] = o_ref[...] * scale * fw_ref[...]

    for par in range(nbuf):
        @pl.when((i % nbuf == par) & (i + nbuf - 1 <= last))
        def _(par=par):
            step(par, True)

        @pl.when((i % nbuf == par) & (i + nbuf - 1 > last))
        def _(par=par):
            step(par, False)


def _combine(dest_flat, x1, rwt, mod, final_w, y_rows, *, row0, n_tok, seq_len, lat):
    d = x1.shape[1]
    tm = 256
    tile0 = row0 // tm
    trows = (d // 2) // LANES
    n_all = dest_flat.shape[0] // TOP_K
    kern = functools.partial(_combine_kernel, tile0=tile0, n_all=n_all, tiles_per_seq=seq_len // tm, d=d, lat=lat)
    return pl.pallas_call(
        kern,
        grid_spec=pltpu.PrefetchScalarGridSpec(
            num_scalar_prefetch=1,
            grid=(n_tok // tm,),
            in_specs=[pl.BlockSpec((tm, d), lambda i, dr: (tile0 + i, 0)),
                      pl.BlockSpec((SUBLANES, tm), lambda i, dr: (0, tile0 + i)),
                      pl.BlockSpec(mod.shape, lambda i, dr: (0, 0)),
                      pl.BlockSpec((1, d), lambda i, dr: (0, 0)),
                      pl.BlockSpec(memory_space=pl.ANY)],
            out_specs=pl.BlockSpec((tm, d), lambda i, dr: (i, 0)),
            scratch_shapes=[pltpu.VMEM((TOP_K, tm * trows, LANES), U32) for _ in range(COMBINE_BUFS)]
            + [pltpu.SemaphoreType.DMA((COMBINE_BUFS,))]),
        out_shape=jax.ShapeDtypeStruct((n_tok, d), F32),
        compiler_params=_cparams(("arbitrary",)),
        name="combine_lat" if lat else "combine_ctx",
    )(dest_flat, x1, rwt, mod, final_w.reshape(1, d), y_rows)


def _gate_layout(w_gates, b_gates, heads):
    d = w_gates.shape[0]
    w = w_gates.reshape(d, 4, heads).transpose(0, 2, 1)
    w = jnp.pad(w, ((0, 0), (0, 0), (0, SUBLANES - 4))).reshape(d, heads * SUBLANES)
    b = b_gates.reshape(4, heads).T
    b = jnp.pad(b, ((0, 0), (0, SUBLANES - 4))).reshape(1, heads * SUBLANES)
    padl = LANES - heads * SUBLANES
    return jnp.pad(w, ((0, 0), (0, padl))), jnp.pad(b, ((0, 0), (0, padl)))


def kernel(x_prompt, x_sample, state_mlstm_c, state_mlstm_n, state_mlstm_m, state_rglru_h, c, c_ctx, w_ada, b_ada,
           norm1_w, w_in, b_gates, conv_w, conv_b, rg_wa, rg_ba, rg_wx, rg_bx, rg_lambda, mlstm_norm_w, w_out,
           norm2_w, router_group_w, router_group_b, router_expert_w, router_expert_b, expert_w_gate, expert_w_up,
           expert_w_down, final_norm_w):
    n_req, seq, d = x_prompt.shape
    n_lat, lat_seq, _ = x_sample.shape
    depth = w_in.shape[0]
    assert depth == 1, "the token-axis plumbing below is written for the single-layer trunk"
    heads, dk, dv = state_mlstm_c.shape[3:]
    rw = state_rglru_h.shape[-1]
    nblk = rg_wa.shape[2]
    assert rw // nblk == LANES
    n_groups, epg = router_expert_w.shape[1], router_expert_w.shape[3]
    n_exp = n_groups * epg
    qk, mw = heads * dk, heads * dv
    nc, nl = n_req * seq, n_lat * lat_seq
    nt = nc + nl
    assert nc % lat_seq == 0 and n_lat + 1 <= SUBLANES
    l = 0

    xp = x_prompt.reshape(nc, d)
    xs = x_sample.reshape(nl, d)
    cvec = jnp.zeros((SUBLANES, d), F32).at[0].set(c_ctx).at[1:1 + n_lat].set(c)
    mod = _ada(cvec, w_ada[l], b_ada[l])

    w = w_in[l]
    g0 = 2 * qk + 2 * mw
    wgate, bgate = _gate_layout(w[:, g0:g0 + 4 * heads], b_gates[l], heads)
    w_cat = jnp.concatenate([w[:, :g0], wgate, w[:, g0 + 4 * heads:]], axis=1).astype(BF16)
    q, kt, v, o, gt, xr, xg = _in_proj(xp, xs, mod, norm1_w[l], w_cat, bgate, lat_seq=lat_seq, heads=heads, dk=dk,
                                       dv=dv, rw=rw)

    mkw = dict(heads=heads, dk=dk, dv=dv)
    ya_c, new_c, new_n, new_m = _mlstm(q, kt, v, o, gt, mlstm_norm_w[l], row0=0, n_seq=n_req, t_len=seq,
                                       emit_state=True, **mkw)
    (ya_l,) = _mlstm(q, kt, v, o, gt, mlstm_norm_w[l], row0=nc, n_seq=n_lat, t_len=lat_seq,
                     state=(state_mlstm_c[:, l], state_mlstm_n[:, l], state_mlstm_m[:, l]), **mkw)

    wg = (0.5 * jnp.concatenate([rg_wa[l, 0], rg_wx[l, 0], rg_wa[l, 1], rg_wx[l, 1]], axis=-1)).astype(BF16)
    bg = 0.5 * jnp.concatenate([rg_ba[l, 0].reshape(nblk, 1, LANES), rg_bx[l, 0].reshape(nblk, 1, LANES),
                                rg_ba[l, 1].reshape(nblk, 1, LANES), rg_bx[l, 1].reshape(nblk, 1, LANES)], axis=-1)
    rargs = (xr, xg, conv_w[l], conv_b[l], wg, bg, rg_lambda[l])
    yb_c, new_h = _rglru(*rargs, row0=0, n_seq=n_req, t_len=seq, seg=seq, emit_state=True)
    (yb_l,) = _rglru(*rargs, row0=nc, n_seq=n_lat, t_len=lat_seq, seg=GRID_W, state=state_rglru_h[:, l])

    r_rows = -(-(SUBLANES + n_exp) // 16) * 16
    wr = jnp.zeros((r_rows, d), F32)
    wr = wr.at[0:n_groups].set(router_group_w[l].T)
    wr = wr.at[SUBLANES:SUBLANES + n_exp].set(router_expert_w[l].transpose(0, 2, 1).reshape(n_exp, d)).astype(BF16)
    br = jnp.zeros((r_rows, LANES), F32)
    br = br.at[0:n_groups, 0].set(router_group_b[l])
    br = br.at[SUBLANES:SUBLANES + n_exp, 0].set(router_expert_b[l].reshape(n_exp))
    x1, hp, ridx, rwt = _out_proj(ya_c, ya_l, yb_c, yb_l, xp, xs, mod, norm2_w[l], w_out[l].astype(BF16), wr, br,
                                  lat_seq=lat_seq, n_groups=n_groups, epg=epg)

    n_blocks = (nt * TOP_K) // MOE_ROWS + n_exp
    dest, row_tok, blk_e, nused = _routing(ridx, nt, n_exp, n_blocks, zero_row=nt)
    only_layer = lambda a: a.reshape(a.shape[1:])
    y_rows = _experts(blk_e, nused, row_tok, hp, only_layer(expert_w_gate), only_layer(expert_w_up),
                      only_layer(expert_w_down), n_blocks)

    dest_flat = dest[0:TOP_K].reshape(-1)
    y_prompt = _combine(dest_flat, x1, rwt, mod, final_norm_w, y_rows, row0=0, n_tok=nc, seq_len=seq, lat=False)
    y_sample = _combine(dest_flat, x1, rwt, mod, final_norm_w, y_rows, row0=nc, n_tok=nl, seq_len=lat_seq,
                        lat=True)

    return (y_prompt.reshape(n_req, seq, d), y_sample.reshape(n_lat, lat_seq, d),
            new_c[:, None], new_n.reshape(n_req, 1, 2, heads, dk), new_m.reshape(n_req, 1, 2, heads),
            new_h[:, None])
```

```python
import functools

import jax
import jax.numpy as jnp
import numpy as np
from jax import lax
from jax.experimental import pallas as pl
from jax.experimental.pallas import tpu as pltpu

F32 = jnp.float32
BF16 = jnp.bfloat16
I32 = jnp.int32
U32 = jnp.uint32

EPS = 1e-6
GRID_W = 64
CONV_LEFT = 2
RGLRU_C = 8.0
TOP_K = 2
LANES = 128
SUBLANES = 8
MLSTM_L = 256
NEG = -1e30
VMEM_LIMIT = 56 * 1024 * 1024

_HIGHEST = lax.Precision.HIGHEST


def _cparams(sem, vmem=VMEM_LIMIT):
    return pltpu.CompilerParams(dimension_semantics=sem, vmem_limit_bytes=vmem)


def _sigmoid(x):
    return 0.5 * jnp.tanh(0.5 * x) + 0.5


def _row_to_col(r, n):
    return jnp.broadcast_to(r, (LANES, n)).T


def _lane_tile(x, reps):
    return x if reps == 1 else jnp.concatenate([x] * reps, axis=1)


def _ada_kernel(c_ref, w_ref, b_ref, o_ref):
    c = c_ref[...]
    s = (c * _sigmoid(c)).astype(BF16)
    o_ref[...] = jnp.dot(s, w_ref[...].astype(BF16), preferred_element_type=F32) + b_ref[...]


def _ada(cvec, w_ada, b_ada):
    d, n = w_ada.shape
    tn = 1024 if n % 1024 == 0 else 512
    assert n % tn == 0
    return pl.pallas_call(
        _ada_kernel,
        grid=(n // tn,),
        in_specs=[pl.BlockSpec((SUBLANES, d), lambda j: (0, 0)),
                  pl.BlockSpec((d, tn), lambda j: (0, j)),
                  pl.BlockSpec((1, tn), lambda j: (0, j))],
        out_specs=pl.BlockSpec((SUBLANES, tn), lambda j: (0, j)),
        out_shape=jax.ShapeDtypeStruct((SUBLANES, n), F32),
        compiler_params=_cparams(("arbitrary",)),
        name="ada",
    )(cvec, w_ada, b_ada.reshape(1, n))


def _modulated_norm(x, w, shift, scale):
    ms = jnp.mean(x * x, axis=-1, keepdims=True)
    return (x * lax.rsqrt(ms + EPS) * w) * (1.0 + scale) + shift


def _mod_row(i, nctx_tiles, tiles_per_lat):
    return jnp.where(i < nctx_tiles, 0, 1 + (i - nctx_tiles) // tiles_per_lat)


def _inproj_kernel(xp_ref, xs_ref, mod_ref, n1_ref, w_ref, bg_ref,
                   q_ref, kt_ref, v_ref, o_ref, gt_ref, xr_ref, xg_ref, wkt_scr,
                   *, nctx_tiles, tiles_per_lat, d, qk, mw, rw, gh, qscale):
    i = pl.program_id(0)

    @pl.when(i == 0)
    def _():
        wkt_scr[...] = w_ref[:, qk:2 * qk].astype(F32).T.astype(BF16)

    x = jnp.where(i < nctx_tiles, xp_ref[...], xs_ref[...])
    row = _mod_row(i, nctx_tiles, tiles_per_lat)
    shift = mod_ref[pl.ds(row, 1), 0:d]
    scale = mod_ref[pl.ds(row, 1), d:2 * d]
    hb = _modulated_norm(x, n1_ref[...], shift, scale).astype(BF16)

    def proj(c0, width):
        return jnp.dot(hb, w_ref[:, c0:c0 + width], preferred_element_type=F32)

    c0 = 0
    q_ref[...] = (proj(c0, qk) * qscale).astype(BF16); c0 += qk
    kt_ref[...] = lax.dot_general(wkt_scr[...], hb, (((1,), (1,)), ((), ())),
                                  preferred_element_type=F32).astype(BF16)
    c0 += qk
    v_ref[...] = proj(c0, mw).astype(BF16); c0 += mw
    o_ref[...] = proj(c0, mw); c0 += mw
    zg = proj(c0, LANES) + bg_ref[...]; c0 += LANES
    lane = lax.broadcasted_iota(I32, zg.shape, 1)
    log_sig = jnp.minimum(zg, 0.0) - jnp.log1p(jnp.exp(-jnp.abs(zg)))
    zg = jnp.where(lane % 2 == 1, log_sig, zg)
    gt_ref[...] = zg.T[0:gh, :]
    xr_ref[...] = proj(c0, rw); c0 += rw
    xg_ref[...] = proj(c0, rw)


def _in_proj(xp, xs, mod, norm1_w, w_cat, bg, *, lat_seq, heads, dk, dv, rw):
    nc, d = xp.shape
    nl = xs.shape[0]
    nt = nc + nl
    tm = 256
    qk, mw, gh = heads * dk, heads * dv, heads * SUBLANES
    nctx_tiles = nc // tm
    kern = functools.partial(_inproj_kernel, nctx_tiles=nctx_tiles, tiles_per_lat=lat_seq // tm, d=d, qk=qk, mw=mw,
                             rw=rw, gh=gh, qscale=dk ** -0.5)
    row = lambda i: (i, 0)
    const = lambda i: (0, 0)
    return pl.pallas_call(
        kern,
        grid=(nt // tm,),
        in_specs=[pl.BlockSpec((tm, d), lambda i: (jnp.minimum(i, nctx_tiles - 1), 0)),
                  pl.BlockSpec((tm, d), lambda i: (jnp.maximum(i - nctx_tiles, 0), 0)),
                  pl.BlockSpec(mod.shape, const),
                  pl.BlockSpec((1, d), const),
                  pl.BlockSpec(w_cat.shape, const, pipeline_mode=pl.Buffered(1)),
                  pl.BlockSpec((1, LANES), const)],
        out_specs=[pl.BlockSpec((tm, qk), row), pl.BlockSpec((qk, tm), lambda i: (0, i)), pl.BlockSpec((tm, mw), row),
                   pl.BlockSpec((tm, mw), row), pl.BlockSpec((gh, tm), lambda i: (0, i)),
                   pl.BlockSpec((tm, rw), row), pl.BlockSpec((tm, rw), row)],
        out_shape=[jax.ShapeDtypeStruct((nt, qk), BF16), jax.ShapeDtypeStruct((qk, nt), BF16),
                   jax.ShapeDtypeStruct((nt, mw), BF16), jax.ShapeDtypeStruct((nt, mw), F32),
                   jax.ShapeDtypeStruct((gh, nt), F32),
                   jax.ShapeDtypeStruct((nt, rw), F32), jax.ShapeDtypeStruct((nt, rw), F32)],
        scratch_shapes=[pltpu.VMEM((qk, d), BF16)],
        compiler_params=_cparams(("arbitrary",)),
        name="in_proj",
    )(xp, xs, mod, norm1_w.reshape(1, d), w_cat, bg)


def _mlstm_kernel(*refs, t_len, dk, dv, has_state, emit_state):
    it = iter(refs)
    q_ref, kt_ref, v_ref, o_ref, gt_ref, nw_ref, tri_ref = (next(it) for _ in range(7))
    if has_state:
        c0_ref, n0_ref, m0_ref = (next(it) for _ in range(3))
    ya_ref = next(it)
    if emit_state:
        cn_ref, nn_ref, mn_ref = (next(it) for _ in range(3))
    hf_scr, hb_scr, c_scr, ma_scr, p_scr, w_scr, em_scr, kw_scr, dm_scr = (next(it) for _ in range(9))
    ln = MLSTM_L
    nchunks = t_len // ln
    assert ln % LANES == 0 and dk == LANES
    gsz = 2 if nchunks % 2 == 0 else 1
    ngroups = nchunks // gsz
    h_scr = (hf_scr, hb_scr)
    ext = dv + LANES
    lrep = ln // LANES

    for d in range(2):
        if has_state:
            c_scr[d, :, 0:dv] = c0_ref[d]
            c_scr[d, :, dv:ext] = _row_to_col(n0_ref[d], dk)
            ma_scr[d] = m0_ref[d]
        else:
            c_scr[d] = jnp.zeros((dk, ext), F32)
            ma_scr[d] = jnp.zeros((1, 1), F32)

    def chunk_start(d, j):
        return pl.multiple_of((j if d == 0 else nchunks - 1 - j) * ln, ln)

    def stage_a(d, j, slot):
        t0 = chunk_start(d, j)
        q = q_ref[pl.ds(t0, ln), :]
        kt = kt_ref[:, pl.ds(t0, ln)]
        g8 = gt_ref[:, pl.ds(t0, ln)]
        cum8 = jnp.dot(g8, tri_ref[d], precision=_HIGHEST, preferred_element_type=F32)
        valid = tri_ref[1 - d] > 0.5
        li = g8[2 * d:2 * d + 1]
        lf = g8[2 * d + 1:2 * d + 2]
        cum_row = cum8[2 * d + 1:2 * d + 2]
        total = jnp.sum(lf, axis=1, keepdims=True)
        a_row = li - cum_row
        cum_col = _row_to_col(cum_row, ln)
        m_prev = ma_scr[d]
        dmat = jnp.where(valid, _lane_tile(cum_col, lrep) + a_row, NEG)
        inter = cum_col + m_prev
        m_t = jnp.maximum(inter, jnp.max(dmat, axis=1, keepdims=True))
        s = jnp.dot(q, kt, preferred_element_type=F32)
        bank, par = slot
        p_scr[d, bank, par] = (s * jnp.exp(dmat - _lane_tile(m_t, lrep))).astype(BF16)
        w_scr[d, bank, par] = jnp.exp(inter - m_t)
        em_scr[d, bank, par] = jnp.exp(-m_t)
        g_row = total + a_row
        m_new = jnp.maximum(total + m_prev, jnp.max(g_row, axis=1, keepdims=True))
        kw_scr[d, bank, par] = (kt.astype(F32) * jnp.exp(g_row - m_new)).astype(BF16)
        decay = jnp.exp(total + m_prev - m_new)
        rid = lax.broadcasted_iota(I32, (SUBLANES, LANES), 0)
        dm_scr[d, bank, par] = jnp.where(rid == 0, decay, m_new)
        ma_scr[d] = m_new

    def stage_b(d, j, slot):
        t0 = chunk_start(d, j)
        q = q_ref[pl.ds(t0, ln), :]
        v_ext = jnp.concatenate([v_ref[pl.ds(t0, ln), :], jnp.ones((ln, LANES), BF16)], axis=1)
        c_st = c_scr[d]
        bank, par = slot
        full = (jnp.dot(p_scr[d, bank, par], v_ext, preferred_element_type=F32)
                + _lane_tile(w_scr[d, bank, par], ext // LANES) * jnp.dot(q, c_st.astype(BF16),
                                                                         preferred_element_type=F32))
        inv = 1.0 / jnp.maximum(jnp.abs(full[:, dv:ext]), em_scr[d, bank, par])
        h_scr[d][pl.ds(t0, ln), :] = full[:, 0:dv] * _lane_tile(inv, dv // LANES)
        decay = dm_scr[d, bank, par, 0:1, 0:1]
        c_scr[d] = decay * c_st + jnp.dot(kw_scr[d, bank, par], v_ext, preferred_element_type=F32)

    def group_a(jj, bank):
        for par in range(gsz):
            for d in range(2):
                stage_a(d, gsz * jj + par, (bank, par))

    def group_b(jj, bank):
        for par in range(gsz):
            for d in range(2):
                stage_b(d, gsz * jj + par, (bank, par))

    group_a(0, 0)

    def body(jj, carry):
        bank = jj % 2
        group_b(jj, bank)
        group_a(jj + 1, 1 - bank)
        return carry

    lax.fori_loop(0, ngroups - 1, body, 0)
    group_b(ngroups - 1, (ngroups - 1) % 2)

    def finish(j, carry):
        t0 = pl.multiple_of(j * ln, ln)
        hs = hf_scr[pl.ds(t0, ln), :] + hb_scr[pl.ds(t0, ln), :]
        ms = jnp.mean(hs * hs, axis=1, keepdims=True)
        y = hs * lax.rsqrt(ms + EPS) * nw_ref[...]
        ya_ref[pl.ds(t0, ln), :] = (_sigmoid(o_ref[pl.ds(t0, ln), :]) * y).astype(BF16)
        return carry

    lax.fori_loop(0, nchunks, finish, 0)

    if emit_state:
        for d in range(2):
            cn_ref[d] = c_scr[d, :, 0:dv]
            nn_ref[d] = c_scr[d, :, dv:ext].T[0:1, :]
            mn_ref[d] = dm_scr[d, (ngroups - 1) % 2, gsz - 1, 1:2, 0:1]


def _mlstm_tri():
    r = np.arange(MLSTM_L)
    fwd = (r[:, None] <= r[None, :]).astype(np.float32)
    return jnp.asarray(np.stack([fwd, fwd.T]))


def _mlstm(q, kt, v, o, gt, norm_w, *, row0, n_seq, t_len, heads, dk, dv, state=None, emit_state=False):
    blk0 = row0 // t_len
    tok = lambda s, h: (blk0 + s, h)
    in_specs = [pl.BlockSpec((t_len, dk), tok), pl.BlockSpec((dk, t_len), lambda s, h: (h, blk0 + s)),
                pl.BlockSpec((t_len, dv), tok), pl.BlockSpec((t_len, dv), tok),
                pl.BlockSpec((SUBLANES, t_len), lambda s, h: (h, blk0 + s)),
                pl.BlockSpec((None, 1, dv), lambda s, h: (h, 0, 0)),
                pl.BlockSpec((2, MLSTM_L, MLSTM_L), lambda s, h: (0, 0, 0))]
    args = [q, kt, v, o, gt, norm_w.reshape(heads, 1, dv), _mlstm_tri()]
    if state is not None:
        c0, n0, m0 = state
        in_specs += [pl.BlockSpec((None, 2, None, dk, dv), lambda s, h: (s, 0, h, 0, 0)),
                     pl.BlockSpec((None, 2, None, 1, dk), lambda s, h: (s, 0, h, 0, 0)),
                     pl.BlockSpec((None, 2, None, 1, 1), lambda s, h: (s, 0, h, 0, 0))]
        args += [c0, n0.reshape(n_seq, 2, heads, 1, dk), m0.reshape(n_seq, 2, heads, 1, 1)]
    out_specs = [pl.BlockSpec((t_len, dv), lambda s, h: (s, h))]
    out_shape = [jax.ShapeDtypeStruct((n_seq * t_len, heads * dv), BF16)]
    if emit_state:
        out_specs += [pl.BlockSpec((None, 2, None, dk, dv), lambda s, h: (s, 0, h, 0, 0)),
                      pl.BlockSpec((None, 2, None, 1, dk), lambda s, h: (s, 0, h, 0, 0)),
                      pl.BlockSpec((None, 2, None, 1, 1), lambda s, h: (s, 0, h, 0, 0))]
        out_shape += [jax.ShapeDtypeStruct((n_seq, 2, heads, dk, dv), F32),
                      jax.ShapeDtypeStruct((n_seq, 2, heads, 1, dk), F32),
                      jax.ShapeDtypeStruct((n_seq, 2, heads, 1, 1), F32)]
    kern = functools.partial(_mlstm_kernel, t_len=t_len, dk=dk, dv=dv, has_state=state is not None,
                             emit_state=emit_state)
    return pl.pallas_call(
        kern,
        grid=(n_seq, heads),
        in_specs=in_specs,
        out_specs=out_specs,
        out_shape=out_shape,
        scratch_shapes=[pltpu.VMEM((t_len, dv), F32), pltpu.VMEM((t_len, dv), F32),
                        pltpu.VMEM((2, dk, dv + LANES), F32), pltpu.VMEM((2, 1, 1), F32),
                        pltpu.VMEM((2, 2, 2, MLSTM_L, MLSTM_L), BF16), pltpu.VMEM((2, 2, 2, MLSTM_L, LANES), F32),
                        pltpu.VMEM((2, 2, 2, MLSTM_L, LANES), F32), pltpu.VMEM((2, 2, 2, dk, MLSTM_L), BF16),
                        pltpu.VMEM((2, 2, 2, SUBLANES, LANES), F32)],
        compiler_params=_cparams(("arbitrary", "arbitrary")),
        name="mlstm_state" if emit_state else "mlstm",
    )(*args)


def _gelu_tanh(x):
    return x * (0.5 * (1.0 + jnp.tanh(0.7978845608028654 * (x + 0.044715 * (x * x * x)))))


def _softplus(x):
    return jnp.maximum(x, 0.0) + jnp.log1p(jnp.exp(-jnp.abs(x)))


SCAN_SPLIT = 1
SCAN_GROUP = 4


def _rglru_kernel(*refs, t_len, seg, sub, pitch, tc, has_state, emit_state):
    it = iter(refs)
    xr_ref, xg_ref, cw_ref, cb_ref, wg_ref, bg_ref, lam_ref = (next(it) for _ in range(7))
    if has_state:
        h0_ref = next(it)
    yb_ref = next(it)
    if emit_state:
        hn_ref = next(it)
    a_scr, u_scr, cin_scr = (next(it) for _ in range(3))
    nchunks = t_len // tc
    piece = min(tc, sub)
    npieces = tc // piece
    ntile = xr_ref.shape[1] // LANES
    chains = [(d, lt) for d in range(2) for lt in range(ntile)]

    def scan_rows(t0, p):
        t = t0 + p * piece
        i = t // sub
        return pl.ds(pl.multiple_of(i * pitch + (t - i * sub), SUBLANES), piece), i

    ka = (-0.5 * RGLRU_C * 1.4426950408889634) * _softplus(-lam_ref[...])

    def gates(c, carry):
        t0 = pl.multiple_of(c * tc, tc)
        pos = lax.broadcasted_iota(I32, (tc, LANES), 0) % seg
        for lt in range(ntile):
            cols = slice(lt * LANES, (lt + 1) * LANES)
            x = xr_ref[pl.ds(t0, tc), cols]
            xc = cb_ref[:, cols] + cw_ref[CONV_LEFT:CONV_LEFT + 1, cols] * x
            for j in range(cw_ref.shape[0]):
                off = j - CONV_LEFT
                if off == 0:
                    continue
                shifted = pltpu.roll(x, (-off) % tc, 0)
                ok = (pos >= -off) if off < 0 else (pos < seg - off)
                xc = xc + cw_ref[j:j + 1, cols] * jnp.where(ok, shifted, 0.0)
            zh = jnp.dot(xc.astype(BF16), wg_ref[lt], preferred_element_type=F32) + bg_ref[lt]
            hx = 0.5 * xc
            for d in range(2):
                kd = ka[d:d + 1, cols]
                a = jnp.exp2(jnp.tanh(zh[:, (2 * d) * LANES:(2 * d + 1) * LANES]) * kd + kd)
                igx = hx * jnp.tanh(zh[:, (2 * d + 1) * LANES:(2 * d + 2) * LANES]) + hx
                u = jnp.sqrt(1.0 - a * a) * igx
                for p in range(npieces):
                    rows, _ = scan_rows(t0, p)
                    a_scr[d, lt, rows, :] = a[p * piece:(p + 1) * piece]
                    u_scr[d, lt, rows, :] = u[p * piece:(p + 1) * piece]
        return carry

    lax.fori_loop(0, nchunks, gates, 0)

    seg_len = sub // SCAN_SPLIT
    seg_chains = [(d, lt, s) for (d, lt) in chains for s in range(SCAN_SPLIT)]

    def scan(jg, carry):
        def rows(d, s, k):
            j = jg * SCAN_GROUP + k
            return pl.ds(s * seg_len + (j if d == 0 else seg_len - 1 - j), SUBLANES, stride=pitch)

        loaded = [[(a_scr[d, lt, rows(d, s, k), :], u_scr[d, lt, rows(d, s, k), :]) for k in range(SCAN_GROUP)]
                  for (d, lt, s) in seg_chains]
        out = []
        for (d, lt, s), (h, p), steps in zip(seg_chains, carry, loaded):
            for k, (a, u) in enumerate(steps):
                h = a * h + u
                p = a * p
                a_scr[d, lt, rows(d, s, k), :] = p
                u_scr[d, lt, rows(d, s, k), :] = h
            out.append((h, p))
        return tuple(out)

    zero = jnp.zeros((SUBLANES, LANES), F32)
    one = jnp.ones((SUBLANES, LANES), F32)
    assert seg_len % SCAN_GROUP == 0
    ends = dict(zip(seg_chains, lax.fori_loop(0, seg_len // SCAN_GROUP, scan, tuple((zero, one) for _ in seg_chains))))

    order = [(i, s) for i in range(SUBLANES) for s in range(SCAN_SPLIT)]
    for (d, lt) in chains:
        cols = slice(lt * LANES, (lt + 1) * LANES)
        cin = h0_ref[d:d + 1, cols] if has_state else jnp.zeros((1, LANES), F32)
        for (i, s) in (order if d == 0 else reversed(order)):
            h, p = ends[(d, lt, s)]
            cin_scr[d, lt, s * SUBLANES + i:s * SUBLANES + i + 1, :] = cin
            cin = h[i:i + 1] + p[i:i + 1] * cin
        if emit_state:
            hn_ref[d:d + 1, cols] = cin

    fpiece = min(tc, seg_len)

    def finish(c, carry):
        t0 = pl.multiple_of(c * tc, tc)
        for lt in range(ntile):
            cols = slice(lt * LANES, (lt + 1) * LANES)
            for p in range(tc // fpiece):
                t = t0 + p * fpiece
                i = t // sub
                local = t - i * sub
                rows = pl.ds(pl.multiple_of(i * pitch + local, SUBLANES), fpiece)
                crow = pl.ds((local // seg_len) * SUBLANES + i, 1)
                h = (u_scr[0, lt, rows, :] + a_scr[0, lt, rows, :] * cin_scr[0, lt, crow, :]
                     + u_scr[1, lt, rows, :] + a_scr[1, lt, rows, :] * cin_scr[1, lt, crow, :])
                nat = pl.ds(pl.multiple_of(t, SUBLANES), fpiece)
                yb_ref[nat, cols] = (h * _gelu_tanh(xg_ref[nat, cols])).astype(BF16)
        return carry

    lax.fori_loop(0, nchunks, finish, 0)


def _rglru(xr, xg, conv_w, conv_b, wg, bg, lam, *, row0, n_seq, t_len, seg, state=None, emit_state=False):
    rw = xr.shape[1]
    ntile = 2
    cb = ntile * LANES
    assert rw % cb == 0
    blk0 = row0 // t_len
    sub = t_len // SUBLANES
    pitch = sub + SUBLANES
    tc = 256
    tok = lambda s, g: (blk0 + s, g)
    in_specs = [pl.BlockSpec((t_len, cb), tok), pl.BlockSpec((t_len, cb), tok),
                pl.BlockSpec((conv_w.shape[0], cb), lambda s, g: (0, g)),
                pl.BlockSpec((1, cb), lambda s, g: (0, g)),
                pl.BlockSpec((ntile, LANES, 4 * LANES), lambda s, g: (g, 0, 0)),
                pl.BlockSpec((ntile, 1, 4 * LANES), lambda s, g: (g, 0, 0)),
                pl.BlockSpec((2, cb), lambda s, g: (0, g))]
    args = [xr, xg, conv_w, conv_b.reshape(1, rw), wg, bg, lam]
    if state is not None:
        in_specs.append(pl.BlockSpec((None, 2, cb), lambda s, g: (s, 0, g)))
        args.append(state)
    out_specs = [pl.BlockSpec((t_len, cb), lambda s, g: (s, g))]
    out_shape = [jax.ShapeDtypeStruct((n_seq * t_len, rw), BF16)]
    if emit_state:
        out_specs.append(pl.BlockSpec((None, 2, cb), lambda s, g: (s, 0, g)))
        out_shape.append(jax.ShapeDtypeStruct((n_seq, 2, rw), F32))
    kern = functools.partial(_rglru_kernel, t_len=t_len, seg=seg, sub=sub, pitch=pitch, tc=tc,
                             has_state=state is not None, emit_state=emit_state)
    return pl.pallas_call(
        kern,
        grid=(n_seq, rw // cb),
        in_specs=in_specs,
        out_specs=out_specs,
        out_shape=out_shape,
        scratch_shapes=[pltpu.VMEM((2, ntile, SUBLANES * pitch, LANES), F32),
                        pltpu.VMEM((2, ntile, SUBLANES * pitch, LANES), F32),
                        pltpu.VMEM((2, ntile, SCAN_SPLIT * SUBLANES, LANES), F32)],
        compiler_params=_cparams(("arbitrary", "arbitrary")),
        name="rglru_state" if emit_state else "rglru",
    )(*args)


def _pack_bf16_pair(lo, hi):
    def rounded_bits(x):
        return pltpu.bitcast(x.astype(BF16).astype(F32), U32)
    return (rounded_bits(lo) >> 16) | rounded_bits(hi)


def _unpack_bf16_pair(w):
    lo = pltpu.bitcast(w << 16, F32).astype(BF16)
    hi = pltpu.bitcast(w & jnp.uint32(0xFFFF0000), F32).astype(BF16)
    return lo, hi


def _outproj_kernel(yac_ref, yal_ref, ybc_ref, ybl_ref, xp_ref, xs_ref, mod_ref, n2_ref, wo_ref, wr_ref, br_ref,
                    x1_ref, hp_ref, ridx_ref, rwt_ref,
                    *, nctx_tiles, ntok_tiles, tiles_per_lat, d, mw, n_groups, epg, sub_rows):
    i = pl.program_id(0)

    @pl.when(i == ntok_tiles)
    def _():
        x1_ref[...] = jnp.zeros_like(x1_ref)
        hp_ref[...] = jnp.zeros_like(hp_ref)
        ridx_ref[...] = jnp.zeros_like(ridx_ref)
        rwt_ref[...] = jnp.zeros_like(rwt_ref)

    def token_tile(x_ref, ya_ref, yb_ref):
        for r0 in range(0, x_ref.shape[0], sub_rows):
            token_rows(x_ref, ya_ref, yb_ref, r0)

    def token_rows(x_ref, ya_ref, yb_ref, r0):
        rows = slice(r0, r0 + sub_rows)
        x, ya, yb = x_ref[rows, :], ya_ref[rows, :], yb_ref[rows, :]
        row = _mod_row(i, nctx_tiles, tiles_per_lat)
        gate1 = mod_ref[pl.ds(row, 1), 2 * d:3 * d]
        shift2 = mod_ref[pl.ds(row, 1), 3 * d:4 * d]
        scale2 = mod_ref[pl.ds(row, 1), 4 * d:5 * d]
        y = (jnp.dot(ya, wo_ref[0:mw, :], preferred_element_type=F32)
             + jnp.dot(yb, wo_ref[mw:, :], preferred_element_type=F32))
        x1 = x + gate1 * y
        x1_ref[rows, :] = x1
        h2 = _modulated_norm(x1, n2_ref[...], shift2, scale2)
        half = d // 2
        packed = _pack_bf16_pair(h2[:, :half], h2[:, half:])
        trows = half // LANES
        for s in range(trows):
            hp_ref[pl.ds(r0 * trows + s, sub_rows, stride=trows), :] = packed[:, s * LANES:(s + 1) * LANES]

        lt = lax.dot_general(wr_ref[...], h2.astype(BF16), (((1,), (1,)), ((), ())),
                             preferred_element_type=F32) + br_ref[:, 0:1]
        gidx = lax.broadcasted_iota(I32, (SUBLANES, lt.shape[1]), 0)
        gl = jnp.where(gidx < n_groups, lt[0:SUBLANES], -jnp.inf)
        gmax = jnp.max(gl, axis=0, keepdims=True)
        grp = jnp.min(jnp.where(gl == gmax, gidx, n_groups), axis=0, keepdims=True)
        p_grp = 1.0 / jnp.sum(jnp.exp(gl - gmax), axis=0, keepdims=True)
        el = lt[SUBLANES:SUBLANES + epg]
        for g in range(1, n_groups):
            el = jnp.where(grp == g, lt[SUBLANES + g * epg:SUBLANES + (g + 1) * epg], el)
        eidx = lax.broadcasted_iota(I32, el.shape, 0)
        v1 = jnp.max(el, axis=0, keepdims=True)
        i1 = jnp.min(jnp.where(el == v1, eidx, epg), axis=0, keepdims=True)
        el2 = jnp.where(eidx == i1, -jnp.inf, el)
        v2 = jnp.max(el2, axis=0, keepdims=True)
        i2 = jnp.min(jnp.where(el2 == v2, eidx, epg), axis=0, keepdims=True)
        e2 = jnp.exp(v2 - v1)
        w1 = p_grp / (1.0 + e2)
        w2 = p_grp * e2 / (1.0 + e2)
        rid = lax.broadcasted_iota(I32, (SUBLANES, sub_rows), 0)
        ridx_ref[:, rows] = jnp.where(rid == 0, grp * epg + i1, jnp.where(rid == 1, grp * epg + i2, 0))
        rwt_ref[:, rows] = jnp.where(rid == 0, w1, jnp.where(rid == 1, w2, 0.0))

    @pl.when(i < nctx_tiles)
    def _():
        token_tile(xp_ref, yac_ref, ybc_ref)

    @pl.when((i >= nctx_tiles) & (i < ntok_tiles))
    def _():
        token_tile(xs_ref, yal_ref, ybl_ref)


def _out_proj(ya_c, ya_l, yb_c, yb_l, xp, xs, mod, norm2_w, w_out, wr, br, *, lat_seq, n_groups, epg):
    nc, d = xp.shape
    nl = xs.shape[0]
    nt = nc + nl
    mw = ya_c.shape[1]
    tm, sub_rows = 512, 256
    trows = (d // 2) // LANES
    nctx_tiles, ntok_tiles = nc // tm, nt // tm
    kern = functools.partial(_outproj_kernel, nctx_tiles=nctx_tiles, ntok_tiles=ntok_tiles,
                             tiles_per_lat=lat_seq // tm, d=d, mw=mw, n_groups=n_groups, epg=epg, sub_rows=sub_rows)
    ctx = lambda i: (jnp.minimum(i, nctx_tiles - 1), 0)
    lat = lambda i: (jnp.clip(i - nctx_tiles, 0, nl // tm - 1), 0)
    row = lambda i: (i, 0)
    const = lambda i: (0, 0)
    return pl.pallas_call(
        kern,
        grid=(ntok_tiles + 1,),
        in_specs=[pl.BlockSpec((tm, mw), ctx), pl.BlockSpec((tm, mw), lat),
                  pl.BlockSpec((tm, yb_c.shape[1]), ctx), pl.BlockSpec((tm, yb_c.shape[1]), lat),
                  pl.BlockSpec((tm, d), ctx), pl.BlockSpec((tm, d), lat),
                  pl.BlockSpec(mod.shape, const),
                  pl.BlockSpec((1, d), const),
                  pl.BlockSpec(w_out.shape, const, pipeline_mode=pl.Buffered(1)),
                  pl.BlockSpec(wr.shape, const),
                  pl.BlockSpec(br.shape, const)],
        out_specs=[pl.BlockSpec((tm, d), row), pl.BlockSpec((tm * trows, LANES), row),
                   pl.BlockSpec((SUBLANES, tm), lambda i: (0, i)), pl.BlockSpec((SUBLANES, tm), lambda i: (0, i))],
        out_shape=[jax.ShapeDtypeStruct((nt + tm, d), F32), jax.ShapeDtypeStruct(((nt + tm) * trows, LANES), U32),
                   jax.ShapeDtypeStruct((SUBLANES, nt + tm), I32), jax.ShapeDtypeStruct((SUBLANES, nt + tm), F32)],
        compiler_params=_cparams(("arbitrary",)),
        name="out_proj",
    )(ya_c, ya_l, yb_c, yb_l, xp, xs, mod, norm2_w.reshape(1, d), w_out, wr, br)


MOE_ROWS = 256


ROUTE_TILE = 512


def _rank_kernel(ridx_ref, tri_ref, rank_ref, cnt_ref, carry_scr, *, n_exp):
    @pl.when(pl.program_id(0) == 0)
    def _():
        carry_scr[...] = jnp.zeros_like(carry_scr)

    e = ridx_ref[...]
    tr = e.shape[1]
    eid = lax.broadcasted_iota(I32, (n_exp, tr), 0)
    carry = carry_scr[:, 0:1]
    ranks = []
    for kk in range(TOP_K):
        hit = eid == e[kk:kk + 1]
        cum = jnp.dot(jnp.where(hit, 1.0, 0.0).astype(BF16), tri_ref[...], preferred_element_type=F32)
        ranks.append(jnp.sum(jnp.where(hit, cum + carry, 0.0), axis=0, keepdims=True) - 1.0)
        carry = carry + cum[:, tr - 1:tr]
    carry_scr[...] = jnp.broadcast_to(carry, carry_scr.shape)
    cnt_ref[...] = jnp.broadcast_to(carry, cnt_ref.shape)
    rid = lax.broadcasted_iota(I32, rank_ref.shape, 0)
    rank_ref[...] = jnp.where(rid == 0, ranks[0], jnp.where(rid == 1, ranks[1], 0.0)).astype(I32)


def _dest_kernel(ridx_ref, rank_ref, pstart_ref, dest_ref, *, n_exp):
    e = ridx_ref[...]
    tr = e.shape[1]
    eid = lax.broadcasted_iota(I32, (n_exp, tr), 0)
    ps = pstart_ref[:, 0:1]
    rows = [jnp.sum(jnp.where(eid == e[kk:kk + 1], ps, 0.0), axis=0, keepdims=True) for kk in range(TOP_K)]
    rid = lax.broadcasted_iota(I32, dest_ref.shape, 0)
    dest_ref[...] = rank_ref[...] + jnp.where(rid == 0, rows[0], jnp.where(rid == 1, rows[1], 0.0)).astype(I32)


def _invert_kernel(dest_ref, fill_ref, rowtok_ref, sem):
    i = pl.program_id(0)
    tr = dest_ref.shape[0] // TOP_K

    @pl.when(i == 0)
    def _():
        fill = pltpu.make_async_copy(fill_ref, rowtok_ref, sem.at[0])
        fill.start()
        fill.wait()

    def body(r, c):
        for kk in range(TOP_K):
            rowtok_ref[dest_ref[kk * tr + r]] = i * tr + r
        return c
    lax.fori_loop(0, tr, body, 0, unroll=16)


def _routing(ridx, n_tok, n_exp, n_blocks, zero_row):
    tr = ROUTE_TILE
    steps = n_tok // tr
    tri = jnp.asarray(np.triu(np.ones((tr, tr), np.float32)), BF16)
    tile = pl.BlockSpec((SUBLANES, tr), lambda i: (0, i))
    cnt_spec = pl.BlockSpec((n_exp, LANES), lambda i: (0, 0))
    rank, cnt = pl.pallas_call(
        functools.partial(_rank_kernel, n_exp=n_exp),
        grid=(steps,),
        in_specs=[tile, pl.BlockSpec((tr, tr), lambda i: (0, 0))],
        out_specs=[tile, cnt_spec],
        out_shape=[jax.ShapeDtypeStruct((SUBLANES, n_tok), I32), jax.ShapeDtypeStruct((n_exp, LANES), F32)],
        scratch_shapes=[pltpu.VMEM((n_exp, LANES), F32)],
        compiler_params=_cparams(("arbitrary",)),
        name="route_rank",
    )(ridx, tri)
    counts = cnt[:, 0].astype(I32)
    padded = (counts + MOE_ROWS - 1) // MOE_ROWS * MOE_ROWS
    pad_end = jnp.cumsum(padded)
    pad_start = pad_end - padded
    dest = pl.pallas_call(
        functools.partial(_dest_kernel, n_exp=n_exp),
        grid=(steps,),
        in_specs=[tile, tile, cnt_spec],
        out_specs=tile,
        out_shape=jax.ShapeDtypeStruct((SUBLANES, n_tok), I32),
        compiler_params=_cparams(("arbitrary",)),
        name="route_dest",
    )(ridx, rank, jnp.broadcast_to(pad_start.astype(F32)[:, None], (n_exp, LANES)))
    row_tok = pl.pallas_call(
        _invert_kernel,
        grid=(steps,),
        in_specs=[pl.BlockSpec((TOP_K * tr,), lambda i: (i,), memory_space=pltpu.SMEM),
                  pl.BlockSpec(memory_space=pl.ANY)],
        out_specs=pl.BlockSpec(memory_space=pltpu.SMEM),
        out_shape=jax.ShapeDtypeStruct((n_blocks * MOE_ROWS,), I32),
        scratch_shapes=[pltpu.SemaphoreType.DMA((1,))],
        compiler_params=_cparams(("arbitrary",)),
        name="route_invert",
    )(dest[0:TOP_K].reshape(TOP_K, steps, tr).transpose(1, 0, 2).reshape(-1),
      jnp.full((n_blocks * MOE_ROWS,), zero_row, I32))
    blk_row0 = jnp.arange(n_blocks, dtype=I32) * MOE_ROWS
    blk_e = jnp.minimum(jnp.sum((pad_end[None, :] <= blk_row0[:, None]).astype(I32), axis=1), n_exp - 1)
    nused = (pad_end[-1:] // MOE_ROWS).astype(I32)
    return dest, row_tok, blk_e, nused


CAST_ROWS = 256
WEIGHT_DMA_SPLIT = 8


GATHER_DEPTH = 6


def _expert_kernel(blk_e_ref, nused_ref, *refs, half, rows, trows):
    tok_refs = refs[:GATHER_DEPTH]
    (src_ref, wg_hbm, wu_hbm, wd_hbm, y_ref, xbuf, xsem, stage_g, stage_u, stage_d, bf_g, bf_u, bf_d,
     wsem) = refs[GATHER_DEPTH:]
    b = pl.program_id(0)
    nused = nused_ref[0]
    w_hbm = (wg_hbm, wu_hbm, wd_hbm)
    stage = (stage_g, stage_u, stage_d)
    wbf = (bf_g, bf_u, bf_d)

    def weight_copy(e, j):
        return pltpu.make_async_copy(w_hbm[j].at[e], stage[j], wsem.at[j])

    def start_weights(e, j):
        step = stage[j].shape[0] // WEIGHT_DMA_SPLIT
        for c in range(WEIGHT_DMA_SPLIT):
            sl = pl.ds(c * step, step)
            pltpu.make_async_copy(w_hbm[j].at[e, sl, :], stage[j].at[sl, :], wsem.at[j]).start(priority=1)

    def row_copy(tok, slot, r):
        src = src_ref.at[pl.ds(pl.multiple_of(tok * trows, trows), trows), :]
        return pltpu.make_async_copy(src, xbuf.at[slot, pl.ds(r * trows, trows), :], xsem.at[slot])

    @pl.when(b >= nused)
    def _():
        y_ref[...] = jnp.zeros_like(y_ref)

    @pl.when(b < nused)
    def _():
        e = blk_e_ref[b]

        @pl.when(b == 0)
        def _():
            for j in range(3):
                start_weights(e, j)

        for first in range(GATHER_DEPTH - 1):
            @pl.when((b == 0) & (nused > first))
            def _(first=first):
                def body(r, c):
                    row_copy(tok_refs[first][r], first, r).start()
                    return c
                lax.fori_loop(0, rows, body, 0, unroll=8)

        @pl.when((b == 0) | (blk_e_ref[jnp.maximum(b - 1, 0)] != e))
        def _():
            nb = lax.while_loop(lambda k: (k < nused) & (blk_e_ref[jnp.minimum(k, nused - 1)] == e),
                                lambda k: k + 1, b + 1)
            for j in range(3):
                weight_copy(e, j).wait()
                n_steps = stage[j].shape[0] // CAST_ROWS

                def cast(c, carry, j=j):
                    sl = pl.ds(pl.multiple_of(c * CAST_ROWS, CAST_ROWS), CAST_ROWS)
                    wbf[j][sl, :] = stage[j][sl, :].astype(BF16)
                    return carry
                lax.fori_loop(0, n_steps, cast, 0)

                @pl.when(nb < nused)
                def _(j=j):
                    start_weights(blk_e_ref[jnp.minimum(nb, nused - 1)], j)

        slot = b % GATHER_DEPTH
        pltpu.make_async_copy(src_ref.at[pl.ds(0, rows * trows), :], xbuf.at[slot], xsem.at[slot]).wait()

        def compute(prefetch):
            parts = [_unpack_bf16_pair(xbuf[slot, pl.ds(s, rows, stride=trows), :]) for s in range(trows)]
            lo = jnp.concatenate([p[0] for p in parts], axis=1)
            hi = jnp.concatenate([p[1] for p in parts], axis=1)
            if prefetch:
                ahead = (b + GATHER_DEPTH - 1) % GATHER_DEPTH
                for r in range(rows):
                    row_copy(tok_refs[GATHER_DEPTH - 1][r], ahead, r).start()

            def up(w_ref):
                return (jnp.dot(lo, w_ref[0:half, :], preferred_element_type=F32)
                        + jnp.dot(hi, w_ref[half:, :], preferred_element_type=F32))

            g = up(wbf[0])
            h = ((g * _sigmoid(g)) * up(wbf[1])).astype(BF16)
            y = jnp.dot(h, wbf[2][...], preferred_element_type=F32)
            packed = _pack_bf16_pair(y[:, :half], y[:, half:])
            for s in range(trows):
                y_ref[pl.ds(s, rows, stride=trows), :] = packed[:, s * LANES:(s + 1) * LANES]

        @pl.when(b + GATHER_DEPTH - 1 < nused)
        def _():
            compute(True)

        @pl.when(b + GATHER_DEPTH - 1 >= nused)
        def _():
            compute(False)


def _experts(blk_e, nused, row_tok, src, wg, wu, wd, n_blocks):
    n_exp, d, ff = wg.shape
    any_spec = pl.BlockSpec(memory_space=pl.ANY)
    trows = (d // 2) // LANES
    def tok_spec(ahead):
        return pl.BlockSpec((MOE_ROWS,), lambda b, be, nu: (jnp.minimum(b + ahead, n_blocks - 1),),
                            memory_space=pltpu.SMEM)

    return pl.pallas_call(
        functools.partial(_expert_kernel, half=d // 2, rows=MOE_ROWS, trows=trows),
        grid_spec=pltpu.PrefetchScalarGridSpec(
            num_scalar_prefetch=2,
            grid=(n_blocks,),
            in_specs=[tok_spec(a) for a in range(GATHER_DEPTH)] + [any_spec, any_spec, any_spec, any_spec],
            out_specs=pl.BlockSpec((MOE_ROWS * trows, LANES), lambda b, be, nu: (b, 0)),
            scratch_shapes=[pltpu.VMEM((GATHER_DEPTH, MOE_ROWS * trows, LANES), U32),
                            pltpu.SemaphoreType.DMA((GATHER_DEPTH,)),
                            pltpu.VMEM((d, ff), F32), pltpu.VMEM((d, ff), F32), pltpu.VMEM((ff, d), F32),
                            pltpu.VMEM((d, ff), BF16), pltpu.VMEM((d, ff), BF16), pltpu.VMEM((ff, d), BF16),
                            pltpu.SemaphoreType.DMA((3,))]),
        out_shape=jax.ShapeDtypeStruct((n_blocks * MOE_ROWS * trows, LANES), U32),
        compiler_params=_cparams(("arbitrary",)),
        name="experts",
    )(blk_e, nused, *([row_tok] * GATHER_DEPTH), src, wg, wu, wd)


COMBINE_BUFS = 4


def _combine_kernel(dest_ref, x1_ref, wt_ref, mod_ref, fw_ref, y_ref, o_ref, *scratch,
                    tile0, n_all, tiles_per_seq, d, lat):
    i = pl.program_id(0)
    last = pl.num_programs(0) - 1
    tm = x1_ref.shape[0]
    bufs, sem = scratch[:-1], scratch[-1]
    nbuf = len(bufs)

    trows = (d // 2) // LANES

    def row_copy(tile, par, kk, r):
        row = dest_ref[kk * n_all + (tile0 + tile) * tm + r]
        src = y_ref.at[pl.ds(pl.multiple_of(row * trows, trows), trows), :]
        return pltpu.make_async_copy(src, bufs[par].at[kk, pl.ds(r * trows, trows), :], sem.at[par])

    for first in range(nbuf - 1):
        @pl.when((i == 0) & (first <= last))
        def _(first=first):
            def body(r, c):
                for kk in range(TOP_K):
                    row_copy(first, first, kk, r).start()
                return c
            lax.fori_loop(0, tm, body, 0, unroll=8)

    def step(par, prefetch):
        for kk in range(TOP_K):
            pltpu.make_async_copy(y_ref.at[pl.ds(0, tm * trows), :], bufs[par].at[kk], sem.at[par]).wait()
        if prefetch:
            for r in range(tm):
                for kk in range(TOP_K):
                    row_copy(i + nbuf - 1, (par + nbuf - 1) % nbuf, kk, r).start(priority=kk)
        row = (1 + i // tiles_per_seq) if lat else 0
        gate2 = mod_ref[pl.ds(row, 1), 5 * d:6 * d]
        wt = wt_ref[...]
        half = d // 2
        w0 = _row_to_col(wt[0:1], tm)
        w1 = _row_to_col(wt[1:2], tm)
        unpack = (lambda p: pltpu.bitcast(p << 16, F32),
                  lambda p: pltpu.bitcast(p & jnp.uint32(0xFFFF0000), F32))
        ssq = jnp.zeros((tm, 1), F32)
        for s in range(trows):
            p0 = bufs[par][0, pl.ds(s, tm, stride=trows), :]
            p1 = bufs[par][1, pl.ds(s, tm, stride=trows), :]
            for side in range(2):
                cols = slice(side * half + s * LANES, side * half + (s + 1) * LANES)
                x = x1_ref[:, cols] + gate2[:, cols] * (w0 * unpack[side](p0) + w1 * unpack[side](p1))
                ssq = ssq + jnp.sum(x * x, axis=-1, keepdims=True)
                o_ref[:, cols] = x
        scale = lax.rsqrt(ssq * (1.0 / d) + EPS)
        o_ref[...] = o_ref[...] * scale * fw_ref[...]

    for par in range(nbuf):
        @pl.when((i % nbuf == par) & (i + nbuf - 1 <= last))
        def _(par=par):
            step(par, True)

        @pl.when((i % nbuf == par) & (i + nbuf - 1 > last))
        def _(par=par):
            step(par, False)


def _combine(dest_flat, x1, rwt, mod, final_w, y_rows, *, row0, n_tok, seq_len, lat):
    d = x1.shape[1]
    tm = 256
    tile0 = row0 // tm
    trows = (d // 2) // LANES
    n_all = dest_flat.shape[0] // TOP_K
    kern = functools.partial(_combine_kernel, tile0=tile0, n_all=n_all, tiles_per_seq=seq_len // tm, d=d, lat=lat)
    return pl.pallas_call(
        kern,
        grid_spec=pltpu.PrefetchScalarGridSpec(
            num_scalar_prefetch=1,
            grid=(n_tok // tm,),
            in_specs=[pl.BlockSpec((tm, d), lambda i, dr: (tile0 + i, 0)),
                      pl.BlockSpec((SUBLANES, tm), lambda i, dr: (0, tile0 + i)),
                      pl.BlockSpec(mod.shape, lambda i, dr: (0, 0)),
                      pl.BlockSpec((1, d), lambda i, dr: (0, 0)),
                      pl.BlockSpec(memory_space=pl.ANY)],
            out_specs=pl.BlockSpec((tm, d), lambda i, dr: (i, 0)),
            scratch_shapes=[pltpu.VMEM((TOP_K, tm * trows, LANES), U32) for _ in range(COMBINE_BUFS)]
            + [pltpu.SemaphoreType.DMA((COMBINE_BUFS,))]),
        out_shape=jax.ShapeDtypeStruct((n_tok, d), F32),
        compiler_params=_cparams(("arbitrary",)),
        name="combine_lat" if lat else "combine_ctx",
    )(dest_flat, x1, rwt, mod, final_w.reshape(1, d), y_rows)


def _gate_layout(w_gates, b_gates, heads):
    d = w_gates.shape[0]
    w = w_gates.reshape(d, 4, heads).transpose(0, 2, 1)
    w = jnp.pad(w, ((0, 0), (0, 0), (0, SUBLANES - 4))).reshape(d, heads * SUBLANES)
    b = b_gates.reshape(4, heads).T
    b = jnp.pad(b, ((0, 0), (0, SUBLANES - 4))).reshape(1, heads * SUBLANES)
    padl = LANES - heads * SUBLANES
    return jnp.pad(w, ((0, 0), (0, padl))), jnp.pad(b, ((0, 0), (0, padl)))


def kernel(x_prompt, x_sample, state_mlstm_c, state_mlstm_n, state_mlstm_m, state_rglru_h, c, c_ctx, w_ada, b_ada,
           norm1_w, w_in, b_gates, conv_w, conv_b, rg_wa, rg_ba, rg_wx, rg_bx, rg_lambda, mlstm_norm_w, w_out,
           norm2_w, router_group_w, router_group_b, router_expert_w, router_expert_b, expert_w_gate, expert_w_up,
           expert_w_down, final_norm_w):
    n_req, seq, d = x_prompt.shape
    n_lat, lat_seq, _ = x_sample.shape
    depth = w_in.shape[0]
    assert depth == 1, "the token-axis plumbing below is written for the single-layer trunk"
    heads, dk, dv = state_mlstm_c.shape[3:]
    rw = state_rglru_h.shape[-1]
    nblk = rg_wa.shape[2]
    assert rw // nblk == LANES
    n_groups, epg = router_expert_w.shape[1], router_expert_w.shape[3]
    n_exp = n_groups * epg
    qk, mw = heads * dk, heads * dv
    nc, nl = n_req * seq, n_lat * lat_seq
    nt = nc + nl
    assert nc % lat_seq == 0 and n_lat + 1 <= SUBLANES
    l = 0

    xp = x_prompt.reshape(nc, d)
    xs = x_sample.reshape(nl, d)
    cvec = jnp.zeros((SUBLANES, d), F32).at[0].set(c_ctx).at[1:1 + n_lat].set(c)
    mod = _ada(cvec, w_ada[l], b_ada[l])

    w = w_in[l]
    g0 = 2 * qk + 2 * mw
    wgate, bgate = _gate_layout(w[:, g0:g0 + 4 * heads], b_gates[l], heads)
    w_cat = jnp.concatenate([w[:, :g0], wgate, w[:, g0 + 4 * heads:]], axis=1).astype(BF16)
    q, kt, v, o, gt, xr, xg = _in_proj(xp, xs, mod, norm1_w[l], w_cat, bgate, lat_seq=lat_seq, heads=heads, dk=dk,
                                       dv=dv, rw=rw)

    mkw = dict(heads=heads, dk=dk, dv=dv)
    ya_c, new_c, new_n, new_m = _mlstm(q, kt, v, o, gt, mlstm_norm_w[l], row0=0, n_seq=n_req, t_len=seq,
                                       emit_state=True, **mkw)
    (ya_l,) = _mlstm(q, kt, v, o, gt, mlstm_norm_w[l], row0=nc, n_seq=n_lat, t_len=lat_seq,
                     state=(state_mlstm_c[:, l], state_mlstm_n[:, l], state_mlstm_m[:, l]), **mkw)

    wg = (0.5 * jnp.concatenate([rg_wa[l, 0], rg_wx[l, 0], rg_wa[l, 1], rg_wx[l, 1]], axis=-1)).astype(BF16)
    bg = 0.5 * jnp.concatenate([rg_ba[l, 0].reshape(nblk, 1, LANES), rg_bx[l, 0].reshape(nblk, 1, LANES),
                                rg_ba[l, 1].reshape(nblk, 1, LANES), rg_bx[l, 1].reshape(nblk, 1, LANES)], axis=-1)
    rargs = (xr, xg, conv_w[l], conv_b[l], wg, bg, rg_lambda[l])
    yb_c, new_h = _rglru(*rargs, row0=0, n_seq=n_req, t_len=seq, seg=seq, emit_state=True)
    (yb_l,) = _rglru(*rargs, row0=nc, n_seq=n_lat, t_len=lat_seq, seg=GRID_W, state=state_rglru_h[:, l])

    r_rows = -(-(SUBLANES + n_exp) // 16) * 16
    wr = jnp.zeros((r_rows, d), F32)
    wr = wr.at[0:n_groups].set(router_group_w[l].T)
    wr = wr.at[SUBLANES:SUBLANES + n_exp].set(router_expert_w[l].transpose(0, 2, 1).reshape(n_exp, d)).astype(BF16)
    br = jnp.zeros((r_rows, LANES), F32)
    br = br.at[0:n_groups, 0].set(router_group_b[l])
    br = br.at[SUBLANES:SUBLANES + n_exp, 0].set(router_expert_b[l].reshape(n_exp))
    x1, hp, ridx, rwt = _out_proj(ya_c, ya_l, yb_c, yb_l, xp, xs, mod, norm2_w[l], w_out[l].astype(BF16), wr, br,
                                  lat_seq=lat_seq, n_groups=n_groups, epg=epg)

    n_blocks = (nt * TOP_K) // MOE_ROWS + n_exp
    dest, row_tok, blk_e, nused = _routing(ridx, nt, n_exp, n_blocks, zero_row=nt)
    only_layer = lambda a: a.reshape(a.shape[1:])
    y_rows = _experts(blk_e, nused, row_tok, hp, only_layer(expert_w_gate), only_layer(expert_w_up),
                      only_layer(expert_w_down), n_blocks)

    dest_flat = dest[0:TOP_K].reshape(-1)
    y_prompt = _combine(dest_flat, x1, rwt, mod, final_norm_w, y_rows, row0=0, n_tok=nc, seq_len=seq, lat=False)
    y_sample = _combine(dest_flat, x1, rwt, mod, final_norm_w, y_rows, row0=nc, n_tok=nl, seq_len=lat_seq,
                        lat=True)

    return (y_prompt.reshape(n_req, seq, d), y_sample.reshape(n_lat, lat_seq, d),
            new_c[:, None], new_n.reshape(n_req, 1, 2, heads, dk), new_m.reshape(n_req, 1, 2, heads),
            new_h[:, None])
```

```python
import functools

import jax
import jax.numpy as jnp
import numpy as np
from jax import lax
from jax.experimental import pallas as pl
from jax.experimental.pallas import tpu as pltpu

F32 = jnp.float32
BF16 = jnp.bfloat16
I32 = jnp.int32
U32 = jnp.uint32

EPS = 1e-6
GRID_W = 64
CONV_LEFT = 2
RGLRU_C = 8.0
TOP_K = 2
LANES = 128
SUBLANES = 8
MLSTM_L = 256
NEG = -1e30
VMEM_LIMIT = 56 * 1024 * 1024

_HIGHEST = lax.Precision.HIGHEST


def _cparams(sem, vmem=VMEM_LIMIT):
    return pltpu.CompilerParams(dimension_semantics=sem, vmem_limit_bytes=vmem)


def _sigmoid(x):
    return 0.5 * jnp.tanh(0.5 * x) + 0.5


def _row_to_col(r, n):
    return jnp.broadcast_to(r, (LANES, n)).T


def _lane_tile(x, reps):
    return x if reps == 1 else jnp.concatenate([x] * reps, axis=1)


def _ada_kernel(c_ref, w_ref, b_ref, o_ref):
    c = c_ref[...]
    s = (c * _sigmoid(c)).astype(BF16)
    o_ref[...] = jnp.dot(s, w_ref[...].astype(BF16), preferred_element_type=F32) + b_ref[...]


def _ada(cvec, w_ada, b_ada):
    d, n = w_ada.shape
    tn = 1024 if n % 1024 == 0 else 512
    assert n % tn == 0
    return pl.pallas_call(
        _ada_kernel,
        grid=(n // tn,),
        in_specs=[pl.BlockSpec((SUBLANES, d), lambda j: (0, 0)),
                  pl.BlockSpec((d, tn), lambda j: (0, j)),
                  pl.BlockSpec((1, tn), lambda j: (0, j))],
        out_specs=pl.BlockSpec((SUBLANES, tn), lambda j: (0, j)),
        out_shape=jax.ShapeDtypeStruct((SUBLANES, n), F32),
        compiler_params=_cparams(("arbitrary",)),
        name="ada",
    )(cvec, w_ada, b_ada.reshape(1, n))


def _modulated_norm(x, w, shift, scale):
    ms = jnp.mean(x * x, axis=-1, keepdims=True)
    return (x * lax.rsqrt(ms + EPS) * w) * (1.0 + scale) + shift


def _mod_row(i, nctx_tiles, tiles_per_lat):
    return jnp.where(i < nctx_tiles, 0, 1 + (i - nctx_tiles) // tiles_per_lat)


def _inproj_kernel(xp_ref, xs_ref, mod_ref, n1_ref, w_ref, bg_ref,
                   q_ref, kt_ref, v_ref, o_ref, gt_ref, xr_ref, xg_ref, wkt_scr,
                   *, nctx_tiles, tiles_per_lat, d, qk, mw, rw, gh, qscale):
    i = pl.program_id(0)

    @pl.when(i == 0)
    def _():
        wkt_scr[...] = w_ref[:, qk:2 * qk].astype(F32).T.astype(BF16)

    x = jnp.where(i < nctx_tiles, xp_ref[...], xs_ref[...])
    row = _mod_row(i, nctx_tiles, tiles_per_lat)
    shift = mod_ref[pl.ds(row, 1), 0:d]
    scale = mod_ref[pl.ds(row, 1), d:2 * d]
    hb = _modulated_norm(x, n1_ref[...], shift, scale).astype(BF16)

    def proj(c0, width):
        return jnp.dot(hb, w_ref[:, c0:c0 + width], preferred_element_type=F32)

    c0 = 0
    q_ref[...] = (proj(c0, qk) * qscale).astype(BF16); c0 += qk
    kt_ref[...] = lax.dot_general(wkt_scr[...], hb, (((1,), (1,)), ((), ())),
                                  preferred_element_type=F32).astype(BF16)
    c0 += qk
    v_ref[...] = proj(c0, mw).astype(BF16); c0 += mw
    o_ref[...] = proj(c0, mw); c0 += mw
    zg = proj(c0, LANES) + bg_ref[...]; c0 += LANES
    lane = lax.broadcasted_iota(I32, zg.shape, 1)
    log_sig = jnp.minimum(zg, 0.0) - jnp.log1p(jnp.exp(-jnp.abs(zg)))
    zg = jnp.where(lane % 2 == 1, log_sig, zg)
    gt_ref[...] = zg.T[0:gh, :]
    xr_ref[...] = proj(c0, rw); c0 += rw
    xg_ref[...] = proj(c0, rw)


def _in_proj(xp, xs, mod, norm1_w, w_cat, bg, *, lat_seq, heads, dk, dv, rw):
    nc, d = xp.shape
    nl = xs.shape[0]
    nt = nc + nl
    tm = 256
    qk, mw, gh = heads * dk, heads * dv, heads * SUBLANES
    nctx_tiles = nc // tm
    kern = functools.partial(_inproj_kernel, nctx_tiles=nctx_tiles, tiles_per_lat=lat_seq // tm, d=d, qk=qk, mw=mw,
                             rw=rw, gh=gh, qscale=dk ** -0.5)
    row = lambda i: (i, 0)
    const = lambda i: (0, 0)
    return pl.pallas_call(
        kern,
        grid=(nt // tm,),
        in_specs=[pl.BlockSpec((tm, d), lambda i: (jnp.minimum(i, nctx_tiles - 1), 0)),
                  pl.BlockSpec((tm, d), lambda i: (jnp.maximum(i - nctx_tiles, 0), 0)),
                  pl.BlockSpec(mod.shape, const),
                  pl.BlockSpec((1, d), const),
                  pl.BlockSpec(w_cat.shape, const, pipeline_mode=pl.Buffered(1)),
                  pl.BlockSpec((1, LANES), const)],
        out_specs=[pl.BlockSpec((tm, qk), row), pl.BlockSpec((qk, tm), lambda i: (0, i)), pl.BlockSpec((tm, mw), row),
                   pl.BlockSpec((tm, mw), row), pl.BlockSpec((gh, tm), lambda i: (0, i)),
                   pl.BlockSpec((tm, rw), row), pl.BlockSpec((tm, rw), row)],
        out_shape=[jax.ShapeDtypeStruct((nt, qk), BF16), jax.ShapeDtypeStruct((qk, nt), BF16),
                   jax.ShapeDtypeStruct((nt, mw), BF16), jax.ShapeDtypeStruct((nt, mw), F32),
                   jax.ShapeDtypeStruct((gh, nt), F32),
                   jax.ShapeDtypeStruct((nt, rw), F32), jax.ShapeDtypeStruct((nt, rw), F32)],
        scratch_shapes=[pltpu.VMEM((qk, d), BF16)],
        compiler_params=_cparams(("arbitrary",)),
        name="in_proj",
    )(xp, xs, mod, norm1_w.reshape(1, d), w_cat, bg)


def _mlstm_kernel(*refs, t_len, dk, dv, has_state, emit_state):
    it = iter(refs)
    q_ref, kt_ref, v_ref, o_ref, gt_ref, nw_ref, tri_ref = (next(it) for _ in range(7))
    if has_state:
        c0_ref, n0_ref, m0_ref = (next(it) for _ in range(3))
    ya_ref = next(it)
    if emit_state:
        cn_ref, nn_ref, mn_ref = (next(it) for _ in range(3))
    hf_scr, hb_scr, c_scr, ma_scr, p_scr, w_scr, em_scr, kw_scr, dm_scr = (next(it) for _ in range(9))
    ln = MLSTM_L
    nchunks = t_len // ln
    assert ln % LANES == 0 and dk == LANES
    gsz = 2 if nchunks % 2 == 0 else 1
    ngroups = nchunks // gsz
    h_scr = (hf_scr, hb_scr)
    ext = dv + LANES
    lrep = ln // LANES

    for d in range(2):
        if has_state:
            c_scr[d, :, 0:dv] = c0_ref[d]
            c_scr[d, :, dv:ext] = _row_to_col(n0_ref[d], dk)
            ma_scr[d] = m0_ref[d]
        else:
            c_scr[d] = jnp.zeros((dk, ext), F32)
            ma_scr[d] = jnp.zeros((1, 1), F32)

    def chunk_start(d, j):
        return pl.multiple_of((j if d == 0 else nchunks - 1 - j) * ln, ln)

    def stage_a(d, j, slot):
        t0 = chunk_start(d, j)
        q = q_ref[pl.ds(t0, ln), :]
        kt = kt_ref[:, pl.ds(t0, ln)]
        g8 = gt_ref[:, pl.ds(t0, ln)]
        cum8 = jnp.dot(g8, tri_ref[d], precision=_HIGHEST, preferred_element_type=F32)
        valid = tri_ref[1 - d] > 0.5
        li = g8[2 * d:2 * d + 1]
        lf = g8[2 * d + 1:2 * d + 2]
        cum_row = cum8[2 * d + 1:2 * d + 2]
        total = jnp.sum(lf, axis=1, keepdims=True)
        a_row = li - cum_row
        cum_col = _row_to_col(cum_row, ln)
        m_prev = ma_scr[d]
        dmat = jnp.where(valid, _lane_tile(cum_col, lrep) + a_row, NEG)
        inter = cum_col + m_prev
        m_t = jnp.maximum(inter, jnp.max(dmat, axis=1, keepdims=True))
        s = jnp.dot(q, kt, preferred_element_type=F32)
        bank, par = slot
        p_scr[d, bank, par] = (s * jnp.exp(dmat - _lane_tile(m_t, lrep))).astype(BF16)
        w_scr[d, bank, par] = jnp.exp(inter - m_t)
        em_scr[d, bank, par] = jnp.exp(-m_t)
        g_row = total + a_row
        m_new = jnp.maximum(total + m_prev, jnp.max(g_row, axis=1, keepdims=True))
        kw_scr[d, bank, par] = (kt.astype(F32) * jnp.exp(g_row - m_new)).astype(BF16)
        decay = jnp.exp(total + m_prev - m_new)
        rid = lax.broadcasted_iota(I32, (SUBLANES, LANES), 0)
        dm_scr[d, bank, par] = jnp.where(rid == 0, decay, m_new)
        ma_scr[d] = m_new

    def stage_b(d, j, slot):
        t0 = chunk_start(d, j)
        q = q_ref[pl.ds(t0, ln), :]
        v_ext = jnp.concatenate([v_ref[pl.ds(t0, ln), :], jnp.ones((ln, LANES), BF16)], axis=1)
        c_st = c_scr[d]
        bank, par = slot
        full = (jnp.dot(p_scr[d, bank, par], v_ext, preferred_element_type=F32)
                + _lane_tile(w_scr[d, bank, par], ext // LANES) * jnp.dot(q, c_st.astype(BF16),
                                                                         preferred_element_type=F32))
        inv = 1.0 / jnp.maximum(jnp.abs(full[:, dv:ext]), em_scr[d, bank, par])
        h_scr[d][pl.ds(t0, ln), :] = full[:, 0:dv] * _lane_tile(inv, dv // LANES)
        decay = dm_scr[d, bank, par, 0:1, 0:1]
        c_scr[d] = decay * c_st + jnp.dot(kw_scr[d, bank, par], v_ext, preferred_element_type=F32)

    def group_a(jj, bank):
        for par in range(gsz):
            for d in range(2):
                stage_a(d, gsz * jj + par, (bank, par))

    def group_b(jj, bank):
        for par in range(gsz):
            for d in range(2):
                stage_b(d, gsz * jj + par, (bank, par))

    group_a(0, 0)

    def body(jj, carry):
        bank = jj % 2
        group_b(jj, bank)
        group_a(jj + 1, 1 - bank)
        return carry

    lax.fori_loop(0, ngroups - 1, body, 0)
    group_b(ngroups - 1, (ngroups - 1) % 2)

    def finish(j, carry):
        t0 = pl.multiple_of(j * ln, ln)
        hs = hf_scr[pl.ds(t0, ln), :] + hb_scr[pl.ds(t0, ln), :]
        ms = jnp.mean(hs * hs, axis=1, keepdims=True)
        y = hs * lax.rsqrt(ms + EPS) * nw_ref[...]
        ya_ref[pl.ds(t0, ln), :] = (_sigmoid(o_ref[pl.ds(t0, ln), :]) * y).astype(BF16)
        return carry

    lax.fori_loop(0, nchunks, finish, 0)

    if emit_state:
        for d in range(2):
            cn_ref[d] = c_scr[d, :, 0:dv]
            nn_ref[d] = c_scr[d, :, dv:ext].T[0:1, :]
            mn_ref[d] = dm_scr[d, (ngroups - 1) % 2, gsz - 1, 1:2, 0:1]


def _mlstm_tri():
    r = np.arange(MLSTM_L)
    fwd = (r[:, None] <= r[None, :]).astype(np.float32)
    return jnp.asarray(np.stack([fwd, fwd.T]))


def _mlstm(q, kt, v, o, gt, norm_w, *, row0, n_seq, t_len, heads, dk, dv, state=None, emit_state=False):
    blk0 = row0 // t_len
    tok = lambda s, h: (blk0 + s, h)
    in_specs = [pl.BlockSpec((t_len, dk), tok), pl.BlockSpec((dk, t_len), lambda s, h: (h, blk0 + s)),
                pl.BlockSpec((t_len, dv), tok), pl.BlockSpec((t_len, dv), tok),
                pl.BlockSpec((SUBLANES, t_len), lambda s, h: (h, blk0 + s)),
                pl.BlockSpec((None, 1, dv), lambda s, h: (h, 0, 0)),
                pl.BlockSpec((2, MLSTM_L, MLSTM_L), lambda s, h: (0, 0, 0))]
    args = [q, kt, v, o, gt, norm_w.reshape(heads, 1, dv), _mlstm_tri()]
    if state is not None:
        c0, n0, m0 = state
        in_specs += [pl.BlockSpec((None, 2, None, dk, dv), lambda s, h: (s, 0, h, 0, 0)),
                     pl.BlockSpec((None, 2, None, 1, dk), lambda s, h: (s, 0, h, 0, 0)),
                     pl.BlockSpec((None, 2, None, 1, 1), lambda s, h: (s, 0, h, 0, 0))]
        args += [c0, n0.reshape(n_seq, 2, heads, 1, dk), m0.reshape(n_seq, 2, heads, 1, 1)]
    out_specs = [pl.BlockSpec((t_len, dv), lambda s, h: (s, h))]
    out_shape = [jax.ShapeDtypeStruct((n_seq * t_len, heads * dv), BF16)]
    if emit_state:
        out_specs += [pl.BlockSpec((None, 2, None, dk, dv), lambda s, h: (s, 0, h, 0, 0)),
                      pl.BlockSpec((None, 2, None, 1, dk), lambda s, h: (s, 0, h, 0, 0)),
                      pl.BlockSpec((None, 2, None, 1, 1), lambda s, h: (s, 0, h, 0, 0))]
        out_shape += [jax.ShapeDtypeStruct((n_seq, 2, heads, dk, dv), F32),
                      jax.ShapeDtypeStruct((n_seq, 2, heads, 1, dk), F32),
                      jax.ShapeDtypeStruct((n_seq, 2, heads, 1, 1), F32)]
    kern = functools.partial(_mlstm_kernel, t_len=t_len, dk=dk, dv=dv, has_state=state is not None,
                             emit_state=emit_state)
    return pl.pallas_call(
        kern,
        grid=(n_seq, heads),
        in_specs=in_specs,
        out_specs=out_specs,
        out_shape=out_shape,
        scratch_shapes=[pltpu.VMEM((t_len, dv), F32), pltpu.VMEM((t_len, dv), F32),
                        pltpu.VMEM((2, dk, dv + LANES), F32), pltpu.VMEM((2, 1, 1), F32),
                        pltpu.VMEM((2, 2, 2, MLSTM_L, MLSTM_L), BF16), pltpu.VMEM((2, 2, 2, MLSTM_L, LANES), F32),
                        pltpu.VMEM((2, 2, 2, MLSTM_L, LANES), F32), pltpu.VMEM((2, 2, 2, dk, MLSTM_L), BF16),
                        pltpu.VMEM((2, 2, 2, SUBLANES, LANES), F32)],
        compiler_params=_cparams(("arbitrary", "arbitrary")),
        name="mlstm_state" if emit_state else "mlstm",
    )(*args)


def _gelu_tanh(x):
    return x * (0.5 * (1.0 + jnp.tanh(0.7978845608028654 * (x + 0.044715 * (x * x * x)))))


def _softplus(x):
    return jnp.maximum(x, 0.0) + jnp.log1p(jnp.exp(-jnp.abs(x)))


SCAN_SPLIT = 1
SCAN_GROUP = 4


def _rglru_kernel(*refs, t_len, seg, sub, pitch, tc, has_state, emit_state):
    it = iter(refs)
    xr_ref, xg_ref, cw_ref, cb_ref, wg_ref, bg_ref, lam_ref = (next(it) for _ in range(7))
    if has_state:
        h0_ref = next(it)
    yb_ref = next(it)
    if emit_state:
        hn_ref = next(it)
    a_scr, u_scr, cin_scr = (next(it) for _ in range(3))
    nchunks = t_len // tc
    piece = min(tc, sub)
    npieces = tc // piece
    ntile = xr_ref.shape[1] // LANES
    chains = [(d, lt) for d in range(2) for lt in range(ntile)]

    def scan_rows(t0, p):
        t = t0 + p * piece
        i = t // sub
        return pl.ds(pl.multiple_of(i * pitch + (t - i * sub), SUBLANES), piece), i

    ka = (-0.5 * RGLRU_C * 1.4426950408889634) * _softplus(-lam_ref[...])

    def gates(c, carry):
        t0 = pl.multiple_of(c * tc, tc)
        pos = lax.broadcasted_iota(I32, (tc, LANES), 0) % seg
        for lt in range(ntile):
            cols = slice(lt * LANES, (lt + 1) * LANES)
            x = xr_ref[pl.ds(t0, tc), cols]
            xc = cb_ref[:, cols] + cw_ref[CONV_LEFT:CONV_LEFT + 1, cols] * x
            for j in range(cw_ref.shape[0]):
                off = j - CONV_LEFT
                if off == 0:
                    continue
                shifted = pltpu.roll(x, (-off) % tc, 0)
                ok = (pos >= -off) if off < 0 else (pos < seg - off)
                xc = xc + cw_ref[j:j + 1, cols] * jnp.where(ok, shifted, 0.0)
            zh = jnp.dot(xc.astype(BF16), wg_ref[lt], preferred_element_type=F32) + bg_ref[lt]
            hx = 0.5 * xc
            for d in range(2):
                kd = ka[d:d + 1, cols]
                a = jnp.exp2(jnp.tanh(zh[:, (2 * d) * LANES:(2 * d + 1) * LANES]) * kd + kd)
                igx = hx * jnp.tanh(zh[:, (2 * d + 1) * LANES:(2 * d + 2) * LANES]) + hx
                u = jnp.sqrt(1.0 - a * a) * igx
                for p in range(npieces):
                    rows, _ = scan_rows(t0, p)
                    a_scr[d, lt, rows, :] = a[p * piece:(p + 1) * piece]
                    u_scr[d, lt, rows, :] = u[p * piece:(p + 1) * piece]
        return carry

    lax.fori_loop(0, nchunks, gates, 0)

    seg_len = sub // SCAN_SPLIT
    seg_chains = [(d, lt, s) for (d, lt) in chains for s in range(SCAN_SPLIT)]

    def scan(jg, carry):
        def rows(d, s, k):
            j = jg * SCAN_GROUP + k
            return pl.ds(s * seg_len + (j if d == 0 else seg_len - 1 - j), SUBLANES, stride=pitch)

        loaded = [[(a_scr[d, lt, rows(d, s, k), :], u_scr[d, lt, rows(d, s, k), :]) for k in range(SCAN_GROUP)]
                  for (d, lt, s) in seg_chains]
        out = []
        for (d, lt, s), (h, p), steps in zip(seg_chains, carry, loaded):
            for k, (a, u) in enumerate(steps):
                h = a * h + u
                p = a * p
                a_scr[d, lt, rows(d, s, k), :] = p
                u_scr[d, lt, rows(d, s, k), :] = h
            out.append((h, p))
        return tuple(out)

    zero = jnp.zeros((SUBLANES, LANES), F32)
    one = jnp.ones((SUBLANES, LANES), F32)
    assert seg_len % SCAN_GROUP == 0
    ends = dict(zip(seg_chains, lax.fori_loop(0, seg_len // SCAN_GROUP, scan, tuple((zero, one) for _ in seg_chains))))

    order = [(i, s) for i in range(SUBLANES) for s in range(SCAN_SPLIT)]
    for (d, lt) in chains:
        cols = slice(lt * LANES, (lt + 1) * LANES)
        cin = h0_ref[d:d + 1, cols] if has_state else jnp.zeros((1, LANES), F32)
        for (i, s) in (order if d == 0 else reversed(order)):
            h, p = ends[(d, lt, s)]
            cin_scr[d, lt, s * SUBLANES + i:s * SUBLANES + i + 1, :] = cin
            cin = h[i:i + 1] + p[i:i + 1] * cin
        if emit_state:
            hn_ref[d:d + 1, cols] = cin

    fpiece = min(tc, seg_len)

    def finish(c, carry):
        t0 = pl.multiple_of(c * tc, tc)
        for lt in range(ntile):
            cols = slice(lt * LANES, (lt + 1) * LANES)
            for p in range(tc // fpiece):
                t = t0 + p * fpiece
                i = t // sub
                local = t - i * sub
                rows = pl.ds(pl.multiple_of(i * pitch + local, SUBLANES), fpiece)
                crow = pl.ds((local // seg_len) * SUBLANES + i, 1)
                h = (u_scr[0, lt, rows, :] + a_scr[0, lt, rows, :] * cin_scr[0, lt, crow, :]
                     + u_scr[1, lt, rows, :] + a_scr[1, lt, rows, :] * cin_scr[1, lt, crow, :])
                nat = pl.ds(pl.multiple_of(t, SUBLANES), fpiece)
                yb_ref[nat, cols] = (h * _gelu_tanh(xg_ref[nat, cols])).astype(BF16)
        return carry

    lax.fori_loop(0, nchunks, finish, 0)


def _rglru(xr, xg, conv_w, conv_b, wg, bg, lam, *, row0, n_seq, t_len, seg, state=None, emit_state=False):
    rw = xr.shape[1]
    ntile = 2
    cb = ntile * LANES
    assert rw % cb == 0
    blk0 = row0 // t_len
    sub = t_len // SUBLANES
    pitch = sub + SUBLANES
    tc = 256
    tok = lambda s, g: (blk0 + s, g)
    in_specs = [pl.BlockSpec((t_len, cb), tok), pl.BlockSpec((t_len, cb), tok),
                pl.BlockSpec((conv_w.shape[0], cb), lambda s, g: (0, g)),
                pl.BlockSpec((1, cb), lambda s, g: (0, g)),
                pl.BlockSpec((ntile, LANES, 4 * LANES), lambda s, g: (g, 0, 0)),
                pl.BlockSpec((ntile, 1, 4 * LANES), lambda s, g: (g, 0, 0)),
                pl.BlockSpec((2, cb), lambda s, g: (0, g))]
    args = [xr, xg, conv_w, conv_b.reshape(1, rw), wg, bg, lam]
    if state is not None:
        in_specs.append(pl.BlockSpec((None, 2, cb), lambda s, g: (s, 0, g)))
        args.append(state)
    out_specs = [pl.BlockSpec((t_len, cb), lambda s, g: (s, g))]
    out_shape = [jax.ShapeDtypeStruct((n_seq * t_len, rw), BF16)]
    if emit_state:
        out_specs.append(pl.BlockSpec((None, 2, cb), lambda s, g: (s, 0, g)))
        out_shape.append(jax.ShapeDtypeStruct((n_seq, 2, rw), F32))
    kern = functools.partial(_rglru_kernel, t_len=t_len, seg=seg, sub=sub, pitch=pitch, tc=tc,
                             has_state=state is not None, emit_state=emit_state)
    return pl.pallas_call(
        kern,
        grid=(n_seq, rw // cb),
        in_specs=in_specs,
        out_specs=out_specs,
        out_shape=out_shape,
        scratch_shapes=[pltpu.VMEM((2, ntile, SUBLANES * pitch, LANES), F32),
                        pltpu.VMEM((2, ntile, SUBLANES * pitch, LANES), F32),
                        pltpu.VMEM((2, ntile, SCAN_SPLIT * SUBLANES, LANES), F32)],
        compiler_params=_cparams(("arbitrary", "arbitrary")),
        name="rglru_state" if emit_state else "rglru",
    )(*args)


def _pack_bf16_pair(lo, hi):
    def rounded_bits(x):
        return pltpu.bitcast(x.astype(BF16).astype(F32), U32)
    return (rounded_bits(lo) >> 16) | rounded_bits(hi)


def _unpack_bf16_pair(w):
    lo = pltpu.bitcast(w << 16, F32).astype(BF16)
    hi = pltpu.bitcast(w & jnp.uint32(0xFFFF0000), F32).astype(BF16)
    return lo, hi


def _outproj_kernel(yac_ref, yal_ref, ybc_ref, ybl_ref, xp_ref, xs_ref, mod_ref, n2_ref, wo_ref, wr_ref, br_ref,
                    x1_ref, hp_ref, ridx_ref, rwt_ref,
                    *, nctx_tiles, ntok_tiles, tiles_per_lat, d, mw, n_groups, epg, sub_rows):
    i = pl.program_id(0)

    @pl.when(i == ntok_tiles)
    def _():
        x1_ref[...] = jnp.zeros_like(x1_ref)
        hp_ref[...] = jnp.zeros_like(hp_ref)
        ridx_ref[...] = jnp.zeros_like(ridx_ref)
        rwt_ref[...] = jnp.zeros_like(rwt_ref)

    def token_tile(x_ref, ya_ref, yb_ref):
        for r0 in range(0, x_ref.shape[0], sub_rows):
            token_rows(x_ref, ya_ref, yb_ref, r0)

    def token_rows(x_ref, ya_ref, yb_ref, r0):
        rows = slice(r0, r0 + sub_rows)
        x, ya, yb = x_ref[rows, :], ya_ref[rows, :], yb_ref[rows, :]
        row = _mod_row(i, nctx_tiles, tiles_per_lat)
        gate1 = mod_ref[pl.ds(row, 1), 2 * d:3 * d]
        shift2 = mod_ref[pl.ds(row, 1), 3 * d:4 * d]
        scale2 = mod_ref[pl.ds(row, 1), 4 * d:5 * d]
        y = (jnp.dot(ya, wo_ref[0:mw, :], preferred_element_type=F32)
             + jnp.dot(yb, wo_ref[mw:, :], preferred_element_type=F32))
        x1 = x + gate1 * y
        x1_ref[rows, :] = x1
        h2 = _modulated_norm(x1, n2_ref[...], shift2, scale2)
        half = d // 2
        packed = _pack_bf16_pair(h2[:, :half], h2[:, half:])
        trows = half // LANES
        for s in range(trows):
            hp_ref[pl.ds(r0 * trows + s, sub_rows, stride=trows), :] = packed[:, s * LANES:(s + 1) * LANES]

        lt = lax.dot_general(wr_ref[...], h2.astype(BF16), (((1,), (1,)), ((), ())),
                             preferred_element_type=F32) + br_ref[:, 0:1]
        gidx = lax.broadcasted_iota(I32, (SUBLANES, lt.shape[1]), 0)
        gl = jnp.where(gidx < n_groups, lt[0:SUBLANES], -jnp.inf)
        gmax = jnp.max(gl, axis=0, keepdims=True)
        grp = jnp.min(jnp.where(gl == gmax, gidx, n_groups), axis=0, keepdims=True)
        p_grp = 1.0 / jnp.sum(jnp.exp(gl - gmax), axis=0, keepdims=True)
        el = lt[SUBLANES:SUBLANES + epg]
        for g in range(1, n_groups):
            el = jnp.where(grp == g, lt[SUBLANES + g * epg:SUBLANES + (g + 1) * epg], el)
        eidx = lax.broadcasted_iota(I32, el.shape, 0)
        v1 = jnp.max(el, axis=0, keepdims=True)
        i1 = jnp.min(jnp.where(el == v1, eidx, epg), axis=0, keepdims=True)
        el2 = jnp.where(eidx == i1, -jnp.inf, el)
        v2 = jnp.max(el2, axis=0, keepdims=True)
        i2 = jnp.min(jnp.where(el2 == v2, eidx, epg), axis=0, keepdims=True)
        e2 = jnp.exp(v2 - v1)
        w1 = p_grp / (1.0 + e2)
        w2 = p_grp * e2 / (1.0 + e2)
        rid = lax.broadcasted_iota(I32, (SUBLANES, sub_rows), 0)
        ridx_ref[:, rows] = jnp.where(rid == 0, grp * epg + i1, jnp.where(rid == 1, grp * epg + i2, 0))
        rwt_ref[:, rows] = jnp.where(rid == 0, w1, jnp.where(rid == 1, w2, 0.0))

    @pl.when(i < nctx_tiles)
    def _():
        token_tile(xp_ref, yac_ref, ybc_ref)

    @pl.when((i >= nctx_tiles) & (i < ntok_tiles))
    def _():
        token_tile(xs_ref, yal_ref, ybl_ref)


def _out_proj(ya_c, ya_l, yb_c, yb_l, xp, xs, mod, norm2_w, w_out, wr, br, *, lat_seq, n_groups, epg):
    nc, d = xp.shape
    nl = xs.shape[0]
    nt = nc + nl
    mw = ya_c.shape[1]
    tm, sub_rows = 512, 256
    trows = (d // 2) // LANES
    nctx_tiles, ntok_tiles = nc // tm, nt // tm
    kern = functools.partial(_outproj_kernel, nctx_tiles=nctx_tiles, ntok_tiles=ntok_tiles,
                             tiles_per_lat=lat_seq // tm, d=d, mw=mw, n_groups=n_groups, epg=epg, sub_rows=sub_rows)
    ctx = lambda i: (jnp.minimum(i, nctx_tiles - 1), 0)
    lat = lambda i: (jnp.clip(i - nctx_tiles, 0, nl // tm - 1), 0)
    row = lambda i: (i, 0)
    const = lambda i: (0, 0)
    return pl.pallas_call(
        kern,
        grid=(ntok_tiles + 1,),
        in_specs=[pl.BlockSpec((tm, mw), ctx), pl.BlockSpec((tm, mw), lat),
                  pl.BlockSpec((tm, yb_c.shape[1]), ctx), pl.BlockSpec((tm, yb_c.shape[1]), lat),
                  pl.BlockSpec((tm, d), ctx), pl.BlockSpec((tm, d), lat),
                  pl.BlockSpec(mod.shape, const),
                  pl.BlockSpec((1, d), const),
                  pl.BlockSpec(w_out.shape, const, pipeline_mode=pl.Buffered(1)),
                  pl.BlockSpec(wr.shape, const),
                  pl.BlockSpec(br.shape, const)],
        out_specs=[pl.BlockSpec((tm, d), row), pl.BlockSpec((tm * trows, LANES), row),
                   pl.BlockSpec((SUBLANES, tm), lambda i: (0, i)), pl.BlockSpec((SUBLANES, tm), lambda i: (0, i))],
        out_shape=[jax.ShapeDtypeStruct((nt + tm, d), F32), jax.ShapeDtypeStruct(((nt + tm) * trows, LANES), U32),
                   jax.ShapeDtypeStruct((SUBLANES, nt + tm), I32), jax.ShapeDtypeStruct((SUBLANES, nt + tm), F32)],
        compiler_params=_cparams(("arbitrary",)),
        name="out_proj",
    )(ya_c, ya_l, yb_c, yb_l, xp, xs, mod, norm2_w.reshape(1, d), w_out, wr, br)


MOE_ROWS = 256


ROUTE_TILE = 1024


def _rank_kernel(ridx_ref, tri_ref, rank_ref, cnt_ref, carry_scr, *, n_exp):
    @pl.when(pl.program_id(0) == 0)
    def _():
        carry_scr[...] = jnp.zeros_like(carry_scr)

    e = ridx_ref[...]
    tr = e.shape[1]
    eid = lax.broadcasted_iota(I32, (n_exp, tr), 0)
    carry = carry_scr[:, 0:1]
    ranks = []
    for kk in range(TOP_K):
        hit = eid == e[kk:kk + 1]
        cum = jnp.dot(jnp.where(hit, 1.0, 0.0).astype(BF16), tri_ref[...], preferred_element_type=F32)
        ranks.append(jnp.sum(jnp.where(hit, cum + carry, 0.0), axis=0, keepdims=True) - 1.0)
        carry = carry + cum[:, tr - 1:tr]
    carry_scr[...] = jnp.broadcast_to(carry, carry_scr.shape)
    cnt_ref[...] = jnp.broadcast_to(carry, cnt_ref.shape)
    rid = lax.broadcasted_iota(I32, rank_ref.shape, 0)
    rank_ref[...] = jnp.where(rid == 0, ranks[0], jnp.where(rid == 1, ranks[1], 0.0)).astype(I32)


def _dest_kernel(ridx_ref, rank_ref, pstart_ref, dest_ref, *, n_exp):
    e = ridx_ref[...]
    tr = e.shape[1]
    eid = lax.broadcasted_iota(I32, (n_exp, tr), 0)
    ps = pstart_ref[:, 0:1]
    rows = [jnp.sum(jnp.where(eid == e[kk:kk + 1], ps, 0.0), axis=0, keepdims=True) for kk in range(TOP_K)]
    rid = lax.broadcasted_iota(I32, dest_ref.shape, 0)
    dest_ref[...] = rank_ref[...] + jnp.where(rid == 0, rows[0], jnp.where(rid == 1, rows[1], 0.0)).astype(I32)


def _invert_kernel(dest_ref, fill_ref, rowtok_ref, sem):
    i = pl.program_id(0)
    tr = dest_ref.shape[0] // TOP_K

    @pl.when(i == 0)
    def _():
        fill = pltpu.make_async_copy(fill_ref, rowtok_ref, sem.at[0])
        fill.start()
        fill.wait()

    def body(r, c):
        for kk in range(TOP_K):
            rowtok_ref[dest_ref[kk * tr + r]] = i * tr + r
        return c
    lax.fori_loop(0, tr, body, 0, unroll=16)


def _routing(ridx, n_tok, n_exp, n_blocks, zero_row):
    tr = ROUTE_TILE if n_tok % ROUTE_TILE == 0 else ROUTE_TILE // 2
    steps = n_tok // tr
    tri = jnp.asarray(np.triu(np.ones((tr, tr), np.float32)), BF16)
    tile = pl.BlockSpec((SUBLANES, tr), lambda i: (0, i))
    cnt_spec = pl.BlockSpec((n_exp, LANES), lambda i: (0, 0))
    rank, cnt = pl.pallas_call(
        functools.partial(_rank_kernel, n_exp=n_exp),
        grid=(steps,),
        in_specs=[tile, pl.BlockSpec((tr, tr), lambda i: (0, 0))],
        out_specs=[tile, cnt_spec],
        out_shape=[jax.ShapeDtypeStruct((SUBLANES, n_tok), I32), jax.ShapeDtypeStruct((n_exp, LANES), F32)],
        scratch_shapes=[pltpu.VMEM((n_exp, LANES), F32)],
        compiler_params=_cparams(("arbitrary",)),
        name="route_rank",
    )(ridx, tri)
    counts = cnt[:, 0].astype(I32)
    padded = (counts + MOE_ROWS - 1) // MOE_ROWS * MOE_ROWS
    pad_end = jnp.cumsum(padded)
    pad_start = pad_end - padded
    dest = pl.pallas_call(
        functools.partial(_dest_kernel, n_exp=n_exp),
        grid=(steps,),
        in_specs=[tile, tile, cnt_spec],
        out_specs=tile,
        out_shape=jax.ShapeDtypeStruct((SUBLANES, n_tok), I32),
        compiler_params=_cparams(("arbitrary",)),
        name="route_dest",
    )(ridx, rank, jnp.broadcast_to(pad_start.astype(F32)[:, None], (n_exp, LANES)))
    row_tok = pl.pallas_call(
        _invert_kernel,
        grid=(steps,),
        in_specs=[pl.BlockSpec((TOP_K * tr,), lambda i: (i,), memory_space=pltpu.SMEM),
                  pl.BlockSpec(memory_space=pl.ANY)],
        out_specs=pl.BlockSpec(memory_space=pltpu.SMEM),
        out_shape=jax.ShapeDtypeStruct((n_blocks * MOE_ROWS,), I32),
        scratch_shapes=[pltpu.SemaphoreType.DMA((1,))],
        compiler_params=_cparams(("arbitrary",)),
        name="route_invert",
    )(dest[0:TOP_K].reshape(TOP_K, steps, tr).transpose(1, 0, 2).reshape(-1),
      jnp.full((n_blocks * MOE_ROWS,), zero_row, I32))
    blk_row0 = jnp.arange(n_blocks, dtype=I32) * MOE_ROWS
    blk_e = jnp.minimum(jnp.sum((pad_end[None, :] <= blk_row0[:, None]).astype(I32), axis=1), n_exp - 1)
    nused = (pad_end[-1:] // MOE_ROWS).astype(I32)
    return dest, row_tok, blk_e, nused


CAST_ROWS = 256
WEIGHT_DMA_SPLIT = 8


GATHER_DEPTH = 4


def _expert_kernel(blk_e_ref, nused_ref, *refs, half, rows, trows):
    tok_refs = refs[:GATHER_DEPTH]
    (src_ref, wg_hbm, wu_hbm, wd_hbm, y_ref, xbuf, xsem, stage_g, stage_u, stage_d, bf_g, bf_u, bf_d,
     wsem) = refs[GATHER_DEPTH:]
    b = pl.program_id(0)
    nused = nused_ref[0]
    w_hbm = (wg_hbm, wu_hbm, wd_hbm)
    stage = (stage_g, stage_u, stage_d)
    wbf = (bf_g, bf_u, bf_d)

    def weight_copy(e, j):
        return pltpu.make_async_copy(w_hbm[j].at[e], stage[j], wsem.at[j])

    def start_weights(e, j):
        step = stage[j].shape[0] // WEIGHT_DMA_SPLIT
        for c in range(WEIGHT_DMA_SPLIT):
            sl = pl.ds(c * step, step)
            pltpu.make_async_copy(w_hbm[j].at[e, sl, :], stage[j].at[sl, :], wsem.at[j]).start(priority=1)

    def row_copy(tok, slot, r):
        src = src_ref.at[pl.ds(pl.multiple_of(tok * trows, trows), trows), :]
        return pltpu.make_async_copy(src, xbuf.at[slot, pl.ds(r * trows, trows), :], xsem.at[slot])

    @pl.when(b >= nused)
    def _():
        y_ref[...] = jnp.zeros_like(y_ref)

    @pl.when(b < nused)
    def _():
        e = blk_e_ref[b]

        @pl.when(b == 0)
        def _():
            for j in range(3):
                start_weights(e, j)

        for first in range(GATHER_DEPTH - 1):
            @pl.when((b == 0) & (nused > first))
            def _(first=first):
                def body(r, c):
                    row_copy(tok_refs[first][r], first, r).start()
                    return c
                lax.fori_loop(0, rows, body, 0, unroll=8)

        @pl.when((b == 0) | (blk_e_ref[jnp.maximum(b - 1, 0)] != e))
        def _():
            nb = lax.while_loop(lambda k: (k < nused) & (blk_e_ref[jnp.minimum(k, nused - 1)] == e),
                                lambda k: k + 1, b + 1)
            for j in range(3):
                weight_copy(e, j).wait()
                n_steps = stage[j].shape[0] // CAST_ROWS

                def cast(c, carry, j=j):
                    sl = pl.ds(pl.multiple_of(c * CAST_ROWS, CAST_ROWS), CAST_ROWS)
                    wbf[j][sl, :] = stage[j][sl, :].astype(BF16)
                    return carry
                lax.fori_loop(0, n_steps, cast, 0)

                @pl.when(nb < nused)
                def _(j=j):
                    start_weights(blk_e_ref[jnp.minimum(nb, nused - 1)], j)

        slot = b % GATHER_DEPTH
        pltpu.make_async_copy(src_ref.at[pl.ds(0, rows * trows), :], xbuf.at[slot], xsem.at[slot]).wait()

        def compute(prefetch):
            parts = [_unpack_bf16_pair(xbuf[slot, pl.ds(s, rows, stride=trows), :]) for s in range(trows)]
            lo = jnp.concatenate([p[0] for p in parts], axis=1)
            hi = jnp.concatenate([p[1] for p in parts], axis=1)
            if prefetch:
                ahead = (b + GATHER_DEPTH - 1) % GATHER_DEPTH
                for r in range(rows):
                    row_copy(tok_refs[GATHER_DEPTH - 1][r], ahead, r).start()

            def up(w_ref):
                return (jnp.dot(lo, w_ref[0:half, :], preferred_element_type=F32)
                        + jnp.dot(hi, w_ref[half:, :], preferred_element_type=F32))

            g = up(wbf[0])
            h = ((g * _sigmoid(g)) * up(wbf[1])).astype(BF16)
            y = jnp.dot(h, wbf[2][...], preferred_element_type=F32)
            packed = _pack_bf16_pair(y[:, :half], y[:, half:])
            for s in range(trows):
                y_ref[pl.ds(s, rows, stride=trows), :] = packed[:, s * LANES:(s + 1) * LANES]

        @pl.when(b + GATHER_DEPTH - 1 < nused)
        def _():
            compute(True)

        @pl.when(b + GATHER_DEPTH - 1 >= nused)
        def _():
            compute(False)


def _experts(blk_e, nused, row_tok, src, wg, wu, wd, n_blocks):
    n_exp, d, ff = wg.shape
    any_spec = pl.BlockSpec(memory_space=pl.ANY)
    trows = (d // 2) // LANES
    def tok_spec(ahead):
        return pl.BlockSpec((MOE_ROWS,), lambda b, be, nu: (jnp.minimum(b + ahead, n_blocks - 1),),
                            memory_space=pltpu.SMEM)

    return pl.pallas_call(
        functools.partial(_expert_kernel, half=d // 2, rows=MOE_ROWS, trows=trows),
        grid_spec=pltpu.PrefetchScalarGridSpec(
            num_scalar_prefetch=2,
            grid=(n_blocks,),
            in_specs=[tok_spec(a) for a in range(GATHER_DEPTH)] + [any_spec, any_spec, any_spec, any_spec],
            out_specs=pl.BlockSpec((MOE_ROWS * trows, LANES), lambda b, be, nu: (b, 0)),
            scratch_shapes=[pltpu.VMEM((GATHER_DEPTH, MOE_ROWS * trows, LANES), U32),
                            pltpu.SemaphoreType.DMA((GATHER_DEPTH,)),
                            pltpu.VMEM((d, ff), F32), pltpu.VMEM((d, ff), F32), pltpu.VMEM((ff, d), F32),
                            pltpu.VMEM((d, ff), BF16), pltpu.VMEM((d, ff), BF16), pltpu.VMEM((ff, d), BF16),
                            pltpu.SemaphoreType.DMA((3,))]),
        out_shape=jax.ShapeDtypeStruct((n_blocks * MOE_ROWS * trows, LANES), U32),
        compiler_params=_cparams(("arbitrary",)),
        name="experts",
    )(blk_e, nused, *([row_tok] * GATHER_DEPTH), src, wg, wu, wd)


COMBINE_BUFS = 3


def _combine_kernel(dest_ref, x1_ref, wt_ref, mod_ref, fw_ref, y_ref, o_ref, *scratch,
                    tile0, n_all, tiles_per_seq, d, lat):
    i = pl.program_id(0)
    last = pl.num_programs(0) - 1
    tm = x1_ref.shape[0]
    bufs, sem = scratch[:-1], scratch[-1]
    nbuf = len(bufs)

    trows = (d // 2) // LANES

    def row_copy(tile, par, kk, r):
        row = dest_ref[kk * n_all + (tile0 + tile) * tm + r]
        src = y_ref.at[pl.ds(pl.multiple_of(row * trows, trows), trows), :]
        return pltpu.make_async_copy(src, bufs[par].at[kk, pl.ds(r * trows, trows), :], sem.at[par])

    for first in range(nbuf - 1):
        @pl.when((i == 0) & (first <= last))
        def _(first=first):
            def body(r, c):
                for kk in range(TOP_K):
                    row_copy(first, first, kk, r).start()
                return c
            lax.fori_loop(0, tm, body, 0, unroll=8)

    def step(par, prefetch):
        for kk in range(TOP_K):
            pltpu.make_async_copy(y_ref.at[pl.ds(0, tm * trows), :], bufs[par].at[kk], sem.at[par]).wait()
        if prefetch:
            for r in range(tm):
                for kk in range(TOP_K):
                    row_copy(i + nbuf - 1, (par + nbuf - 1) % nbuf, kk, r).start(priority=kk)
        row = (1 + i // tiles_per_seq) if lat else 0
        gate2 = mod_ref[pl.ds(row, 1), 5 * d:6 * d]
        wt = wt_ref[...]
        half = d // 2
        w0 = _row_to_col(wt[0:1], tm)
        w1 = _row_to_col(wt[1:2], tm)
        unpack = (lambda p: pltpu.bitcast(p << 16, F32),
                  lambda p: pltpu.bitcast(p & jnp.uint32(0xFFFF0000), F32))
        ssq = jnp.zeros((tm, 1), F32)
        for s in range(trows):
            p0 = bufs[par][0, pl.ds(s, tm, stride=trows), :]
            p1 = bufs[par][1, pl.ds(s, tm, stride=trows), :]
            for side in range(2):
                cols = slice(side * half + s * LANES, side * half + (s + 1) * LANES)
                x = x1_ref[:, cols] + gate2[:, cols] * (w0 * unpack[side](p0) + w1 * unpack[side](p1))
                ssq = ssq + jnp.sum(x * x, axis=-1, keepdims=True)
                o_ref[:, cols] = x
        scale = lax.rsqrt(ssq * (1.0 / d) + EPS)
        o_ref[...] = o_ref[...] * scale * fw_ref[...]

    for par in range(nbuf):
        @pl.when((i % nbuf == par) & (i + nbuf - 1 <= last))
        def _(par=par):
            step(par, True)

        @pl.when((i % nbuf == par) & (i + nbuf - 1 > last))
        def _(par=par):
            step(par, False)


def _combine(dest_flat, x1, rwt, mod, final_w, y_rows, *, row0, n_tok, seq_len, lat):
    d = x1.shape[1]
    tm = 256
    tile0 = row0 // tm
    trows = (d // 2) // LANES
    n_all = dest_flat.shape[0] // TOP_K
    kern = functools.partial(_combine_kernel, tile0=tile0, n_all=n_all, tiles_per_seq=seq_len // tm, d=d, lat=lat)
    return pl.pallas_call(
        kern,
        grid_spec=pltpu.PrefetchScalarGridSpec(
            num_scalar_prefetch=1,
            grid=(n_tok // tm,),
            in_specs=[pl.BlockSpec((tm, d), lambda i, dr: (tile0 + i, 0)),
                      pl.BlockSpec((SUBLANES, tm), lambda i, dr: (0, tile0 + i)),
                      pl.BlockSpec(mod.shape, lambda i, dr: (0, 0)),
                      pl.BlockSpec((1, d), lambda i, dr: (0, 0)),
                      pl.BlockSpec(memory_space=pl.ANY)],
            out_specs=pl.BlockSpec((tm, d), lambda i, dr: (i, 0)),
            scratch_shapes=[pltpu.VMEM((TOP_K, tm * trows, LANES), U32) for _ in range(COMBINE_BUFS)]
            + [pltpu.SemaphoreType.DMA((COMBINE_BUFS,))]),
        out_shape=jax.ShapeDtypeStruct((n_tok, d), F32),
        compiler_params=_cparams(("arbitrary",)),
        name="combine_lat" if lat else "combine_ctx",
    )(dest_flat, x1, rwt, mod, final_w.reshape(1, d), y_rows)


def _gate_layout(w_gates, b_gates, heads):
    d = w_gates.shape[0]
    w = w_gates.reshape(d, 4, heads).transpose(0, 2, 1)
    w = jnp.pad(w, ((0, 0), (0, 0), (0, SUBLANES - 4))).reshape(d, heads * SUBLANES)
    b = b_gates.reshape(4, heads).T
    b = jnp.pad(b, ((0, 0), (0, SUBLANES - 4))).reshape(1, heads * SUBLANES)
    padl = LANES - heads * SUBLANES
    return jnp.pad(w, ((0, 0), (0, padl))), jnp.pad(b, ((0, 0), (0, padl)))


def kernel(x_prompt, x_sample, state_mlstm_c, state_mlstm_n, state_mlstm_m, state_rglru_h, c, c_ctx, w_ada, b_ada,
           norm1_w, w_in, b_gates, conv_w, conv_b, rg_wa, rg_ba, rg_wx, rg_bx, rg_lambda, mlstm_norm_w, w_out,
           norm2_w, router_group_w, router_group_b, router_expert_w, router_expert_b, expert_w_gate, expert_w_up,
           expert_w_down, final_norm_w):
    n_req, seq, d = x_prompt.shape
    n_lat, lat_seq, _ = x_sample.shape
    depth = w_in.shape[0]
    assert depth == 1, "the token-axis plumbing below is written for the single-layer trunk"
    heads, dk, dv = state_mlstm_c.shape[3:]
    rw = state_rglru_h.shape[-1]
    nblk = rg_wa.shape[2]
    assert rw // nblk == LANES
    n_groups, epg = router_expert_w.shape[1], router_expert_w.shape[3]
    n_exp = n_groups * epg
    qk, mw = heads * dk, heads * dv
    nc, nl = n_req * seq, n_lat * lat_seq
    nt = nc + nl
    assert nc % lat_seq == 0 and n_lat + 1 <= SUBLANES
    l = 0

    xp = x_prompt.reshape(nc, d)
    xs = x_sample.reshape(nl, d)
    cvec = jnp.zeros((SUBLANES, d), F32).at[0].set(c_ctx).at[1:1 + n_lat].set(c)
    mod = _ada(cvec, w_ada[l], b_ada[l])

    w = w_in[l]
    g0 = 2 * qk + 2 * mw
    wgate, bgate = _gate_layout(w[:, g0:g0 + 4 * heads], b_gates[l], heads)
    w_cat = jnp.concatenate([w[:, :g0], wgate, w[:, g0 + 4 * heads:]], axis=1).astype(BF16)
    q, kt, v, o, gt, xr, xg = _in_proj(xp, xs, mod, norm1_w[l], w_cat, bgate, lat_seq=lat_seq, heads=heads, dk=dk,
                                       dv=dv, rw=rw)

    mkw = dict(heads=heads, dk=dk, dv=dv)
    ya_c, new_c, new_n, new_m = _mlstm(q, kt, v, o, gt, mlstm_norm_w[l], row0=0, n_seq=n_req, t_len=seq,
                                       emit_state=True, **mkw)
    (ya_l,) = _mlstm(q, kt, v, o, gt, mlstm_norm_w[l], row0=nc, n_seq=n_lat, t_len=lat_seq,
                     state=(state_mlstm_c[:, l], state_mlstm_n[:, l], state_mlstm_m[:, l]), **mkw)

    wg = (0.5 * jnp.concatenate([rg_wa[l, 0], rg_wx[l, 0], rg_wa[l, 1], rg_wx[l, 1]], axis=-1)).astype(BF16)
    bg = 0.5 * jnp.concatenate([rg_ba[l, 0].reshape(nblk, 1, LANES), rg_bx[l, 0].reshape(nblk, 1, LANES),
                                rg_ba[l, 1].reshape(nblk, 1, LANES), rg_bx[l, 1].reshape(nblk, 1, LANES)], axis=-1)
    rargs = (xr, xg, conv_w[l], conv_b[l], wg, bg, rg_lambda[l])
    yb_c, new_h = _rglru(*rargs, row0=0, n_seq=n_req, t_len=seq, seg=seq, emit_state=True)
    (yb_l,) = _rglru(*rargs, row0=nc, n_seq=n_lat, t_len=lat_seq, seg=GRID_W, state=state_rglru_h[:, l])

    r_rows = -(-(SUBLANES + n_exp) // 16) * 16
    wr = jnp.zeros((r_rows, d), F32)
    wr = wr.at[0:n_groups].set(router_group_w[l].T)
    wr = wr.at[SUBLANES:SUBLANES + n_exp].set(router_expert_w[l].transpose(0, 2, 1).reshape(n_exp, d)).astype(BF16)
    br = jnp.zeros((r_rows, LANES), F32)
    br = br.at[0:n_groups, 0].set(router_group_b[l])
    br = br.at[SUBLANES:SUBLANES + n_exp, 0].set(router_expert_b[l].reshape(n_exp))
    x1, hp, ridx, rwt = _out_proj(ya_c, ya_l, yb_c, yb_l, xp, xs, mod, norm2_w[l], w_out[l].astype(BF16), wr, br,
                                  lat_seq=lat_seq, n_groups=n_groups, epg=epg)

    n_blocks = (nt * TOP_K) // MOE_ROWS + n_exp
    dest, row_tok, blk_e, nused = _routing(ridx, nt, n_exp, n_blocks, zero_row=nt)
    only_layer = lambda a: a.reshape(a.shape[1:])
    y_rows = _experts(blk_e, nused, row_tok, hp, only_layer(expert_w_gate), only_layer(expert_w_up),
                      only_layer(expert_w_down), n_blocks)

    dest_flat = dest[0:TOP_K].reshape(-1)
    y_prompt = _combine(dest_flat, x1, rwt, mod, final_norm_w, y_rows, row0=0, n_tok=nc, seq_len=seq, lat=False)
    y_sample = _combine(dest_flat, x1, rwt, mod, final_norm_w, y_rows, row0=nc, n_tok=nl, seq_len=lat_seq,
                        lat=True)

    return (y_prompt.reshape(n_req, seq, d), y_sample.reshape(n_lat, lat_seq, d),
            new_c[:, None], new_n.reshape(n_req, 1, 2, heads, dk), new_m.reshape(n_req, 1, 2, heads),
            new_h[:, None])
```

```python
import functools

import jax
import jax.numpy as jnp
import numpy as np
from jax import lax
from jax.experimental import pallas as pl
from jax.experimental.pallas import tpu as pltpu

F32 = jnp.float32
BF16 = jnp.bfloat16
I32 = jnp.int32
U32 = jnp.uint32

EPS = 1e-6
GRID_W = 64
CONV_LEFT = 2
RGLRU_C = 8.0
TOP_K = 2
LANES = 128
SUBLANES = 8
MLSTM_L = 256
NEG = -1e30
VMEM_LIMIT = 56 * 1024 * 1024

_HIGHEST = lax.Precision.HIGHEST


def _cparams(sem, vmem=VMEM_LIMIT):
    return pltpu.CompilerParams(dimension_semantics=sem, vmem_limit_bytes=vmem)


def _sigmoid(x):
    return 0.5 * jnp.tanh(0.5 * x) + 0.5


def _row_to_col(r, n):
    return jnp.broadcast_to(r, (LANES, n)).T


def _lane_tile(x, reps):
    return x if reps == 1 else jnp.concatenate([x] * reps, axis=1)


def _ada_kernel(c_ref, w_ref, b_ref, o_ref):
    c = c_ref[...]
    s = (c * _sigmoid(c)).astype(BF16)
    o_ref[...] = jnp.dot(s, w_ref[...].astype(BF16), preferred_element_type=F32) + b_ref[...]


def _ada(cvec, w_ada, b_ada):
    d, n = w_ada.shape
    tn = 1024 if n % 1024 == 0 else 512
    assert n % tn == 0
    return pl.pallas_call(
        _ada_kernel,
        grid=(n // tn,),
        in_specs=[pl.BlockSpec((SUBLANES, d), lambda j: (0, 0)),
                  pl.BlockSpec((d, tn), lambda j: (0, j)),
                  pl.BlockSpec((1, tn), lambda j: (0, j))],
        out_specs=pl.BlockSpec((SUBLANES, tn), lambda j: (0, j)),
        out_shape=jax.ShapeDtypeStruct((SUBLANES, n), F32),
        compiler_params=_cparams(("arbitrary",)),
        name="ada",
    )(cvec, w_ada, b_ada.reshape(1, n))


def _modulated_norm(x, w, shift, scale):
    ms = jnp.mean(x * x, axis=-1, keepdims=True)
    return (x * lax.rsqrt(ms + EPS) * w) * (1.0 + scale) + shift


def _mod_row(i, nctx_tiles, tiles_per_lat):
    return jnp.where(i < nctx_tiles, 0, 1 + (i - nctx_tiles) // tiles_per_lat)


def _inproj_kernel(xp_ref, xs_ref, mod_ref, n1_ref, w_ref, bg_ref,
                   q_ref, kt_ref, v_ref, o_ref, gt_ref, xr_ref, xg_ref, wkt_scr,
                   *, nctx_tiles, tiles_per_lat, d, qk, mw, rw, gh, qscale):
    i = pl.program_id(0)

    @pl.when(i == 0)
    def _():
        wkt_scr[...] = w_ref[:, qk:2 * qk].astype(F32).T.astype(BF16)

    x = jnp.where(i < nctx_tiles, xp_ref[...], xs_ref[...])
    row = _mod_row(i, nctx_tiles, tiles_per_lat)
    shift = mod_ref[pl.ds(row, 1), 0:d]
    scale = mod_ref[pl.ds(row, 1), d:2 * d]
    hb = _modulated_norm(x, n1_ref[...], shift, scale).astype(BF16)

    def proj(c0, width):
        return jnp.dot(hb, w_ref[:, c0:c0 + width], preferred_element_type=F32)

    c0 = 0
    q_ref[...] = (proj(c0, qk) * qscale).astype(BF16); c0 += qk
    kt_ref[...] = lax.dot_general(wkt_scr[...], hb, (((1,), (1,)), ((), ())),
                                  preferred_element_type=F32).astype(BF16)
    c0 += qk
    v_ref[...] = proj(c0, mw).astype(BF16); c0 += mw
    o_ref[...] = proj(c0, mw); c0 += mw
    zg = proj(c0, LANES) + bg_ref[...]; c0 += LANES
    lane = lax.broadcasted_iota(I32, zg.shape, 1)
    log_sig = jnp.minimum(zg, 0.0) - jnp.log1p(jnp.exp(-jnp.abs(zg)))
    zg = jnp.where(lane % 2 == 1, log_sig, zg)
    gt_ref[...] = zg.T[0:gh, :]
    xr_ref[...] = proj(c0, rw); c0 += rw
    xg_ref[...] = proj(c0, rw)


def _in_proj(xp, xs, mod, norm1_w, w_cat, bg, *, lat_seq, heads, dk, dv, rw):
    nc, d = xp.shape
    nl = xs.shape[0]
    nt = nc + nl
    tm = 256
    qk, mw, gh = heads * dk, heads * dv, heads * SUBLANES
    nctx_tiles = nc // tm
    kern = functools.partial(_inproj_kernel, nctx_tiles=nctx_tiles, tiles_per_lat=lat_seq // tm, d=d, qk=qk, mw=mw,
                             rw=rw, gh=gh, qscale=dk ** -0.5)
    row = lambda i: (i, 0)
    const = lambda i: (0, 0)
    return pl.pallas_call(
        kern,
        grid=(nt // tm,),
        in_specs=[pl.BlockSpec((tm, d), lambda i: (jnp.minimum(i, nctx_tiles - 1), 0)),
                  pl.BlockSpec((tm, d), lambda i: (jnp.maximum(i - nctx_tiles, 0), 0)),
                  pl.BlockSpec(mod.shape, const),
                  pl.BlockSpec((1, d), const),
                  pl.BlockSpec(w_cat.shape, const, pipeline_mode=pl.Buffered(1)),
                  pl.BlockSpec((1, LANES), const)],
        out_specs=[pl.BlockSpec((tm, qk), row), pl.BlockSpec((qk, tm), lambda i: (0, i)), pl.BlockSpec((tm, mw), row),
                   pl.BlockSpec((tm, mw), row), pl.BlockSpec((gh, tm), lambda i: (0, i)),
                   pl.BlockSpec((tm, rw), row), pl.BlockSpec((tm, rw), row)],
        out_shape=[jax.ShapeDtypeStruct((nt, qk), BF16), jax.ShapeDtypeStruct((qk, nt), BF16),
                   jax.ShapeDtypeStruct((nt, mw), BF16), jax.ShapeDtypeStruct((nt, mw), F32),
                   jax.ShapeDtypeStruct((gh, nt), F32),
                   jax.ShapeDtypeStruct((nt, rw), F32), jax.ShapeDtypeStruct((nt, rw), F32)],
        scratch_shapes=[pltpu.VMEM((qk, d), BF16)],
        compiler_params=_cparams(("arbitrary",)),
        name="in_proj",
    )(xp, xs, mod, norm1_w.reshape(1, d), w_cat, bg)


def _mlstm_kernel(*refs, t_len, dk, dv, has_state, emit_state):
    it = iter(refs)
    q_ref, kt_ref, v_ref, o_ref, gt_ref, nw_ref, tri_ref = (next(it) for _ in range(7))
    if has_state:
        c0_ref, n0_ref, m0_ref = (next(it) for _ in range(3))
    ya_ref = next(it)
    if emit_state:
        cn_ref, nn_ref, mn_ref = (next(it) for _ in range(3))
    hf_scr, hb_scr, c_scr, ma_scr, p_scr, w_scr, em_scr, kw_scr, dm_scr = (next(it) for _ in range(9))
    ln = MLSTM_L
    nchunks = t_len // ln
    assert ln % LANES == 0 and dk == LANES
    gsz = 2 if nchunks % 2 == 0 else 1
    ngroups = nchunks // gsz
    h_scr = (hf_scr, hb_scr)
    ext = dv + LANES
    lrep = ln // LANES

    for d in range(2):
        if has_state:
            c_scr[d, :, 0:dv] = c0_ref[d]
            c_scr[d, :, dv:ext] = _row_to_col(n0_ref[d], dk)
            ma_scr[d] = m0_ref[d]
        else:
            c_scr[d] = jnp.zeros((dk, ext), F32)
            ma_scr[d] = jnp.zeros((1, 1), F32)

    def chunk_start(d, j):
        return pl.multiple_of((j if d == 0 else nchunks - 1 - j) * ln, ln)

    def stage_a(d, j, slot):
        t0 = chunk_start(d, j)
        q = q_ref[pl.ds(t0, ln), :]
        kt = kt_ref[:, pl.ds(t0, ln)]
        g8 = gt_ref[:, pl.ds(t0, ln)]
        cum8 = jnp.dot(g8, tri_ref[d], precision=_HIGHEST, preferred_element_type=F32)
        valid = tri_ref[1 - d] > 0.5
        li = g8[2 * d:2 * d + 1]
        lf = g8[2 * d + 1:2 * d + 2]
        cum_row = cum8[2 * d + 1:2 * d + 2]
        total = jnp.sum(lf, axis=1, keepdims=True)
        a_row = li - cum_row
        cum_col = _row_to_col(cum_row, ln)
        m_prev = ma_scr[d]
        dmat = jnp.where(valid, _lane_tile(cum_col, lrep) + a_row, NEG)
        inter = cum_col + m_prev
        m_t = jnp.maximum(inter, jnp.max(dmat, axis=1, keepdims=True))
        s = jnp.dot(q, kt, preferred_element_type=F32)
        bank, par = slot
        p_scr[d, bank, par] = (s * jnp.exp(dmat - _lane_tile(m_t, lrep))).astype(BF16)
        w_scr[d, bank, par] = jnp.exp(inter - m_t)
        em_scr[d, bank, par] = jnp.exp(-m_t)
        g_row = total + a_row
        m_new = jnp.maximum(total + m_prev, jnp.max(g_row, axis=1, keepdims=True))
        kw_scr[d, bank, par] = (kt.astype(F32) * jnp.exp(g_row - m_new)).astype(BF16)
        decay = jnp.exp(total + m_prev - m_new)
        rid = lax.broadcasted_iota(I32, (SUBLANES, LANES), 0)
        dm_scr[d, bank, par] = jnp.where(rid == 0, decay, m_new)
        ma_scr[d] = m_new

    def stage_b(d, j, slot):
        t0 = chunk_start(d, j)
        q = q_ref[pl.ds(t0, ln), :]
        v_ext = jnp.concatenate([v_ref[pl.ds(t0, ln), :], jnp.ones((ln, LANES), BF16)], axis=1)
        c_st = c_scr[d]
        bank, par = slot
        full = (jnp.dot(p_scr[d, bank, par], v_ext, preferred_element_type=F32)
                + _lane_tile(w_scr[d, bank, par], ext // LANES) * jnp.dot(q, c_st.astype(BF16),
                                                                         preferred_element_type=F32))
        inv = 1.0 / jnp.maximum(jnp.abs(full[:, dv:ext]), em_scr[d, bank, par])
        h_scr[d][pl.ds(t0, ln), :] = full[:, 0:dv] * _lane_tile(inv, dv // LANES)
        decay = dm_scr[d, bank, par, 0:1, 0:1]
        c_scr[d] = decay * c_st + jnp.dot(kw_scr[d, bank, par], v_ext, preferred_element_type=F32)

    def group_a(jj, bank):
        for par in range(gsz):
            for d in range(2):
                stage_a(d, gsz * jj + par, (bank, par))

    def group_b(jj, bank):
        for par in range(gsz):
            for d in range(2):
                stage_b(d, gsz * jj + par, (bank, par))

    group_a(0, 0)

    def body(jj, carry):
        bank = jj % 2
        group_b(jj, bank)
        group_a(jj + 1, 1 - bank)
        return carry

    lax.fori_loop(0, ngroups - 1, body, 0)
    group_b(ngroups - 1, (ngroups - 1) % 2)

    def finish(j, carry):
        t0 = pl.multiple_of(j * ln, ln)
        hs = hf_scr[pl.ds(t0, ln), :] + hb_scr[pl.ds(t0, ln), :]
        ms = jnp.mean(hs * hs, axis=1, keepdims=True)
        y = hs * lax.rsqrt(ms + EPS) * nw_ref[...]
        ya_ref[pl.ds(t0, ln), :] = (_sigmoid(o_ref[pl.ds(t0, ln), :]) * y).astype(BF16)
        return carry

    lax.fori_loop(0, nchunks, finish, 0)

    if emit_state:
        for d in range(2):
            cn_ref[d] = c_scr[d, :, 0:dv]
            nn_ref[d] = c_scr[d, :, dv:ext].T[0:1, :]
            mn_ref[d] = dm_scr[d, (ngroups - 1) % 2, gsz - 1, 1:2, 0:1]


def _mlstm_tri():
    r = np.arange(MLSTM_L)
    fwd = (r[:, None] <= r[None, :]).astype(np.float32)
    return jnp.asarray(np.stack([fwd, fwd.T]))


def _mlstm(q, kt, v, o, gt, norm_w, *, row0, n_seq, t_len, heads, dk, dv, state=None, emit_state=False):
    blk0 = row0 // t_len
    tok = lambda s, h: (blk0 + s, h)
    in_specs = [pl.BlockSpec((t_len, dk), tok), pl.BlockSpec((dk, t_len), lambda s, h: (h, blk0 + s)),
                pl.BlockSpec((t_len, dv), tok), pl.BlockSpec((t_len, dv), tok),
                pl.BlockSpec((SUBLANES, t_len), lambda s, h: (h, blk0 + s)),
                pl.BlockSpec((None, 1, dv), lambda s, h: (h, 0, 0)),
                pl.BlockSpec((2, MLSTM_L, MLSTM_L), lambda s, h: (0, 0, 0))]
    args = [q, kt, v, o, gt, norm_w.reshape(heads, 1, dv), _mlstm_tri()]
    if state is not None:
        c0, n0, m0 = state
        in_specs += [pl.BlockSpec((None, 2, None, dk, dv), lambda s, h: (s, 0, h, 0, 0)),
                     pl.BlockSpec((None, 2, None, 1, dk), lambda s, h: (s, 0, h, 0, 0)),
                     pl.BlockSpec((None, 2, None, 1, 1), lambda s, h: (s, 0, h, 0, 0))]
        args += [c0, n0.reshape(n_seq, 2, heads, 1, dk), m0.reshape(n_seq, 2, heads, 1, 1)]
    out_specs = [pl.BlockSpec((t_len, dv), lambda s, h: (s, h))]
    out_shape = [jax.ShapeDtypeStruct((n_seq * t_len, heads * dv), BF16)]
    if emit_state:
        out_specs += [pl.BlockSpec((None, 2, None, dk, dv), lambda s, h: (s, 0, h, 0, 0)),
                      pl.BlockSpec((None, 2, None, 1, dk), lambda s, h: (s, 0, h, 0, 0)),
                      pl.BlockSpec((None, 2, None, 1, 1), lambda s, h: (s, 0, h, 0, 0))]
        out_shape += [jax.ShapeDtypeStruct((n_seq, 2, heads, dk, dv), F32),
                      jax.ShapeDtypeStruct((n_seq, 2, heads, 1, dk), F32),
                      jax.ShapeDtypeStruct((n_seq, 2, heads, 1, 1), F32)]
    kern = functools.partial(_mlstm_kernel, t_len=t_len, dk=dk, dv=dv, has_state=state is not None,
                             emit_state=emit_state)
    return pl.pallas_call(
        kern,
        grid=(n_seq, heads),
        in_specs=in_specs,
        out_specs=out_specs,
        out_shape=out_shape,
        scratch_shapes=[pltpu.VMEM((t_len, dv), F32), pltpu.VMEM((t_len, dv), F32),
                        pltpu.VMEM((2, dk, dv + LANES), F32), pltpu.VMEM((2, 1, 1), F32),
                        pltpu.VMEM((2, 2, 2, MLSTM_L, MLSTM_L), BF16), pltpu.VMEM((2, 2, 2, MLSTM_L, LANES), F32),
                        pltpu.VMEM((2, 2, 2, MLSTM_L, LANES), F32), pltpu.VMEM((2, 2, 2, dk, MLSTM_L), BF16),
                        pltpu.VMEM((2, 2, 2, SUBLANES, LANES), F32)],
        compiler_params=_cparams(("arbitrary", "arbitrary")),
        name="mlstm_state" if emit_state else "mlstm",
    )(*args)


def _gelu_tanh(x):
    return x * (0.5 * (1.0 + jnp.tanh(0.7978845608028654 * (x + 0.044715 * (x * x * x)))))


def _softplus(x):
    return jnp.maximum(x, 0.0) + jnp.log1p(jnp.exp(-jnp.abs(x)))


SCAN_SPLIT = 1
SCAN_GROUP = 4


def _rglru_kernel(*refs, t_len, seg, sub, pitch, tc, has_state, emit_state):
    it = iter(refs)
    xr_ref, xg_ref, cw_ref, cb_ref, wg_ref, bg_ref, lam_ref = (next(it) for _ in range(7))
    if has_state:
        h0_ref = next(it)
    yb_ref = next(it)
    if emit_state:
        hn_ref = next(it)
    a_scr, u_scr, cin_scr = (next(it) for _ in range(3))
    nchunks = t_len // tc
    piece = min(tc, sub)
    npieces = tc // piece
    ntile = xr_ref.shape[1] // LANES
    chains = [(d, lt) for d in range(2) for lt in range(ntile)]

    def scan_rows(t0, p):
        t = t0 + p * piece
        i = t // sub
        return pl.ds(pl.multiple_of(i * pitch + (t - i * sub), SUBLANES), piece), i

    ka = (-0.5 * RGLRU_C * 1.4426950408889634) * _softplus(-lam_ref[...])

    def gates(c, carry):
        t0 = pl.multiple_of(c * tc, tc)
        pos = lax.broadcasted_iota(I32, (tc, LANES), 0) % seg
        for lt in range(ntile):
            cols = slice(lt * LANES, (lt + 1) * LANES)
            x = xr_ref[pl.ds(t0, tc), cols]
            xc = cb_ref[:, cols] + cw_ref[CONV_LEFT:CONV_LEFT + 1, cols] * x
            for j in range(cw_ref.shape[0]):
                off = j - CONV_LEFT
                if off == 0:
                    continue
                shifted = pltpu.roll(x, (-off) % tc, 0)
                ok = (pos >= -off) if off < 0 else (pos < seg - off)
                xc = xc + cw_ref[j:j + 1, cols] * jnp.where(ok, shifted, 0.0)
            zh = jnp.dot(xc.astype(BF16), wg_ref[lt], preferred_element_type=F32) + bg_ref[lt]
            hx = 0.5 * xc
            for d in range(2):
                kd = ka[d:d + 1, cols]
                a = jnp.exp2(jnp.tanh(zh[:, (2 * d) * LANES:(2 * d + 1) * LANES]) * kd + kd)
                igx = hx * jnp.tanh(zh[:, (2 * d + 1) * LANES:(2 * d + 2) * LANES]) + hx
                u = jnp.sqrt(1.0 - a * a) * igx
                for p in range(npieces):
                    rows, _ = scan_rows(t0, p)
                    a_scr[d, lt, rows, :] = a[p * piece:(p + 1) * piece]
                    u_scr[d, lt, rows, :] = u[p * piece:(p + 1) * piece]
        return carry

    lax.fori_loop(0, nchunks, gates, 0)

    seg_len = sub // SCAN_SPLIT
    seg_chains = [(d, lt, s) for (d, lt) in chains for s in range(SCAN_SPLIT)]

    group = max(1, SCAN_GROUP * 4 // len(seg_chains))

    def scan(jg, carry):
        def rows(d, s, k):
            j = jg * group + k
            return pl.ds(s * seg_len + (j if d == 0 else seg_len - 1 - j), SUBLANES, stride=pitch)

        loaded = [[(a_scr[d, lt, rows(d, s, k), :], u_scr[d, lt, rows(d, s, k), :]) for k in range(group)]
                  for (d, lt, s) in seg_chains]
        out = []
        for (d, lt, s), (h, p), steps in zip(seg_chains, carry, loaded):
            for k, (a, u) in enumerate(steps):
                h = a * h + u
                p = a * p
                a_scr[d, lt, rows(d, s, k), :] = p
                u_scr[d, lt, rows(d, s, k), :] = h
            out.append((h, p))
        return tuple(out)

    zero = jnp.zeros((SUBLANES, LANES), F32)
    one = jnp.ones((SUBLANES, LANES), F32)
    assert seg_len % group == 0
    ends = dict(zip(seg_chains, lax.fori_loop(0, seg_len // group, scan, tuple((zero, one) for _ in seg_chains))))

    order = [(i, s) for i in range(SUBLANES) for s in range(SCAN_SPLIT)]
    for (d, lt) in chains:
        cols = slice(lt * LANES, (lt + 1) * LANES)
        cin = h0_ref[d:d + 1, cols] if has_state else jnp.zeros((1, LANES), F32)
        for (i, s) in (order if d == 0 else reversed(order)):
            h, p = ends[(d, lt, s)]
            cin_scr[d, lt, s * SUBLANES + i:s * SUBLANES + i + 1, :] = cin
            cin = h[i:i + 1] + p[i:i + 1] * cin
        if emit_state:
            hn_ref[d:d + 1, cols] = cin

    fpiece = min(tc, seg_len)

    def finish(c, carry):
        t0 = pl.multiple_of(c * tc, tc)
        for lt in range(ntile):
            cols = slice(lt * LANES, (lt + 1) * LANES)
            for p in range(tc // fpiece):
                t = t0 + p * fpiece
                i = t // sub
                local = t - i * sub
                rows = pl.ds(pl.multiple_of(i * pitch + local, SUBLANES), fpiece)
                crow = pl.ds((local // seg_len) * SUBLANES + i, 1)
                h = (u_scr[0, lt, rows, :] + a_scr[0, lt, rows, :] * cin_scr[0, lt, crow, :]
                     + u_scr[1, lt, rows, :] + a_scr[1, lt, rows, :] * cin_scr[1, lt, crow, :])
                nat = pl.ds(pl.multiple_of(t, SUBLANES), fpiece)
                yb_ref[nat, cols] = (h * _gelu_tanh(xg_ref[nat, cols])).astype(BF16)
        return carry

    lax.fori_loop(0, nchunks, finish, 0)


def _rglru(xr, xg, conv_w, conv_b, wg, bg, lam, *, row0, n_seq, t_len, seg, state=None, emit_state=False):
    rw = xr.shape[1]
    ntile = 4 if (t_len <= 512 and rw % (4 * LANES) == 0) else 2
    cb = ntile * LANES
    assert rw % cb == 0
    blk0 = row0 // t_len
    sub = t_len // SUBLANES
    pitch = sub + SUBLANES
    tc = 256
    tok = lambda s, g: (blk0 + s, g)
    in_specs = [pl.BlockSpec((t_len, cb), tok), pl.BlockSpec((t_len, cb), tok),
                pl.BlockSpec((conv_w.shape[0], cb), lambda s, g: (0, g)),
                pl.BlockSpec((1, cb), lambda s, g: (0, g)),
                pl.BlockSpec((ntile, LANES, 4 * LANES), lambda s, g: (g, 0, 0)),
                pl.BlockSpec((ntile, 1, 4 * LANES), lambda s, g: (g, 0, 0)),
                pl.BlockSpec((2, cb), lambda s, g: (0, g))]
    args = [xr, xg, conv_w, conv_b.reshape(1, rw), wg, bg, lam]
    if state is not None:
        in_specs.append(pl.BlockSpec((None, 2, cb), lambda s, g: (s, 0, g)))
        args.append(state)
    out_specs = [pl.BlockSpec((t_len, cb), lambda s, g: (s, g))]
    out_shape = [jax.ShapeDtypeStruct((n_seq * t_len, rw), BF16)]
    if emit_state:
        out_specs.append(pl.BlockSpec((None, 2, cb), lambda s, g: (s, 0, g)))
        out_shape.append(jax.ShapeDtypeStruct((n_seq, 2, rw), F32))
    kern = functools.partial(_rglru_kernel, t_len=t_len, seg=seg, sub=sub, pitch=pitch, tc=tc,
                             has_state=state is not None, emit_state=emit_state)
    return pl.pallas_call(
        kern,
        grid=(n_seq, rw // cb),
        in_specs=in_specs,
        out_specs=out_specs,
        out_shape=out_shape,
        scratch_shapes=[pltpu.VMEM((2, ntile, SUBLANES * pitch, LANES), F32),
                        pltpu.VMEM((2, ntile, SUBLANES * pitch, LANES), F32),
                        pltpu.VMEM((2, ntile, SCAN_SPLIT * SUBLANES, LANES), F32)],
        compiler_params=_cparams(("arbitrary", "arbitrary")),
        name="rglru_state" if emit_state else "rglru",
    )(*args)


def _pack_bf16_pair(lo, hi):
    def rounded_bits(x):
        return pltpu.bitcast(x.astype(BF16).astype(F32), U32)
    return (rounded_bits(lo) >> 16) | rounded_bits(hi)


def _unpack_bf16_pair(w):
    lo = pltpu.bitcast(w << 16, F32).astype(BF16)
    hi = pltpu.bitcast(w & jnp.uint32(0xFFFF0000), F32).astype(BF16)
    return lo, hi


def _outproj_kernel(yac_ref, yal_ref, ybc_ref, ybl_ref, xp_ref, xs_ref, mod_ref, n2_ref, wo_ref, wr_ref, br_ref,
                    x1_ref, hp_ref, ridx_ref, rwt_ref,
                    *, nctx_tiles, ntok_tiles, tiles_per_lat, d, mw, n_groups, epg, sub_rows):
    i = pl.program_id(0)

    @pl.when(i == ntok_tiles)
    def _():
        x1_ref[...] = jnp.zeros_like(x1_ref)
        hp_ref[...] = jnp.zeros_like(hp_ref)
        ridx_ref[...] = jnp.zeros_like(ridx_ref)
        rwt_ref[...] = jnp.zeros_like(rwt_ref)

    def token_tile(x_ref, ya_ref, yb_ref):
        for r0 in range(0, x_ref.shape[0], sub_rows):
            token_rows(x_ref, ya_ref, yb_ref, r0)

    def token_rows(x_ref, ya_ref, yb_ref, r0):
        rows = slice(r0, r0 + sub_rows)
        x, ya, yb = x_ref[rows, :], ya_ref[rows, :], yb_ref[rows, :]
        row = _mod_row(i, nctx_tiles, tiles_per_lat)
        gate1 = mod_ref[pl.ds(row, 1), 2 * d:3 * d]
        shift2 = mod_ref[pl.ds(row, 1), 3 * d:4 * d]
        scale2 = mod_ref[pl.ds(row, 1), 4 * d:5 * d]
        y = (jnp.dot(ya, wo_ref[0:mw, :], preferred_element_type=F32)
             + jnp.dot(yb, wo_ref[mw:, :], preferred_element_type=F32))
        x1 = x + gate1 * y
        x1_ref[rows, :] = x1
        h2 = _modulated_norm(x1, n2_ref[...], shift2, scale2)
        half = d // 2
        packed = _pack_bf16_pair(h2[:, :half], h2[:, half:])
        trows = half // LANES
        for s in range(trows):
            hp_ref[pl.ds(r0 * trows + s, sub_rows, stride=trows), :] = packed[:, s * LANES:(s + 1) * LANES]

        lt = lax.dot_general(wr_ref[...], h2.astype(BF16), (((1,), (1,)), ((), ())),
                             preferred_element_type=F32) + br_ref[:, 0:1]
        gidx = lax.broadcasted_iota(I32, (SUBLANES, lt.shape[1]), 0)
        gl = jnp.where(gidx < n_groups, lt[0:SUBLANES], -jnp.inf)
        gmax = jnp.max(gl, axis=0, keepdims=True)
        grp = jnp.min(jnp.where(gl == gmax, gidx, n_groups), axis=0, keepdims=True)
        p_grp = 1.0 / jnp.sum(jnp.exp(gl - gmax), axis=0, keepdims=True)
        el = lt[SUBLANES:SUBLANES + epg]
        for g in range(1, n_groups):
            el = jnp.where(grp == g, lt[SUBLANES + g * epg:SUBLANES + (g + 1) * epg], el)
        eidx = lax.broadcasted_iota(I32, el.shape, 0)
        v1 = jnp.max(el, axis=0, keepdims=True)
        i1 = jnp.min(jnp.where(el == v1, eidx, epg), axis=0, keepdims=True)
        el2 = jnp.where(eidx == i1, -jnp.inf, el)
        v2 = jnp.max(el2, axis=0, keepdims=True)
        i2 = jnp.min(jnp.where(el2 == v2, eidx, epg), axis=0, keepdims=True)
        e2 = jnp.exp(v2 - v1)
        w1 = p_grp / (1.0 + e2)
        w2 = p_grp * e2 / (1.0 + e2)
        rid = lax.broadcasted_iota(I32, (SUBLANES, sub_rows), 0)
        ridx_ref[:, rows] = jnp.where(rid == 0, grp * epg + i1, jnp.where(rid == 1, grp * epg + i2, 0))
        rwt_ref[:, rows] = jnp.where(rid == 0, w1, jnp.where(rid == 1, w2, 0.0))

    @pl.when(i < nctx_tiles)
    def _():
        token_tile(xp_ref, yac_ref, ybc_ref)

    @pl.when((i >= nctx_tiles) & (i < ntok_tiles))
    def _():
        token_tile(xs_ref, yal_ref, ybl_ref)


def _out_proj(ya_c, ya_l, yb_c, yb_l, xp, xs, mod, norm2_w, w_out, wr, br, *, lat_seq, n_groups, epg):
    nc, d = xp.shape
    nl = xs.shape[0]
    nt = nc + nl
    mw = ya_c.shape[1]
    tm, sub_rows = 512, 256
    trows = (d // 2) // LANES
    nctx_tiles, ntok_tiles = nc // tm, nt // tm
    kern = functools.partial(_outproj_kernel, nctx_tiles=nctx_tiles, ntok_tiles=ntok_tiles,
                             tiles_per_lat=lat_seq // tm, d=d, mw=mw, n_groups=n_groups, epg=epg, sub_rows=sub_rows)
    ctx = lambda i: (jnp.minimum(i, nctx_tiles - 1), 0)
    lat = lambda i: (jnp.clip(i - nctx_tiles, 0, nl // tm - 1), 0)
    row = lambda i: (i, 0)
    const = lambda i: (0, 0)
    return pl.pallas_call(
        kern,
        grid=(ntok_tiles + 1,),
        in_specs=[pl.BlockSpec((tm, mw), ctx), pl.BlockSpec((tm, mw), lat),
                  pl.BlockSpec((tm, yb_c.shape[1]), ctx), pl.BlockSpec((tm, yb_c.shape[1]), lat),
                  pl.BlockSpec((tm, d), ctx), pl.BlockSpec((tm, d), lat),
                  pl.BlockSpec(mod.shape, const),
                  pl.BlockSpec((1, d), const),
                  pl.BlockSpec(w_out.shape, const, pipeline_mode=pl.Buffered(1)),
                  pl.BlockSpec(wr.shape, const),
                  pl.BlockSpec(br.shape, const)],
        out_specs=[pl.BlockSpec((tm, d), row), pl.BlockSpec((tm * trows, LANES), row),
                   pl.BlockSpec((SUBLANES, tm), lambda i: (0, i)), pl.BlockSpec((SUBLANES, tm), lambda i: (0, i))],
        out_shape=[jax.ShapeDtypeStruct((nt + tm, d), F32), jax.ShapeDtypeStruct(((nt + tm) * trows, LANES), U32),
                   jax.ShapeDtypeStruct((SUBLANES, nt + tm), I32), jax.ShapeDtypeStruct((SUBLANES, nt + tm), F32)],
        compiler_params=_cparams(("arbitrary",)),
        name="out_proj",
    )(ya_c, ya_l, yb_c, yb_l, xp, xs, mod, norm2_w.reshape(1, d), w_out, wr, br)


MOE_ROWS = 256


ROUTE_TILE = 1024


def _rank_kernel(ridx_ref, tri_ref, rank_ref, cnt_ref, carry_scr, *, n_exp):
    @pl.when(pl.program_id(0) == 0)
    def _():
        carry_scr[...] = jnp.zeros_like(carry_scr)

    e = ridx_ref[...]
    tr = e.shape[1]
    eid = lax.broadcasted_iota(I32, (n_exp, tr), 0)
    carry = carry_scr[:, 0:1]
    ranks = []
    for kk in range(TOP_K):
        hit = eid == e[kk:kk + 1]
        cum = jnp.dot(jnp.where(hit, 1.0, 0.0).astype(BF16), tri_ref[...], preferred_element_type=F32)
        ranks.append(jnp.sum(jnp.where(hit, cum + carry, 0.0), axis=0, keepdims=True) - 1.0)
        carry = carry + cum[:, tr - 1:tr]
    carry_scr[...] = jnp.broadcast_to(carry, carry_scr.shape)
    cnt_ref[...] = jnp.broadcast_to(carry, cnt_ref.shape)
    rid = lax.broadcasted_iota(I32, rank_ref.shape, 0)
    rank_ref[...] = jnp.where(rid == 0, ranks[0], jnp.where(rid == 1, ranks[1], 0.0)).astype(I32)


def _dest_kernel(ridx_ref, rank_ref, pstart_ref, dest_ref, *, n_exp):
    e = ridx_ref[...]
    tr = e.shape[1]
    eid = lax.broadcasted_iota(I32, (n_exp, tr), 0)
    ps = pstart_ref[:, 0:1]
    rows = [jnp.sum(jnp.where(eid == e[kk:kk + 1], ps, 0.0), axis=0, keepdims=True) for kk in range(TOP_K)]
    rid = lax.broadcasted_iota(I32, dest_ref.shape, 0)
    dest_ref[...] = rank_ref[...] + jnp.where(rid == 0, rows[0], jnp.where(rid == 1, rows[1], 0.0)).astype(I32)


def _invert_kernel(dest_ref, fill_ref, rowtok_ref, sem):
    i = pl.program_id(0)
    tr = dest_ref.shape[0] // TOP_K

    @pl.when(i == 0)
    def _():
        fill = pltpu.make_async_copy(fill_ref, rowtok_ref, sem.at[0])
        fill.start()
        fill.wait()

    def body(r, c):
        for kk in range(TOP_K):
            rowtok_ref[dest_ref[kk * tr + r]] = i * tr + r
        return c
    lax.fori_loop(0, tr, body, 0, unroll=16)


def _routing(ridx, n_tok, n_exp, n_blocks, zero_row):
    tr = ROUTE_TILE if n_tok % ROUTE_TILE == 0 else ROUTE_TILE // 2
    steps = n_tok // tr
    tri = jnp.asarray(np.triu(np.ones((tr, tr), np.float32)), BF16)
    tile = pl.BlockSpec((SUBLANES, tr), lambda i: (0, i))
    cnt_spec = pl.BlockSpec((n_exp, LANES), lambda i: (0, 0))
    rank, cnt = pl.pallas_call(
        functools.partial(_rank_kernel, n_exp=n_exp),
        grid=(steps,),
        in_specs=[tile, pl.BlockSpec((tr, tr), lambda i: (0, 0))],
        out_specs=[tile, cnt_spec],
        out_shape=[jax.ShapeDtypeStruct((SUBLANES, n_tok), I32), jax.ShapeDtypeStruct((n_exp, LANES), F32)],
        scratch_shapes=[pltpu.VMEM((n_exp, LANES), F32)],
        compiler_params=_cparams(("arbitrary",)),
        name="route_rank",
    )(ridx, tri)
    counts = cnt[:, 0].astype(I32)
    padded = (counts + MOE_ROWS - 1) // MOE_ROWS * MOE_ROWS
    pad_end = jnp.cumsum(padded)
    pad_start = pad_end - padded
    dest = pl.pallas_call(
        functools.partial(_dest_kernel, n_exp=n_exp),
        grid=(steps,),
        in_specs=[tile, tile, cnt_spec],
        out_specs=tile,
        out_shape=jax.ShapeDtypeStruct((SUBLANES, n_tok), I32),
        compiler_params=_cparams(("arbitrary",)),
        name="route_dest",
    )(ridx, rank, jnp.broadcast_to(pad_start.astype(F32)[:, None], (n_exp, LANES)))
    row_tok = pl.pallas_call(
        _invert_kernel,
        grid=(steps,),
        in_specs=[pl.BlockSpec((TOP_K * tr,), lambda i: (i,), memory_space=pltpu.SMEM),
                  pl.BlockSpec(memory_space=pl.ANY)],
        out_specs=pl.BlockSpec(memory_space=pltpu.SMEM),
        out_shape=jax.ShapeDtypeStruct((n_blocks * MOE_ROWS,), I32),
        scratch_shapes=[pltpu.SemaphoreType.DMA((1,))],
        compiler_params=_cparams(("arbitrary",)),
        name="route_invert",
    )(dest[0:TOP_K].reshape(TOP_K, steps, tr).transpose(1, 0, 2).reshape(-1),
      jnp.full((n_blocks * MOE_ROWS,), zero_row, I32))
    blk_row0 = jnp.arange(n_blocks, dtype=I32) * MOE_ROWS
    blk_e = jnp.minimum(jnp.sum((pad_end[None, :] <= blk_row0[:, None]).astype(I32), axis=1), n_exp - 1)
    nused = (pad_end[-1:] // MOE_ROWS).astype(I32)
    return dest, row_tok, blk_e, nused


CAST_ROWS = 256
WEIGHT_DMA_SPLIT = 8


GATHER_DEPTH = 4


def _expert_kernel(blk_e_ref, nused_ref, *refs, half, rows, trows):
    tok_refs = refs[:GATHER_DEPTH]
    (src_ref, wg_hbm, wu_hbm, wd_hbm, y_ref, xbuf, xsem, stage_g, stage_u, stage_d, bf_g, bf_u, bf_d,
     wsem) = refs[GATHER_DEPTH:]
    b = pl.program_id(0)
    nused = nused_ref[0]
    w_hbm = (wg_hbm, wu_hbm, wd_hbm)
    stage = (stage_g, stage_u, stage_d)
    wbf = (bf_g, bf_u, bf_d)

    def weight_copy(e, j):
        return pltpu.make_async_copy(w_hbm[j].at[e], stage[j], wsem.at[j])

    def start_weights(e, j):
        step = stage[j].shape[0] // WEIGHT_DMA_SPLIT
        for c in range(WEIGHT_DMA_SPLIT):
            sl = pl.ds(c * step, step)
            pltpu.make_async_copy(w_hbm[j].at[e, sl, :], stage[j].at[sl, :], wsem.at[j]).start(priority=1)

    def row_copy(tok, slot, r):
        src = src_ref.at[pl.ds(pl.multiple_of(tok * trows, trows), trows), :]
        return pltpu.make_async_copy(src, xbuf.at[slot, pl.ds(r * trows, trows), :], xsem.at[slot])

    @pl.when(b >= nused)
    def _():
        y_ref[...] = jnp.zeros_like(y_ref)

    @pl.when(b < nused)
    def _():
        e = blk_e_ref[b]

        @pl.when(b == 0)
        def _():
            for j in range(3):
                start_weights(e, j)

        for first in range(GATHER_DEPTH - 1):
            @pl.when((b == 0) & (nused > first))
            def _(first=first):
                def body(r, c):
                    row_copy(tok_refs[first][r], first, r).start()
                    return c
                lax.fori_loop(0, rows, body, 0, unroll=8)

        @pl.when((b == 0) | (blk_e_ref[jnp.maximum(b - 1, 0)] != e))
        def _():
            nb = lax.while_loop(lambda k: (k < nused) & (blk_e_ref[jnp.minimum(k, nused - 1)] == e),
                                lambda k: k + 1, b + 1)
            for j in range(3):
                weight_copy(e, j).wait()
                n_steps = stage[j].shape[0] // CAST_ROWS

                def cast(c, carry, j=j):
                    sl = pl.ds(pl.multiple_of(c * CAST_ROWS, CAST_ROWS), CAST_ROWS)
                    wbf[j][sl, :] = stage[j][sl, :].astype(BF16)
                    return carry
                lax.fori_loop(0, n_steps, cast, 0)

                @pl.when(nb < nused)
                def _(j=j):
                    start_weights(blk_e_ref[jnp.minimum(nb, nused - 1)], j)

        slot = b % GATHER_DEPTH
        pltpu.make_async_copy(src_ref.at[pl.ds(0, rows * trows), :], xbuf.at[slot], xsem.at[slot]).wait()

        def compute(prefetch):
            parts = [_unpack_bf16_pair(xbuf[slot, pl.ds(s, rows, stride=trows), :]) for s in range(trows)]
            lo = jnp.concatenate([p[0] for p in parts], axis=1)
            hi = jnp.concatenate([p[1] for p in parts], axis=1)
            if prefetch:
                ahead = (b + GATHER_DEPTH - 1) % GATHER_DEPTH
                for r in range(rows):
                    row_copy(tok_refs[GATHER_DEPTH - 1][r], ahead, r).start()

            def up(w_ref):
                return (jnp.dot(lo, w_ref[0:half, :], preferred_element_type=F32)
                        + jnp.dot(hi, w_ref[half:, :], preferred_element_type=F32))

            g = up(wbf[0])
            h = ((g * _sigmoid(g)) * up(wbf[1])).astype(BF16)
            y = jnp.dot(h, wbf[2][...], preferred_element_type=F32)
            packed = _pack_bf16_pair(y[:, :half], y[:, half:])
            for s in range(trows):
                y_ref[pl.ds(s, rows, stride=trows), :] = packed[:, s * LANES:(s + 1) * LANES]

        @pl.when(b + GATHER_DEPTH - 1 < nused)
        def _():
            compute(True)

        @pl.when(b + GATHER_DEPTH - 1 >= nused)
        def _():
            compute(False)


def _experts(blk_e, nused, row_tok, src, wg, wu, wd, n_blocks):
    n_exp, d, ff = wg.shape
    any_spec = pl.BlockSpec(memory_space=pl.ANY)
    trows = (d // 2) // LANES
    def tok_spec(ahead):
        return pl.BlockSpec((MOE_ROWS,), lambda b, be, nu: (jnp.minimum(b + ahead, n_blocks - 1),),
                            memory_space=pltpu.SMEM)

    return pl.pallas_call(
        functools.partial(_expert_kernel, half=d // 2, rows=MOE_ROWS, trows=trows),
        grid_spec=pltpu.PrefetchScalarGridSpec(
            num_scalar_prefetch=2,
            grid=(n_blocks,),
            in_specs=[tok_spec(a) for a in range(GATHER_DEPTH)] + [any_spec, any_spec, any_spec, any_spec],
            out_specs=pl.BlockSpec((MOE_ROWS * trows, LANES), lambda b, be, nu: (b, 0)),
            scratch_shapes=[pltpu.VMEM((GATHER_DEPTH, MOE_ROWS * trows, LANES), U32),
                            pltpu.SemaphoreType.DMA((GATHER_DEPTH,)),
                            pltpu.VMEM((d, ff), F32), pltpu.VMEM((d, ff), F32), pltpu.VMEM((ff, d), F32),
                            pltpu.VMEM((d, ff), BF16), pltpu.VMEM((d, ff), BF16), pltpu.VMEM((ff, d), BF16),
                            pltpu.SemaphoreType.DMA((3,))]),
        out_shape=jax.ShapeDtypeStruct((n_blocks * MOE_ROWS * trows, LANES), U32),
        compiler_params=_cparams(("arbitrary",)),
        name="experts",
    )(blk_e, nused, *([row_tok] * GATHER_DEPTH), src, wg, wu, wd)


COMBINE_BUFS = 3


def _combine_kernel(dest_ref, x1_ref, wt_ref, mod_ref, fw_ref, y_ref, o_ref, *scratch,
                    tile0, n_all, tiles_per_seq, d, lat):
    i = pl.program_id(0)
    last = pl.num_programs(0) - 1
    tm = x1_ref.shape[0]
    bufs, sem = scratch[:-1], scratch[-1]
    nbuf = len(bufs)

    trows = (d // 2) // LANES

    def row_copy(tile, par, kk, r):
        row = dest_ref[kk * n_all + (tile0 + tile) * tm + r]
        src = y_ref.at[pl.ds(pl.multiple_of(row * trows, trows), trows), :]
        return pltpu.make_async_copy(src, bufs[par].at[kk, pl.ds(r * trows, trows), :], sem.at[par])

    for first in range(nbuf - 1):
        @pl.when((i == 0) & (first <= last))
        def _(first=first):
            def body(r, c):
                for kk in range(TOP_K):
                    row_copy(first, first, kk, r).start()
                return c
            lax.fori_loop(0, tm, body, 0, unroll=8)

    def step(par, prefetch):
        for kk in range(TOP_K):
            pltpu.make_async_copy(y_ref.at[pl.ds(0, tm * trows), :], bufs[par].at[kk], sem.at[par]).wait()
        if prefetch:
            for r in range(tm):
                for kk in range(TOP_K):
                    row_copy(i + nbuf - 1, (par + nbuf - 1) % nbuf, kk, r).start(priority=kk)
        row = (1 + i // tiles_per_seq) if lat else 0
        gate2 = mod_ref[pl.ds(row, 1), 5 * d:6 * d]
        wt = wt_ref[...]
        half = d // 2
        w0 = _row_to_col(wt[0:1], tm)
        w1 = _row_to_col(wt[1:2], tm)
        unpack = (lambda p: pltpu.bitcast(p << 16, F32),
                  lambda p: pltpu.bitcast(p & jnp.uint32(0xFFFF0000), F32))
        ssq = jnp.zeros((tm, 1), F32)
        for s in range(trows):
            p0 = bufs[par][0, pl.ds(s, tm, stride=trows), :]
            p1 = bufs[par][1, pl.ds(s, tm, stride=trows), :]
            for side in range(2):
                cols = slice(side * half + s * LANES, side * half + (s + 1) * LANES)
                x = x1_ref[:, cols] + gate2[:, cols] * (w0 * unpack[side](p0) + w1 * unpack[side](p1))
                ssq = ssq + jnp.sum(x * x, axis=-1, keepdims=True)
                o_ref[:, cols] = x
        scale = lax.rsqrt(ssq * (1.0 / d) + EPS)
        o_ref[...] = o_ref[...] * scale * fw_ref[...]

    for par in range(nbuf):
        @pl.when((i % nbuf == par) & (i + nbuf - 1 <= last))
        def _(par=par):
            step(par, True)

        @pl.when((i % nbuf == par) & (i + nbuf - 1 > last))
        def _(par=par):
            step(par, False)


def _combine(dest_flat, x1, rwt, mod, final_w, y_rows, *, row0, n_tok, seq_len, lat):
    d = x1.shape[1]
    tm = 256
    tile0 = row0 // tm
    trows = (d // 2) // LANES
    n_all = dest_flat.shape[0] // TOP_K
    kern = functools.partial(_combine_kernel, tile0=tile0, n_all=n_all, tiles_per_seq=seq_len // tm, d=d, lat=lat)
    return pl.pallas_call(
        kern,
        grid_spec=pltpu.PrefetchScalarGridSpec(
            num_scalar_prefetch=1,
            grid=(n_tok // tm,),
            in_specs=[pl.BlockSpec((tm, d), lambda i, dr: (tile0 + i, 0)),
                      pl.BlockSpec((SUBLANES, tm), lambda i, dr: (0, tile0 + i)),
                      pl.BlockSpec(mod.shape, lambda i, dr: (0, 0)),
                      pl.BlockSpec((1, d), lambda i, dr: (0, 0)),
                      pl.BlockSpec(memory_space=pl.ANY)],
            out_specs=pl.BlockSpec((tm, d), lambda i, dr: (i, 0)),
            scratch_shapes=[pltpu.VMEM((TOP_K, tm * trows, LANES), U32) for _ in range(COMBINE_BUFS)]
            + [pltpu.SemaphoreType.DMA((COMBINE_BUFS,))]),
        out_shape=jax.ShapeDtypeStruct((n_tok, d), F32),
        compiler_params=_cparams(("arbitrary",)),
        name="combine_lat" if lat else "combine_ctx",
    )(dest_flat, x1, rwt, mod, final_w.reshape(1, d), y_rows)


def _gate_layout(w_gates, b_gates, heads):
    d = w_gates.shape[0]
    w = w_gates.reshape(d, 4, heads).transpose(0, 2, 1)
    w = jnp.pad(w, ((0, 0), (0, 0), (0, SUBLANES - 4))).reshape(d, heads * SUBLANES)
    b = b_gates.reshape(4, heads).T
    b = jnp.pad(b, ((0, 0), (0, SUBLANES - 4))).reshape(1, heads * SUBLANES)
    padl = LANES - heads * SUBLANES
    return jnp.pad(w, ((0, 0), (0, padl))), jnp.pad(b, ((0, 0), (0, padl)))


def kernel(x_prompt, x_sample, state_mlstm_c, state_mlstm_n, state_mlstm_m, state_rglru_h, c, c_ctx, w_ada, b_ada,
           norm1_w, w_in, b_gates, conv_w, conv_b, rg_wa, rg_ba, rg_wx, rg_bx, rg_lambda, mlstm_norm_w, w_out,
           norm2_w, router_group_w, router_group_b, router_expert_w, router_expert_b, expert_w_gate, expert_w_up,
           expert_w_down, final_norm_w):
    n_req, seq, d = x_prompt.shape
    n_lat, lat_seq, _ = x_sample.shape
    depth = w_in.shape[0]
    assert depth == 1, "the token-axis plumbing below is written for the single-layer trunk"
    heads, dk, dv = state_mlstm_c.shape[3:]
    rw = state_rglru_h.shape[-1]
    nblk = rg_wa.shape[2]
    assert rw // nblk == LANES
    n_groups, epg = router_expert_w.shape[1], router_expert_w.shape[3]
    n_exp = n_groups * epg
    qk, mw = heads * dk, heads * dv
    nc, nl = n_req * seq, n_lat * lat_seq
    nt = nc + nl
    assert nc % lat_seq == 0 and n_lat + 1 <= SUBLANES
    l = 0

    xp = x_prompt.reshape(nc, d)
    xs = x_sample.reshape(nl, d)
    cvec = jnp.zeros((SUBLANES, d), F32).at[0].set(c_ctx).at[1:1 + n_lat].set(c)
    mod = _ada(cvec, w_ada[l], b_ada[l])

    w = w_in[l]
    g0 = 2 * qk + 2 * mw
    wgate, bgate = _gate_layout(w[:, g0:g0 + 4 * heads], b_gates[l], heads)
    w_cat = jnp.concatenate([w[:, :g0], wgate, w[:, g0 + 4 * heads:]], axis=1).astype(BF16)
    q, kt, v, o, gt, xr, xg = _in_proj(xp, xs, mod, norm1_w[l], w_cat, bgate, lat_seq=lat_seq, heads=heads, dk=dk,
                                       dv=dv, rw=rw)

    mkw = dict(heads=heads, dk=dk, dv=dv)
    ya_c, new_c, new_n, new_m = _mlstm(q, kt, v, o, gt, mlstm_norm_w[l], row0=0, n_seq=n_req, t_len=seq,
                                       emit_state=True, **mkw)
    (ya_l,) = _mlstm(q, kt, v, o, gt, mlstm_norm_w[l], row0=nc, n_seq=n_lat, t_len=lat_seq,
                     state=(state_mlstm_c[:, l], state_mlstm_n[:, l], state_mlstm_m[:, l]), **mkw)

    wg = (0.5 * jnp.concatenate([rg_wa[l, 0], rg_wx[l, 0], rg_wa[l, 1], rg_wx[l, 1]], axis=-1)).astype(BF16)
    bg = 0.5 * jnp.concatenate([rg_ba[l, 0].reshape(nblk, 1, LANES), rg_bx[l, 0].reshape(nblk, 1, LANES),
                                rg_ba[l, 1].reshape(nblk, 1, LANES), rg_bx[l, 1].reshape(nblk, 1, LANES)], axis=-1)
    rargs = (xr, xg, conv_w[l], conv_b[l], wg, bg, rg_lambda[l])
    yb_c, new_h = _rglru(*rargs, row0=0, n_seq=n_req, t_len=seq, seg=seq, emit_state=True)
    (yb_l,) = _rglru(*rargs, row0=nc, n_seq=n_lat, t_len=lat_seq, seg=GRID_W, state=state_rglru_h[:, l])

    r_rows = -(-(SUBLANES + n_exp) // 16) * 16
    wr = jnp.zeros((r_rows, d), F32)
    wr = wr.at[0:n_groups].set(router_group_w[l].T)
    wr = wr.at[SUBLANES:SUBLANES + n_exp].set(router_expert_w[l].transpose(0, 2, 1).reshape(n_exp, d)).astype(BF16)
    br = jnp.zeros((r_rows, LANES), F32)
    br = br.at[0:n_groups, 0].set(router_group_b[l])
    br = br.at[SUBLANES:SUBLANES + n_exp, 0].set(router_expert_b[l].reshape(n_exp))
    x1, hp, ridx, rwt = _out_proj(ya_c, ya_l, yb_c, yb_l, xp, xs, mod, norm2_w[l], w_out[l].astype(BF16), wr, br,
                                  lat_seq=lat_seq, n_groups=n_groups, epg=epg)

    n_blocks = (nt * TOP_K) // MOE_ROWS + n_exp
    dest, row_tok, blk_e, nused = _routing(ridx, nt, n_exp, n_blocks, zero_row=nt)
    only_layer = lambda a: a.reshape(a.shape[1:])
    y_rows = _experts(blk_e, nused, row_tok, hp, only_layer(expert_w_gate), only_layer(expert_w_up),
                      only_layer(expert_w_down), n_blocks)

    dest_flat = dest[0:TOP_K].reshape(-1)
    y_prompt = _combine(dest_flat, x1, rwt, mod, final_norm_w, y_rows, row0=0, n_tok=nc, seq_len=seq, lat=False)
    y_sample = _combine(dest_flat, x1, rwt, mod, final_norm_w, y_rows, row0=nc, n_tok=nl, seq_len=lat_seq,
                        lat=True)

    return (y_prompt.reshape(n_req, seq, d), y_sample.reshape(n_lat, lat_seq, d),
            new_c[:, None], new_n.reshape(n_req, 1, 2, heads, dk), new_m.reshape(n_req, 1, 2, heads),
            new_h[:, None])
```
